```python
import jax, jax.numpy as jnp
from jax import lax
import numpy as np

D_MODEL = 1024
BATCH = 8
SEQ = 2048
DEPTH = 2

N_EVEN = (DEPTH + 1) // 2
N_ODD = DEPTH // 2
MIX_DIM = D_MODEL
HEAD_DIM = 64
ROPE_THETA = 10000.0
NORM_EPS = 1e-6
NEG_INF = -1e30
BIG = 1e9
NSA_DIM = D_MODEL // 2
N_Q_HEADS = NSA_DIM // HEAD_DIM
GQA = 4
N_KV_HEADS = N_Q_HEADS // GQA
KV_DIM = N_KV_HEADS * HEAD_DIM
CMP_BLOCK = 32
CMP_STRIDE = 16
CMP_HIDDEN = 256
SEL_BLOCK = 64
N_SEL = 8
N_LOCAL = 2
WINDOW = 512
Q_BLOCK = 128
ATTN_SCALE = HEAD_DIM ** -0.5
CONV_DIM = D_MODEL // 2
CONV_WIDTH = 3
EVEN_SIZES = (NSA_DIM,) + (KV_DIM,) * 6 + (N_Q_HEADS * 3,) + (CONV_DIM,) * 3
EVEN_COLS = sum(EVEN_SIZES)
RWKV_DIM = D_MODEL // 2
N_RWKV_HEADS = RWKV_DIM // HEAD_DIM
DECAY_LORA = 64
AAA_LORA = 64
GATE_LORA = 128
RWKV_SIZES = (RWKV_DIM,) * 3 + (DECAY_LORA, AAA_LORA, GATE_LORA)
RWKV_COLS = sum(RWKV_SIZES)
LNX_EPS = 64e-5
POOL_DIM = D_MODEL // 2
POOL_WINDOWS = (2, 4, 8, 16)
POOL_GROUP = POOL_DIM // len(POOL_WINDOWS)
ODD_COLS = RWKV_COLS + POOL_DIM
N_EXPERTS = 16
N_EXPERT_GROUPS = 4
EXPERTS_PER_GROUP = N_EXPERTS // N_EXPERT_GROUPS
GROUP_SCORE_TOPK = 2
TOP_K = 2
D_EXPERT = D_MODEL // 2

kernel_name = 'hybrid_nsa_conv_rwkv7_pool_moe'


def rms_norm(x, g):
    xf = x.astype(jnp.float32)
    y = xf * lax.rsqrt(jnp.mean(xf * xf, axis=-1, keepdims=True) + NORM_EPS)
    return (y * g.astype(jnp.float32)).astype(x.dtype)


def modulate(h, shift, scale):
    return h * (1 + scale[:, None, :]) + shift[:, None, :]


def split_cols(t, sizes):
    bounds = [int(b) for b in np.cumsum(sizes)[:-1]]
    return jnp.split(t, bounds, axis=-1)


def to_heads(t):
    b, s, _ = t.shape
    return t.reshape(b, s, -1, HEAD_DIM)


def rope(t):
    s = t.shape[1]
    half = HEAD_DIM // 2
    inv = ROPE_THETA ** (-jnp.arange(half, dtype=jnp.float32) / half)
    ang = jnp.arange(s, dtype=jnp.float32)[:, None] * inv[None, :]
    cos = jnp.cos(ang)[None, :, None, :]
    sin = jnp.sin(ang)[None, :, None, :]
    tf = t.astype(jnp.float32)
    t1, t2 = tf[..., :half], tf[..., half:]
    return jnp.concatenate([t1 * cos - t2 * sin, t2 * cos + t1 * sin], axis=-1).astype(t.dtype)


def masked_softmax(s, mask):
    p = jax.nn.softmax(jnp.where(mask, s.astype(jnp.float32), NEG_INF), axis=-1)
    return p * mask


def nsa_compressed(q, k_tok, v_tok, cmp_pos, cmp_w1, cmp_w2):
    b, s = q.shape[:2]
    n_cmp = (s - CMP_BLOCK) // CMP_STRIDE + 1
    idx = jnp.arange(n_cmp)[:, None] * CMP_STRIDE + jnp.arange(CMP_BLOCK)[None, :]

    def compress(tok, j):
        blocks = tok[:, idx] + cmp_pos[j][None, None, :, None, :]
        flat = blocks.transpose(0, 1, 3, 2, 4).reshape(b, n_cmp, N_KV_HEADS, CMP_BLOCK * HEAD_DIM)
        return jax.nn.gelu(flat @ cmp_w1[j]) @ cmp_w2[j]

    kc = compress(k_tok, 0)
    vc = compress(v_tok, 1)
    t = jnp.arange(s)
    mask = (jnp.arange(n_cmp) * CMP_STRIDE + CMP_BLOCK - 1)[None, :] <= t[:, None]
    sc = jnp.einsum('bshgd,bchd->bhgsc', q, kc) * ATTN_SCALE
    p = masked_softmax(sc, mask)
    out = jnp.einsum('bhgsc,bchd->bshgd', p.astype(vc.dtype), vc)
    return out, p.sum(axis=2)


def nsa_select(imp):
    s, n_cmp = imp.shape[2], imp.shape[3]
    n_blk = s // SEL_BLOCK
    r = SEL_BLOCK // CMP_STRIDE
    c = CMP_BLOCK // CMP_STRIDE
    need = r * n_blk + c - 1
    imp = jnp.pad(imp, ((0, 0), (0, 0), (0, 0), (0, need - n_cmp)))
    p_slc = jnp.zeros(imp.shape[:3] + (n_blk,), jnp.float32)
    for m in range(r):
        for n in range(c):
            p_slc = p_slc + imp[..., m + n: m + n + r * n_blk: r]
    t = jnp.arange(s)[:, None]
    j = jnp.arange(n_blk)[None, :]
    cur = t // SEL_BLOCK
    valid = j * SEL_BLOCK <= t
    forced = (j == 0) | ((cur - j >= 0) & (cur - j < N_LOCAL))
    score = jnp.where(forced, BIG, jnp.where(valid, p_slc, -BIG))
    top, idx = lax.top_k(score, min(N_SEL, n_blk))
    return idx, top > -0.5 * BIG


def nsa_selected(q, k, v, sel_idx, sel_ok):
    b, s = q.shape[:2]
    n_blk = s // SEL_BLOCK
    n_qb = s // Q_BLOCK
    n_sel = sel_idx.shape[-1]
    kb = k.reshape(b, n_blk, SEL_BLOCK, N_KV_HEADS, HEAD_DIM).transpose(0, 3, 1, 2, 4)
    vb = v.reshape(b, n_blk, SEL_BLOCK, N_KV_HEADS, HEAD_DIM).transpose(0, 3, 1, 2, 4)
    qs = jnp.moveaxis(q.reshape(b, n_qb, Q_BLOCK, N_KV_HEADS, GQA, HEAD_DIM), 1, 0)
    ids = jnp.moveaxis(sel_idx.reshape(b, N_KV_HEADS, n_qb, Q_BLOCK, n_sel), 2, 0)
    oks = jnp.moveaxis(sel_ok.reshape(b, N_KV_HEADS, n_qb, Q_BLOCK, n_sel), 2, 0)
    bi = jnp.arange(b)[:, None, None, None]
    hi = jnp.arange(N_KV_HEADS)[None, :, None, None]

    def one_block(args):
        blk, qb, ib, ok = args
        kg = kb[bi, hi, ib]
        vg = vb[bi, hi, ib]
        t = blk * Q_BLOCK + jnp.arange(Q_BLOCK)
        kpos = ib[..., None] * SEL_BLOCK + jnp.arange(SEL_BLOCK)
        mask = ok[..., None] & (kpos <= t[:, None, None])
        mask = mask.reshape(b, N_KV_HEADS, 1, Q_BLOCK, n_sel * SEL_BLOCK)
        sc = jnp.einsum('bqhgd,bhqnkd->bhgqnk', qb, kg).reshape(
            b, N_KV_HEADS, GQA, Q_BLOCK, n_sel * SEL_BLOCK) * ATTN_SCALE
        p = masked_softmax(sc, mask).reshape(b, N_KV_HEADS, GQA, Q_BLOCK, n_sel, SEL_BLOCK)
        return jnp.einsum('bhgqnk,bhqnkd->bqhgd', p.astype(vg.dtype), vg)

    out = lax.map(one_block, (jnp.arange(n_qb), qs, ids, oks))
    return jnp.moveaxis(out, 0, 1).reshape(b, s, N_KV_HEADS, GQA, HEAD_DIM)


def nsa_window(q, k, v):
    b, s = q.shape[:2]
    kp = jnp.pad(k, ((0, 0), (WINDOW, 0), (0, 0), (0, 0)))
    vp = jnp.pad(v, ((0, 0), (WINDOW, 0), (0, 0), (0, 0)))
    span = WINDOW + Q_BLOCK

    def one_block(blk):
        start = blk * Q_BLOCK
        qb = lax.dynamic_slice_in_dim(q, start, Q_BLOCK, axis=1)
        kw = lax.dynamic_slice_in_dim(kp, start, span, axis=1)
        vw = lax.dynamic_slice_in_dim(vp, start, span, axis=1)
        t = start + jnp.arange(Q_BLOCK)
        s_pos = start - WINDOW + jnp.arange(span)
        diff = t[:, None] - s_pos[None, :]
        mask = (diff >= 0) & (diff < WINDOW) & (s_pos[None, :] >= 0)
        sc = jnp.einsum('bqhgd,bkhd->bhgqk', qb, kw) * ATTN_SCALE
        p = masked_softmax(sc, mask)
        return jnp.einsum('bhgqk,bkhd->bqhgd', p.astype(vw.dtype), vw)

    out = lax.map(one_block, jnp.arange(s // Q_BLOCK))
    return jnp.moveaxis(out, 0, 1).reshape(b, s, N_KV_HEADS, GQA, HEAD_DIM)


def short_conv(xb, b_gate, c_gate, conv_w):
    s = xb.shape[1]
    u = jnp.pad(c_gate * xb, ((0, 0), (CONV_WIDTH - 1, 0), (0, 0)))
    y = conv_w[CONV_WIDTH - 1] * u[:, CONV_WIDTH - 1: CONV_WIDTH - 1 + s]
    for kk in range(CONV_WIDTH - 1):
        y = y + conv_w[kk] * u[:, kk: kk + s]
    return b_gate * y


def nsa_conv_mixer(h, w_in, cmp_pos, cmp_w1, cmp_w2, conv_w, w_out):
    b, s, _ = h.shape
    q, kc, vc, ks, vs, kw, vw, gl, xb, bg, cg = split_cols(h @ w_in, EVEN_SIZES)
    q_h = to_heads(q)
    q_nope = q_h.reshape(b, s, N_KV_HEADS, GQA, HEAD_DIM)
    q_rot = rope(q_h).reshape(b, s, N_KV_HEADS, GQA, HEAD_DIM)
    o_cmp, imp = nsa_compressed(q_nope, to_heads(kc), to_heads(vc), cmp_pos, cmp_w1, cmp_w2)
    sel_idx, sel_ok = nsa_select(imp)
    o_slc = nsa_selected(q_rot, rope(to_heads(ks)), to_heads(vs), sel_idx, sel_ok)
    o_win = nsa_window(q_rot, rope(to_heads(kw)), to_heads(vw))
    g = jax.nn.sigmoid(gl).reshape(b, s, N_KV_HEADS, GQA, 3)
    o_nsa = (g[..., 0:1] * o_cmp + g[..., 1:2] * o_slc + g[..., 2:3] * o_win).reshape(b, s, NSA_DIM)
    y_conv = short_conv(xb, bg, cg, conv_w)
    return jnp.concatenate([o_nsa.astype(h.dtype), y_conv.astype(h.dtype)], axis=-1) @ w_out


def rwkv7_scan(r, w, k, v, a, bb):
    b, s, h, n = r.shape

    def step(state, inp):
        r_t, w_t, k_t, v_t, a_t, b_t = inp
        sa = jnp.einsum('bhij,bhj->bhi', state, a_t)
        state = (state * w_t[:, :, None, :] + sa[..., None] * b_t[:, :, None, :]
                 + v_t[..., None] * k_t[:, :, None, :])
        return state, jnp.einsum('bhij,bhj->bhi', state, r_t)

    xs = tuple(jnp.moveaxis(t, 1, 0) for t in (r, w, k, v, a, bb))
    _, ys = lax.scan(step, jnp.zeros((b, h, n, n), jnp.float32), xs)
    return jnp.moveaxis(ys, 0, 1)


def multiscale_pool(u, pool_w, pool_scale):
    b, s, _ = u.shape
    ug = u.reshape(b, s, len(POOL_WINDOWS), POOL_GROUP).astype(jnp.float32)
    cs = jnp.pad(jnp.cumsum(ug, axis=1), ((0, 0), (1, 0), (0, 0), (0, 0)))
    end = jnp.arange(1, s + 1)
    outs = []
    for gi, win in enumerate(POOL_WINDOWS):
        start = jnp.maximum(end - win, 0)
        mean = (cs[:, end, gi] - cs[:, start, gi]) / (end - start).astype(jnp.float32)[None, :, None]
        outs.append(mean - ug[:, :, gi])
    pooled = jnp.stack(outs, axis=2).astype(u.dtype)
    mixed = jnp.einsum('bsgc,gcd->bsgd', pooled, pool_w).reshape(b, s, POOL_DIM)
    return mixed * pool_scale


def rwkv_pool_mixer(h, w_in, mu, w0, w2, a0, a2, g2, k_k, k_a, r_k, lnx_w, lnx_b,
                    pool_w, pool_scale, w_out):
    b, s, _ = h.shape
    f32 = jnp.float32
    proj = h @ w_in
    rw, u = proj[..., :RWKV_COLS], proj[..., RWKV_COLS:]
    rw_prev = jnp.pad(rw, ((0, 0), (1, 0), (0, 0)))[:, :s]
    rw = rw + (rw_prev - rw) * mu
    r, k, v, wl, al, gl = split_cols(rw, RWKV_SIZES)
    w_log = -jax.nn.softplus(-(w0 + jnp.tanh(wl) @ w2).astype(f32)) - 0.5
    decay = jnp.exp(-jnp.exp(w_log))
    a = jax.nn.sigmoid((a0 + al @ a2).astype(f32))
    g = jax.nn.sigmoid(gl) @ g2
    k_mod = k.astype(f32) * (1 + (a - 1) * k_a.astype(f32))

    def hd(t):
        return t.astype(f32).reshape(b, s, N_RWKV_HEADS, HEAD_DIM)

    kk = hd(k * k_k)
    kk = kk / jnp.maximum(jnp.sqrt(jnp.sum(kk * kk, axis=-1, keepdims=True)), 1e-12)
    r_h, k_h, v_h = hd(r), hd(k_mod), hd(v)
    y = rwkv7_scan(r_h, hd(decay), k_h, v_h, -kk, kk * hd(a))
    mean = jnp.mean(y, axis=-1, keepdims=True)
    var = jnp.mean(jnp.square(y - mean), axis=-1, keepdims=True)
    y = ((y - mean) * lax.rsqrt(var + LNX_EPS)).reshape(b, s, RWKV_DIM) * lnx_w + lnx_b
    bonus = jnp.sum(r_h * k_h * r_k, axis=-1, keepdims=True) * v_h
    o_rwkv = ((y + bonus.reshape(b, s, RWKV_DIM)) * g).astype(h.dtype)
    o_pool = multiscale_pool(u, pool_w, pool_scale).astype(h.dtype)
    return jnp.concatenate([o_rwkv, o_pool], axis=-1) @ w_out


def grouped_moe(h, router_w, router_b, w_gate, w_up, w_down):
    b, s, d = h.shape
    t = h.reshape(b * s, d)
    scores = jax.nn.sigmoid((t @ router_w).astype(jnp.float32))
    biased = scores + router_b.astype(jnp.float32)
    grp = biased.reshape(-1, N_EXPERT_GROUPS, EXPERTS_PER_GROUP)
    grp_score = lax.top_k(grp, GROUP_SCORE_TOPK)[0].sum(axis=-1)
    top_grp = jnp.argmax(grp_score, axis=-1)
    in_group = (jnp.arange(N_EXPERTS) // EXPERTS_PER_GROUP)[None, :] == top_grp[:, None]
    _, idx = lax.top_k(jnp.where(in_group, biased, NEG_INF), TOP_K)
    w_sel = jnp.take_along_axis(scores, idx, axis=-1)
    w_sel = w_sel / jnp.sum(w_sel, axis=-1, keepdims=True)
    gates = jnp.sum(jax.nn.one_hot(idx, N_EXPERTS, dtype=jnp.float32) * w_sel[..., None], axis=1)
    gates = gates.astype(t.dtype)
    out = jnp.zeros_like(t)
    for e in range(N_EXPERTS):
        he = jax.nn.silu(t @ w_gate[e]) * (t @ w_up[e])
        out = out + gates[:, e:e + 1] * (he @ w_down[e])
    return out.reshape(b, s, d)


def setup_inputs(seed: int = 0) -> dict:
    key = jax.random.key(seed)
    ks = iter(jax.random.split(key, 48))
    f32 = jnp.float32

    def nrm(shape, scale):
        return scale * jax.random.normal(next(ks), shape, f32)

    def uni(shape, lo, hi):
        return jax.random.uniform(next(ks), shape, f32, lo, hi)

    D = D_MODEL
    return {
        'x': nrm((BATCH, SEQ, D), 1.0),
        'c': nrm((BATCH, D), 1.0),
        'ada_w': nrm((DEPTH, D, 6 * D), 0.5 * D ** -0.5),
        'ada_b': nrm((DEPTH, 6 * D), 0.02),
        'norm_mix': 1.0 + nrm((DEPTH, D), 0.05),
        'norm_ffn': 1.0 + nrm((DEPTH, D), 0.05),
        'even_w_in': nrm((N_EVEN, D, EVEN_COLS), D ** -0.5),
        'even_cmp_pos': nrm((N_EVEN, 2, CMP_BLOCK, HEAD_DIM), 0.02),
        'even_cmp_w1': nrm((N_EVEN, 2, CMP_BLOCK * HEAD_DIM, CMP_HIDDEN), (CMP_BLOCK * HEAD_DIM) ** -0.5),
        'even_cmp_w2': nrm((N_EVEN, 2, CMP_HIDDEN, HEAD_DIM), CMP_HIDDEN ** -0.5),
        'even_conv_w': nrm((N_EVEN, CONV_WIDTH, CONV_DIM), CONV_WIDTH ** -0.5),
        'even_w_out': nrm((N_EVEN, MIX_DIM, D), MIX_DIM ** -0.5),
        'odd_w_in': nrm((N_ODD, D, ODD_COLS), D ** -0.5),
        'odd_mu': uni((N_ODD, RWKV_COLS), 0.0, 1.0),
        'odd_w0': uni((N_ODD, RWKV_DIM), -6.0, -1.0),
        'odd_w2': nrm((N_ODD, DECAY_LORA, RWKV_DIM), 0.1),
        'odd_a0': nrm((N_ODD, RWKV_DIM), 0.1),
        'odd_a2': nrm((N_ODD, AAA_LORA, RWKV_DIM), AAA_LORA ** -0.5),
        'odd_g2': nrm((N_ODD, GATE_LORA, RWKV_DIM), GATE_LORA ** -0.5),
        'odd_k_k': 0.85 + nrm((N_ODD, RWKV_DIM), 0.05),
        'odd_k_a': 1.0 + nrm((N_ODD, RWKV_DIM), 0.05),
        'odd_r_k': nrm((N_ODD, N_RWKV_HEADS, HEAD_DIM), 0.1),
        'odd_lnx_w': 1.0 + nrm((N_ODD, RWKV_DIM), 0.05),
        'odd_lnx_b': nrm((N_ODD, RWKV_DIM), 0.02),
        'odd_pool_w': nrm((N_ODD, len(POOL_WINDOWS), POOL_GROUP, POOL_GROUP), POOL_GROUP ** -0.5),
        'odd_pool_scale': 1.0 + nrm((N_ODD, POOL_DIM), 0.1),
        'odd_w_out': nrm((N_ODD, MIX_DIM, D), MIX_DIM ** -0.5),
        'router_w': nrm((D, N_EXPERTS), D ** -0.5),
        'router_b': nrm((N_EXPERTS,), 0.01),
        'moe_w_gate': nrm((DEPTH, N_EXPERTS, D, D_EXPERT), D ** -0.5),
        'moe_w_up': nrm((DEPTH, N_EXPERTS, D, D_EXPERT), D ** -0.5),
        'moe_w_down': nrm((DEPTH, N_EXPERTS, D_EXPERT, D), D_EXPERT ** -0.5),
        'final_norm': 1.0 + nrm((D,), 0.05),
    }


def reference(x, c, ada_w, ada_b, norm_mix, norm_ffn,
              even_w_in, even_cmp_pos, even_cmp_w1, even_cmp_w2, even_conv_w, even_w_out,
              odd_w_in, odd_mu, odd_w0, odd_w2, odd_a0, odd_a2, odd_g2, odd_k_k, odd_k_a, odd_r_k,
              odd_lnx_w, odd_lnx_b, odd_pool_w, odd_pool_scale, odd_w_out,
              router_w, router_b, moe_w_gate, moe_w_up, moe_w_down, final_norm):
    cond = jax.nn.silu(c)
    for layer in range(DEPTH):
        mod = cond @ ada_w[layer] + ada_b[layer]
        sh1, sc1, g1, sh2, sc2, g2 = jnp.split(mod, 6, axis=-1)
        h = modulate(rms_norm(x, norm_mix[layer]), sh1, sc1)
        i = layer // 2
        if layer % 2 == 0:
            y = nsa_conv_mixer(h, even_w_in[i], even_cmp_pos[i], even_cmp_w1[i], even_cmp_w2[i],
                               even_conv_w[i], even_w_out[i])
        else:
            y = rwkv_pool_mixer(h, odd_w_in[i], odd_mu[i], odd_w0[i], odd_w2[i], odd_a0[i], odd_a2[i],
                                odd_g2[i], odd_k_k[i], odd_k_a[i], odd_r_k[i], odd_lnx_w[i], odd_lnx_b[i],
                                odd_pool_w[i], odd_pool_scale[i], odd_w_out[i])
        x = x + g1[:, None, :] * y
        h = modulate(rms_norm(x, norm_ffn[layer]), sh2, sc2)
        x = x + g2[:, None, :] * grouped_moe(h, router_w, router_b, moe_w_gate[layer],
                                             moe_w_up[layer], moe_w_down[layer])
    return rms_norm(x, final_norm)
```

```python
import functools

import jax
import jax.numpy as jnp
import numpy as np
from jax import lax
from jax.experimental import pallas as pl
from jax.experimental.pallas import tpu as pltpu

F32 = jnp.float32
BF16 = jnp.bfloat16
HIGHEST = lax.Precision.HIGHEST

D_MODEL = 1024
DEPTH = 2
HEAD_DIM = 64
ROPE_THETA = 10000.0
NORM_EPS = 1e-6
NEG_INF = -1e30
BIG = 1e9
NSA_DIM = 512
N_KV_HEADS = 2
GQA = 4
KV_DIM = 128
CMP_BLOCK = 32
CMP_STRIDE = 16
CMP_HIDDEN = 256
SEL_BLOCK = 64
N_SEL = 8
N_LOCAL = 2
WINDOW = 512
Q_BLOCK = 128
ATTN_SCALE = HEAD_DIM ** -0.5
CONV_DIM = 512
RWKV_DIM = 512
N_RWKV_HEADS = 8
LNX_EPS = 64e-5
POOL_WINDOWS = (2, 4, 8, 16)
POOL_GROUP = 128
N_EXPERTS = 16
N_EXPERT_GROUPS = 4
EXPERTS_PER_GROUP = 4
D_EXPERT = 512

LANES = 128
SUBLANES = 8
VMEM_LIMIT = 56 * 1024 * 1024

TOK_TILE = 512
MOE_TOK_TILE = 1024
SEL_CHUNK = 512
SCAN_CHUNK = 64
CONV_HALO = 8
POOL_HALO = 16

EVEN_PAD_COLS = 3072
ODD_PAD_COLS = 2432
ODD_RW_COLS = 1920


def _cparams(sem):
    return pltpu.CompilerParams(dimension_semantics=sem, vmem_limit_bytes=VMEM_LIMIT)


def _nt(a, b, precision=None):
    return lax.dot_general(a, b, (((1,), (1,)), ((), ())), preferred_element_type=F32,
                           precision=precision)


def _tn(a, b):
    return lax.dot_general(a, b, (((0,), (0,)), ((), ())), preferred_element_type=F32)


def _mm(a, b, precision=None):
    return jnp.dot(a, b, preferred_element_type=F32, precision=precision)


def _norm_mod(x, g, sh, sc):
    ms = jnp.mean(x * x, axis=-1, keepdims=True)
    return (x * lax.rsqrt(ms + NORM_EPS) * g) * (1.0 + sc) + sh


def _split_sum(x, ones_bf16):
    hi = x.astype(BF16)
    lo = (x - hi.astype(F32)).astype(BF16)
    return _mm(hi, ones_bf16) + _mm(lo, ones_bf16)


def _ada_kernel(c_ref, w_ref, b_ref, o_ref):
    c = c_ref[...]
    cond = c * jax.nn.sigmoid(c)
    o_ref[0] = _mm(cond, w_ref[0], precision=HIGHEST) + b_ref[0]


def ada_modulation(c, ada_w, ada_b):
    depth, d, cols = ada_w.shape
    b = c.shape[0]
    tn = 1536
    return pl.pallas_call(
        _ada_kernel,
        out_shape=jax.ShapeDtypeStruct((depth, b, cols), F32),
        grid=(depth, cols // tn),
        in_specs=[pl.BlockSpec((b, d), lambda l, j: (0, 0)),
                  pl.BlockSpec((1, d, tn), lambda l, j: (l, 0, j)),
                  pl.BlockSpec((1, 1, tn), lambda l, j: (l, 0, j))],
        out_specs=pl.BlockSpec((1, b, tn), lambda l, j: (l, 0, j)),
        compiler_params=_cparams(("parallel", "parallel")),
        name="ada_modulation",
    )(c, ada_w, ada_b.reshape(depth, 1, cols))


def _rope128(t, cos, sin, lane):
    rot = jnp.where((lane % HEAD_DIM) < HEAD_DIM // 2,
                    -pltpu.roll(t, LANES - HEAD_DIM // 2, 1), pltpu.roll(t, HEAD_DIM // 2, 1))
    return t * cos + rot * sin


def _even_in_kernel(x_ref, g_ref, sh_ref, sc_ref, w_ref, cos_ref, sin_ref,
                    qn_ref, qr_ref, kc_ref, vc_ref, ks_ref, vs_ref, kw_ref, vw_ref,
                    gate_ref, u_ref, bg_ref):
    h = _norm_mod(x_ref[...], g_ref[...], sh_ref[0], sc_ref[0])
    proj = _mm(h.astype(BF16), w_ref[...])
    cos = cos_ref[...]
    sin = sin_ref[...]
    lane = lax.broadcasted_iota(jnp.int32, (1, LANES), 1)
    for i in range(NSA_DIM // LANES):
        q = proj[:, i * LANES:(i + 1) * LANES]
        qn_ref[:, i * LANES:(i + 1) * LANES] = q.astype(BF16)
        qr_ref[:, i * LANES:(i + 1) * LANES] = _rope128(q, cos, sin, lane).astype(BF16)
    o = NSA_DIM
    kc_ref[...] = proj[:, o:o + 128]
    vc_ref[...] = proj[:, o + 128:o + 256]
    ks_ref[...] = _rope128(proj[:, o + 256:o + 384], cos, sin, lane).astype(BF16)
    vs_ref[...] = proj[:, o + 384:o + 512].astype(BF16)
    kw_ref[...] = _rope128(proj[:, o + 512:o + 640], cos, sin, lane).astype(BF16)
    vw_ref[...] = proj[:, o + 640:o + 768].astype(BF16)
    o += 768
    gate_ref[...] = jax.nn.sigmoid(proj[:, o:o + 256])
    o += 256
    xb = proj[:, o:o + 512]
    bg_ref[...] = proj[:, o + 512:o + 1024]
    u_ref[...] = proj[:, o + 1024:o + 1536] * xb


def even_in_proj(x2, g, sh, sc, w_pad, cos, sin, seq):
    n, d = x2.shape
    tm = min(TOK_TILE, seq)
    tpb = seq // tm
    row = lambda i: (i, 0)
    per_b = lambda i: (i // tpb, 0, 0)
    pos = lambda i: (i % tpb, 0)
    outs = [((n, 512), BF16), ((n, 512), BF16), ((n, 128), F32), ((n, 128), F32),
            ((n, 128), BF16), ((n, 128), BF16), ((n, 128), BF16), ((n, 128), BF16),
            ((n, 256), F32), ((n, 512), F32), ((n, 512), F32)]
    return pl.pallas_call(
        _even_in_kernel,
        out_shape=[jax.ShapeDtypeStruct(s, t) for s, t in outs],
        grid=(n // tm,),
        in_specs=[pl.BlockSpec((tm, d), row),
                  pl.BlockSpec((1, d), lambda i: (0, 0)),
                  pl.BlockSpec((1, 1, d), per_b),
                  pl.BlockSpec((1, 1, d), per_b),
                  pl.BlockSpec((d, EVEN_PAD_COLS), lambda i: (0, 0)),
                  pl.BlockSpec((tm, LANES), pos),
                  pl.BlockSpec((tm, LANES), pos)],
        out_specs=[pl.BlockSpec((tm, s[1]), row) for s, _ in outs],
        compiler_params=_cparams(("parallel",)),
        name="even_in_proj",
    )(x2, g, sh, sc, w_pad, cos, sin)


def _compress_kernel(k_ref, v_ref, pos_ref, w1_ref, w2_ref, ko_ref, vo_ref):
    for j, (src, dst) in enumerate(((k_ref, ko_ref), (v_ref, vo_ref))):
        xr = src[0]
        n_rows = xr.shape[0]
        a0 = _mm((xr + pos_ref[j, 0]).astype(BF16), w1_ref[j, 0])
        a1 = _mm((xr + pos_ref[j, 1]).astype(BF16), w1_ref[j, 1])
        hid = a0 + pltpu.roll(a1, n_rows - 1, 0)
        hid = jax.nn.gelu(hid)
        dst[0] = _mm(hid.astype(BF16), w2_ref[j]).astype(BF16)


def compress_kv(kc, vc, pos_ext, w1_ext, w2_ext, batch, seq):
    rows = seq // CMP_STRIDE
    width = CMP_STRIDE * KV_DIM
    kr = kc.reshape(batch, rows, width)
    vr = vc.reshape(batch, rows, width)
    blk = pl.BlockSpec((1, rows, width), lambda b: (b, 0, 0))
    oblk = pl.BlockSpec((1, rows, KV_DIM), lambda b: (b, 0, 0))
    return pl.pallas_call(
        _compress_kernel,
        out_shape=[jax.ShapeDtypeStruct((batch, rows, KV_DIM), BF16)] * 2,
        grid=(batch,),
        in_specs=[blk, blk,
                  pl.BlockSpec(pos_ext.shape, lambda b: (0, 0, 0, 0)),
                  pl.BlockSpec(w1_ext.shape, lambda b: (0, 0, 0, 0)),
                  pl.BlockSpec(w2_ext.shape, lambda b: (0, 0, 0))],
        out_specs=[oblk, oblk],
        compiler_params=_cparams(("parallel",)),
        name="compress_kv",
    )(kr, vr, pos_ext, w1_ext, w2_ext)


def _softmax_rows(sc, mask):
    s = jnp.where(mask, sc, NEG_INF)
    m = jnp.max(s, axis=1, keepdims=True)
    e = jnp.where(mask, jnp.exp(s - m), 0.0)
    l = jnp.sum(e, axis=1, keepdims=True)
    inv = jnp.where(l > 0.0, 1.0 / jnp.where(l > 0.0, l, 1.0), 0.0)
    return e * inv


def _nsa_kernel(qn_ref, qr_ref, kc_ref, vc_ref, ks_ref, vs_ref, kw_ref, vw_ref, gate_ref,
                selq_ref, msel_ref, eexp_ref, o_ref, *, seq, n_sel, sel_chunk, win_len):
    h = pl.program_id(1)
    qt = pl.program_id(2)
    t0 = qt * Q_BLOCK
    n_blk = seq // SEL_BLOCK
    n_cmp_pad = seq // CMP_STRIDE
    tpos = t0 + lax.broadcasted_iota(jnp.int32, (Q_BLOCK, 1), 0)
    lane = lax.broadcasted_iota(jnp.int32, (1, LANES), 1)
    head_lanes = (lane // HEAD_DIM) == h

    qn = qn_ref[...]
    qr = qr_ref[...]
    qn_g = [_mm(qn, selq_ref[0, g]).astype(BF16) for g in range(GQA)]
    qr_g = [_mm(qr, selq_ref[0, g]).astype(BF16) for g in range(GQA)]

    kc = kc_ref[0]
    vc = vc_ref[0]
    cpos = lax.broadcasted_iota(jnp.int32, (1, n_cmp_pad), 1) * CMP_STRIDE + (CMP_BLOCK - 1)
    cmask = cpos <= tpos
    imp = jnp.zeros((Q_BLOCK, n_cmp_pad), F32)
    o_cmp = []
    for g in range(GQA):
        p = _softmax_rows(_nt(qn_g[g], kc) * ATTN_SCALE, cmask)
        imp = imp + p
        o_cmp.append(_mm(p.astype(BF16), vc))

    pslc = _mm(imp, msel_ref[...], precision=HIGHEST)
    jblk = lax.broadcasted_iota(jnp.int32, (1, LANES), 1)
    cur = tpos // SEL_BLOCK
    valid = jblk * SEL_BLOCK <= tpos
    forced = (jblk == 0) | ((cur - jblk >= 0) & (cur - jblk < N_LOCAL))
    score = jnp.where(forced, BIG, jnp.where(valid, pslc, -BIG))
    rank = jnp.zeros((Q_BLOCK, LANES), jnp.int32)
    for jp in range(n_blk):
        col = score[:, jp:jp + 1]
        beats = (col > score) | ((col == score) & (jblk > jp))
        rank = rank + beats.astype(jnp.int32)
    sel = ((rank < n_sel) & (score > -0.5 * BIG)).astype(BF16)

    n_chunks = (t0 + Q_BLOCK - 1) // sel_chunk + 1

    def sel_body(c, carry):
        ms, ls, accs = carry
        start = pl.multiple_of(c * sel_chunk, sel_chunk)
        kblk = ks_ref[0, pl.ds(start, sel_chunk), :]
        vblk = vs_ref[0, pl.ds(start, sel_chunk), :]
        kpos = start + lax.broadcasted_iota(jnp.int32, (1, sel_chunk), 1)
        mask = (_mm(sel, eexp_ref[c]) > 0.5) & (kpos <= tpos)
        new_m, new_l, new_acc = [], [], []
        for g in range(GQA):
            s = jnp.where(mask, _nt(qr_g[g], kblk) * ATTN_SCALE, NEG_INF)
            m_new = jnp.maximum(ms[g], jnp.max(s, axis=1, keepdims=True))
            alpha = jnp.exp(ms[g] - m_new)
            p = jnp.where(mask, jnp.exp(s - m_new), 0.0)
            new_m.append(m_new)
            new_l.append(alpha * ls[g] + jnp.sum(p, axis=1, keepdims=True))
            new_acc.append(alpha * accs[g] + _mm(p.astype(BF16), vblk))
        return tuple(new_m), tuple(new_l), tuple(new_acc)

    init = (tuple(jnp.full((Q_BLOCK, 1), NEG_INF, F32) for _ in range(GQA)),
            tuple(jnp.zeros((Q_BLOCK, 1), F32) for _ in range(GQA)),
            tuple(jnp.zeros((Q_BLOCK, LANES), F32) for _ in range(GQA)))
    _, ls, accs = lax.fori_loop(0, n_chunks, sel_body, init)
    o_slc = []
    for g in range(GQA):
        inv = jnp.where(ls[g] > 0.0, 1.0 / jnp.where(ls[g] > 0.0, ls[g], 1.0), 0.0)
        o_slc.append(accs[g] * inv)

    ws = pl.multiple_of(jnp.maximum(qt - WINDOW // Q_BLOCK, 0) * Q_BLOCK, Q_BLOCK)
    kwb = kw_ref[0, pl.ds(ws, win_len), :]
    vwb = vw_ref[0, pl.ds(ws, win_len), :]
    diff = tpos - (ws + lax.broadcasted_iota(jnp.int32, (1, win_len), 1))
    wmask = (diff >= 0) & (diff < WINDOW)
    o_win = []
    for g in range(GQA):
        p = _softmax_rows(_nt(qr_g[g], kwb) * ATTN_SCALE, wmask)
        o_win.append(_mm(p.astype(BF16), vwb))

    gate = gate_ref[...]
    og = []
    for g in range(GQA):
        o = (gate[:, 3 * g:3 * g + 1] * o_cmp[g] + gate[:, 3 * g + 1:3 * g + 2] * o_slc[g]
             + gate[:, 3 * g + 2:3 * g + 3] * o_win[g])
        og.append(jnp.where(head_lanes, o, pltpu.roll(o, HEAD_DIM, 1)))
    low = lane < HEAD_DIM
    o_ref[:, 0:LANES] = jnp.where(low, og[0], og[1])
    o_ref[:, LANES:2 * LANES] = jnp.where(low, og[2], og[3])


def nsa_attention(qn, qr, kcmp, vcmp, ks, vs, kw, vw, gate, selq, msel, eexp, batch, seq):
    n = batch * seq
    nq = seq // Q_BLOCK
    sel_chunk = min(SEL_CHUNK, seq)
    win_len = min(WINDOW + Q_BLOCK, seq)
    n_sel = min(N_SEL, seq // SEL_BLOCK)
    qspec = pl.BlockSpec((Q_BLOCK, GQA * HEAD_DIM), lambda b, h, q: (b * nq + q, h))
    cspec = pl.BlockSpec((1, seq // CMP_STRIDE, KV_DIM), lambda b, h, q: (b, 0, 0))
    kspec = pl.BlockSpec((1, seq, KV_DIM), lambda b, h, q: (b, 0, 0))
    kern = functools.partial(_nsa_kernel, seq=seq, n_sel=n_sel, sel_chunk=sel_chunk, win_len=win_len)
    return pl.pallas_call(
        kern,
        out_shape=jax.ShapeDtypeStruct((n, NSA_DIM), F32),
        grid=(batch, N_KV_HEADS, nq),
        in_specs=[qspec, qspec, cspec, cspec, kspec, kspec, kspec, kspec,
                  pl.BlockSpec((Q_BLOCK, LANES), lambda b, h, q: (b * nq + q, h)),
                  pl.BlockSpec((1, GQA, GQA * HEAD_DIM, LANES), lambda b, h, q: (h, 0, 0, 0)),
                  pl.BlockSpec(msel.shape, lambda b, h, q: (0, 0)),
                  pl.BlockSpec(eexp.shape, lambda b, h, q: (0, 0, 0))],
        out_specs=qspec,
        compiler_params=_cparams(("parallel", "parallel", "arbitrary")),
        name="nsa_attention",
    )(qn, qr, kcmp, vcmp, ks, vs, kw, vw, gate, selq, msel, eexp)


def _route(h2, rwt_ref, rb_ref, gates_ref):
    logits = _nt(rwt_ref[...], h2, precision=HIGHEST)
    scores = jax.nn.sigmoid(logits)
    biased = scores + rb_ref[...]
    rows = [biased[e:e + 1, :] for e in range(N_EXPERTS)]
    srow = [scores[e:e + 1, :] for e in range(N_EXPERTS)]
    gscore = []
    for gi in range(N_EXPERT_GROUPS):
        r = rows[gi * EXPERTS_PER_GROUP:(gi + 1) * EXPERTS_PER_GROUP]
        best = None
        for a in range(EXPERTS_PER_GROUP):
            for b in range(a + 1, EXPERTS_PER_GROUP):
                pair = r[a] + r[b]
                best = pair if best is None else jnp.maximum(best, pair)
        gscore.append(best)
    top_val = gscore[0]
    top_grp = jnp.zeros_like(top_val, dtype=jnp.int32)
    for gi in range(1, N_EXPERT_GROUPS):
        upd = gscore[gi] > top_val
        top_grp = jnp.where(upd, gi, top_grp)
        top_val = jnp.where(upd, gscore[gi], top_val)
    masked = [jnp.where(top_grp == e // EXPERTS_PER_GROUP, rows[e], NEG_INF) for e in range(N_EXPERTS)]
    b1 = masked[0]
    i1 = jnp.zeros_like(top_grp)
    for e in range(1, N_EXPERTS):
        upd = masked[e] > b1
        i1 = jnp.where(upd, e, i1)
        b1 = jnp.where(upd, masked[e], b1)
    b2 = None
    i2 = None
    for e in range(N_EXPERTS):
        v = jnp.where(i1 == e, -jnp.inf, masked[e])
        if b2 is None:
            b2, i2 = v, jnp.zeros_like(top_grp)
        else:
            upd = v > b2
            i2 = jnp.where(upd, e, i2)
            b2 = jnp.where(upd, v, b2)
    s1 = jnp.zeros_like(top_val)
    s2 = jnp.zeros_like(top_val)
    for e in range(N_EXPERTS):
        s1 = s1 + jnp.where(i1 == e, srow[e], 0.0)
        s2 = s2 + jnp.where(i2 == e, srow[e], 0.0)
    tot = s1 + s2
    w1 = s1 / tot
    w2 = s2 / tot
    for e in range(N_EXPERTS):
        gates_ref[e:e + 1, :] = jnp.where(i1 == e, w1, 0.0) + jnp.where(i2 == e, w2, 0.0)


def _tail(cat_bf16, x_ref, g1_ref, wout_ref, ng_ref, sh2_ref, sc2_ref, rwt_ref, rb_ref,
          x1_ref, h2_ref, gates_ref):
    y = _mm(cat_bf16, wout_ref[...])
    x1 = x_ref[...] + g1_ref[0] * y
    x1_ref[...] = x1
    h2 = _norm_mod(x1, ng_ref[...], sh2_ref[0], sc2_ref[0])
    h2_ref[...] = h2.astype(BF16)
    _route(h2, rwt_ref, rb_ref, gates_ref)


def _even_out_kernel(o_ref, u_ref, uh_ref, bg_ref, cw_ref, x_ref, g1_ref, wout_ref, ng_ref, sh2_ref,
                     sc2_ref, rwt_ref, rb_ref, x1_ref, h2_ref, gates_ref, *, tiles_per_seq):
    first = (pl.program_id(0) % tiles_per_seq) == 0
    u = u_ref[...]
    tm = u.shape[0]
    halo = jnp.where(first, 0.0, uh_ref[...])
    ext = jnp.concatenate([halo, u], axis=0)
    u1 = pltpu.roll(ext, 1, 0)[CONV_HALO:]
    u2 = pltpu.roll(ext, 2, 0)[CONV_HALO:]
    cw = cw_ref[...]
    y_conv = bg_ref[...] * (cw[2:3] * u + cw[1:2] * u1 + cw[0:1] * u2)
    cat = jnp.concatenate([o_ref[...], y_conv], axis=1).astype(BF16)
    _tail(cat, x_ref, g1_ref, wout_ref, ng_ref, sh2_ref, sc2_ref, rwt_ref, rb_ref,
          x1_ref, h2_ref, gates_ref)


def _tail_specs(tm, d, tpb):
    row = lambda i: (i, 0)
    per_b = lambda i: (i // tpb, 0, 0)
    const2 = lambda i: (0, 0)
    ins = [pl.BlockSpec((tm, d), row),
           pl.BlockSpec((1, 1, d), per_b),
           pl.BlockSpec((d, d), const2),
           pl.BlockSpec((1, d), const2),
           pl.BlockSpec((1, 1, d), per_b),
           pl.BlockSpec((1, 1, d), per_b),
           pl.BlockSpec((N_EXPERTS, d), const2),
           pl.BlockSpec((N_EXPERTS, 1), const2)]
    outs = [pl.BlockSpec((tm, d), row), pl.BlockSpec((tm, d), row),
            pl.BlockSpec((N_EXPERTS, tm), lambda i: (0, i))]
    return ins, outs


def _tail_out_shapes(n, d):
    return [jax.ShapeDtypeStruct((n, d), F32), jax.ShapeDtypeStruct((n, d), BF16),
            jax.ShapeDtypeStruct((N_EXPERTS, n), F32)]


def even_out_proj(o_nsa, u, bg, conv_w, x2, g1, w_out, ng, sh2, sc2, rwt, rb, seq):
    n, d = x2.shape
    tm = min(TOK_TILE, seq)
    tpb = seq // tm
    row = lambda i: (i, 0)
    halo = lambda i: (jnp.maximum(i * (tm // CONV_HALO) - 1, 0), 0)
    tin, tout = _tail_specs(tm, d, tpb)
    return pl.pallas_call(
        functools.partial(_even_out_kernel, tiles_per_seq=tpb),
        out_shape=_tail_out_shapes(n, d),
        grid=(n // tm,),
        in_specs=[pl.BlockSpec((tm, NSA_DIM), row),
                  pl.BlockSpec((tm, CONV_DIM), row),
                  pl.BlockSpec((CONV_HALO, CONV_DIM), halo),
                  pl.BlockSpec((tm, CONV_DIM), row),
                  pl.BlockSpec(conv_w.shape, lambda i: (0, 0))] + tin,
        out_specs=tout,
        compiler_params=_cparams(("parallel",)),
        name="even_out_proj",
    )(o_nsa, u, u, bg, conv_w, x2, g1, w_out, ng, sh2, sc2, rwt, rb)


def _moe_kernel(h_ref, gate_ref, wg_ref, wu_ref, wd_ref, x_ref, g2_ref, fn_ref, o_ref, acc_ref,
                *, final_norm):
    e = pl.program_id(1)

    @pl.when(e == 0)
    def _():
        acc_ref[...] = jnp.zeros_like(acc_ref)

    h = h_ref[...]
    a = _mm(h, wg_ref[0].astype(BF16))
    b = _mm(h, wu_ref[0].astype(BF16))
    he = (a * jax.nn.sigmoid(a)) * b
    acc_ref[...] += gate_ref[0] * _mm(he.astype(BF16), wd_ref[0].astype(BF16))

    @pl.when(e == N_EXPERTS - 1)
    def _():
        x = x_ref[...] + g2_ref[0] * acc_ref[...]
        if final_norm:
            ms = jnp.mean(x * x, axis=-1, keepdims=True)
            x = x * lax.rsqrt(ms + NORM_EPS) * fn_ref[...]
        o_ref[...] = x


def moe_dense(h2, gates3, w_gate, w_up, w_down, x1, g2, fnorm, seq, final_norm):
    n, d = x1.shape
    tm = min(MOE_TOK_TILE, seq)
    tpb = seq // tm
    row = lambda i, e: (i, 0)
    return pl.pallas_call(
        functools.partial(_moe_kernel, final_norm=final_norm),
        out_shape=jax.ShapeDtypeStruct((n, d), F32),
        grid=(n // tm, N_EXPERTS),
        in_specs=[pl.BlockSpec((tm, d), row),
                  pl.BlockSpec((1, tm, 1), lambda i, e: (e, i, 0)),
                  pl.BlockSpec((1, d, D_EXPERT), lambda i, e: (e, 0, 0)),
                  pl.BlockSpec((1, d, D_EXPERT), lambda i, e: (e, 0, 0)),
                  pl.BlockSpec((1, D_EXPERT, d), lambda i, e: (e, 0, 0)),
                  pl.BlockSpec((tm, d), row),
                  pl.BlockSpec((1, 1, d), lambda i, e: (i // tpb, 0, 0)),
                  pl.BlockSpec((1, d), lambda i, e: (0, 0))],
        out_specs=pl.BlockSpec((tm, d), row),
        scratch_shapes=[pltpu.VMEM((tm, d), F32)],
        compiler_params=_cparams(("parallel", "arbitrary")),
        name="moe_dense",
    )(h2, gates3, w_gate, w_up, w_down, x1, g2, fnorm)


def _odd_in_kernel(x_ref, g_ref, sh_ref, sc_ref, w_ref, mu_ref, w0_ref, w2_ref, a0_ref, a2_ref,
                   g2_ref, kk_ref, ka_ref, ones_ref, pw_ref, ps_ref,
                   r_ref, lw_ref, km_ref, v_ref, kn_ref, kb_ref, gg_ref, op_ref,
                   rw_carry, u_carry, *, tiles_per_seq, tm):
    i = pl.program_id(0)
    first = (i % tiles_per_seq) == 0
    h = _norm_mod(x_ref[...], g_ref[...], sh_ref[0], sc_ref[0])
    proj = _mm(h.astype(BF16), w_ref[...])

    rw = proj[:, :ODD_RW_COLS]
    row0 = jnp.where(first, 0.0, rw_carry[0:1, :])
    ridx = lax.broadcasted_iota(jnp.int32, (tm, 1), 0)
    prev = jnp.where(ridx == 0, row0, pltpu.roll(rw, 1, 0))
    rw_carry[0:1, :] = rw[tm - 1:tm, :]
    rw = rw + (prev - rw) * mu_ref[...]

    r = rw[:, 0:512]
    k = rw[:, 512:1024]
    v = rw[:, 1024:1536]
    wl = rw[:, 1536:1664]
    al = rw[:, 1664:1792]
    gl = rw[:, 1792:1920]
    z = -(w0_ref[...] + _mm(jnp.tanh(wl).astype(BF16), w2_ref[...]))
    softplus = jnp.maximum(z, 0.0) + jnp.log1p(jnp.exp(-jnp.abs(z)))
    w_log = -softplus - 0.5
    a = jax.nn.sigmoid(a0_ref[...] + _mm(al.astype(BF16), a2_ref[...]))
    gg_ref[...] = _mm(jax.nn.sigmoid(gl).astype(BF16), g2_ref[...])
    kk0 = k * kk_ref[...]
    ss = _split_sum(kk0 * kk0, ones_ref[...])
    kk = kk0 / jnp.maximum(jnp.sqrt(ss), 1e-12)
    r_ref[...] = r
    lw_ref[...] = -jnp.exp(w_log)
    km_ref[...] = k * (1.0 + (a - 1.0) * ka_ref[...])
    v_ref[...] = v
    kn_ref[...] = kk
    kb_ref[...] = kk * a

    u = proj[:, ODD_RW_COLS:]
    halo = jnp.where(first, 0.0, u_carry[...])
    u_carry[...] = u[tm - POOL_HALO:, :]
    ext = jnp.concatenate([halo, u], axis=0)
    tseq = (i % tiles_per_seq) * tm + ridx
    for gi, win in enumerate(POOL_WINDOWS):
        xg = ext[:, gi * POOL_GROUP:(gi + 1) * POOL_GROUP]
        s = xg
        step = 1
        while step < win:
            s = s + pltpu.roll(s, step, 0)
            step *= 2
        cnt = jnp.minimum(tseq + 1, win).astype(F32)
        pooled = s[POOL_HALO:] / cnt - xg[POOL_HALO:]
        mixed = _mm(pooled.astype(BF16), pw_ref[gi])
        op_ref[:, gi * POOL_GROUP:(gi + 1) * POOL_GROUP] = (
            mixed * ps_ref[:, gi * POOL_GROUP:(gi + 1) * POOL_GROUP])


def odd_in_proj(x2, g, sh, sc, w_pad, mu_pad, w0, w2p, a0, a2p, g2, k_k, k_a, ones_bd, pool_w,
                pool_scale, seq):
    n, d = x2.shape
    tm = min(TOK_TILE, seq)
    tpb = seq // tm
    row = lambda i: (i, 0)
    per_b = lambda i: (i // tpb, 0, 0)
    c2 = lambda i: (0, 0)
    full2 = lambda a: pl.BlockSpec(a.shape, c2)
    return pl.pallas_call(
        functools.partial(_odd_in_kernel, tiles_per_seq=tpb, tm=tm),
        out_shape=[jax.ShapeDtypeStruct((n, RWKV_DIM), F32)] * 8,
        grid=(n // tm,),
        in_specs=[pl.BlockSpec((tm, d), row), pl.BlockSpec((1, d), c2),
                  pl.BlockSpec((1, 1, d), per_b), pl.BlockSpec((1, 1, d), per_b),
                  full2(w_pad), full2(mu_pad), full2(w0), full2(w2p), full2(a0), full2(a2p),
                  full2(g2), full2(k_k), full2(k_a), full2(ones_bd),
                  pl.BlockSpec(pool_w.shape, lambda i: (0, 0, 0)), full2(pool_scale)],
        out_specs=[pl.BlockSpec((tm, RWKV_DIM), row)] * 8,
        scratch_shapes=[pltpu.VMEM((SUBLANES, ODD_RW_COLS), F32),
                        pltpu.VMEM((POOL_HALO, RWKV_DIM), F32)],
        compiler_params=_cparams(("arbitrary",)),
        name="odd_in_proj",
    )(x2, g, sh, sc, w_pad, mu_pad, w0, w2p, a0, a2p, g2, k_k, k_a, ones_bd, pool_w, pool_scale)


def _scan_kernel(r_ref, lw_ref, km_ref, v_ref, kn_ref, kb_ref, y_ref, st_ref, *, chunk):
    c = pl.program_id(1)

    @pl.when(c == 0)
    def _():
        st_ref[...] = jnp.zeros_like(st_ref)

    L = chunk
    ti = lax.broadcasted_iota(jnp.int32, (L, L), 0)
    si = lax.broadcasted_iota(jnp.int32, (L, L), 1)
    strict = si < ti
    incl = si <= ti
    lane = lax.broadcasted_iota(jnp.int32, (1, LANES), 1)
    rowi = lax.broadcasted_iota(jnp.int32, (L, 1), 0)
    bi = lax.broadcasted_iota(jnp.int32, (LANES, LANES), 0) // HEAD_DIM
    bj = lax.broadcasted_iota(jnp.int32, (LANES, LANES), 1) // HEAD_DIM
    same_head = bi == bj
    eye = (lax.broadcasted_iota(jnp.int32, (LANES, LANES), 0)
           == lax.broadcasted_iota(jnp.int32, (LANES, LANES), 1))

    for p in range(N_RWKV_HEADS // 2):
        sl = slice(p * LANES, (p + 1) * LANES)
        lw = lw_ref[:, sl]
        cum = lw
        step = 1
        while step < L:
            cum = cum + jnp.where(rowi >= step, pltpu.roll(cum, step, 0), 0.0)
            step *= 2
        cum_l = cum[L - 1:L, :]
        g_in = jnp.exp(cum)
        g_inv = jnp.exp(-cum)
        g_tail = jnp.exp(cum_l - cum)
        r = r_ref[:, sl]
        km = km_ref[:, sl]
        v = v_ref[:, sl]
        kn = kn_ref[:, sl]
        kb = kb_ref[:, sl]
        at = -kn * jnp.exp(cum - lw)
        rt = r * g_in
        bt = (kb * g_inv).astype(BF16)
        kt = (km * g_inv).astype(BF16)
        bh = kb * g_tail
        kh = km * g_tail

        w_pair = jnp.zeros((L, LANES), F32)
        u_pair = jnp.zeros((L, LANES), F32)
        qe_pair = jnp.zeros((L, LANES), F32)
        y0_pair = jnp.zeros((L, LANES), F32)
        for hh in range(2):
            hm = (lane // HEAD_DIM) == hh
            at_h = jnp.where(hm, at, 0.0)
            rt_h = jnp.where(hm, rt, 0.0)
            v_h = jnp.where(hm, v, 0.0).astype(BF16)
            at_hb = at_h.astype(BF16)
            rt_hb = rt_h.astype(BF16)
            nmat = jnp.where(strict, _nt(at_hb, bt), 0.0)
            a_ak = jnp.where(strict, _nt(at_hb, kt), 0.0)
            a_rb = jnp.where(incl, _nt(rt_hb, bt), 0.0).astype(BF16)
            a_rk = jnp.where(incl, _nt(rt_hb, kt), 0.0).astype(BF16)
            x = jnp.concatenate([at_h, _mm(a_ak.astype(BF16), v_h)], axis=1)
            npow = nmat
            step = 1
            while step < L:
                nb = npow.astype(BF16)
                x = x + _mm(nb, x.astype(BF16))
                step *= 2
                if step < L:
                    npow = _mm(nb, nb)
            w_h = x[:, :LANES]
            u_h = x[:, LANES:]
            w_pair = w_pair + w_h
            u_pair = u_pair + u_h
            qe_pair = qe_pair + rt_h + _mm(a_rb, w_h.astype(BF16))
            y0_pair = y0_pair + _mm(a_rb, u_h.astype(BF16)) + _mm(a_rk, v_h)

        st = st_ref[p]
        y_ref[:, sl] = _mm(qe_pair.astype(BF16), st.astype(BF16)) + y0_pair
        bh_b = bh.astype(BF16)
        wu = jnp.concatenate([w_pair, u_pair], axis=1).astype(BF16)
        bwu = _tn(bh_b, wu)
        mt = jnp.where(eye, jnp.broadcast_to(jnp.exp(cum_l), (LANES, LANES)), 0.0)
        mt = mt + jnp.where(same_head, bwu[:, :LANES], 0.0)
        ct = jnp.where(same_head, bwu[:, LANES:] + _tn(kh.astype(BF16), v.astype(BF16)), 0.0)
        st_ref[p] = _mm(mt.astype(BF16), st.astype(BF16)) + ct


def rwkv_scan(r, lw, km, v, kn, kb, batch, seq):
    n = batch * seq
    chunk = min(SCAN_CHUNK, seq)
    nc = seq // chunk
    blk = pl.BlockSpec((chunk, RWKV_DIM), lambda b, c: (b * nc + c, 0))
    return pl.pallas_call(
        functools.partial(_scan_kernel, chunk=chunk),
        out_shape=jax.ShapeDtypeStruct((n, RWKV_DIM), F32),
        grid=(batch, nc),
        in_specs=[blk] * 6,
        out_specs=blk,
        scratch_shapes=[pltpu.VMEM((N_RWKV_HEADS // 2, LANES, LANES), F32)],
        compiler_params=_cparams(("parallel", "arbitrary")),
        name="rwkv_scan",
    )(r, lw, km, v, kn, kb)


def _odd_out_kernel(y_ref, r_ref, km_ref, v_ref, gg_ref, op_ref, rk_ref, lnw_ref, lnb_ref, ones_ref,
                    x_ref, g1_ref, wout_ref, ng_ref, sh2_ref, sc2_ref, rwt_ref, rb_ref,
                    x1_ref, h2_ref, gates_ref):
    ones = ones_ref[...]
    inv = 1.0 / HEAD_DIM
    y = y_ref[...]
    mean = _split_sum(y, ones) * inv
    yc = y - mean
    var = _split_sum(yc * yc, ones) * inv
    yn = yc * lax.rsqrt(var + LNX_EPS) * lnw_ref[...] + lnb_ref[...]
    bonus = _split_sum(r_ref[...] * km_ref[...] * rk_ref[...], ones) * v_ref[...]
    o_rwkv = (yn + bonus) * gg_ref[...]
    cat = jnp.concatenate([o_rwkv, op_ref[...]], axis=1).astype(BF16)
    _tail(cat, x_ref, g1_ref, wout_ref, ng_ref, sh2_ref, sc2_ref, rwt_ref, rb_ref,
          x1_ref, h2_ref, gates_ref)


def odd_out_proj(y, r, km, v, gg, opool, r_k, lnx_w, lnx_b, ones_bd, x2, g1, w_out, ng, sh2, sc2,
                 rwt, rb, seq):
    n, d = x2.shape
    tm = min(TOK_TILE, seq)
    tpb = seq // tm
    row = lambda i: (i, 0)
    c2 = lambda i: (0, 0)
    act = pl.BlockSpec((tm, RWKV_DIM), row)
    vec = pl.BlockSpec((1, RWKV_DIM), c2)
    tin, tout = _tail_specs(tm, d, tpb)
    return pl.pallas_call(
        _odd_out_kernel,
        out_shape=_tail_out_shapes(n, d),
        grid=(n // tm,),
        in_specs=[act] * 6 + [vec, vec, vec, pl.BlockSpec(ones_bd.shape, c2)] + tin,
        out_specs=tout,
        compiler_params=_cparams(("parallel",)),
        name="odd_out_proj",
    )(y, r, km, v, gg, opool, r_k, lnx_w, lnx_b, ones_bd, x2, g1, w_out, ng, sh2, sc2, rwt, rb)


def _rope_tables(seq):
    half = HEAD_DIM // 2
    inv = ROPE_THETA ** (-jnp.arange(half, dtype=F32) / half)
    ang = jnp.arange(seq, dtype=F32)[:, None] * inv[None, :]
    return jnp.tile(jnp.cos(ang), (1, LANES // half)), jnp.tile(jnp.sin(ang), (1, LANES // half))


def _even_w_pad(w_in):
    d = w_in.shape[0]
    q_kv = w_in[:, :NSA_DIM + 6 * KV_DIM]
    gl = w_in[:, NSA_DIM + 6 * KV_DIM:NSA_DIM + 6 * KV_DIM + 24]
    rest = w_in[:, NSA_DIM + 6 * KV_DIM + 24:]
    z = jnp.zeros((d, LANES - 12), w_in.dtype)
    return jnp.concatenate([q_kv, gl[:, :12], z, gl[:, 12:], z, rest], axis=1).astype(BF16)


def _compress_params(cmp_pos, cmp_w1, cmp_w2):
    eye = jnp.eye(N_KV_HEADS, dtype=F32)
    w1r = cmp_w1.reshape(2, 2, CMP_STRIDE, HEAD_DIM, CMP_HIDDEN)
    w1_ext = jnp.einsum('kpmdn,gh->kpmgdhn', w1r, eye).reshape(
        2, 2, CMP_STRIDE * KV_DIM, N_KV_HEADS * CMP_HIDDEN).astype(BF16)
    w2_ext = jnp.einsum('knd,gh->kgnhd', cmp_w2, eye).reshape(
        2, N_KV_HEADS * CMP_HIDDEN, KV_DIM).astype(BF16)
    pos = cmp_pos.reshape(2, 2, CMP_STRIDE, 1, HEAD_DIM)
    pos_ext = jnp.broadcast_to(pos, (2, 2, CMP_STRIDE, N_KV_HEADS, HEAD_DIM)).reshape(
        2, 2, 1, CMP_STRIDE * KV_DIM)
    return pos_ext, w1_ext, w2_ext


def _nsa_tables(seq):
    n_blk = seq // SEL_BLOCK
    n_cmp = (seq - CMP_BLOCK) // CMP_STRIDE + 1
    n_cmp_pad = seq // CMP_STRIDE
    r = SEL_BLOCK // CMP_STRIDE
    c = CMP_BLOCK // CMP_STRIDE
    msel = np.zeros((n_cmp_pad, LANES), np.float32)
    for j in range(n_blk):
        for m in range(r):
            for n in range(c):
                idx = r * j + m + n
                if idx < n_cmp:
                    msel[idx, j] += 1.0
    chunk = min(SEL_CHUNK, seq)
    kblk = np.arange(seq) // SEL_BLOCK
    eexp = (kblk[None, :] == np.arange(LANES)[:, None]).astype(np.float32)
    eexp = eexp.reshape(LANES, seq // chunk, chunk).transpose(1, 0, 2)
    selq = np.zeros((N_KV_HEADS, GQA, GQA * HEAD_DIM, LANES), np.float32)
    for h in range(N_KV_HEADS):
        for g in range(GQA):
            for dd in range(HEAD_DIM):
                selq[h, g, g * HEAD_DIM + dd, h * HEAD_DIM + dd] = 1.0
    return jnp.asarray(msel), jnp.asarray(eexp, dtype=BF16), jnp.asarray(selq, dtype=BF16)


def _odd_params(w_in, mu, w2, a2):
    d = w_in.shape[0]
    z64 = jnp.zeros((d, 64), w_in.dtype)
    w_pad = jnp.concatenate([w_in[:, :1536], w_in[:, 1536:1600], z64, w_in[:, 1600:1664], z64,
                             w_in[:, 1664:]], axis=1).astype(BF16)
    m64 = jnp.zeros((64,), mu.dtype)
    mu_pad = jnp.concatenate([mu[:1536], mu[1536:1600], m64, mu[1600:1664], m64, mu[1664:]])[None, :]
    zr = jnp.zeros((64, RWKV_DIM), w2.dtype)
    w2p = jnp.concatenate([w2, zr], axis=0).astype(BF16)
    a2p = jnp.concatenate([a2, zr], axis=0).astype(BF16)
    return w_pad, mu_pad, w2p, a2p


def _head_ones():
    idx = np.arange(RWKV_DIM) // HEAD_DIM
    return jnp.asarray((idx[:, None] == idx[None, :]).astype(np.float32), dtype=BF16)


def kernel(x, c, ada_w, ada_b, norm_mix, norm_ffn, even_w_in, even_cmp_pos, even_cmp_w1, even_cmp_w2,
           even_conv_w, even_w_out, odd_w_in, odd_mu, odd_w0, odd_w2, odd_a0, odd_a2, odd_g2, odd_k_k,
           odd_k_a, odd_r_k, odd_lnx_w, odd_lnx_b, odd_pool_w, odd_pool_scale, odd_w_out,
           router_w, router_b, moe_w_gate, moe_w_up, moe_w_down, final_norm):
    batch, seq, d = x.shape
    n = batch * seq
    depth = ada_w.shape[0]
    x2 = x.reshape(n, d)
    mod = ada_modulation(c, ada_w, ada_b)
    rwt = router_w.T
    rb = router_b.reshape(N_EXPERTS, 1)
    fnorm = final_norm.reshape(1, d)
    cos, sin = _rope_tables(seq)
    msel, eexp, selq = _nsa_tables(seq)
    ones_bd = _head_ones()

    for layer in range(depth):
        m = mod[layer].reshape(batch, 6, 1, d)
        sh1, sc1, g1, sh2, sc2, g2 = (m[:, k] for k in range(6))
        ng_mix = norm_mix[layer].reshape(1, d)
        ng_ffn = norm_ffn[layer].reshape(1, d)
        i = layer // 2
        if layer % 2 == 0:
            (qn, qr, kc, vc, ks, vs, kw, vw, gate, u, bg) = even_in_proj(
                x2, ng_mix, sh1, sc1, _even_w_pad(even_w_in[i]), cos, sin, seq)
            pos_ext, w1_ext, w2_ext = _compress_params(even_cmp_pos[i], even_cmp_w1[i], even_cmp_w2[i])
            kcmp, vcmp = compress_kv(kc, vc, pos_ext, w1_ext, w2_ext, batch, seq)
            o_nsa = nsa_attention(qn, qr, kcmp, vcmp, ks.reshape(batch, seq, KV_DIM),
                                  vs.reshape(batch, seq, KV_DIM), kw.reshape(batch, seq, KV_DIM),
                                  vw.reshape(batch, seq, KV_DIM), gate, selq, msel, eexp, batch, seq)
            x1, h2, gates = even_out_proj(o_nsa, u, bg, even_conv_w[i], x2, g1,
                                          even_w_out[i].astype(BF16), ng_ffn, sh2, sc2, rwt, rb, seq)
        else:
            w_pad, mu_pad, w2p, a2p = _odd_params(odd_w_in[i], odd_mu[i], odd_w2[i], odd_a2[i])
            vec = lambda a: a.reshape(1, RWKV_DIM)
            (r, lw, km, v, kn, kb, gg, opool) = odd_in_proj(
                x2, ng_mix, sh1, sc1, w_pad, mu_pad, vec(odd_w0[i]), w2p, vec(odd_a0[i]), a2p,
                odd_g2[i].astype(BF16), vec(odd_k_k[i]), vec(odd_k_a[i]), ones_bd,
                odd_pool_w[i].astype(BF16), vec(odd_pool_scale[i]), seq)
            y = rwkv_scan(r, lw, km, v, kn, kb, batch, seq)
            x1, h2, gates = odd_out_proj(y, r, km, v, gg, opool, vec(odd_r_k[i]), vec(odd_lnx_w[i]),
                                         vec(odd_lnx_b[i]), ones_bd, x2, g1,
                                         odd_w_out[i].astype(BF16), ng_ffn, sh2, sc2, rwt, rb, seq)
        x2 = moe_dense(h2, gates.reshape(N_EXPERTS, n, 1), moe_w_gate[layer], moe_w_up[layer],
                       moe_w_down[layer], x1, g2, fnorm, seq, final_norm=(layer == depth - 1))
    return x2.reshape(batch, seq, d)
```

```python
import functools

import jax
import jax.numpy as jnp
import numpy as np
from jax import lax
from jax.experimental import pallas as pl
from jax.experimental.pallas import tpu as pltpu

F32 = jnp.float32
BF16 = jnp.bfloat16
HIGHEST = lax.Precision.HIGHEST

D_MODEL = 1024
DEPTH = 2
HEAD_DIM = 64
ROPE_THETA = 10000.0
NORM_EPS = 1e-6
NEG_INF = -1e30
BIG = 1e9
NSA_DIM = 512
N_KV_HEADS = 2
GQA = 4
KV_DIM = 128
CMP_BLOCK = 32
CMP_STRIDE = 16
CMP_HIDDEN = 256
SEL_BLOCK = 64
N_SEL = 8
N_LOCAL = 2
WINDOW = 512
Q_BLOCK = 128
ATTN_SCALE = HEAD_DIM ** -0.5
CONV_DIM = 512
RWKV_DIM = 512
N_RWKV_HEADS = 8
LNX_EPS = 64e-5
POOL_WINDOWS = (2, 4, 8, 16)
POOL_GROUP = 128
N_EXPERTS = 16
N_EXPERT_GROUPS = 4
EXPERTS_PER_GROUP = 4
D_EXPERT = 512

LANES = 128
SUBLANES = 8
VMEM_LIMIT = 56 * 1024 * 1024

TOK_TILE = 512
MOE_TOK_TILE = 1024
SEL_CHUNK = 512
SCAN_CHUNK = 64
SCAN_CHUNKS_PER_STEP = 4
CONV_HALO = 8
POOL_HALO = 16

EVEN_PAD_COLS = 3072
ODD_PAD_COLS = 2432
ODD_RW_COLS = 1920


def _cparams(sem):
    return pltpu.CompilerParams(dimension_semantics=sem, vmem_limit_bytes=VMEM_LIMIT)


def _nt(a, b, precision=None):
    return lax.dot_general(a, b, (((1,), (1,)), ((), ())), preferred_element_type=F32,
                           precision=precision)


def _tn(a, b):
    return lax.dot_general(a, b, (((0,), (0,)), ((), ())), preferred_element_type=F32)


def _mm(a, b, precision=None):
    return jnp.dot(a, b, preferred_element_type=F32, precision=precision)


def _norm_mod(x, g, sh, sc):
    ms = jnp.mean(x * x, axis=-1, keepdims=True)
    return (x * lax.rsqrt(ms + NORM_EPS) * g) * (1.0 + sc) + sh


def _split_sum(x, ones_bf16):
    hi = x.astype(BF16)
    lo = (x - hi.astype(F32)).astype(BF16)
    return _mm(hi, ones_bf16) + _mm(lo, ones_bf16)


def _ada_kernel(c_ref, w_ref, b_ref, o_ref):
    c = c_ref[...]
    cond = c * jax.nn.sigmoid(c)
    o_ref[0] = _mm(cond, w_ref[0], precision=HIGHEST) + b_ref[0]


def ada_modulation(c, ada_w, ada_b):
    depth, d, cols = ada_w.shape
    b = c.shape[0]
    tn = 1536
    return pl.pallas_call(
        _ada_kernel,
        out_shape=jax.ShapeDtypeStruct((depth, b, cols), F32),
        grid=(depth, cols // tn),
        in_specs=[pl.BlockSpec((b, d), lambda l, j: (0, 0)),
                  pl.BlockSpec((1, d, tn), lambda l, j: (l, 0, j)),
                  pl.BlockSpec((1, 1, tn), lambda l, j: (l, 0, j))],
        out_specs=pl.BlockSpec((1, b, tn), lambda l, j: (l, 0, j)),
        compiler_params=_cparams(("parallel", "parallel")),
        name="ada_modulation",
    )(c, ada_w, ada_b.reshape(depth, 1, cols))


def _rope128(t, cos, sin, lane):
    rot = jnp.where((lane % HEAD_DIM) < HEAD_DIM // 2,
                    -pltpu.roll(t, LANES - HEAD_DIM // 2, 1), pltpu.roll(t, HEAD_DIM // 2, 1))
    return t * cos + rot * sin


def _even_in_kernel(x_ref, g_ref, sh_ref, sc_ref, w_ref, cos_ref, sin_ref,
                    qn_ref, qr_ref, kc_ref, vc_ref, ks_ref, vs_ref, kw_ref, vw_ref,
                    gate_ref, u_ref, bg_ref):
    h = _norm_mod(x_ref[...], g_ref[...], sh_ref[0], sc_ref[0])
    proj = _mm(h.astype(BF16), w_ref[...])
    cos = cos_ref[...]
    sin = sin_ref[...]
    lane = lax.broadcasted_iota(jnp.int32, (1, LANES), 1)
    for i in range(NSA_DIM // LANES):
        q = proj[:, i * LANES:(i + 1) * LANES]
        qn_ref[:, i * LANES:(i + 1) * LANES] = q.astype(BF16)
        qr_ref[:, i * LANES:(i + 1) * LANES] = _rope128(q, cos, sin, lane).astype(BF16)
    o = NSA_DIM
    kc_ref[...] = proj[:, o:o + 128]
    vc_ref[...] = proj[:, o + 128:o + 256]
    ks_ref[...] = _rope128(proj[:, o + 256:o + 384], cos, sin, lane).astype(BF16)
    vs_ref[...] = proj[:, o + 384:o + 512].astype(BF16)
    kw_ref[...] = _rope128(proj[:, o + 512:o + 640], cos, sin, lane).astype(BF16)
    vw_ref[...] = proj[:, o + 640:o + 768].astype(BF16)
    o += 768
    gate_ref[...] = jax.nn.sigmoid(proj[:, o:o + 256])
    o += 256
    xb = proj[:, o:o + 512]
    bg_ref[...] = proj[:, o + 512:o + 1024]
    u_ref[...] = proj[:, o + 1024:o + 1536] * xb


def even_in_proj(x2, g, sh, sc, w_pad, cos, sin, seq):
    n, d = x2.shape
    tm = min(TOK_TILE, seq)
    tpb = seq // tm
    row = lambda i: (i, 0)
    per_b = lambda i: (i // tpb, 0, 0)
    pos = lambda i: (i % tpb, 0)
    outs = [((n, 512), BF16), ((n, 512), BF16), ((n, 128), F32), ((n, 128), F32),
            ((n, 128), BF16), ((n, 128), BF16), ((n, 128), BF16), ((n, 128), BF16),
            ((n, 256), F32), ((n, 512), F32), ((n, 512), F32)]
    return pl.pallas_call(
        _even_in_kernel,
        out_shape=[jax.ShapeDtypeStruct(s, t) for s, t in outs],
        grid=(n // tm,),
        in_specs=[pl.BlockSpec((tm, d), row),
                  pl.BlockSpec((1, d), lambda i: (0, 0)),
                  pl.BlockSpec((1, 1, d), per_b),
                  pl.BlockSpec((1, 1, d), per_b),
                  pl.BlockSpec((d, EVEN_PAD_COLS), lambda i: (0, 0)),
                  pl.BlockSpec((tm, LANES), pos),
                  pl.BlockSpec((tm, LANES), pos)],
        out_specs=[pl.BlockSpec((tm, s[1]), row) for s, _ in outs],
        compiler_params=_cparams(("parallel",)),
        name="even_in_proj",
    )(x2, g, sh, sc, w_pad, cos, sin)


def _compress_kernel(k_ref, v_ref, pos_ref, w1_ref, w2_ref, ko_ref, vo_ref):
    for j, (src, dst) in enumerate(((k_ref, ko_ref), (v_ref, vo_ref))):
        xr = src[0]
        n_rows = xr.shape[0]
        a0 = _mm((xr + pos_ref[j, 0]).astype(BF16), w1_ref[j, 0])
        a1 = _mm((xr + pos_ref[j, 1]).astype(BF16), w1_ref[j, 1])
        hid = a0 + pltpu.roll(a1, n_rows - 1, 0)
        hid = jax.nn.gelu(hid)
        dst[0] = _mm(hid.astype(BF16), w2_ref[j]).astype(BF16)


def compress_kv(kc, vc, pos_ext, w1_ext, w2_ext, batch, seq):
    rows = seq // CMP_STRIDE
    width = CMP_STRIDE * KV_DIM
    kr = kc.reshape(batch, rows, width)
    vr = vc.reshape(batch, rows, width)
    blk = pl.BlockSpec((1, rows, width), lambda b: (b, 0, 0))
    oblk = pl.BlockSpec((1, rows, KV_DIM), lambda b: (b, 0, 0))
    return pl.pallas_call(
        _compress_kernel,
        out_shape=[jax.ShapeDtypeStruct((batch, rows, KV_DIM), BF16)] * 2,
        grid=(batch,),
        in_specs=[blk, blk,
                  pl.BlockSpec(pos_ext.shape, lambda b: (0, 0, 0, 0)),
                  pl.BlockSpec(w1_ext.shape, lambda b: (0, 0, 0, 0)),
                  pl.BlockSpec(w2_ext.shape, lambda b: (0, 0, 0))],
        out_specs=[oblk, oblk],
        compiler_params=_cparams(("parallel",)),
        name="compress_kv",
    )(kr, vr, pos_ext, w1_ext, w2_ext)


def _softmax_rows(sc, mask):
    s = jnp.where(mask, sc, NEG_INF)
    m = jnp.max(s, axis=1, keepdims=True)
    e = jnp.where(mask, jnp.exp(s - m), 0.0)
    l = jnp.sum(e, axis=1, keepdims=True)
    inv = jnp.where(l > 0.0, 1.0 / jnp.where(l > 0.0, l, 1.0), 0.0)
    return e * inv


def _nsa_kernel(qn_ref, qr_ref, kc_ref, vc_ref, ks_ref, vs_ref, kw_ref, vw_ref, gate_ref,
                selq_ref, msel_ref, eexp_ref, o_ref, *, seq, n_sel, sel_chunk, win_len):
    h = pl.program_id(1)
    qt = pl.program_id(2)
    t0 = qt * Q_BLOCK
    n_blk = seq // SEL_BLOCK
    n_cmp_pad = seq // CMP_STRIDE
    tpos = t0 + lax.broadcasted_iota(jnp.int32, (Q_BLOCK, 1), 0)
    lane = lax.broadcasted_iota(jnp.int32, (1, LANES), 1)
    head_lanes = (lane // HEAD_DIM) == h

    qn = qn_ref[...]
    qr = qr_ref[...]
    qn_g = [_mm(qn, selq_ref[0, g]).astype(BF16) for g in range(GQA)]
    qr_g = [_mm(qr, selq_ref[0, g]).astype(BF16) for g in range(GQA)]

    kc = kc_ref[0]
    vc = vc_ref[0]
    cpos = lax.broadcasted_iota(jnp.int32, (1, n_cmp_pad), 1) * CMP_STRIDE + (CMP_BLOCK - 1)
    cmask = cpos <= tpos
    imp = jnp.zeros((Q_BLOCK, n_cmp_pad), F32)
    o_cmp = []
    for g in range(GQA):
        p = _softmax_rows(_nt(qn_g[g], kc) * ATTN_SCALE, cmask)
        imp = imp + p
        o_cmp.append(_mm(p.astype(BF16), vc))

    pslc = _mm(imp, msel_ref[...], precision=HIGHEST)
    jblk = lax.broadcasted_iota(jnp.int32, (1, LANES), 1)
    cur = tpos // SEL_BLOCK
    valid = jblk * SEL_BLOCK <= tpos
    forced = (jblk == 0) | ((cur - jblk >= 0) & (cur - jblk < N_LOCAL))
    score = jnp.where(forced, BIG, jnp.where(valid, pslc, -BIG))
    rank = jnp.zeros((Q_BLOCK, LANES), jnp.int32)
    for jp in range(n_blk):
        col = score[:, jp:jp + 1]
        beats = (col > score) | ((col == score) & (jblk > jp))
        rank = rank + beats.astype(jnp.int32)
    sel = ((rank < n_sel) & (score > -0.5 * BIG)).astype(BF16)

    n_chunks = (t0 + Q_BLOCK - 1) // sel_chunk + 1

    def sel_body(c, carry):
        ms, ls, accs = carry
        start = pl.multiple_of(c * sel_chunk, sel_chunk)
        kblk = ks_ref[0, pl.ds(start, sel_chunk), :]
        vblk = vs_ref[0, pl.ds(start, sel_chunk), :]
        kpos = start + lax.broadcasted_iota(jnp.int32, (1, sel_chunk), 1)
        mask = (_mm(sel, eexp_ref[c]) > 0.5) & (kpos <= tpos)
        new_m, new_l, new_acc = [], [], []
        for g in range(GQA):
            s = jnp.where(mask, _nt(qr_g[g], kblk) * ATTN_SCALE, NEG_INF)
            m_new = jnp.maximum(ms[g], jnp.max(s, axis=1, keepdims=True))
            alpha = jnp.exp(ms[g] - m_new)
            p = jnp.where(mask, jnp.exp(s - m_new), 0.0)
            new_m.append(m_new)
            new_l.append(alpha * ls[g] + jnp.sum(p, axis=1, keepdims=True))
            new_acc.append(alpha * accs[g] + _mm(p.astype(BF16), vblk))
        return tuple(new_m), tuple(new_l), tuple(new_acc)

    init = (tuple(jnp.full((Q_BLOCK, 1), NEG_INF, F32) for _ in range(GQA)),
            tuple(jnp.zeros((Q_BLOCK, 1), F32) for _ in range(GQA)),
            tuple(jnp.zeros((Q_BLOCK, LANES), F32) for _ in range(GQA)))
    _, ls, accs = lax.fori_loop(0, n_chunks, sel_body, init)
    o_slc = []
    for g in range(GQA):
        inv = jnp.where(ls[g] > 0.0, 1.0 / jnp.where(ls[g] > 0.0, ls[g], 1.0), 0.0)
        o_slc.append(accs[g] * inv)

    ws = pl.multiple_of(jnp.maximum(qt - WINDOW // Q_BLOCK, 0) * Q_BLOCK, Q_BLOCK)
    kwb = kw_ref[0, pl.ds(ws, win_len), :]
    vwb = vw_ref[0, pl.ds(ws, win_len), :]
    diff = tpos - (ws + lax.broadcasted_iota(jnp.int32, (1, win_len), 1))
    wmask = (diff >= 0) & (diff < WINDOW)
    o_win = []
    for g in range(GQA):
        p = _softmax_rows(_nt(qr_g[g], kwb) * ATTN_SCALE, wmask)
        o_win.append(_mm(p.astype(BF16), vwb))

    gate = gate_ref[...]
    og = []
    for g in range(GQA):
        o = (gate[:, 3 * g:3 * g + 1] * o_cmp[g] + gate[:, 3 * g + 1:3 * g + 2] * o_slc[g]
             + gate[:, 3 * g + 2:3 * g + 3] * o_win[g])
        og.append(jnp.where(head_lanes, o, pltpu.roll(o, HEAD_DIM, 1)))
    low = lane < HEAD_DIM
    o_ref[:, 0:LANES] = jnp.where(low, og[0], og[1])
    o_ref[:, LANES:2 * LANES] = jnp.where(low, og[2], og[3])


def nsa_attention(qn, qr, kcmp, vcmp, ks, vs, kw, vw, gate, selq, msel, eexp, batch, seq):
    n = batch * seq
    nq = seq // Q_BLOCK
    sel_chunk = min(SEL_CHUNK, seq)
    win_len = min(WINDOW + Q_BLOCK, seq)
    n_sel = min(N_SEL, seq // SEL_BLOCK)
    qspec = pl.BlockSpec((Q_BLOCK, GQA * HEAD_DIM), lambda b, h, q: (b * nq + q, h))
    cspec = pl.BlockSpec((1, seq // CMP_STRIDE, KV_DIM), lambda b, h, q: (b, 0, 0))
    kspec = pl.BlockSpec((1, seq, KV_DIM), lambda b, h, q: (b, 0, 0))
    kern = functools.partial(_nsa_kernel, seq=seq, n_sel=n_sel, sel_chunk=sel_chunk, win_len=win_len)
    return pl.pallas_call(
        kern,
        out_shape=jax.ShapeDtypeStruct((n, NSA_DIM), F32),
        grid=(batch, N_KV_HEADS, nq),
        in_specs=[qspec, qspec, cspec, cspec, kspec, kspec, kspec, kspec,
                  pl.BlockSpec((Q_BLOCK, LANES), lambda b, h, q: (b * nq + q, h)),
                  pl.BlockSpec((1, GQA, GQA * HEAD_DIM, LANES), lambda b, h, q: (h, 0, 0, 0)),
                  pl.BlockSpec(msel.shape, lambda b, h, q: (0, 0)),
                  pl.BlockSpec(eexp.shape, lambda b, h, q: (0, 0, 0))],
        out_specs=qspec,
        compiler_params=_cparams(("parallel", "parallel", "arbitrary")),
        name="nsa_attention",
    )(qn, qr, kcmp, vcmp, ks, vs, kw, vw, gate, selq, msel, eexp)


def _route(h2, rwt_ref, rb_ref, gates_ref):
    logits = _nt(rwt_ref[...], h2, precision=HIGHEST)
    scores = jax.nn.sigmoid(logits)
    biased = scores + rb_ref[...]
    rows = [biased[e:e + 1, :] for e in range(N_EXPERTS)]
    srow = [scores[e:e + 1, :] for e in range(N_EXPERTS)]
    gscore = []
    for gi in range(N_EXPERT_GROUPS):
        r = rows[gi * EXPERTS_PER_GROUP:(gi + 1) * EXPERTS_PER_GROUP]
        best = None
        for a in range(EXPERTS_PER_GROUP):
            for b in range(a + 1, EXPERTS_PER_GROUP):
                pair = r[a] + r[b]
                best = pair if best is None else jnp.maximum(best, pair)
        gscore.append(best)
    top_val = gscore[0]
    top_grp = jnp.zeros_like(top_val, dtype=jnp.int32)
    for gi in range(1, N_EXPERT_GROUPS):
        upd = gscore[gi] > top_val
        top_grp = jnp.where(upd, gi, top_grp)
        top_val = jnp.where(upd, gscore[gi], top_val)
    masked = [jnp.where(top_grp == e // EXPERTS_PER_GROUP, rows[e], NEG_INF) for e in range(N_EXPERTS)]
    b1 = masked[0]
    i1 = jnp.zeros_like(top_grp)
    for e in range(1, N_EXPERTS):
        upd = masked[e] > b1
        i1 = jnp.where(upd, e, i1)
        b1 = jnp.where(upd, masked[e], b1)
    b2 = None
    i2 = None
    for e in range(N_EXPERTS):
        v = jnp.where(i1 == e, -jnp.inf, masked[e])
        if b2 is None:
            b2, i2 = v, jnp.zeros_like(top_grp)
        else:
            upd = v > b2
            i2 = jnp.where(upd, e, i2)
            b2 = jnp.where(upd, v, b2)
    s1 = jnp.zeros_like(top_val)
    s2 = jnp.zeros_like(top_val)
    for e in range(N_EXPERTS):
        s1 = s1 + jnp.where(i1 == e, srow[e], 0.0)
        s2 = s2 + jnp.where(i2 == e, srow[e], 0.0)
    tot = s1 + s2
    w1 = s1 / tot
    w2 = s2 / tot
    for e in range(N_EXPERTS):
        gates_ref[e:e + 1, :] = jnp.where(i1 == e, w1, 0.0) + jnp.where(i2 == e, w2, 0.0)


def _tail(cat_bf16, x_ref, g1_ref, wout_ref, ng_ref, sh2_ref, sc2_ref, rwt_ref, rb_ref,
          x1_ref, h2_ref, gates_ref):
    y = _mm(cat_bf16, wout_ref[...])
    x1 = x_ref[...] + g1_ref[0] * y
    x1_ref[...] = x1
    h2 = _norm_mod(x1, ng_ref[...], sh2_ref[0], sc2_ref[0])
    h2_ref[...] = h2.astype(BF16)
    _route(h2, rwt_ref, rb_ref, gates_ref)


def _even_out_kernel(o_ref, u_ref, uh_ref, bg_ref, cw_ref, x_ref, g1_ref, wout_ref, ng_ref, sh2_ref,
                     sc2_ref, rwt_ref, rb_ref, x1_ref, h2_ref, gates_ref, *, tiles_per_seq):
    first = (pl.program_id(0) % tiles_per_seq) == 0
    u = u_ref[...]
    tm = u.shape[0]
    halo = jnp.where(first, 0.0, uh_ref[...])
    ext = jnp.concatenate([halo, u], axis=0)
    u1 = pltpu.roll(ext, 1, 0)[CONV_HALO:]
    u2 = pltpu.roll(ext, 2, 0)[CONV_HALO:]
    cw = cw_ref[...]
    y_conv = bg_ref[...] * (cw[2:3] * u + cw[1:2] * u1 + cw[0:1] * u2)
    cat = jnp.concatenate([o_ref[...], y_conv], axis=1).astype(BF16)
    _tail(cat, x_ref, g1_ref, wout_ref, ng_ref, sh2_ref, sc2_ref, rwt_ref, rb_ref,
          x1_ref, h2_ref, gates_ref)


def _tail_specs(tm, d, tpb):
    row = lambda i: (i, 0)
    per_b = lambda i: (i // tpb, 0, 0)
    const2 = lambda i: (0, 0)
    ins = [pl.BlockSpec((tm, d), row),
           pl.BlockSpec((1, 1, d), per_b),
           pl.BlockSpec((d, d), const2),
           pl.BlockSpec((1, d), const2),
           pl.BlockSpec((1, 1, d), per_b),
           pl.BlockSpec((1, 1, d), per_b),
           pl.BlockSpec((N_EXPERTS, d), const2),
           pl.BlockSpec((N_EXPERTS, 1), const2)]
    outs = [pl.BlockSpec((tm, d), row), pl.BlockSpec((tm, d), row),
            pl.BlockSpec((N_EXPERTS, tm), lambda i: (0, i))]
    return ins, outs


def _tail_out_shapes(n, d):
    return [jax.ShapeDtypeStruct((n, d), F32), jax.ShapeDtypeStruct((n, d), BF16),
            jax.ShapeDtypeStruct((N_EXPERTS, n), F32)]


def even_out_proj(o_nsa, u, bg, conv_w, x2, g1, w_out, ng, sh2, sc2, rwt, rb, seq):
    n, d = x2.shape
    tm = min(TOK_TILE, seq)
    tpb = seq // tm
    row = lambda i: (i, 0)
    halo = lambda i: (jnp.maximum(i * (tm // CONV_HALO) - 1, 0), 0)
    tin, tout = _tail_specs(tm, d, tpb)
    return pl.pallas_call(
        functools.partial(_even_out_kernel, tiles_per_seq=tpb),
        out_shape=_tail_out_shapes(n, d),
        grid=(n // tm,),
        in_specs=[pl.BlockSpec((tm, NSA_DIM), row),
                  pl.BlockSpec((tm, CONV_DIM), row),
                  pl.BlockSpec((CONV_HALO, CONV_DIM), halo),
                  pl.BlockSpec((tm, CONV_DIM), row),
                  pl.BlockSpec(conv_w.shape, lambda i: (0, 0))] + tin,
        out_specs=tout,
        compiler_params=_cparams(("parallel",)),
        name="even_out_proj",
    )(o_nsa, u, u, bg, conv_w, x2, g1, w_out, ng, sh2, sc2, rwt, rb)


def _moe_kernel(h_ref, gate_ref, wg_ref, wu_ref, wd_ref, x_ref, g2_ref, fn_ref, o_ref, acc_ref,
                *, final_norm):
    e = pl.program_id(1)

    @pl.when(e == 0)
    def _():
        acc_ref[...] = jnp.zeros_like(acc_ref)

    h = h_ref[...]
    a = _mm(h, wg_ref[0, 0].astype(BF16))
    b = _mm(h, wu_ref[0, 0].astype(BF16))
    he = (a * jax.nn.sigmoid(a)) * b
    acc_ref[...] += gate_ref[0] * _mm(he.astype(BF16), wd_ref[0, 0].astype(BF16))

    @pl.when(e == N_EXPERTS - 1)
    def _():
        x = x_ref[...] + g2_ref[0] * acc_ref[...]
        if final_norm:
            ms = jnp.mean(x * x, axis=-1, keepdims=True)
            x = x * lax.rsqrt(ms + NORM_EPS) * fn_ref[...]
        o_ref[...] = x


def moe_dense(h2, gates3, w_gate, w_up, w_down, layer, x1, g2, fnorm, seq, final_norm):
    n, d = x1.shape
    tm = min(MOE_TOK_TILE, seq)
    tpb = seq // tm
    row = lambda i, e: (i, 0)
    return pl.pallas_call(
        functools.partial(_moe_kernel, final_norm=final_norm),
        out_shape=jax.ShapeDtypeStruct((n, d), F32),
        grid=(n // tm, N_EXPERTS),
        in_specs=[pl.BlockSpec((tm, d), row),
                  pl.BlockSpec((1, tm, 1), lambda i, e: (e, i, 0)),
                  pl.BlockSpec((1, 1, d, D_EXPERT), lambda i, e: (layer, e, 0, 0)),
                  pl.BlockSpec((1, 1, d, D_EXPERT), lambda i, e: (layer, e, 0, 0)),
                  pl.BlockSpec((1, 1, D_EXPERT, d), lambda i, e: (layer, e, 0, 0)),
                  pl.BlockSpec((tm, d), row),
                  pl.BlockSpec((1, 1, d), lambda i, e: (i // tpb, 0, 0)),
                  pl.BlockSpec((1, d), lambda i, e: (0, 0))],
        out_specs=pl.BlockSpec((tm, d), row),
        scratch_shapes=[pltpu.VMEM((tm, d), F32)],
        compiler_params=_cparams(("parallel", "arbitrary")),
        name="moe_dense",
    )(h2, gates3, w_gate, w_up, w_down, x1, g2, fnorm)


def _odd_in_kernel(x_ref, g_ref, sh_ref, sc_ref, w_ref, mu_ref, w0_ref, w2_ref, a0_ref, a2_ref,
                   g2_ref, kk_ref, ka_ref, ones_ref, pw_ref, ps_ref,
                   r_ref, lw_ref, km_ref, v_ref, kn_ref, kb_ref, gg_ref, op_ref,
                   rw_carry, u_carry, *, tiles_per_seq, tm):
    i = pl.program_id(0)
    first = (i % tiles_per_seq) == 0
    h = _norm_mod(x_ref[...], g_ref[...], sh_ref[0], sc_ref[0])
    proj = _mm(h.astype(BF16), w_ref[...])

    rw = proj[:, :ODD_RW_COLS]
    row0 = jnp.where(first, 0.0, rw_carry[0:1, :])
    ridx = lax.broadcasted_iota(jnp.int32, (tm, 1), 0)
    prev = jnp.where(ridx == 0, row0, pltpu.roll(rw, 1, 0))
    rw_carry[0:1, :] = rw[tm - 1:tm, :]
    rw = rw + (prev - rw) * mu_ref[...]

    r = rw[:, 0:512]
    k = rw[:, 512:1024]
    v = rw[:, 1024:1536]
    wl = rw[:, 1536:1664]
    al = rw[:, 1664:1792]
    gl = rw[:, 1792:1920]
    z = -(w0_ref[...] + _mm(jnp.tanh(wl).astype(BF16), w2_ref[...]))
    softplus = jnp.maximum(z, 0.0) + jnp.log1p(jnp.exp(-jnp.abs(z)))
    w_log = -softplus - 0.5
    a = jax.nn.sigmoid(a0_ref[...] + _mm(al.astype(BF16), a2_ref[...]))
    gg_ref[...] = _mm(jax.nn.sigmoid(gl).astype(BF16), g2_ref[...])
    kk0 = k * kk_ref[...]
    ss = _split_sum(kk0 * kk0, ones_ref[...])
    kk = kk0 / jnp.maximum(jnp.sqrt(ss), 1e-12)
    r_ref[...] = r
    lw_ref[...] = -jnp.exp(w_log)
    km_ref[...] = k * (1.0 + (a - 1.0) * ka_ref[...])
    v_ref[...] = v
    kn_ref[...] = kk
    kb_ref[...] = kk * a

    u = proj[:, ODD_RW_COLS:]
    halo = jnp.where(first, 0.0, u_carry[...])
    u_carry[...] = u[tm - POOL_HALO:, :]
    ext = jnp.concatenate([halo, u], axis=0)
    tseq = (i % tiles_per_seq) * tm + ridx
    for gi, win in enumerate(POOL_WINDOWS):
        xg = ext[:, gi * POOL_GROUP:(gi + 1) * POOL_GROUP]
        s = xg
        step = 1
        while step < win:
            s = s + pltpu.roll(s, step, 0)
            step *= 2
        cnt = jnp.minimum(tseq + 1, win).astype(F32)
        pooled = s[POOL_HALO:] / cnt - xg[POOL_HALO:]
        mixed = _mm(pooled.astype(BF16), pw_ref[gi])
        op_ref[:, gi * POOL_GROUP:(gi + 1) * POOL_GROUP] = (
            mixed * ps_ref[:, gi * POOL_GROUP:(gi + 1) * POOL_GROUP])


def odd_in_proj(x2, g, sh, sc, w_pad, mu_pad, w0, w2p, a0, a2p, g2, k_k, k_a, ones_bd, pool_w,
                pool_scale, seq):
    n, d = x2.shape
    tm = min(TOK_TILE, seq)
    tpb = seq // tm
    row = lambda i: (i, 0)
    per_b = lambda i: (i // tpb, 0, 0)
    c2 = lambda i: (0, 0)
    full2 = lambda a: pl.BlockSpec(a.shape, c2)
    return pl.pallas_call(
        functools.partial(_odd_in_kernel, tiles_per_seq=tpb, tm=tm),
        out_shape=[jax.ShapeDtypeStruct((n, RWKV_DIM), F32)] * 8,
        grid=(n // tm,),
        in_specs=[pl.BlockSpec((tm, d), row), pl.BlockSpec((1, d), c2),
                  pl.BlockSpec((1, 1, d), per_b), pl.BlockSpec((1, 1, d), per_b),
                  full2(w_pad), full2(mu_pad), full2(w0), full2(w2p), full2(a0), full2(a2p),
                  full2(g2), full2(k_k), full2(k_a), full2(ones_bd),
                  pl.BlockSpec(pool_w.shape, lambda i: (0, 0, 0)), full2(pool_scale)],
        out_specs=[pl.BlockSpec((tm, RWKV_DIM), row)] * 8,
        scratch_shapes=[pltpu.VMEM((SUBLANES, ODD_RW_COLS), F32),
                        pltpu.VMEM((POOL_HALO, RWKV_DIM), F32)],
        compiler_params=_cparams(("arbitrary",)),
        name="odd_in_proj",
    )(x2, g, sh, sc, w_pad, mu_pad, w0, w2p, a0, a2p, g2, k_k, k_a, ones_bd, pool_w, pool_scale)


def _bmm(a, b):
    return lax.dot_general(a, b, (((2,), (1,)), ((0,), (0,))), preferred_element_type=F32)


def _bnt(a, b):
    return lax.dot_general(a, b, (((2,), (2,)), ((0,), (0,))), preferred_element_type=F32)


def _btn(a, b):
    return lax.dot_general(a, b, (((1,), (1,)), ((0,), (0,))), preferred_element_type=F32)


def _scan_prep_kernel(r_ref, lw_ref, km_ref, v_ref, kn_ref, kb_ref, qe_ref, y0_ref, mt_ref, ct_ref,
                      *, chunk, cb):
    L = chunk
    rows = cb * L
    n_pairs = N_RWKV_HEADS // 2
    two = 2 * L
    rowt = lax.broadcasted_iota(jnp.int32, (rows, 1), 0) % L
    lane = lax.broadcasted_iota(jnp.int32, (1, 1, LANES), 2)
    low = lane < HEAD_DIM
    ri = lax.broadcasted_iota(jnp.int32, (two, two), 0)
    ci = lax.broadcasted_iota(jnp.int32, (two, two), 1)
    same_blk = (ri // L) == (ci // L)
    strict = same_blk & ((ci % L) < (ri % L))
    incl = same_blk & ((ci % L) <= (ri % L))
    li = lax.broadcasted_iota(jnp.int32, (LANES, LANES), 0)
    lj = lax.broadcasted_iota(jnp.int32, (LANES, LANES), 1)
    same_head = (li // HEAD_DIM) == (lj // HEAD_DIM)
    eye = li == lj

    lw = lw_ref[...]
    cum = lw
    step = 1
    while step < L:
        cum = cum + jnp.where(rowt >= step, pltpu.roll(cum, step, 0), 0.0)
        step *= 2

    def to3(x):
        x3 = x.reshape(cb, L, RWKV_DIM)
        return jnp.concatenate([x3[:, :, p * LANES:(p + 1) * LANES] for p in range(n_pairs)], axis=0)

    def stack2(x):
        return jnp.concatenate([jnp.where(low, x, 0.0), jnp.where(low, 0.0, x)], axis=1)

    def fold(x):
        return x[:, :L, :] + x[:, L:, :]

    cum3 = to3(cum)
    lw3 = to3(lw)
    cum_l = cum3[:, L - 1:L, :]
    g_inv = jnp.exp(-cum3)
    g_tail = jnp.exp(cum_l - cum3)
    kb = to3(kb_ref[...])
    km = to3(km_ref[...])
    v = to3(v_ref[...])
    at_s = stack2(-to3(kn_ref[...]) * jnp.exp(cum3 - lw3))
    rt_s = stack2(to3(r_ref[...]) * jnp.exp(cum3))
    v_s = stack2(v).astype(BF16)
    lhs = jnp.concatenate([at_s, rt_s], axis=1).astype(BF16)
    rhs = jnp.concatenate([stack2(kb * g_inv), stack2(km * g_inv)], axis=1).astype(BF16)
    prod = _bnt(lhs, rhs)
    nmat = jnp.where(strict, prod[:, :two, :two], 0.0)
    a_ak = jnp.where(strict, prod[:, :two, two:], 0.0).astype(BF16)
    a_rb = jnp.where(incl, prod[:, two:, :two], 0.0).astype(BF16)
    a_rk = jnp.where(incl, prod[:, two:, two:], 0.0).astype(BF16)

    x = jnp.concatenate([at_s, _bmm(a_ak, v_s)], axis=2)
    npow = nmat
    step = 1
    while step < L:
        nb = npow.astype(BF16)
        x = x + _bmm(nb, x.astype(BF16))
        step *= 2
        if step < L:
            npow = _bmm(nb, nb)
    qy = _bmm(a_rb, x.astype(BF16))
    qe = fold(rt_s + qy[:, :, :LANES])
    y0 = fold(qy[:, :, LANES:] + _bmm(a_rk, v_s))
    wu = fold(x).astype(BF16)
    bwu = _btn((kb * g_tail).astype(BF16), wu)
    kv = _btn((km * g_tail).astype(BF16), v.astype(BF16))
    g_l = jnp.broadcast_to(jnp.exp(cum_l), (n_pairs * cb, LANES, LANES))
    mt = jnp.where(eye, g_l, 0.0) + jnp.where(same_head, bwu[:, :, :LANES], 0.0)
    ct = jnp.where(same_head, bwu[:, :, LANES:] + kv, 0.0)
    for p in range(n_pairs):
        sl = slice(p * LANES, (p + 1) * LANES)
        qe_ref[:, sl] = qe[p * cb:(p + 1) * cb].reshape(rows, LANES)
        y0_ref[:, sl] = y0[p * cb:(p + 1) * cb].reshape(rows, LANES)
        mt_ref[:, p] = mt[p * cb:(p + 1) * cb].astype(BF16)
        ct_ref[:, p] = ct[p * cb:(p + 1) * cb]


def _scan_state_kernel(qe_ref, y0_ref, mt_ref, ct_ref, y_ref, st_ref, *, batch):
    @pl.when(pl.program_id(0) == 0)
    def _():
        st_ref[...] = jnp.zeros_like(st_ref)

    n_pairs = N_RWKV_HEADS // 2
    qe = qe_ref[...]
    qe3 = jnp.concatenate([qe[:, :, p * LANES:(p + 1) * LANES] for p in range(n_pairs)], axis=0)
    st = st_ref[...].astype(BF16)
    y = _bmm(qe3.astype(BF16), st)
    for p in range(n_pairs):
        sl = slice(p * LANES, (p + 1) * LANES)
        y_ref[:, :, sl] = y[p * batch:(p + 1) * batch] + y0_ref[:, :, sl]
    mt = jnp.concatenate([mt_ref[:, 0, p] for p in range(n_pairs)], axis=0)
    ct = jnp.concatenate([ct_ref[:, 0, p] for p in range(n_pairs)], axis=0)
    st_ref[...] = _bmm(mt, st) + ct


def rwkv_scan(r, lw, km, v, kn, kb, batch, seq):
    n = batch * seq
    chunk = min(SCAN_CHUNK, seq)
    nc = seq // chunk
    cb = min(SCAN_CHUNKS_PER_STEP, nc)
    n_pairs = N_RWKV_HEADS // 2
    blk = pl.BlockSpec((cb * chunk, RWKV_DIM), lambda i: (i, 0))
    mblk = pl.BlockSpec((cb, n_pairs, LANES, LANES), lambda i: (i, 0, 0, 0))
    qe, y0, mt, ct = pl.pallas_call(
        functools.partial(_scan_prep_kernel, chunk=chunk, cb=cb),
        out_shape=[jax.ShapeDtypeStruct((n, RWKV_DIM), F32), jax.ShapeDtypeStruct((n, RWKV_DIM), F32),
                   jax.ShapeDtypeStruct((n // chunk, n_pairs, LANES, LANES), BF16),
                   jax.ShapeDtypeStruct((n // chunk, n_pairs, LANES, LANES), F32)],
        grid=(n // (cb * chunk),),
        in_specs=[blk] * 6,
        out_specs=[blk, blk, mblk, mblk],
        compiler_params=_cparams(("parallel",)),
        name="rwkv_scan_prep",
    )(r, lw, km, v, kn, kb)
    sblk = pl.BlockSpec((batch, chunk, RWKV_DIM), lambda c: (0, c, 0))
    smblk = pl.BlockSpec((batch, 1, n_pairs, LANES, LANES), lambda c: (0, c, 0, 0, 0))
    y = pl.pallas_call(
        functools.partial(_scan_state_kernel, batch=batch),
        out_shape=jax.ShapeDtypeStruct((batch, seq, RWKV_DIM), F32),
        grid=(nc,),
        in_specs=[sblk, sblk, smblk, smblk],
        out_specs=sblk,
        scratch_shapes=[pltpu.VMEM((n_pairs * batch, LANES, LANES), F32)],
        compiler_params=_cparams(("arbitrary",)),
        name="rwkv_scan_state",
    )(qe.reshape(batch, seq, RWKV_DIM), y0.reshape(batch, seq, RWKV_DIM),
      mt.reshape(batch, nc, n_pairs, LANES, LANES), ct.reshape(batch, nc, n_pairs, LANES, LANES))
    return y.reshape(n, RWKV_DIM)


def _odd_out_kernel(y_ref, r_ref, km_ref, v_ref, gg_ref, op_ref, rk_ref, lnw_ref, lnb_ref, ones_ref,
                    x_ref, g1_ref, wout_ref, ng_ref, sh2_ref, sc2_ref, rwt_ref, rb_ref,
                    x1_ref, h2_ref, gates_ref):
    ones = ones_ref[...]
    inv = 1.0 / HEAD_DIM
    y = y_ref[...]
    mean = _split_sum(y, ones) * inv
    yc = y - mean
    var = _split_sum(yc * yc, ones) * inv
    yn = yc * lax.rsqrt(var + LNX_EPS) * lnw_ref[...] + lnb_ref[...]
    bonus = _split_sum(r_ref[...] * km_ref[...] * rk_ref[...], ones) * v_ref[...]
    o_rwkv = (yn + bonus) * gg_ref[...]
    cat = jnp.concatenate([o_rwkv, op_ref[...]], axis=1).astype(BF16)
    _tail(cat, x_ref, g1_ref, wout_ref, ng_ref, sh2_ref, sc2_ref, rwt_ref, rb_ref,
          x1_ref, h2_ref, gates_ref)


def odd_out_proj(y, r, km, v, gg, opool, r_k, lnx_w, lnx_b, ones_bd, x2, g1, w_out, ng, sh2, sc2,
                 rwt, rb, seq):
    n, d = x2.shape
    tm = min(TOK_TILE, seq)
    tpb = seq // tm
    row = lambda i: (i, 0)
    c2 = lambda i: (0, 0)
    act = pl.BlockSpec((tm, RWKV_DIM), row)
    vec = pl.BlockSpec((1, RWKV_DIM), c2)
    tin, tout = _tail_specs(tm, d, tpb)
    return pl.pallas_call(
        _odd_out_kernel,
        out_shape=_tail_out_shapes(n, d),
        grid=(n // tm,),
        in_specs=[act] * 6 + [vec, vec, vec, pl.BlockSpec(ones_bd.shape, c2)] + tin,
        out_specs=tout,
        compiler_params=_cparams(("parallel",)),
        name="odd_out_proj",
    )(y, r, km, v, gg, opool, r_k, lnx_w, lnx_b, ones_bd, x2, g1, w_out, ng, sh2, sc2, rwt, rb)


def _rope_tables(seq):
    half = HEAD_DIM // 2
    inv = ROPE_THETA ** (-jnp.arange(half, dtype=F32) / half)
    ang = jnp.arange(seq, dtype=F32)[:, None] * inv[None, :]
    return jnp.tile(jnp.cos(ang), (1, LANES // half)), jnp.tile(jnp.sin(ang), (1, LANES // half))


def _even_w_pad(w_in):
    d = w_in.shape[0]
    q_kv = w_in[:, :NSA_DIM + 6 * KV_DIM]
    gl = w_in[:, NSA_DIM + 6 * KV_DIM:NSA_DIM + 6 * KV_DIM + 24]
    rest = w_in[:, NSA_DIM + 6 * KV_DIM + 24:]
    z = jnp.zeros((d, LANES - 12), w_in.dtype)
    return jnp.concatenate([q_kv, gl[:, :12], z, gl[:, 12:], z, rest], axis=1).astype(BF16)


def _compress_params(cmp_pos, cmp_w1, cmp_w2):
    eye = jnp.eye(N_KV_HEADS, dtype=F32)
    w1r = cmp_w1.reshape(2, 2, CMP_STRIDE, HEAD_DIM, CMP_HIDDEN)
    w1_ext = jnp.einsum('kpmdn,gh->kpmgdhn', w1r, eye).reshape(
        2, 2, CMP_STRIDE * KV_DIM, N_KV_HEADS * CMP_HIDDEN).astype(BF16)
    w2_ext = jnp.einsum('knd,gh->kgnhd', cmp_w2, eye).reshape(
        2, N_KV_HEADS * CMP_HIDDEN, KV_DIM).astype(BF16)
    pos = cmp_pos.reshape(2, 2, CMP_STRIDE, 1, HEAD_DIM)
    pos_ext = jnp.broadcast_to(pos, (2, 2, CMP_STRIDE, N_KV_HEADS, HEAD_DIM)).reshape(
        2, 2, 1, CMP_STRIDE * KV_DIM)
    return pos_ext, w1_ext, w2_ext


def _nsa_tables(seq):
    n_blk = seq // SEL_BLOCK
    n_cmp = (seq - CMP_BLOCK) // CMP_STRIDE + 1
    n_cmp_pad = seq // CMP_STRIDE
    r = SEL_BLOCK // CMP_STRIDE
    c = CMP_BLOCK // CMP_STRIDE
    msel = np.zeros((n_cmp_pad, LANES), np.float32)
    for j in range(n_blk):
        for m in range(r):
            for n in range(c):
                idx = r * j + m + n
                if idx < n_cmp:
                    msel[idx, j] += 1.0
    chunk = min(SEL_CHUNK, seq)
    kblk = np.arange(seq) // SEL_BLOCK
    eexp = (kblk[None, :] == np.arange(LANES)[:, None]).astype(np.float32)
    eexp = eexp.reshape(LANES, seq // chunk, chunk).transpose(1, 0, 2)
    selq = np.zeros((N_KV_HEADS, GQA, GQA * HEAD_DIM, LANES), np.float32)
    for h in range(N_KV_HEADS):
        for g in range(GQA):
            for dd in range(HEAD_DIM):
                selq[h, g, g * HEAD_DIM + dd, h * HEAD_DIM + dd] = 1.0
    return jnp.asarray(msel), jnp.asarray(eexp, dtype=BF16), jnp.asarray(selq, dtype=BF16)


def _odd_params(w_in, mu, w2, a2):
    d = w_in.shape[0]
    z64 = jnp.zeros((d, 64), w_in.dtype)
    w_pad = jnp.concatenate([w_in[:, :1536], w_in[:, 1536:1600], z64, w_in[:, 1600:1664], z64,
                             w_in[:, 1664:]], axis=1).astype(BF16)
    m64 = jnp.zeros((64,), mu.dtype)
    mu_pad = jnp.concatenate([mu[:1536], mu[1536:1600], m64, mu[1600:1664], m64, mu[1664:]])[None, :]
    zr = jnp.zeros((64, RWKV_DIM), w2.dtype)
    w2p = jnp.concatenate([w2, zr], axis=0).astype(BF16)
    a2p = jnp.concatenate([a2, zr], axis=0).astype(BF16)
    return w_pad, mu_pad, w2p, a2p


def _head_ones():
    idx = np.arange(RWKV_DIM) // HEAD_DIM
    return jnp.asarray((idx[:, None] == idx[None, :]).astype(np.float32), dtype=BF16)


def kernel(x, c, ada_w, ada_b, norm_mix, norm_ffn, even_w_in, even_cmp_pos, even_cmp_w1, even_cmp_w2,
           even_conv_w, even_w_out, odd_w_in, odd_mu, odd_w0, odd_w2, odd_a0, odd_a2, odd_g2, odd_k_k,
           odd_k_a, odd_r_k, odd_lnx_w, odd_lnx_b, odd_pool_w, odd_pool_scale, odd_w_out,
           router_w, router_b, moe_w_gate, moe_w_up, moe_w_down, final_norm):
    batch, seq, d = x.shape
    n = batch * seq
    depth = ada_w.shape[0]
    x2 = x.reshape(n, d)
    mod = ada_modulation(c, ada_w, ada_b)
    rwt = router_w.T
    rb = router_b.reshape(N_EXPERTS, 1)
    fnorm = final_norm.reshape(1, d)
    cos, sin = _rope_tables(seq)
    msel, eexp, selq = _nsa_tables(seq)
    ones_bd = _head_ones()

    for layer in range(depth):
        m = mod[layer].reshape(batch, 6, 1, d)
        sh1, sc1, g1, sh2, sc2, g2 = (m[:, k] for k in range(6))
        ng_mix = norm_mix[layer].reshape(1, d)
        ng_ffn = norm_ffn[layer].reshape(1, d)
        i = layer // 2
        if layer % 2 == 0:
            (qn, qr, kc, vc, ks, vs, kw, vw, gate, u, bg) = even_in_proj(
                x2, ng_mix, sh1, sc1, _even_w_pad(even_w_in[i]), cos, sin, seq)
            pos_ext, w1_ext, w2_ext = _compress_params(even_cmp_pos[i], even_cmp_w1[i], even_cmp_w2[i])
            kcmp, vcmp = compress_kv(kc, vc, pos_ext, w1_ext, w2_ext, batch, seq)
            o_nsa = nsa_attention(qn, qr, kcmp, vcmp, ks.reshape(batch, seq, KV_DIM),
                                  vs.reshape(batch, seq, KV_DIM), kw.reshape(batch, seq, KV_DIM),
                                  vw.reshape(batch, seq, KV_DIM), gate, selq, msel, eexp, batch, seq)
            x1, h2, gates = even_out_proj(o_nsa, u, bg, even_conv_w[i], x2, g1,
                                          even_w_out[i].astype(BF16), ng_ffn, sh2, sc2, rwt, rb, seq)
        else:
            w_pad, mu_pad, w2p, a2p = _odd_params(odd_w_in[i], odd_mu[i], odd_w2[i], odd_a2[i])
            vec = lambda a: a.reshape(1, RWKV_DIM)
            (r, lw, km, v, kn, kb, gg, opool) = odd_in_proj(
                x2, ng_mix, sh1, sc1, w_pad, mu_pad, vec(odd_w0[i]), w2p, vec(odd_a0[i]), a2p,
                odd_g2[i].astype(BF16), vec(odd_k_k[i]), vec(odd_k_a[i]), ones_bd,
                odd_pool_w[i].astype(BF16), vec(odd_pool_scale[i]), seq)
            y = rwkv_scan(r, lw, km, v, kn, kb, batch, seq)
            x1, h2, gates = odd_out_proj(y, r, km, v, gg, opool, vec(odd_r_k[i]), vec(odd_lnx_w[i]),
                                         vec(odd_lnx_b[i]), ones_bd, x2, g1,
                                         odd_w_out[i].astype(BF16), ng_ffn, sh2, sc2, rwt, rb, seq)
        x2 = moe_dense(h2, gates.reshape(N_EXPERTS, n, 1), moe_w_gate, moe_w_up, moe_w_down, layer,
                       x1, g2, fnorm, seq, final_norm=(layer == depth - 1))
    return x2.reshape(batch, seq, d)
```

```python
import functools

import jax
import jax.numpy as jnp
import numpy as np
from jax import lax
from jax.experimental import pallas as pl
from jax.experimental.pallas import tpu as pltpu

F32 = jnp.float32
BF16 = jnp.bfloat16
HIGHEST = lax.Precision.HIGHEST

D_MODEL = 1024
DEPTH = 2
HEAD_DIM = 64
ROPE_THETA = 10000.0
NORM_EPS = 1e-6
NEG_INF = -1e30
BIG = 1e9
NSA_DIM = 512
N_KV_HEADS = 2
GQA = 4
KV_DIM = 128
CMP_BLOCK = 32
CMP_STRIDE = 16
CMP_HIDDEN = 256
SEL_BLOCK = 64
N_SEL = 8
N_LOCAL = 2
WINDOW = 512
Q_BLOCK = 128
ATTN_SCALE = HEAD_DIM ** -0.5
CONV_DIM = 512
RWKV_DIM = 512
N_RWKV_HEADS = 8
LNX_EPS = 64e-5
POOL_WINDOWS = (2, 4, 8, 16)
POOL_GROUP = 128
N_EXPERTS = 16
N_EXPERT_GROUPS = 4
EXPERTS_PER_GROUP = 4
D_EXPERT = 512

LANES = 128
SUBLANES = 8
VMEM_LIMIT = 56 * 1024 * 1024

TOK_TILE = 512
MOE_TOK_TILE = 1024
SEL_CHUNK = 512
SCAN_CHUNK = 64
SCAN_CHUNKS_PER_STEP = 4
CONV_HALO = 8
POOL_HALO = 16

EVEN_PAD_COLS = 3072
ODD_PAD_COLS = 2432
ODD_RW_COLS = 1920


def _cparams(sem):
    return pltpu.CompilerParams(dimension_semantics=sem, vmem_limit_bytes=VMEM_LIMIT)


def _nt(a, b, precision=None):
    return lax.dot_general(a, b, (((1,), (1,)), ((), ())), preferred_element_type=F32,
                           precision=precision)


def _tn(a, b):
    return lax.dot_general(a, b, (((0,), (0,)), ((), ())), preferred_element_type=F32)


def _mm(a, b, precision=None):
    return jnp.dot(a, b, preferred_element_type=F32, precision=precision)


def _norm_mod(x, g, sh, sc):
    ms = jnp.mean(x * x, axis=-1, keepdims=True)
    return (x * lax.rsqrt(ms + NORM_EPS) * g) * (1.0 + sc) + sh


def _split_sum(x, ones_bf16):
    hi = x.astype(BF16)
    lo = (x - hi.astype(F32)).astype(BF16)
    return _mm(hi, ones_bf16) + _mm(lo, ones_bf16)


def _ada_kernel(c_ref, w_ref, b_ref, o_ref):
    c = c_ref[...]
    cond = c * jax.nn.sigmoid(c)
    o_ref[0] = _mm(cond, w_ref[0], precision=HIGHEST) + b_ref[0]


def ada_modulation(c, ada_w, ada_b):
    depth, d, cols = ada_w.shape
    b = c.shape[0]
    tn = 1536
    return pl.pallas_call(
        _ada_kernel,
        out_shape=jax.ShapeDtypeStruct((depth, b, cols), F32),
        grid=(depth, cols // tn),
        in_specs=[pl.BlockSpec((b, d), lambda l, j: (0, 0)),
                  pl.BlockSpec((1, d, tn), lambda l, j: (l, 0, j)),
                  pl.BlockSpec((1, 1, tn), lambda l, j: (l, 0, j))],
        out_specs=pl.BlockSpec((1, b, tn), lambda l, j: (l, 0, j)),
        compiler_params=_cparams(("parallel", "parallel")),
        name="ada_modulation",
    )(c, ada_w, ada_b.reshape(depth, 1, cols))


def _rope128(t, cos, sin, lane):
    rot = jnp.where((lane % HEAD_DIM) < HEAD_DIM // 2,
                    -pltpu.roll(t, LANES - HEAD_DIM // 2, 1), pltpu.roll(t, HEAD_DIM // 2, 1))
    return t * cos + rot * sin


def _even_in_kernel(x_ref, g_ref, sh_ref, sc_ref, w_ref, cos_ref, sin_ref,
                    qn_ref, qr_ref, kc_ref, vc_ref, ks_ref, vs_ref, kw_ref, vw_ref,
                    gate_ref, u_ref, bg_ref):
    h = _norm_mod(x_ref[...], g_ref[...], sh_ref[0], sc_ref[0])
    proj = _mm(h.astype(BF16), w_ref[...])
    cos = cos_ref[...]
    sin = sin_ref[...]
    lane = lax.broadcasted_iota(jnp.int32, (1, LANES), 1)
    for i in range(NSA_DIM // LANES):
        q = proj[:, i * LANES:(i + 1) * LANES] * ATTN_SCALE
        qn_ref[:, i * LANES:(i + 1) * LANES] = q.astype(BF16)
        qr_ref[:, i * LANES:(i + 1) * LANES] = _rope128(q, cos, sin, lane).astype(BF16)
    o = NSA_DIM
    kc_ref[...] = proj[:, o:o + 128]
    vc_ref[...] = proj[:, o + 128:o + 256]
    ks_ref[...] = _rope128(proj[:, o + 256:o + 384], cos, sin, lane).astype(BF16)
    vs_ref[...] = proj[:, o + 384:o + 512].astype(BF16)
    kw_ref[...] = _rope128(proj[:, o + 512:o + 640], cos, sin, lane).astype(BF16)
    vw_ref[...] = proj[:, o + 640:o + 768].astype(BF16)
    o += 768
    gate_ref[...] = jax.nn.sigmoid(proj[:, o:o + 256])
    o += 256
    xb = proj[:, o:o + 512]
    bg_ref[...] = proj[:, o + 512:o + 1024]
    u_ref[...] = proj[:, o + 1024:o + 1536] * xb


def even_in_proj(x2, g, sh, sc, w_pad, cos, sin, seq):
    n, d = x2.shape
    tm = min(TOK_TILE, seq)
    tpb = seq // tm
    row = lambda i: (i, 0)
    per_b = lambda i: (i // tpb, 0, 0)
    pos = lambda i: (i % tpb, 0)
    outs = [((n, 512), BF16), ((n, 512), BF16), ((n, 128), F32), ((n, 128), F32),
            ((n, 128), BF16), ((n, 128), BF16), ((n, 128), BF16), ((n, 128), BF16),
            ((n, 256), F32), ((n, 512), F32), ((n, 512), F32)]
    return pl.pallas_call(
        _even_in_kernel,
        out_shape=[jax.ShapeDtypeStruct(s, t) for s, t in outs],
        grid=(n // tm,),
        in_specs=[pl.BlockSpec((tm, d), row),
                  pl.BlockSpec((1, d), lambda i: (0, 0)),
                  pl.BlockSpec((1, 1, d), per_b),
                  pl.BlockSpec((1, 1, d), per_b),
                  pl.BlockSpec((d, EVEN_PAD_COLS), lambda i: (0, 0)),
                  pl.BlockSpec((tm, LANES), pos),
                  pl.BlockSpec((tm, LANES), pos)],
        out_specs=[pl.BlockSpec((tm, s[1]), row) for s, _ in outs],
        compiler_params=_cparams(("parallel",)),
        name="even_in_proj",
    )(x2, g, sh, sc, w_pad, cos, sin)


def _compress_kernel(k_ref, v_ref, pos_ref, w1_ref, w2_ref, ko_ref, vo_ref):
    for j, (src, dst) in enumerate(((k_ref, ko_ref), (v_ref, vo_ref))):
        xr = src[0]
        n_rows = xr.shape[0]
        a0 = _mm((xr + pos_ref[j, 0]).astype(BF16), w1_ref[j, 0])
        a1 = _mm((xr + pos_ref[j, 1]).astype(BF16), w1_ref[j, 1])
        hid = a0 + pltpu.roll(a1, n_rows - 1, 0)
        hid = jax.nn.gelu(hid)
        dst[0] = _mm(hid.astype(BF16), w2_ref[j]).astype(BF16)


def compress_kv(kc, vc, pos_ext, w1_ext, w2_ext, batch, seq):
    rows = seq // CMP_STRIDE
    width = CMP_STRIDE * KV_DIM
    kr = kc.reshape(batch, rows, width)
    vr = vc.reshape(batch, rows, width)
    blk = pl.BlockSpec((1, rows, width), lambda b: (b, 0, 0))
    oblk = pl.BlockSpec((1, rows, KV_DIM), lambda b: (b, 0, 0))
    return pl.pallas_call(
        _compress_kernel,
        out_shape=[jax.ShapeDtypeStruct((batch, rows, KV_DIM), BF16)] * 2,
        grid=(batch,),
        in_specs=[blk, blk,
                  pl.BlockSpec(pos_ext.shape, lambda b: (0, 0, 0, 0)),
                  pl.BlockSpec(w1_ext.shape, lambda b: (0, 0, 0, 0)),
                  pl.BlockSpec(w2_ext.shape, lambda b: (0, 0, 0))],
        out_specs=[oblk, oblk],
        compiler_params=_cparams(("parallel",)),
        name="compress_kv",
    )(kr, vr, pos_ext, w1_ext, w2_ext)


def _safe_inv(l):
    return jnp.where(l > 0.0, 1.0 / jnp.where(l > 0.0, l, 1.0), 0.0)


def _nsa_kernel(qn_ref, qr_ref, kc_ref, vc_ref, ks_ref, vs_ref, kw_ref, vw_ref, gate_ref,
                selq_ref, mselt_ref, eexp_ref, o_ref, *, seq, n_sel, sel_chunk, win_len):
    h = pl.program_id(1)
    qt = pl.program_id(2)
    t0 = qt * Q_BLOCK
    n_blk = seq // SEL_BLOCK
    n_cmp_pad = seq // CMP_STRIDE
    rows = GQA * Q_BLOCK
    tpos = t0 + lax.broadcasted_iota(jnp.int32, (1, Q_BLOCK, 1), 1)
    lane = lax.broadcasted_iota(jnp.int32, (1, LANES), 1)
    head_lanes = (lane // HEAD_DIM) == h

    qn = qn_ref[...]
    qr = qr_ref[...]
    qn4 = jnp.concatenate([_mm(qn, selq_ref[0, g]) for g in range(GQA)], axis=0).astype(BF16)
    qr4 = jnp.concatenate([_mm(qr, selq_ref[0, g]) for g in range(GQA)], axis=0).astype(BF16)

    kc = kc_ref[0]
    vc = vc_ref[0]
    cpos = lax.broadcasted_iota(jnp.int32, (1, 1, n_cmp_pad), 2) * CMP_STRIDE + (CMP_BLOCK - 1)
    cmask = cpos <= tpos
    s = jnp.where(cmask, _nt(qn4, kc).reshape(GQA, Q_BLOCK, n_cmp_pad), NEG_INF)
    e = jnp.where(cmask, jnp.exp(s - jnp.max(s, axis=2, keepdims=True)), 0.0)
    p = e * _safe_inv(jnp.sum(e, axis=2, keepdims=True))
    imp = jnp.sum(p, axis=0)
    o_cmp = _mm(p.reshape(rows, n_cmp_pad).astype(BF16), vc)

    pslc = _nt(mselt_ref[...], imp, precision=HIGHEST)[:n_blk]
    tq = t0 + lax.broadcasted_iota(jnp.int32, (1, Q_BLOCK), 1)
    jblk = lax.broadcasted_iota(jnp.int32, (n_blk, 1), 0)
    cur = tq // SEL_BLOCK
    valid = jblk * SEL_BLOCK <= tq
    forced = (jblk == 0) | ((cur - jblk >= 0) & (cur - jblk < N_LOCAL))
    score = jnp.where(forced, BIG, jnp.where(valid, pslc, -BIG))
    rank = jnp.zeros((n_blk, Q_BLOCK), jnp.int32)
    for jp in range(n_blk):
        row = score[jp:jp + 1, :]
        beats = (row > score) | ((row == score) & (jblk > jp))
        rank = rank + beats.astype(jnp.int32)
    sel = jnp.where((rank < n_sel) & (score > -0.5 * BIG), 1.0, 0.0)
    sel_t = jnp.concatenate([sel, jnp.zeros((LANES - n_blk, Q_BLOCK), F32)], axis=0).astype(BF16)

    n_chunks = (t0 + Q_BLOCK - 1) // sel_chunk + 1

    def sel_body(c, carry):
        m, l, acc = carry
        start = pl.multiple_of(c * sel_chunk, sel_chunk)
        kblk = ks_ref[0, pl.ds(start, sel_chunk), :]
        vblk = vs_ref[0, pl.ds(start, sel_chunk), :]
        kpos = start + lax.broadcasted_iota(jnp.int32, (1, 1, sel_chunk), 2)
        mask = (_tn(sel_t, eexp_ref[c]) > 0.5)[None] & (kpos <= tpos)
        s = jnp.where(mask, _nt(qr4, kblk).reshape(GQA, Q_BLOCK, sel_chunk), NEG_INF)
        m_new = jnp.maximum(m, jnp.max(s, axis=2, keepdims=True))
        alpha = jnp.exp(m - m_new)
        p = jnp.exp(s - m_new)
        l_new = alpha * l + jnp.sum(p, axis=2, keepdims=True)
        pv = _mm(p.reshape(rows, sel_chunk).astype(BF16), vblk).reshape(GQA, Q_BLOCK, LANES)
        return m_new, l_new, alpha * acc + pv

    init = (jnp.full((GQA, Q_BLOCK, 1), NEG_INF, F32), jnp.zeros((GQA, Q_BLOCK, 1), F32),
            jnp.zeros((GQA, Q_BLOCK, LANES), F32))
    _, l, acc = lax.fori_loop(0, n_chunks, sel_body, init)
    o_slc = acc * _safe_inv(l)

    ws = pl.multiple_of(jnp.maximum(qt - WINDOW // Q_BLOCK, 0) * Q_BLOCK, Q_BLOCK)
    kwb = kw_ref[0, pl.ds(ws, win_len), :]
    vwb = vw_ref[0, pl.ds(ws, win_len), :]
    diff = tpos - (ws + lax.broadcasted_iota(jnp.int32, (1, 1, win_len), 2))
    wmask = (diff >= 0) & (diff < WINDOW)
    s = jnp.where(wmask, _nt(qr4, kwb).reshape(GQA, Q_BLOCK, win_len), NEG_INF)
    e = jnp.exp(s - jnp.max(s, axis=2, keepdims=True))
    o_win = (_mm(e.reshape(rows, win_len).astype(BF16), vwb).reshape(GQA, Q_BLOCK, LANES)
             * _safe_inv(jnp.sum(e, axis=2, keepdims=True)))

    gate = gate_ref[...]
    o_cmp = o_cmp.reshape(GQA, Q_BLOCK, LANES)
    og = []
    for g in range(GQA):
        o = (gate[:, 3 * g:3 * g + 1] * o_cmp[g] + gate[:, 3 * g + 1:3 * g + 2] * o_slc[g]
             + gate[:, 3 * g + 2:3 * g + 3] * o_win[g])
        og.append(jnp.where(head_lanes, o, pltpu.roll(o, HEAD_DIM, 1)))
    low = lane < HEAD_DIM
    o_ref[:, 0:LANES] = jnp.where(low, og[0], og[1])
    o_ref[:, LANES:2 * LANES] = jnp.where(low, og[2], og[3])


def nsa_attention(qn, qr, kcmp, vcmp, ks, vs, kw, vw, gate, selq, msel, eexp, batch, seq):
    n = batch * seq
    nq = seq // Q_BLOCK
    sel_chunk = min(SEL_CHUNK, seq)
    win_len = min(WINDOW + Q_BLOCK, seq)
    n_sel = min(N_SEL, seq // SEL_BLOCK)
    qspec = pl.BlockSpec((Q_BLOCK, GQA * HEAD_DIM), lambda b, h, q: (b * nq + q, h))
    cspec = pl.BlockSpec((1, seq // CMP_STRIDE, KV_DIM), lambda b, h, q: (b, 0, 0))
    kspec = pl.BlockSpec((1, seq, KV_DIM), lambda b, h, q: (b, 0, 0))
    kern = functools.partial(_nsa_kernel, seq=seq, n_sel=n_sel, sel_chunk=sel_chunk, win_len=win_len)
    return pl.pallas_call(
        kern,
        out_shape=jax.ShapeDtypeStruct((n, NSA_DIM), F32),
        grid=(batch, N_KV_HEADS, nq),
        in_specs=[qspec, qspec, cspec, cspec, kspec, kspec, kspec, kspec,
                  pl.BlockSpec((Q_BLOCK, LANES), lambda b, h, q: (b * nq + q, h)),
                  pl.BlockSpec((1, GQA, GQA * HEAD_DIM, LANES), lambda b, h, q: (h, 0, 0, 0)),
                  pl.BlockSpec(msel.shape, lambda b, h, q: (0, 0)),
                  pl.BlockSpec(eexp.shape, lambda b, h, q: (0, 0, 0))],
        out_specs=qspec,
        compiler_params=_cparams(("parallel", "parallel", "arbitrary")),
        name="nsa_attention",
    )(qn, qr, kcmp, vcmp, ks, vs, kw, vw, gate, selq, msel, eexp)


def _route(h2, rwt_ref, rb_ref, gates_ref):
    logits = _nt(rwt_ref[...], h2, precision=HIGHEST)
    scores = jax.nn.sigmoid(logits)
    biased = scores + rb_ref[...]
    rows = [biased[e:e + 1, :] for e in range(N_EXPERTS)]
    srow = [scores[e:e + 1, :] for e in range(N_EXPERTS)]
    gscore = []
    for gi in range(N_EXPERT_GROUPS):
        r = rows[gi * EXPERTS_PER_GROUP:(gi + 1) * EXPERTS_PER_GROUP]
        best = None
        for a in range(EXPERTS_PER_GROUP):
            for b in range(a + 1, EXPERTS_PER_GROUP):
                pair = r[a] + r[b]
                best = pair if best is None else jnp.maximum(best, pair)
        gscore.append(best)
    top_val = gscore[0]
    top_grp = jnp.zeros_like(top_val, dtype=jnp.int32)
    for gi in range(1, N_EXPERT_GROUPS):
        upd = gscore[gi] > top_val
        top_grp = jnp.where(upd, gi, top_grp)
        top_val = jnp.where(upd, gscore[gi], top_val)
    masked = [jnp.where(top_grp == e // EXPERTS_PER_GROUP, rows[e], NEG_INF) for e in range(N_EXPERTS)]
    b1 = masked[0]
    i1 = jnp.zeros_like(top_grp)
    for e in range(1, N_EXPERTS):
        upd = masked[e] > b1
        i1 = jnp.where(upd, e, i1)
        b1 = jnp.where(upd, masked[e], b1)
    b2 = None
    i2 = None
    for e in range(N_EXPERTS):
        v = jnp.where(i1 == e, -jnp.inf, masked[e])
        if b2 is None:
            b2, i2 = v, jnp.zeros_like(top_grp)
        else:
            upd = v > b2
            i2 = jnp.where(upd, e, i2)
            b2 = jnp.where(upd, v, b2)
    s1 = jnp.zeros_like(top_val)
    s2 = jnp.zeros_like(top_val)
    for e in range(N_EXPERTS):
        s1 = s1 + jnp.where(i1 == e, srow[e], 0.0)
        s2 = s2 + jnp.where(i2 == e, srow[e], 0.0)
    tot = s1 + s2
    w1 = s1 / tot
    w2 = s2 / tot
    for e in range(N_EXPERTS):
        gates_ref[e:e + 1, :] = jnp.where(i1 == e, w1, 0.0) + jnp.where(i2 == e, w2, 0.0)


def _tail(cat_bf16, x_ref, g1_ref, wout_ref, ng_ref, sh2_ref, sc2_ref, rwt_ref, rb_ref,
          x1_ref, h2_ref, gates_ref):
    y = _mm(cat_bf16, wout_ref[...])
    x1 = x_ref[...] + g1_ref[0] * y
    x1_ref[...] = x1
    h2 = _norm_mod(x1, ng_ref[...], sh2_ref[0], sc2_ref[0])
    h2_ref[...] = h2.astype(BF16)
    _route(h2, rwt_ref, rb_ref, gates_ref)


def _even_out_kernel(o_ref, u_ref, uh_ref, bg_ref, cw_ref, x_ref, g1_ref, wout_ref, ng_ref, sh2_ref,
                     sc2_ref, rwt_ref, rb_ref, x1_ref, h2_ref, gates_ref, *, tiles_per_seq):
    first = (pl.program_id(0) % tiles_per_seq) == 0
    u = u_ref[...]
    tm = u.shape[0]
    halo = jnp.where(first, 0.0, uh_ref[...])
    ext = jnp.concatenate([halo, u], axis=0)
    u1 = pltpu.roll(ext, 1, 0)[CONV_HALO:]
    u2 = pltpu.roll(ext, 2, 0)[CONV_HALO:]
    cw = cw_ref[...]
    y_conv = bg_ref[...] * (cw[2:3] * u + cw[1:2] * u1 + cw[0:1] * u2)
    cat = jnp.concatenate([o_ref[...], y_conv], axis=1).astype(BF16)
    _tail(cat, x_ref, g1_ref, wout_ref, ng_ref, sh2_ref, sc2_ref, rwt_ref, rb_ref,
          x1_ref, h2_ref, gates_ref)


def _tail_specs(tm, d, tpb):
    row = lambda i: (i, 0)
    per_b = lambda i: (i // tpb, 0, 0)
    const2 = lambda i: (0, 0)
    ins = [pl.BlockSpec((tm, d), row),
           pl.BlockSpec((1, 1, d), per_b),
           pl.BlockSpec((d, d), const2),
           pl.BlockSpec((1, d), const2),
           pl.BlockSpec((1, 1, d), per_b),
           pl.BlockSpec((1, 1, d), per_b),
           pl.BlockSpec((N_EXPERTS, d), const2),
           pl.BlockSpec((N_EXPERTS, 1), const2)]
    outs = [pl.BlockSpec((tm, d), row), pl.BlockSpec((tm, d), row),
            pl.BlockSpec((N_EXPERTS, tm), lambda i: (0, i))]
    return ins, outs


def _tail_out_shapes(n, d):
    return [jax.ShapeDtypeStruct((n, d), F32), jax.ShapeDtypeStruct((n, d), BF16),
            jax.ShapeDtypeStruct((N_EXPERTS, n), F32)]


def even_out_proj(o_nsa, u, bg, conv_w, x2, g1, w_out, ng, sh2, sc2, rwt, rb, seq):
    n, d = x2.shape
    tm = min(TOK_TILE, seq)
    tpb = seq // tm
    row = lambda i: (i, 0)
    halo = lambda i: (jnp.maximum(i * (tm // CONV_HALO) - 1, 0), 0)
    tin, tout = _tail_specs(tm, d, tpb)
    return pl.pallas_call(
        functools.partial(_even_out_kernel, tiles_per_seq=tpb),
        out_shape=_tail_out_shapes(n, d),
        grid=(n // tm,),
        in_specs=[pl.BlockSpec((tm, NSA_DIM), row),
                  pl.BlockSpec((tm, CONV_DIM), row),
                  pl.BlockSpec((CONV_HALO, CONV_DIM), halo),
                  pl.BlockSpec((tm, CONV_DIM), row),
                  pl.BlockSpec(conv_w.shape, lambda i: (0, 0))] + tin,
        out_specs=tout,
        compiler_params=_cparams(("parallel",)),
        name="even_out_proj",
    )(o_nsa, u, u, bg, conv_w, x2, g1, w_out, ng, sh2, sc2, rwt, rb)


def _moe_kernel(h_ref, gate_ref, wg_ref, wu_ref, wd_ref, x_ref, g2_ref, fn_ref, o_ref, acc_ref,
                *, final_norm):
    e = pl.program_id(1)

    @pl.when(e == 0)
    def _():
        acc_ref[...] = jnp.zeros_like(acc_ref)

    h = h_ref[...]
    a = _mm(h, wg_ref[0, 0].astype(BF16))
    b = _mm(h, wu_ref[0, 0].astype(BF16))
    he = (a * jax.nn.sigmoid(a)) * b
    acc_ref[...] += gate_ref[0] * _mm(he.astype(BF16), wd_ref[0, 0].astype(BF16))

    @pl.when(e == N_EXPERTS - 1)
    def _():
        x = x_ref[...] + g2_ref[0] * acc_ref[...]
        if final_norm:
            ms = jnp.mean(x * x, axis=-1, keepdims=True)
            x = x * lax.rsqrt(ms + NORM_EPS) * fn_ref[...]
        o_ref[...] = x


def moe_dense(h2, gates3, w_gate, w_up, w_down, layer, x1, g2, fnorm, seq, final_norm):
    n, d = x1.shape
    tm = min(MOE_TOK_TILE, seq)
    tpb = seq // tm
    row = lambda i, e: (i, 0)
    return pl.pallas_call(
        functools.partial(_moe_kernel, final_norm=final_norm),
        out_shape=jax.ShapeDtypeStruct((n, d), F32),
        grid=(n // tm, N_EXPERTS),
        in_specs=[pl.BlockSpec((tm, d), row),
                  pl.BlockSpec((1, tm, 1), lambda i, e: (e, i, 0)),
                  pl.BlockSpec((1, 1, d, D_EXPERT), lambda i, e: (layer, e, 0, 0)),
                  pl.BlockSpec((1, 1, d, D_EXPERT), lambda i, e: (layer, e, 0, 0)),
                  pl.BlockSpec((1, 1, D_EXPERT, d), lambda i, e: (layer, e, 0, 0)),
                  pl.BlockSpec((tm, d), row),
                  pl.BlockSpec((1, 1, d), lambda i, e: (i // tpb, 0, 0)),
                  pl.BlockSpec((1, d), lambda i, e: (0, 0))],
        out_specs=pl.BlockSpec((tm, d), row),
        scratch_shapes=[pltpu.VMEM((tm, d), F32)],
        compiler_params=_cparams(("parallel", "arbitrary")),
        name="moe_dense",
    )(h2, gates3, w_gate, w_up, w_down, x1, g2, fnorm)


def _odd_in_kernel(x_ref, g_ref, sh_ref, sc_ref, w_ref, mu_ref, w0_ref, w2_ref, a0_ref, a2_ref,
                   g2_ref, kk_ref, ka_ref, ones_ref, pw_ref, ps_ref,
                   r_ref, lw_ref, km_ref, v_ref, kn_ref, kb_ref, gg_ref, op_ref,
                   rw_carry, u_carry, *, tiles_per_seq, tm):
    i = pl.program_id(0)
    first = (i % tiles_per_seq) == 0
    h = _norm_mod(x_ref[...], g_ref[...], sh_ref[0], sc_ref[0])
    proj = _mm(h.astype(BF16), w_ref[...])

    rw = proj[:, :ODD_RW_COLS]
    row0 = jnp.where(first, 0.0, rw_carry[0:1, :])
    ridx = lax.broadcasted_iota(jnp.int32, (tm, 1), 0)
    prev = jnp.where(ridx == 0, row0, pltpu.roll(rw, 1, 0))
    rw_carry[0:1, :] = rw[tm - 1:tm, :]
    rw = rw + (prev - rw) * mu_ref[...]

    r = rw[:, 0:512]
    k = rw[:, 512:1024]
    v = rw[:, 1024:1536]
    wl = rw[:, 1536:1664]
    al = rw[:, 1664:1792]
    gl = rw[:, 1792:1920]
    z = -(w0_ref[...] + _mm(jnp.tanh(wl).astype(BF16), w2_ref[...]))
    softplus = jnp.maximum(z, 0.0) + jnp.log1p(jnp.exp(-jnp.abs(z)))
    w_log = -softplus - 0.5
    a = jax.nn.sigmoid(a0_ref[...] + _mm(al.astype(BF16), a2_ref[...]))
    gg_ref[...] = _mm(jax.nn.sigmoid(gl).astype(BF16), g2_ref[...])
    kk0 = k * kk_ref[...]
    ss = _split_sum(kk0 * kk0, ones_ref[...])
    kk = kk0 / jnp.maximum(jnp.sqrt(ss), 1e-12)
    r_ref[...] = r
    lw_ref[...] = -jnp.exp(w_log)
    km_ref[...] = k * (1.0 + (a - 1.0) * ka_ref[...])
    v_ref[...] = v
    kn_ref[...] = kk
    kb_ref[...] = kk * a

    u = proj[:, ODD_RW_COLS:]
    halo = jnp.where(first, 0.0, u_carry[...])
    u_carry[...] = u[tm - POOL_HALO:, :]
    ext = jnp.concatenate([halo, u], axis=0)
    tseq = (i % tiles_per_seq) * tm + ridx
    for gi, win in enumerate(POOL_WINDOWS):
        xg = ext[:, gi * POOL_GROUP:(gi + 1) * POOL_GROUP]
        s = xg
        step = 1
        while step < win:
            s = s + pltpu.roll(s, step, 0)
            step *= 2
        cnt = jnp.minimum(tseq + 1, win).astype(F32)
        pooled = s[POOL_HALO:] / cnt - xg[POOL_HALO:]
        mixed = _mm(pooled.astype(BF16), pw_ref[gi])
        op_ref[:, gi * POOL_GROUP:(gi + 1) * POOL_GROUP] = (
            mixed * ps_ref[:, gi * POOL_GROUP:(gi + 1) * POOL_GROUP])


def odd_in_proj(x2, g, sh, sc, w_pad, mu_pad, w0, w2p, a0, a2p, g2, k_k, k_a, ones_bd, pool_w,
                pool_scale, seq):
    n, d = x2.shape
    tm = min(TOK_TILE, seq)
    tpb = seq // tm
    row = lambda i: (i, 0)
    per_b = lambda i: (i // tpb, 0, 0)
    c2 = lambda i: (0, 0)
    full2 = lambda a: pl.BlockSpec(a.shape, c2)
    return pl.pallas_call(
        functools.partial(_odd_in_kernel, tiles_per_seq=tpb, tm=tm),
        out_shape=[jax.ShapeDtypeStruct((n, RWKV_DIM), F32)] * 8,
        grid=(n // tm,),
        in_specs=[pl.BlockSpec((tm, d), row), pl.BlockSpec((1, d), c2),
                  pl.BlockSpec((1, 1, d), per_b), pl.BlockSpec((1, 1, d), per_b),
                  full2(w_pad), full2(mu_pad), full2(w0), full2(w2p), full2(a0), full2(a2p),
                  full2(g2), full2(k_k), full2(k_a), full2(ones_bd),
                  pl.BlockSpec(pool_w.shape, lambda i: (0, 0, 0)), full2(pool_scale)],
        out_specs=[pl.BlockSpec((tm, RWKV_DIM), row)] * 8,
        scratch_shapes=[pltpu.VMEM((SUBLANES, ODD_RW_COLS), F32),
                        pltpu.VMEM((POOL_HALO, RWKV_DIM), F32)],
        compiler_params=_cparams(("arbitrary",)),
        name="odd_in_proj",
    )(x2, g, sh, sc, w_pad, mu_pad, w0, w2p, a0, a2p, g2, k_k, k_a, ones_bd, pool_w, pool_scale)


def _bmm(a, b):
    return lax.dot_general(a, b, (((2,), (1,)), ((0,), (0,))), preferred_element_type=F32)


def _bnt(a, b):
    return lax.dot_general(a, b, (((2,), (2,)), ((0,), (0,))), preferred_element_type=F32)


def _btn(a, b):
    return lax.dot_general(a, b, (((1,), (1,)), ((0,), (0,))), preferred_element_type=F32)


def _scan_prep_kernel(r_ref, lw_ref, km_ref, v_ref, kn_ref, kb_ref, qe_ref, y0_ref, mt_ref, ct_ref,
                      *, chunk, cb):
    L = chunk
    rows = cb * L
    n_pairs = N_RWKV_HEADS // 2
    two = 2 * L
    rowt = lax.broadcasted_iota(jnp.int32, (rows, 1), 0) % L
    lane = lax.broadcasted_iota(jnp.int32, (1, 1, LANES), 2)
    low = lane < HEAD_DIM
    ri = lax.broadcasted_iota(jnp.int32, (two, two), 0)
    ci = lax.broadcasted_iota(jnp.int32, (two, two), 1)
    same_blk = (ri // L) == (ci // L)
    strict = same_blk & ((ci % L) < (ri % L))
    incl = same_blk & ((ci % L) <= (ri % L))
    li = lax.broadcasted_iota(jnp.int32, (LANES, LANES), 0)
    lj = lax.broadcasted_iota(jnp.int32, (LANES, LANES), 1)
    same_head = (li // HEAD_DIM) == (lj // HEAD_DIM)
    eye = li == lj

    lw = lw_ref[...]
    cum = lw
    step = 1
    while step < L:
        cum = cum + jnp.where(rowt >= step, pltpu.roll(cum, step, 0), 0.0)
        step *= 2

    def to3(x):
        x3 = x.reshape(cb, L, RWKV_DIM)
        return jnp.concatenate([x3[:, :, p * LANES:(p + 1) * LANES] for p in range(n_pairs)], axis=0)

    def stack2(x):
        return jnp.concatenate([jnp.where(low, x, 0.0), jnp.where(low, 0.0, x)], axis=1)

    def fold(x):
        return x[:, :L, :] + x[:, L:, :]

    cum3 = to3(cum)
    lw3 = to3(lw)
    cum_l = cum3[:, L - 1:L, :]
    g_inv = jnp.exp(-cum3)
    g_tail = jnp.exp(cum_l - cum3)
    kb = to3(kb_ref[...])
    km = to3(km_ref[...])
    v = to3(v_ref[...])
    at_s = stack2(-to3(kn_ref[...]) * jnp.exp(cum3 - lw3))
    rt_s = stack2(to3(r_ref[...]) * jnp.exp(cum3))
    v_s = stack2(v).astype(BF16)
    lhs = jnp.concatenate([at_s, rt_s], axis=1).astype(BF16)
    rhs = jnp.concatenate([stack2(kb * g_inv), stack2(km * g_inv)], axis=1).astype(BF16)
    prod = _bnt(lhs, rhs)
    nmat = jnp.where(strict, prod[:, :two, :two], 0.0)
    a_ak = jnp.where(strict, prod[:, :two, two:], 0.0).astype(BF16)
    a_rb = jnp.where(incl, prod[:, two:, :two], 0.0).astype(BF16)
    a_rk = jnp.where(incl, prod[:, two:, two:], 0.0).astype(BF16)

    x = jnp.concatenate([at_s, _bmm(a_ak, v_s)], axis=2)
    npow = nmat
    step = 1
    while step < L:
        nb = npow.astype(BF16)
        x = x + _bmm(nb, x.astype(BF16))
        step *= 2
        if step < L:
            npow = _bmm(nb, nb)
    qy = _bmm(a_rb, x.astype(BF16))
    qe = fold(rt_s + qy[:, :, :LANES])
    y0 = fold(qy[:, :, LANES:] + _bmm(a_rk, v_s))
    wu = fold(x).astype(BF16)
    bwu = _btn((kb * g_tail).astype(BF16), wu)
    kv = _btn((km * g_tail).astype(BF16), v.astype(BF16))
    g_l = jnp.broadcast_to(jnp.exp(cum_l), (n_pairs * cb, LANES, LANES))
    mt = jnp.where(eye, g_l, 0.0) + jnp.where(same_head, bwu[:, :, :LANES], 0.0)
    ct = jnp.where(same_head, bwu[:, :, LANES:] + kv, 0.0)
    for p in range(n_pairs):
        sl = slice(p * LANES, (p + 1) * LANES)
        qe_ref[:, sl] = qe[p * cb:(p + 1) * cb].reshape(rows, LANES)
        y0_ref[:, sl] = y0[p * cb:(p + 1) * cb].reshape(rows, LANES)
        mt_ref[:, p] = mt[p * cb:(p + 1) * cb].astype(BF16)
        ct_ref[:, p] = ct[p * cb:(p + 1) * cb]


def _scan_state_kernel(qe_ref, y0_ref, mt_ref, ct_ref, y_ref, st_ref, *, batch):
    @pl.when(pl.program_id(0) == 0)
    def _():
        st_ref[...] = jnp.zeros_like(st_ref)

    n_pairs = N_RWKV_HEADS // 2
    qe = qe_ref[...]
    qe3 = jnp.concatenate([qe[:, :, p * LANES:(p + 1) * LANES] for p in range(n_pairs)], axis=0)
    st = st_ref[...].astype(BF16)
    y = _bmm(qe3.astype(BF16), st)
    for p in range(n_pairs):
        sl = slice(p * LANES, (p + 1) * LANES)
        y_ref[:, :, sl] = y[p * batch:(p + 1) * batch] + y0_ref[:, :, sl]
    mt = jnp.concatenate([mt_ref[:, 0, p] for p in range(n_pairs)], axis=0)
    ct = jnp.concatenate([ct_ref[:, 0, p] for p in range(n_pairs)], axis=0)
    st_ref[...] = _bmm(mt, st) + ct


def rwkv_scan(r, lw, km, v, kn, kb, batch, seq):
    n = batch * seq
    chunk = min(SCAN_CHUNK, seq)
    nc = seq // chunk
    cb = min(SCAN_CHUNKS_PER_STEP, nc)
    n_pairs = N_RWKV_HEADS // 2
    blk = pl.BlockSpec((cb * chunk, RWKV_DIM), lambda i: (i, 0))
    mblk = pl.BlockSpec((cb, n_pairs, LANES, LANES), lambda i: (i, 0, 0, 0))
    qe, y0, mt, ct = pl.pallas_call(
        functools.partial(_scan_prep_kernel, chunk=chunk, cb=cb),
        out_shape=[jax.ShapeDtypeStruct((n, RWKV_DIM), F32), jax.ShapeDtypeStruct((n, RWKV_DIM), F32),
                   jax.ShapeDtypeStruct((n // chunk, n_pairs, LANES, LANES), BF16),
                   jax.ShapeDtypeStruct((n // chunk, n_pairs, LANES, LANES), F32)],
        grid=(n // (cb * chunk),),
        in_specs=[blk] * 6,
        out_specs=[blk, blk, mblk, mblk],
        compiler_params=_cparams(("parallel",)),
        name="rwkv_scan_prep",
    )(r, lw, km, v, kn, kb)
    sblk = pl.BlockSpec((batch, chunk, RWKV_DIM), lambda c: (0, c, 0))
    smblk = pl.BlockSpec((batch, 1, n_pairs, LANES, LANES), lambda c: (0, c, 0, 0, 0))
    y = pl.pallas_call(
        functools.partial(_scan_state_kernel, batch=batch),
        out_shape=jax.ShapeDtypeStruct((batch, seq, RWKV_DIM), F32),
        grid=(nc,),
        in_specs=[sblk, sblk, smblk, smblk],
        out_specs=sblk,
        scratch_shapes=[pltpu.VMEM((n_pairs * batch, LANES, LANES), F32)],
        compiler_params=_cparams(("arbitrary",)),
        name="rwkv_scan_state",
    )(qe.reshape(batch, seq, RWKV_DIM), y0.reshape(batch, seq, RWKV_DIM),
      mt.reshape(batch, nc, n_pairs, LANES, LANES), ct.reshape(batch, nc, n_pairs, LANES, LANES))
    return y.reshape(n, RWKV_DIM)


def _odd_out_kernel(y_ref, r_ref, km_ref, v_ref, gg_ref, op_ref, rk_ref, lnw_ref, lnb_ref, ones_ref,
                    x_ref, g1_ref, wout_ref, ng_ref, sh2_ref, sc2_ref, rwt_ref, rb_ref,
                    x1_ref, h2_ref, gates_ref):
    ones = ones_ref[...]
    inv = 1.0 / HEAD_DIM
    y = y_ref[...]
    mean = _split_sum(y, ones) * inv
    yc = y - mean
    var = _split_sum(yc * yc, ones) * inv
    yn = yc * lax.rsqrt(var + LNX_EPS) * lnw_ref[...] + lnb_ref[...]
    bonus = _split_sum(r_ref[...] * km_ref[...] * rk_ref[...], ones) * v_ref[...]
    o_rwkv = (yn + bonus) * gg_ref[...]
    cat = jnp.concatenate([o_rwkv, op_ref[...]], axis=1).astype(BF16)
    _tail(cat, x_ref, g1_ref, wout_ref, ng_ref, sh2_ref, sc2_ref, rwt_ref, rb_ref,
          x1_ref, h2_ref, gates_ref)


def odd_out_proj(y, r, km, v, gg, opool, r_k, lnx_w, lnx_b, ones_bd, x2, g1, w_out, ng, sh2, sc2,
                 rwt, rb, seq):
    n, d = x2.shape
    tm = min(TOK_TILE, seq)
    tpb = seq // tm
    row = lambda i: (i, 0)
    c2 = lambda i: (0, 0)
    act = pl.BlockSpec((tm, RWKV_DIM), row)
    vec = pl.BlockSpec((1, RWKV_DIM), c2)
    tin, tout = _tail_specs(tm, d, tpb)
    return pl.pallas_call(
        _odd_out_kernel,
        out_shape=_tail_out_shapes(n, d),
        grid=(n // tm,),
        in_specs=[act] * 6 + [vec, vec, vec, pl.BlockSpec(ones_bd.shape, c2)] + tin,
        out_specs=tout,
        compiler_params=_cparams(("parallel",)),
        name="odd_out_proj",
    )(y, r, km, v, gg, opool, r_k, lnx_w, lnx_b, ones_bd, x2, g1, w_out, ng, sh2, sc2, rwt, rb)


def _rope_tables(seq):
    half = HEAD_DIM // 2
    inv = ROPE_THETA ** (-jnp.arange(half, dtype=F32) / half)
    ang = jnp.arange(seq, dtype=F32)[:, None] * inv[None, :]
    return jnp.tile(jnp.cos(ang), (1, LANES // half)), jnp.tile(jnp.sin(ang), (1, LANES // half))


def _even_w_pad(w_in):
    d = w_in.shape[0]
    q_kv = w_in[:, :NSA_DIM + 6 * KV_DIM]
    gl = w_in[:, NSA_DIM + 6 * KV_DIM:NSA_DIM + 6 * KV_DIM + 24]
    rest = w_in[:, NSA_DIM + 6 * KV_DIM + 24:]
    z = jnp.zeros((d, LANES - 12), w_in.dtype)
    return jnp.concatenate([q_kv, gl[:, :12], z, gl[:, 12:], z, rest], axis=1).astype(BF16)


def _compress_params(cmp_pos, cmp_w1, cmp_w2):
    eye = jnp.eye(N_KV_HEADS, dtype=F32)
    w1r = cmp_w1.reshape(2, 2, CMP_STRIDE, HEAD_DIM, CMP_HIDDEN)
    w1_ext = jnp.einsum('kpmdn,gh->kpmgdhn', w1r, eye).reshape(
        2, 2, CMP_STRIDE * KV_DIM, N_KV_HEADS * CMP_HIDDEN).astype(BF16)
    w2_ext = jnp.einsum('knd,gh->kgnhd', cmp_w2, eye).reshape(
        2, N_KV_HEADS * CMP_HIDDEN, KV_DIM).astype(BF16)
    pos = cmp_pos.reshape(2, 2, CMP_STRIDE, 1, HEAD_DIM)
    pos_ext = jnp.broadcast_to(pos, (2, 2, CMP_STRIDE, N_KV_HEADS, HEAD_DIM)).reshape(
        2, 2, 1, CMP_STRIDE * KV_DIM)
    return pos_ext, w1_ext, w2_ext


def _nsa_tables(seq):
    n_blk = seq // SEL_BLOCK
    n_cmp = (seq - CMP_BLOCK) // CMP_STRIDE + 1
    n_cmp_pad = seq // CMP_STRIDE
    r = SEL_BLOCK // CMP_STRIDE
    c = CMP_BLOCK // CMP_STRIDE
    msel = np.zeros((n_cmp_pad, LANES), np.float32)
    for j in range(n_blk):
        for m in range(r):
            for n in range(c):
                idx = r * j + m + n
                if idx < n_cmp:
                    msel[idx, j] += 1.0
    chunk = min(SEL_CHUNK, seq)
    kblk = np.arange(seq) // SEL_BLOCK
    eexp = (kblk[None, :] == np.arange(LANES)[:, None]).astype(np.float32)
    eexp = eexp.reshape(LANES, seq // chunk, chunk).transpose(1, 0, 2)
    selq = np.zeros((N_KV_HEADS, GQA, GQA * HEAD_DIM, LANES), np.float32)
    for h in range(N_KV_HEADS):
        for g in range(GQA):
            for dd in range(HEAD_DIM):
                selq[h, g, g * HEAD_DIM + dd, h * HEAD_DIM + dd] = 1.0
    return jnp.asarray(msel.T), jnp.asarray(eexp, dtype=BF16), jnp.asarray(selq, dtype=BF16)


def _odd_params(w_in, mu, w2, a2):
    d = w_in.shape[0]
    z64 = jnp.zeros((d, 64), w_in.dtype)
    w_pad = jnp.concatenate([w_in[:, :1536], w_in[:, 1536:1600], z64, w_in[:, 1600:1664], z64,
                             w_in[:, 1664:]], axis=1).astype(BF16)
    m64 = jnp.zeros((64,), mu.dtype)
    mu_pad = jnp.concatenate([mu[:1536], mu[1536:1600], m64, mu[1600:1664], m64, mu[1664:]])[None, :]
    zr = jnp.zeros((64, RWKV_DIM), w2.dtype)
    w2p = jnp.concatenate([w2, zr], axis=0).astype(BF16)
    a2p = jnp.concatenate([a2, zr], axis=0).astype(BF16)
    return w_pad, mu_pad, w2p, a2p


def _head_ones():
    idx = np.arange(RWKV_DIM) // HEAD_DIM
    return jnp.asarray((idx[:, None] == idx[None, :]).astype(np.float32), dtype=BF16)


def kernel(x, c, ada_w, ada_b, norm_mix, norm_ffn, even_w_in, even_cmp_pos, even_cmp_w1, even_cmp_w2,
           even_conv_w, even_w_out, odd_w_in, odd_mu, odd_w0, odd_w2, odd_a0, odd_a2, odd_g2, odd_k_k,
           odd_k_a, odd_r_k, odd_lnx_w, odd_lnx_b, odd_pool_w, odd_pool_scale, odd_w_out,
           router_w, router_b, moe_w_gate, moe_w_up, moe_w_down, final_norm):
    batch, seq, d = x.shape
    n = batch * seq
    depth = ada_w.shape[0]
    x2 = x.reshape(n, d)
    mod = ada_modulation(c, ada_w, ada_b)
    rwt = router_w.T
    rb = router_b.reshape(N_EXPERTS, 1)
    fnorm = final_norm.reshape(1, d)
    cos, sin = _rope_tables(seq)
    msel, eexp, selq = _nsa_tables(seq)
    ones_bd = _head_ones()

    for layer in range(depth):
        m = mod[layer].reshape(batch, 6, 1, d)
        sh1, sc1, g1, sh2, sc2, g2 = (m[:, k] for k in range(6))
        ng_mix = norm_mix[layer].reshape(1, d)
        ng_ffn = norm_ffn[layer].reshape(1, d)
        i = layer // 2
        if layer % 2 == 0:
            (qn, qr, kc, vc, ks, vs, kw, vw, gate, u, bg) = even_in_proj(
                x2, ng_mix, sh1, sc1, _even_w_pad(even_w_in[i]), cos, sin, seq)
            pos_ext, w1_ext, w2_ext = _compress_params(even_cmp_pos[i], even_cmp_w1[i], even_cmp_w2[i])
            kcmp, vcmp = compress_kv(kc, vc, pos_ext, w1_ext, w2_ext, batch, seq)
            o_nsa = nsa_attention(qn, qr, kcmp, vcmp, ks.reshape(batch, seq, KV_DIM),
                                  vs.reshape(batch, seq, KV_DIM), kw.reshape(batch, seq, KV_DIM),
                                  vw.reshape(batch, seq, KV_DIM), gate, selq, msel, eexp, batch, seq)
            x1, h2, gates = even_out_proj(o_nsa, u, bg, even_conv_w[i], x2, g1,
                                          even_w_out[i].astype(BF16), ng_ffn, sh2, sc2, rwt, rb, seq)
        else:
            w_pad, mu_pad, w2p, a2p = _odd_params(odd_w_in[i], odd_mu[i], odd_w2[i], odd_a2[i])
            vec = lambda a: a.reshape(1, RWKV_DIM)
            (r, lw, km, v, kn, kb, gg, opool) = odd_in_proj(
                x2, ng_mix, sh1, sc1, w_pad, mu_pad, vec(odd_w0[i]), w2p, vec(odd_a0[i]), a2p,
                odd_g2[i].astype(BF16), vec(odd_k_k[i]), vec(odd_k_a[i]), ones_bd,
                odd_pool_w[i].astype(BF16), vec(odd_pool_scale[i]), seq)
            y = rwkv_scan(r, lw, km, v, kn, kb, batch, seq)
            x1, h2, gates = odd_out_proj(y, r, km, v, gg, opool, vec(odd_r_k[i]), vec(odd_lnx_w[i]),
                                         vec(odd_lnx_b[i]), ones_bd, x2, g1,
                                         odd_w_out[i].astype(BF16), ng_ffn, sh2, sc2, rwt, rb, seq)
        x2 = moe_dense(h2, gates.reshape(N_EXPERTS, n, 1), moe_w_gate, moe_w_up, moe_w_down, layer,
                       x1, g2, fnorm, seq, final_norm=(layer == depth - 1))
    return x2.reshape(batch, seq, d)
```

```python
import functools

import jax
import jax.numpy as jnp
import numpy as np
from jax import lax
from jax.experimental import pallas as pl
from jax.experimental.pallas import tpu as pltpu

F32 = jnp.float32
BF16 = jnp.bfloat16
HIGHEST = lax.Precision.HIGHEST

D_MODEL = 1024
DEPTH = 2
HEAD_DIM = 64
ROPE_THETA = 10000.0
NORM_EPS = 1e-6
NEG_INF = -1e30
BIG = 1e9
NSA_DIM = 512
N_KV_HEADS = 2
GQA = 4
KV_DIM = 128
CMP_BLOCK = 32
CMP_STRIDE = 16
CMP_HIDDEN = 256
SEL_BLOCK = 64
N_SEL = 8
N_LOCAL = 2
WINDOW = 512
Q_BLOCK = 128
ATTN_SCALE = HEAD_DIM ** -0.5
CONV_DIM = 512
RWKV_DIM = 512
N_RWKV_HEADS = 8
LNX_EPS = 64e-5
POOL_WINDOWS = (2, 4, 8, 16)
POOL_GROUP = 128
N_EXPERTS = 16
N_EXPERT_GROUPS = 4
EXPERTS_PER_GROUP = 4
D_EXPERT = 512

LANES = 128
SUBLANES = 8
VMEM_LIMIT = 56 * 1024 * 1024

TOK_TILE = 512
MOE_ROW_TILE = 256
MOE_DMA_TILE = 256
SEL_CHUNK = 512
SCAN_CHUNK = 64
SCAN_CHUNKS_PER_STEP = 4
CONV_HALO = 8
POOL_HALO = 16

EVEN_PAD_COLS = 3072
ODD_PAD_COLS = 2432
ODD_RW_COLS = 1920


def _cparams(sem):
    return pltpu.CompilerParams(dimension_semantics=sem, vmem_limit_bytes=VMEM_LIMIT)


def _nt(a, b, precision=None):
    return lax.dot_general(a, b, (((1,), (1,)), ((), ())), preferred_element_type=F32,
                           precision=precision)


def _tn(a, b):
    return lax.dot_general(a, b, (((0,), (0,)), ((), ())), preferred_element_type=F32)


def _mm(a, b, precision=None):
    return jnp.dot(a, b, preferred_element_type=F32, precision=precision)


def _norm_mod(x, g, sh, sc):
    ms = jnp.mean(x * x, axis=-1, keepdims=True)
    return (x * lax.rsqrt(ms + NORM_EPS) * g) * (1.0 + sc) + sh


def _split_sum(x, ones_bf16):
    hi = x.astype(BF16)
    lo = (x - hi.astype(F32)).astype(BF16)
    return _mm(hi, ones_bf16) + _mm(lo, ones_bf16)


def _ada_kernel(c_ref, w_ref, b_ref, o_ref):
    c = c_ref[...]
    cond = c * jax.nn.sigmoid(c)
    o_ref[0] = _mm(cond, w_ref[0], precision=HIGHEST) + b_ref[0]


def ada_modulation(c, ada_w, ada_b):
    depth, d, cols = ada_w.shape
    b = c.shape[0]
    tn = 1536
    return pl.pallas_call(
        _ada_kernel,
        out_shape=jax.ShapeDtypeStruct((depth, b, cols), F32),
        grid=(depth, cols // tn),
        in_specs=[pl.BlockSpec((b, d), lambda l, j: (0, 0)),
                  pl.BlockSpec((1, d, tn), lambda l, j: (l, 0, j)),
                  pl.BlockSpec((1, 1, tn), lambda l, j: (l, 0, j))],
        out_specs=pl.BlockSpec((1, b, tn), lambda l, j: (l, 0, j)),
        compiler_params=_cparams(("parallel", "parallel")),
        name="ada_modulation",
    )(c, ada_w, ada_b.reshape(depth, 1, cols))


def _rope128(t, cos, sin, lane):
    rot = jnp.where((lane % HEAD_DIM) < HEAD_DIM // 2,
                    -pltpu.roll(t, LANES - HEAD_DIM // 2, 1), pltpu.roll(t, HEAD_DIM // 2, 1))
    return t * cos + rot * sin


def _even_in_kernel(x_ref, g_ref, sh_ref, sc_ref, w_ref, cos_ref, sin_ref,
                    qn_ref, qr_ref, kc_ref, vc_ref, ks_ref, vs_ref, kw_ref, vw_ref,
                    gate_ref, u_ref, bg_ref):
    h = _norm_mod(x_ref[...], g_ref[...], sh_ref[0], sc_ref[0])
    proj = _mm(h.astype(BF16), w_ref[...])
    cos = cos_ref[...]
    sin = sin_ref[...]
    lane = lax.broadcasted_iota(jnp.int32, (1, LANES), 1)
    for i in range(NSA_DIM // LANES):
        q = proj[:, i * LANES:(i + 1) * LANES] * ATTN_SCALE
        qn_ref[:, i * LANES:(i + 1) * LANES] = q.astype(BF16)
        qr_ref[:, i * LANES:(i + 1) * LANES] = _rope128(q, cos, sin, lane).astype(BF16)
    o = NSA_DIM
    kc_ref[...] = proj[:, o:o + 128]
    vc_ref[...] = proj[:, o + 128:o + 256]
    ks_ref[...] = _rope128(proj[:, o + 256:o + 384], cos, sin, lane).astype(BF16)
    vs_ref[...] = proj[:, o + 384:o + 512].astype(BF16)
    kw_ref[...] = _rope128(proj[:, o + 512:o + 640], cos, sin, lane).astype(BF16)
    vw_ref[...] = proj[:, o + 640:o + 768].astype(BF16)
    o += 768
    gate_ref[...] = jax.nn.sigmoid(proj[:, o:o + 256])
    o += 256
    xb = proj[:, o:o + 512]
    bg_ref[...] = proj[:, o + 512:o + 1024]
    u_ref[...] = proj[:, o + 1024:o + 1536] * xb


def even_in_proj(x2, g, sh, sc, w_pad, cos, sin, seq):
    n, d = x2.shape
    tm = min(TOK_TILE, seq)
    tpb = seq // tm
    row = lambda i: (i, 0)
    per_b = lambda i: (i // tpb, 0, 0)
    pos = lambda i: (i % tpb, 0)
    outs = [((n, 512), BF16), ((n, 512), BF16), ((n, 128), F32), ((n, 128), F32),
            ((n, 128), BF16), ((n, 128), BF16), ((n, 128), BF16), ((n, 128), BF16),
            ((n, 256), F32), ((n, 512), F32), ((n, 512), F32)]
    return pl.pallas_call(
        _even_in_kernel,
        out_shape=[jax.ShapeDtypeStruct(s, t) for s, t in outs],
        grid=(n // tm,),
        in_specs=[pl.BlockSpec((tm, d), row),
                  pl.BlockSpec((1, d), lambda i: (0, 0)),
                  pl.BlockSpec((1, 1, d), per_b),
                  pl.BlockSpec((1, 1, d), per_b),
                  pl.BlockSpec((d, EVEN_PAD_COLS), lambda i: (0, 0)),
                  pl.BlockSpec((tm, LANES), pos),
                  pl.BlockSpec((tm, LANES), pos)],
        out_specs=[pl.BlockSpec((tm, s[1]), row) for s, _ in outs],
        compiler_params=_cparams(("parallel",)),
        name="even_in_proj",
    )(x2, g, sh, sc, w_pad, cos, sin)


def _compress_kernel(k_ref, v_ref, pos_ref, w1_ref, w2_ref, ko_ref, vo_ref):
    for j, (src, dst) in enumerate(((k_ref, ko_ref), (v_ref, vo_ref))):
        xr = src[0]
        n_rows = xr.shape[0]
        a0 = _mm((xr + pos_ref[j, 0]).astype(BF16), w1_ref[j, 0])
        a1 = _mm((xr + pos_ref[j, 1]).astype(BF16), w1_ref[j, 1])
        hid = a0 + pltpu.roll(a1, n_rows - 1, 0)
        hid = jax.nn.gelu(hid)
        dst[0] = _mm(hid.astype(BF16), w2_ref[j]).astype(BF16)


def compress_kv(kc, vc, pos_ext, w1_ext, w2_ext, batch, seq):
    rows = seq // CMP_STRIDE
    width = CMP_STRIDE * KV_DIM
    kr = kc.reshape(batch, rows, width)
    vr = vc.reshape(batch, rows, width)
    blk = pl.BlockSpec((1, rows, width), lambda b: (b, 0, 0))
    oblk = pl.BlockSpec((1, rows, KV_DIM), lambda b: (b, 0, 0))
    return pl.pallas_call(
        _compress_kernel,
        out_shape=[jax.ShapeDtypeStruct((batch, rows, KV_DIM), BF16)] * 2,
        grid=(batch,),
        in_specs=[blk, blk,
                  pl.BlockSpec(pos_ext.shape, lambda b: (0, 0, 0, 0)),
                  pl.BlockSpec(w1_ext.shape, lambda b: (0, 0, 0, 0)),
                  pl.BlockSpec(w2_ext.shape, lambda b: (0, 0, 0))],
        out_specs=[oblk, oblk],
        compiler_params=_cparams(("parallel",)),
        name="compress_kv",
    )(kr, vr, pos_ext, w1_ext, w2_ext)


def _safe_inv(l):
    return jnp.where(l > 0.0, 1.0 / jnp.where(l > 0.0, l, 1.0), 0.0)


def _nsa_kernel(qn_ref, qr_ref, kc_ref, vc_ref, ks_ref, vs_ref, kw_ref, vw_ref, gate_ref,
                selq_ref, mselt_ref, eexp_ref, o_ref, *, seq, n_sel, sel_chunk, win_len):
    h = pl.program_id(1)
    qt = pl.program_id(2)
    t0 = qt * Q_BLOCK
    n_blk = seq // SEL_BLOCK
    n_cmp_pad = seq // CMP_STRIDE
    rows = GQA * Q_BLOCK
    tpos = t0 + lax.broadcasted_iota(jnp.int32, (1, Q_BLOCK, 1), 1)
    lane = lax.broadcasted_iota(jnp.int32, (1, LANES), 1)
    head_lanes = (lane // HEAD_DIM) == h

    qn = qn_ref[...]
    qr = qr_ref[...]
    qn4 = jnp.concatenate([_mm(qn, selq_ref[0, g]) for g in range(GQA)], axis=0).astype(BF16)
    qr4 = jnp.concatenate([_mm(qr, selq_ref[0, g]) for g in range(GQA)], axis=0).astype(BF16)

    kc = kc_ref[0]
    vc = vc_ref[0]
    cpos = lax.broadcasted_iota(jnp.int32, (1, 1, n_cmp_pad), 2) * CMP_STRIDE + (CMP_BLOCK - 1)
    cmask = cpos <= tpos
    s = jnp.where(cmask, _nt(qn4, kc).reshape(GQA, Q_BLOCK, n_cmp_pad), NEG_INF)
    e = jnp.where(cmask, jnp.exp(s - jnp.max(s, axis=2, keepdims=True)), 0.0)
    p = e * _safe_inv(jnp.sum(e, axis=2, keepdims=True))
    imp = jnp.sum(p, axis=0)
    o_cmp = _mm(p.reshape(rows, n_cmp_pad).astype(BF16), vc)

    pslc = _nt(mselt_ref[...], imp, precision=HIGHEST)[:n_blk]
    tq = t0 + lax.broadcasted_iota(jnp.int32, (1, Q_BLOCK), 1)
    jblk = lax.broadcasted_iota(jnp.int32, (n_blk, 1), 0)
    cur = tq // SEL_BLOCK
    valid = jblk * SEL_BLOCK <= tq
    forced = (jblk == 0) | ((cur - jblk >= 0) & (cur - jblk < N_LOCAL))
    score = jnp.where(forced, BIG, jnp.where(valid, pslc, -BIG))
    rank = jnp.zeros((n_blk, Q_BLOCK), jnp.int32)
    for jp in range(n_blk):
        row = score[jp:jp + 1, :]
        beats = (row > score) | ((row == score) & (jblk > jp))
        rank = rank + beats.astype(jnp.int32)
    sel = jnp.where((rank < n_sel) & (score > -0.5 * BIG), 1.0, 0.0)
    sel_t = jnp.concatenate([sel, jnp.zeros((LANES - n_blk, Q_BLOCK), F32)], axis=0).astype(BF16)

    n_chunks = (t0 + Q_BLOCK - 1) // sel_chunk + 1

    def sel_body(c, carry):
        m, l, acc = carry
        start = pl.multiple_of(c * sel_chunk, sel_chunk)
        kblk = ks_ref[0, pl.ds(start, sel_chunk), :]
        vblk = vs_ref[0, pl.ds(start, sel_chunk), :]
        kpos = start + lax.broadcasted_iota(jnp.int32, (1, 1, sel_chunk), 2)
        mask = (_tn(sel_t, eexp_ref[c]) > 0.5)[None] & (kpos <= tpos)
        s = jnp.where(mask, _nt(qr4, kblk).reshape(GQA, Q_BLOCK, sel_chunk), NEG_INF)
        m_new = jnp.maximum(m, jnp.max(s, axis=2, keepdims=True))
        alpha = jnp.exp(m - m_new)
        p = jnp.exp(s - m_new)
        l_new = alpha * l + jnp.sum(p, axis=2, keepdims=True)
        pv = _mm(p.reshape(rows, sel_chunk).astype(BF16), vblk).reshape(GQA, Q_BLOCK, LANES)
        return m_new, l_new, alpha * acc + pv

    init = (jnp.full((GQA, Q_BLOCK, 1), NEG_INF, F32), jnp.zeros((GQA, Q_BLOCK, 1), F32),
            jnp.zeros((GQA, Q_BLOCK, LANES), F32))
    _, l, acc = lax.fori_loop(0, n_chunks, sel_body, init)
    o_slc = acc * _safe_inv(l)

    ws = pl.multiple_of(jnp.maximum(qt - WINDOW // Q_BLOCK, 0) * Q_BLOCK, Q_BLOCK)
    kwb = kw_ref[0, pl.ds(ws, win_len), :]
    vwb = vw_ref[0, pl.ds(ws, win_len), :]
    diff = tpos - (ws + lax.broadcasted_iota(jnp.int32, (1, 1, win_len), 2))
    wmask = (diff >= 0) & (diff < WINDOW)
    s = jnp.where(wmask, _nt(qr4, kwb).reshape(GQA, Q_BLOCK, win_len), NEG_INF)
    e = jnp.exp(s - jnp.max(s, axis=2, keepdims=True))
    o_win = (_mm(e.reshape(rows, win_len).astype(BF16), vwb).reshape(GQA, Q_BLOCK, LANES)
             * _safe_inv(jnp.sum(e, axis=2, keepdims=True)))

    gate = gate_ref[...]
    o_cmp = o_cmp.reshape(GQA, Q_BLOCK, LANES)
    og = []
    for g in range(GQA):
        o = (gate[:, 3 * g:3 * g + 1] * o_cmp[g] + gate[:, 3 * g + 1:3 * g + 2] * o_slc[g]
             + gate[:, 3 * g + 2:3 * g + 3] * o_win[g])
        og.append(jnp.where(head_lanes, o, pltpu.roll(o, HEAD_DIM, 1)))
    low = lane < HEAD_DIM
    o_ref[:, 0:LANES] = jnp.where(low, og[0], og[1])
    o_ref[:, LANES:2 * LANES] = jnp.where(low, og[2], og[3])


def nsa_attention(qn, qr, kcmp, vcmp, ks, vs, kw, vw, gate, selq, msel, eexp, batch, seq):
    n = batch * seq
    nq = seq // Q_BLOCK
    sel_chunk = min(SEL_CHUNK, seq)
    win_len = min(WINDOW + Q_BLOCK, seq)
    n_sel = min(N_SEL, seq // SEL_BLOCK)
    qspec = pl.BlockSpec((Q_BLOCK, GQA * HEAD_DIM), lambda b, h, q: (b * nq + q, h))
    cspec = pl.BlockSpec((1, seq // CMP_STRIDE, KV_DIM), lambda b, h, q: (b, 0, 0))
    kspec = pl.BlockSpec((1, seq, KV_DIM), lambda b, h, q: (b, 0, 0))
    kern = functools.partial(_nsa_kernel, seq=seq, n_sel=n_sel, sel_chunk=sel_chunk, win_len=win_len)
    return pl.pallas_call(
        kern,
        out_shape=jax.ShapeDtypeStruct((n, NSA_DIM), F32),
        grid=(batch, N_KV_HEADS, nq),
        in_specs=[qspec, qspec, cspec, cspec, kspec, kspec, kspec, kspec,
                  pl.BlockSpec((Q_BLOCK, LANES), lambda b, h, q: (b * nq + q, h)),
                  pl.BlockSpec((1, GQA, GQA * HEAD_DIM, LANES), lambda b, h, q: (h, 0, 0, 0)),
                  pl.BlockSpec(msel.shape, lambda b, h, q: (0, 0)),
                  pl.BlockSpec(eexp.shape, lambda b, h, q: (0, 0, 0))],
        out_specs=qspec,
        compiler_params=_cparams(("parallel", "parallel", "arbitrary")),
        name="nsa_attention",
    )(qn, qr, kcmp, vcmp, ks, vs, kw, vw, gate, selq, msel, eexp)


def _route(h2, rwt_ref, rb_ref):
    logits = _nt(rwt_ref[...], h2, precision=HIGHEST)
    scores = jax.nn.sigmoid(logits)
    biased = scores + rb_ref[...]
    rows = [biased[e:e + 1, :] for e in range(N_EXPERTS)]
    srow = [scores[e:e + 1, :] for e in range(N_EXPERTS)]
    gscore = []
    for gi in range(N_EXPERT_GROUPS):
        r = rows[gi * EXPERTS_PER_GROUP:(gi + 1) * EXPERTS_PER_GROUP]
        best = None
        for a in range(EXPERTS_PER_GROUP):
            for b in range(a + 1, EXPERTS_PER_GROUP):
                pair = r[a] + r[b]
                best = pair if best is None else jnp.maximum(best, pair)
        gscore.append(best)
    top_val = gscore[0]
    top_grp = jnp.zeros_like(top_val, dtype=jnp.int32)
    for gi in range(1, N_EXPERT_GROUPS):
        upd = gscore[gi] > top_val
        top_grp = jnp.where(upd, gi, top_grp)
        top_val = jnp.where(upd, gscore[gi], top_val)
    masked = [jnp.where(top_grp == e // EXPERTS_PER_GROUP, rows[e], NEG_INF) for e in range(N_EXPERTS)]
    b1 = masked[0]
    i1 = jnp.zeros_like(top_grp)
    for e in range(1, N_EXPERTS):
        upd = masked[e] > b1
        i1 = jnp.where(upd, e, i1)
        b1 = jnp.where(upd, masked[e], b1)
    b2 = None
    i2 = None
    for e in range(N_EXPERTS):
        v = jnp.where(i1 == e, -jnp.inf, masked[e])
        if b2 is None:
            b2, i2 = v, jnp.zeros_like(top_grp)
        else:
            upd = v > b2
            i2 = jnp.where(upd, e, i2)
            b2 = jnp.where(upd, v, b2)
    s1 = jnp.zeros_like(top_val)
    s2 = jnp.zeros_like(top_val)
    for e in range(N_EXPERTS):
        s1 = s1 + jnp.where(i1 == e, srow[e], 0.0)
        s2 = s2 + jnp.where(i2 == e, srow[e], 0.0)
    tot = s1 + s2
    return i1, i2, s1 / tot, s2 / tot


def _tail(cat_bf16, x_ref, g1_ref, wout_ref, ng_ref, sh2_ref, sc2_ref, rwt_ref, rb_ref, ustrict_ref,
          x1_ref, hw_ref, ridx_ref, cnt_ref, carry_ref):
    y = _mm(cat_bf16, wout_ref[...])
    x1 = x_ref[...] + g1_ref[0] * y
    x1_ref[...] = x1
    h2 = _norm_mod(x1, ng_ref[...], sh2_ref[0], sc2_ref[0])
    tm, d = h2.shape
    hw_ref[:, :d] = h2

    i1, i2, w1, w2 = _route(h2, rwt_ref, rb_ref)
    meta_t = jnp.concatenate([w1, w2, i1.astype(F32), i2.astype(F32),
                              jnp.zeros((LANES - 4, tm), F32)], axis=0)
    eye = (lax.broadcasted_iota(jnp.int32, (tm, tm), 0)
           == lax.broadcasted_iota(jnp.int32, (tm, tm), 1)).astype(F32)
    hw_ref[:, d:] = _nt(eye, meta_t, precision=HIGHEST)

    @pl.when(pl.program_id(0) == 0)
    def _():
        carry_ref[...] = jnp.zeros_like(carry_ref)

    eidx = lax.broadcasted_iota(jnp.int32, (N_EXPERTS, 1), 0)
    hit1 = eidx == i1
    hit2 = eidx == i2
    cnt = jnp.where(hit1 | hit2, 1.0, 0.0)
    before = _mm(cnt.astype(BF16), ustrict_ref[...]) + carry_ref[:, 0:1]
    carry = carry_ref[...] + jnp.sum(cnt, axis=1, keepdims=True)
    carry_ref[...] = carry
    cnt_ref[...] = carry.astype(jnp.int32)
    r1 = jnp.sum(jnp.where(hit1, before, 0.0), axis=0, keepdims=True)
    r2 = jnp.sum(jnp.where(hit2, before, 0.0), axis=0, keepdims=True)
    ridx_ref[0:1, :] = i1
    ridx_ref[1:2, :] = i2
    ridx_ref[2:3, :] = r1.astype(jnp.int32)
    ridx_ref[3:4, :] = r2.astype(jnp.int32)


def _even_out_kernel(o_ref, u_ref, uh_ref, bg_ref, cw_ref, *tail_refs, tiles_per_seq):
    first = (pl.program_id(0) % tiles_per_seq) == 0
    u = u_ref[...]
    tm = u.shape[0]
    halo = jnp.where(first, 0.0, uh_ref[...])
    ext = jnp.concatenate([halo, u], axis=0)
    u1 = pltpu.roll(ext, 1, 0)[CONV_HALO:]
    u2 = pltpu.roll(ext, 2, 0)[CONV_HALO:]
    cw = cw_ref[...]
    y_conv = bg_ref[...] * (cw[2:3] * u + cw[1:2] * u1 + cw[0:1] * u2)
    cat = jnp.concatenate([o_ref[...], y_conv], axis=1).astype(BF16)
    _tail(cat, *tail_refs)


def _tail_specs(tm, d, tpb):
    row = lambda i: (i, 0)
    per_b = lambda i: (i // tpb, 0, 0)
    const2 = lambda i: (0, 0)
    ins = [pl.BlockSpec((tm, d), row),
           pl.BlockSpec((1, 1, d), per_b),
           pl.BlockSpec((d, d), const2),
           pl.BlockSpec((1, d), const2),
           pl.BlockSpec((1, 1, d), per_b),
           pl.BlockSpec((1, 1, d), per_b),
           pl.BlockSpec((N_EXPERTS, d), const2),
           pl.BlockSpec((N_EXPERTS, 1), const2),
           pl.BlockSpec((tm, tm), const2)]
    outs = [pl.BlockSpec((tm, d), row), pl.BlockSpec((tm, d + LANES), row),
            pl.BlockSpec((4, tm), lambda i: (0, i)),
            pl.BlockSpec((N_EXPERTS, LANES), const2)]
    scratch = [pltpu.VMEM((N_EXPERTS, LANES), F32)]
    return ins, outs, scratch


def _tail_out_shapes(n, d):
    return [jax.ShapeDtypeStruct((n, d), F32), jax.ShapeDtypeStruct((n, d + LANES), F32),
            jax.ShapeDtypeStruct((4, n), jnp.int32), jax.ShapeDtypeStruct((N_EXPERTS, LANES), jnp.int32)]


def _strict_upper(tm):
    return jnp.asarray(np.triu(np.ones((tm, tm), np.float32), 1), dtype=BF16)


def even_out_proj(o_nsa, u, bg, conv_w, x2, g1, w_out, ng, sh2, sc2, rwt, rb, seq):
    n, d = x2.shape
    tm = min(TOK_TILE, seq)
    tpb = seq // tm
    row = lambda i: (i, 0)
    halo = lambda i: (jnp.maximum(i * (tm // CONV_HALO) - 1, 0), 0)
    tin, tout, tscratch = _tail_specs(tm, d, tpb)
    return pl.pallas_call(
        functools.partial(_even_out_kernel, tiles_per_seq=tpb),
        out_shape=_tail_out_shapes(n, d),
        grid=(n // tm,),
        in_specs=[pl.BlockSpec((tm, NSA_DIM), row),
                  pl.BlockSpec((tm, CONV_DIM), row),
                  pl.BlockSpec((CONV_HALO, CONV_DIM), halo),
                  pl.BlockSpec((tm, CONV_DIM), row),
                  pl.BlockSpec(conv_w.shape, lambda i: (0, 0))] + tin,
        out_specs=tout,
        scratch_shapes=tscratch,
        compiler_params=_cparams(("arbitrary",)),
        name="even_out_proj",
    )(o_nsa, u, u, bg, conv_w, x2, g1, w_out, ng, sh2, sc2, rwt, rb, _strict_upper(tm))


def _dispatch_plan(ridx, counts, n):
    cnt = counts[:, 0]
    padded = (cnt + MOE_ROW_TILE - 1) // MOE_ROW_TILE * MOE_ROW_TILE
    ends = jnp.cumsum(padded)
    starts = ends - padded
    eids = jnp.arange(N_EXPERTS, dtype=jnp.int32)[:, None]
    base0 = jnp.sum(jnp.where(eids == ridx[0][None, :], starts[:, None], 0), axis=0)
    base1 = jnp.sum(jnp.where(eids == ridx[1][None, :], starts[:, None], 0), axis=0)
    dest = jnp.stack([base0 + ridx[2], base1 + ridx[3]]).astype(jnp.int32)
    td = min(MOE_DMA_TILE, n)
    dest3 = dest.reshape(2, n // td, td).transpose(1, 0, 2)
    n_tiles = (2 * n) // MOE_ROW_TILE + N_EXPERTS
    tile_start = jnp.arange(n_tiles, dtype=jnp.int32) * MOE_ROW_TILE
    tile_expert = jnp.minimum(jnp.sum(tile_start[:, None] >= ends[None, :], axis=1),
                              N_EXPERTS - 1).astype(jnp.int32)
    n_used = (ends[-1] // MOE_ROW_TILE).reshape(1).astype(jnp.int32)
    last_tile = jnp.where(cnt > 0, ends // MOE_ROW_TILE - 1, -1)
    tail = n_used[0] + jnp.arange(N_EXPERTS, dtype=jnp.int32)
    zero_tiles = jnp.concatenate([last_tile, jnp.where(tail < n_tiles, tail, -1)]).astype(jnp.int32)
    return dest3, tile_expert, n_used, zero_tiles, n_tiles


def _dispatch_kernel(ztile_ref, dest_ref, hw_ref, xs_hbm, zbuf, zsem, sem):
    td = hw_ref.shape[0]

    @pl.when(pl.program_id(0) == 0)
    def _():
        zbuf[...] = jnp.zeros_like(zbuf)

        def zero_copy(k):
            start = pl.multiple_of(ztile_ref[k] * MOE_ROW_TILE, MOE_ROW_TILE)
            return pltpu.make_async_copy(zbuf, xs_hbm.at[pl.ds(start, MOE_ROW_TILE)], zsem)

        for k in range(2 * N_EXPERTS):
            @pl.when(ztile_ref[k] >= 0)
            def _():
                zero_copy(k).start()
        for k in range(2 * N_EXPERTS):
            @pl.when(ztile_ref[k] >= 0)
            def _():
                zero_copy(k).wait()

    for r in range(td):
        for slot in range(2):
            pltpu.make_async_copy(hw_ref.at[pl.ds(r, 1)],
                                  xs_hbm.at[pl.ds(dest_ref[0, slot, r], 1)], sem).start()
    for slot in range(2):
        pltpu.make_async_copy(hw_ref, xs_hbm.at[pl.ds(0, td)], sem).wait()


def moe_dispatch(hw, dest3, zero_tiles, n_tiles):
    n, cols = hw.shape
    td = dest3.shape[2]
    rows = n_tiles * MOE_ROW_TILE
    return pl.pallas_call(
        _dispatch_kernel,
        out_shape=jax.ShapeDtypeStruct((rows, cols), F32),
        grid_spec=pltpu.PrefetchScalarGridSpec(
            num_scalar_prefetch=1,
            grid=(n // td,),
            in_specs=[pl.BlockSpec((1, 2, td), lambda i, z: (i, 0, 0), memory_space=pltpu.SMEM),
                      pl.BlockSpec((td, cols), lambda i, z: (i, 0))],
            out_specs=pl.BlockSpec(memory_space=pl.ANY),
            scratch_shapes=[pltpu.VMEM((MOE_ROW_TILE, cols), F32), pltpu.SemaphoreType.DMA(()),
                            pltpu.SemaphoreType.DMA(())]),
        compiler_params=_cparams(("arbitrary",)),
        name="moe_dispatch",
    )(zero_tiles, dest3, hw)


def _expert_kernel(te_ref, nused_ref, xs_ref, wg_ref, wu_ref, wd_ref, ys_ref, wgb, wub, wdb):
    t = pl.program_id(0)
    e = te_ref[t]

    @pl.when((t == 0) | (e != te_ref[jnp.maximum(t - 1, 0)]))
    def _():
        wgb[...] = wg_ref[0, 0].astype(BF16)
        wub[...] = wu_ref[0, 0].astype(BF16)
        wdb[...] = wd_ref[0, 0].astype(BF16)

    @pl.when(t < nused_ref[0])
    def _():
        d = xs_ref.shape[1] - LANES
        x = xs_ref[:, :d].astype(BF16)
        meta = xs_ref[:, d:]
        gate = jnp.where(meta[:, 2:3] == e.astype(F32), meta[:, 0:1], meta[:, 1:2])
        a = _mm(x, wgb[...])
        b = _mm(x, wub[...])
        he = (a * jax.nn.sigmoid(a)) * b
        ys_ref[...] = gate * _mm(he.astype(BF16), wdb[...])

    @pl.when(t >= nused_ref[0])
    def _():
        ys_ref[...] = jnp.zeros_like(ys_ref)


def moe_experts(xs, tile_expert, n_used, w_gate, w_up, w_down, layer, d):
    rows, cols = xs.shape
    n_tiles = rows // MOE_ROW_TILE
    wspec = lambda shape: pl.BlockSpec((1, 1) + shape, lambda t, te, nu: (layer, te[t], 0, 0))
    return pl.pallas_call(
        _expert_kernel,
        out_shape=jax.ShapeDtypeStruct((rows, d), F32),
        grid_spec=pltpu.PrefetchScalarGridSpec(
            num_scalar_prefetch=2,
            grid=(n_tiles,),
            in_specs=[pl.BlockSpec((MOE_ROW_TILE, cols),
                                   lambda t, te, nu: (jnp.minimum(t, nu[0] - 1), 0)),
                      wspec((d, D_EXPERT)), wspec((d, D_EXPERT)), wspec((D_EXPERT, d))],
            out_specs=pl.BlockSpec((MOE_ROW_TILE, d), lambda t, te, nu: (t, 0)),
            scratch_shapes=[pltpu.VMEM((d, D_EXPERT), BF16), pltpu.VMEM((d, D_EXPERT), BF16),
                            pltpu.VMEM((D_EXPERT, d), BF16)]),
        compiler_params=_cparams(("arbitrary",)),
        name="moe_experts",
    )(tile_expert, n_used, xs, w_gate, w_up, w_down)


def _combine_kernel(dest_ref, ys_hbm, x_ref, g2_ref, fn_ref, o_ref, buf, sem, *, final_norm):
    tc = x_ref.shape[0]

    for r in range(tc):
        for slot in range(2):
            pltpu.make_async_copy(ys_hbm.at[pl.ds(dest_ref[0, slot, r], 1)],
                                  buf.at[slot, pl.ds(r, 1)], sem).start()
    for slot in range(2):
        pltpu.make_async_copy(ys_hbm.at[pl.ds(0, tc)], buf.at[slot], sem).wait()
    x = x_ref[...] + g2_ref[0] * (buf[0] + buf[1])
    if final_norm:
        ms = jnp.mean(x * x, axis=-1, keepdims=True)
        x = x * lax.rsqrt(ms + NORM_EPS) * fn_ref[...]
    o_ref[...] = x


def moe_combine(ys, dest3, x1, g2, fnorm, seq, final_norm):
    n, d = x1.shape
    tc = dest3.shape[2]
    tpb = seq // tc
    return pl.pallas_call(
        functools.partial(_combine_kernel, final_norm=final_norm),
        out_shape=jax.ShapeDtypeStruct((n, d), F32),
        grid=(n // tc,),
        in_specs=[pl.BlockSpec((1, 2, tc), lambda i: (i, 0, 0), memory_space=pltpu.SMEM),
                  pl.BlockSpec(memory_space=pl.ANY),
                  pl.BlockSpec((tc, d), lambda i: (i, 0)),
                  pl.BlockSpec((1, 1, d), lambda i: (i // tpb, 0, 0)),
                  pl.BlockSpec((1, d), lambda i: (0, 0))],
        out_specs=pl.BlockSpec((tc, d), lambda i: (i, 0)),
        scratch_shapes=[pltpu.VMEM((2, tc, d), F32), pltpu.SemaphoreType.DMA(())],
        compiler_params=_cparams(("arbitrary",)),
        name="moe_combine",
    )(dest3, ys, x1, g2, fnorm)


def moe_sparse(hw, ridx, counts, w_gate, w_up, w_down, layer, x1, g2, fnorm, seq, final_norm):
    n, d = x1.shape
    dest3, tile_expert, n_used, zero_tiles, n_tiles = _dispatch_plan(ridx, counts, n)
    xs = moe_dispatch(hw, dest3, zero_tiles, n_tiles)
    ys = moe_experts(xs, tile_expert, n_used, w_gate, w_up, w_down, layer, d)
    return moe_combine(ys, dest3, x1, g2, fnorm, seq, final_norm)


def _odd_in_kernel(x_ref, g_ref, sh_ref, sc_ref, w_ref, mu_ref, w0_ref, w2_ref, a0_ref, a2_ref,
                   g2_ref, kk_ref, ka_ref, ones_ref, pw_ref, ps_ref,
                   r_ref, lw_ref, km_ref, v_ref, kn_ref, kb_ref, gg_ref, op_ref,
                   rw_carry, u_carry, *, tiles_per_seq, tm):
    i = pl.program_id(0)
    first = (i % tiles_per_seq) == 0
    h = _norm_mod(x_ref[...], g_ref[...], sh_ref[0], sc_ref[0])
    proj = _mm(h.astype(BF16), w_ref[...])

    rw = proj[:, :ODD_RW_COLS]
    row0 = jnp.where(first, 0.0, rw_carry[0:1, :])
    ridx = lax.broadcasted_iota(jnp.int32, (tm, 1), 0)
    prev = jnp.where(ridx == 0, row0, pltpu.roll(rw, 1, 0))
    rw_carry[0:1, :] = rw[tm - 1:tm, :]
    rw = rw + (prev - rw) * mu_ref[...]

    r = rw[:, 0:512]
    k = rw[:, 512:1024]
    v = rw[:, 1024:1536]
    wl = rw[:, 1536:1664]
    al = rw[:, 1664:1792]
    gl = rw[:, 1792:1920]
    z = -(w0_ref[...] + _mm(jnp.tanh(wl).astype(BF16), w2_ref[...]))
    softplus = jnp.maximum(z, 0.0) + jnp.log1p(jnp.exp(-jnp.abs(z)))
    w_log = -softplus - 0.5
    a = jax.nn.sigmoid(a0_ref[...] + _mm(al.astype(BF16), a2_ref[...]))
    gg_ref[...] = _mm(jax.nn.sigmoid(gl).astype(BF16), g2_ref[...])
    kk0 = k * kk_ref[...]
    ss = _split_sum(kk0 * kk0, ones_ref[...])
    kk = kk0 / jnp.maximum(jnp.sqrt(ss), 1e-12)
    r_ref[...] = r
    lw_ref[...] = -jnp.exp(w_log)
    km_ref[...] = k * (1.0 + (a - 1.0) * ka_ref[...])
    v_ref[...] = v
    kn_ref[...] = kk
    kb_ref[...] = kk * a

    u = proj[:, ODD_RW_COLS:]
    halo = jnp.where(first, 0.0, u_carry[...])
    u_carry[...] = u[tm - POOL_HALO:, :]
    ext = jnp.concatenate([halo, u], axis=0)
    tseq = (i % tiles_per_seq) * tm + ridx
    for gi, win in enumerate(POOL_WINDOWS):
        xg = ext[:, gi * POOL_GROUP:(gi + 1) * POOL_GROUP]
        s = xg
        step = 1
        while step < win:
            s = s + pltpu.roll(s, step, 0)
            step *= 2
        cnt = jnp.minimum(tseq + 1, win).astype(F32)
        pooled = s[POOL_HALO:] / cnt - xg[POOL_HALO:]
        mixed = _mm(pooled.astype(BF16), pw_ref[gi])
        op_ref[:, gi * POOL_GROUP:(gi + 1) * POOL_GROUP] = (
            mixed * ps_ref[:, gi * POOL_GROUP:(gi + 1) * POOL_GROUP])


def odd_in_proj(x2, g, sh, sc, w_pad, mu_pad, w0, w2p, a0, a2p, g2, k_k, k_a, ones_bd, pool_w,
                pool_scale, seq):
    n, d = x2.shape
    tm = min(TOK_TILE, seq)
    tpb = seq // tm
    row = lambda i: (i, 0)
    per_b = lambda i: (i // tpb, 0, 0)
    c2 = lambda i: (0, 0)
    full2 = lambda a: pl.BlockSpec(a.shape, c2)
    return pl.pallas_call(
        functools.partial(_odd_in_kernel, tiles_per_seq=tpb, tm=tm),
        out_shape=[jax.ShapeDtypeStruct((n, RWKV_DIM), F32)] * 8,
        grid=(n // tm,),
        in_specs=[pl.BlockSpec((tm, d), row), pl.BlockSpec((1, d), c2),
                  pl.BlockSpec((1, 1, d), per_b), pl.BlockSpec((1, 1, d), per_b),
                  full2(w_pad), full2(mu_pad), full2(w0), full2(w2p), full2(a0), full2(a2p),
                  full2(g2), full2(k_k), full2(k_a), full2(ones_bd),
                  pl.BlockSpec(pool_w.shape, lambda i: (0, 0, 0)), full2(pool_scale)],
        out_specs=[pl.BlockSpec((tm, RWKV_DIM), row)] * 8,
        scratch_shapes=[pltpu.VMEM((SUBLANES, ODD_RW_COLS), F32),
                        pltpu.VMEM((POOL_HALO, RWKV_DIM), F32)],
        compiler_params=_cparams(("arbitrary",)),
        name="odd_in_proj",
    )(x2, g, sh, sc, w_pad, mu_pad, w0, w2p, a0, a2p, g2, k_k, k_a, ones_bd, pool_w, pool_scale)


def _bmm(a, b):
    return lax.dot_general(a, b, (((2,), (1,)), ((0,), (0,))), preferred_element_type=F32)


def _bnt(a, b):
    return lax.dot_general(a, b, (((2,), (2,)), ((0,), (0,))), preferred_element_type=F32)


def _btn(a, b):
    return lax.dot_general(a, b, (((1,), (1,)), ((0,), (0,))), preferred_element_type=F32)


def _scan_prep_kernel(r_ref, lw_ref, km_ref, v_ref, kn_ref, kb_ref, qe_ref, y0_ref, mt_ref, ct_ref,
                      *, chunk, cb):
    L = chunk
    rows = cb * L
    n_pairs = N_RWKV_HEADS // 2
    two = 2 * L
    rowt = lax.broadcasted_iota(jnp.int32, (rows, 1), 0) % L
    lane = lax.broadcasted_iota(jnp.int32, (1, 1, LANES), 2)
    low = lane < HEAD_DIM
    ri = lax.broadcasted_iota(jnp.int32, (two, two), 0)
    ci = lax.broadcasted_iota(jnp.int32, (two, two), 1)
    same_blk = (ri // L) == (ci // L)
    strict = same_blk & ((ci % L) < (ri % L))
    incl = same_blk & ((ci % L) <= (ri % L))
    li = lax.broadcasted_iota(jnp.int32, (LANES, LANES), 0)
    lj = lax.broadcasted_iota(jnp.int32, (LANES, LANES), 1)
    same_head = (li // HEAD_DIM) == (lj // HEAD_DIM)
    eye = li == lj

    lw = lw_ref[...]
    cum = lw
    step = 1
    while step < L:
        cum = cum + jnp.where(rowt >= step, pltpu.roll(cum, step, 0), 0.0)
        step *= 2

    def to3(x):
        x3 = x.reshape(cb, L, RWKV_DIM)
        return jnp.concatenate([x3[:, :, p * LANES:(p + 1) * LANES] for p in range(n_pairs)], axis=0)

    def stack2(x):
        return jnp.concatenate([jnp.where(low, x, 0.0), jnp.where(low, 0.0, x)], axis=1)

    def fold(x):
        return x[:, :L, :] + x[:, L:, :]

    cum3 = to3(cum)
    lw3 = to3(lw)
    cum_l = cum3[:, L - 1:L, :]
    g_inv = jnp.exp(-cum3)
    g_tail = jnp.exp(cum_l - cum3)
    kb = to3(kb_ref[...])
    km = to3(km_ref[...])
    v = to3(v_ref[...])
    at_s = stack2(-to3(kn_ref[...]) * jnp.exp(cum3 - lw3))
    rt_s = stack2(to3(r_ref[...]) * jnp.exp(cum3))
    v_s = stack2(v).astype(BF16)
    lhs = jnp.concatenate([at_s, rt_s], axis=1).astype(BF16)
    rhs = jnp.concatenate([stack2(kb * g_inv), stack2(km * g_inv)], axis=1).astype(BF16)
    prod = _bnt(lhs, rhs)
    nmat = jnp.where(strict, prod[:, :two, :two], 0.0)
    a_ak = jnp.where(strict, prod[:, :two, two:], 0.0).astype(BF16)
    a_rb = jnp.where(incl, prod[:, two:, :two], 0.0).astype(BF16)
    a_rk = jnp.where(incl, prod[:, two:, two:], 0.0).astype(BF16)

    x = jnp.concatenate([at_s, _bmm(a_ak, v_s)], axis=2)
    npow = nmat
    step = 1
    while step < L:
        nb = npow.astype(BF16)
        x = x + _bmm(nb, x.astype(BF16))
        step *= 2
        if step < L:
            npow = _bmm(nb, nb)
    qy = _bmm(a_rb, x.astype(BF16))
    qe = fold(rt_s + qy[:, :, :LANES])
    y0 = fold(qy[:, :, LANES:] + _bmm(a_rk, v_s))
    wu = fold(x).astype(BF16)
    bwu = _btn((kb * g_tail).astype(BF16), wu)
    kv = _btn((km * g_tail).astype(BF16), v.astype(BF16))
    g_l = jnp.broadcast_to(jnp.exp(cum_l), (n_pairs * cb, LANES, LANES))
    mt = jnp.where(eye, g_l, 0.0) + jnp.where(same_head, bwu[:, :, :LANES], 0.0)
    ct = jnp.where(same_head, bwu[:, :, LANES:] + kv, 0.0)
    for p in range(n_pairs):
        sl = slice(p * LANES, (p + 1) * LANES)
        qe_ref[:, sl] = qe[p * cb:(p + 1) * cb].reshape(rows, LANES)
        y0_ref[:, sl] = y0[p * cb:(p + 1) * cb].reshape(rows, LANES)
        mt_ref[:, p] = mt[p * cb:(p + 1) * cb].astype(BF16)
        ct_ref[:, p] = ct[p * cb:(p + 1) * cb]


def _scan_state_kernel(qe_ref, y0_ref, mt_ref, ct_ref, y_ref, st_ref, *, batch):
    @pl.when(pl.program_id(0) == 0)
    def _():
        st_ref[...] = jnp.zeros_like(st_ref)

    n_pairs = N_RWKV_HEADS // 2
    qe = qe_ref[...]
    qe3 = jnp.concatenate([qe[:, :, p * LANES:(p + 1) * LANES] for p in range(n_pairs)], axis=0)
    st = st_ref[...].astype(BF16)
    y = _bmm(qe3.astype(BF16), st)
    for p in range(n_pairs):
        sl = slice(p * LANES, (p + 1) * LANES)
        y_ref[:, :, sl] = y[p * batch:(p + 1) * batch] + y0_ref[:, :, sl]
    mt = jnp.concatenate([mt_ref[:, 0, p] for p in range(n_pairs)], axis=0)
    ct = jnp.concatenate([ct_ref[:, 0, p] for p in range(n_pairs)], axis=0)
    st_ref[...] = _bmm(mt, st) + ct


def rwkv_scan(r, lw, km, v, kn, kb, batch, seq):
    n = batch * seq
    chunk = min(SCAN_CHUNK, seq)
    nc = seq // chunk
    cb = min(SCAN_CHUNKS_PER_STEP, nc)
    n_pairs = N_RWKV_HEADS // 2
    blk = pl.BlockSpec((cb * chunk, RWKV_DIM), lambda i: (i, 0))
    mblk = pl.BlockSpec((cb, n_pairs, LANES, LANES), lambda i: (i, 0, 0, 0))
    qe, y0, mt, ct = pl.pallas_call(
        functools.partial(_scan_prep_kernel, chunk=chunk, cb=cb),
        out_shape=[jax.ShapeDtypeStruct((n, RWKV_DIM), F32), jax.ShapeDtypeStruct((n, RWKV_DIM), F32),
                   jax.ShapeDtypeStruct((n // chunk, n_pairs, LANES, LANES), BF16),
                   jax.ShapeDtypeStruct((n // chunk, n_pairs, LANES, LANES), F32)],
        grid=(n // (cb * chunk),),
        in_specs=[blk] * 6,
        out_specs=[blk, blk, mblk, mblk],
        compiler_params=_cparams(("parallel",)),
        name="rwkv_scan_prep",
    )(r, lw, km, v, kn, kb)
    sblk = pl.BlockSpec((batch, chunk, RWKV_DIM), lambda c: (0, c, 0))
    smblk = pl.BlockSpec((batch, 1, n_pairs, LANES, LANES), lambda c: (0, c, 0, 0, 0))
    y = pl.pallas_call(
        functools.partial(_scan_state_kernel, batch=batch),
        out_shape=jax.ShapeDtypeStruct((batch, seq, RWKV_DIM), F32),
        grid=(nc,),
        in_specs=[sblk, sblk, smblk, smblk],
        out_specs=sblk,
        scratch_shapes=[pltpu.VMEM((n_pairs * batch, LANES, LANES), F32)],
        compiler_params=_cparams(("arbitrary",)),
        name="rwkv_scan_state",
    )(qe.reshape(batch, seq, RWKV_DIM), y0.reshape(batch, seq, RWKV_DIM),
      mt.reshape(batch, nc, n_pairs, LANES, LANES), ct.reshape(batch, nc, n_pairs, LANES, LANES))
    return y.reshape(n, RWKV_DIM)


def _odd_out_kernel(y_ref, r_ref, km_ref, v_ref, gg_ref, op_ref, rk_ref, lnw_ref, lnb_ref, ones_ref,
                    *tail_refs):
    ones = ones_ref[...]
    inv = 1.0 / HEAD_DIM
    y = y_ref[...]
    mean = _split_sum(y, ones) * inv
    yc = y - mean
    var = _split_sum(yc * yc, ones) * inv
    yn = yc * lax.rsqrt(var + LNX_EPS) * lnw_ref[...] + lnb_ref[...]
    bonus = _split_sum(r_ref[...] * km_ref[...] * rk_ref[...], ones) * v_ref[...]
    o_rwkv = (yn + bonus) * gg_ref[...]
    cat = jnp.concatenate([o_rwkv, op_ref[...]], axis=1).astype(BF16)
    _tail(cat, *tail_refs)


def odd_out_proj(y, r, km, v, gg, opool, r_k, lnx_w, lnx_b, ones_bd, x2, g1, w_out, ng, sh2, sc2,
                 rwt, rb, seq):
    n, d = x2.shape
    tm = min(TOK_TILE, seq)
    tpb = seq // tm
    row = lambda i: (i, 0)
    c2 = lambda i: (0, 0)
    act = pl.BlockSpec((tm, RWKV_DIM), row)
    vec = pl.BlockSpec((1, RWKV_DIM), c2)
    tin, tout, tscratch = _tail_specs(tm, d, tpb)
    return pl.pallas_call(
        _odd_out_kernel,
        out_shape=_tail_out_shapes(n, d),
        grid=(n // tm,),
        in_specs=[act] * 6 + [vec, vec, vec, pl.BlockSpec(ones_bd.shape, c2)] + tin,
        out_specs=tout,
        scratch_shapes=tscratch,
        compiler_params=_cparams(("arbitrary",)),
        name="odd_out_proj",
    )(y, r, km, v, gg, opool, r_k, lnx_w, lnx_b, ones_bd, x2, g1, w_out, ng, sh2, sc2, rwt, rb,
      _strict_upper(tm))


def _rope_tables(seq):
    half = HEAD_DIM // 2
    inv = ROPE_THETA ** (-jnp.arange(half, dtype=F32) / half)
    ang = jnp.arange(seq, dtype=F32)[:, None] * inv[None, :]
    return jnp.tile(jnp.cos(ang), (1, LANES // half)), jnp.tile(jnp.sin(ang), (1, LANES // half))


def _even_w_pad(w_in):
    d = w_in.shape[0]
    q_kv = w_in[:, :NSA_DIM + 6 * KV_DIM]
    gl = w_in[:, NSA_DIM + 6 * KV_DIM:NSA_DIM + 6 * KV_DIM + 24]
    rest = w_in[:, NSA_DIM + 6 * KV_DIM + 24:]
    z = jnp.zeros((d, LANES - 12), w_in.dtype)
    return jnp.concatenate([q_kv, gl[:, :12], z, gl[:, 12:], z, rest], axis=1).astype(BF16)


def _compress_params(cmp_pos, cmp_w1, cmp_w2):
    eye = jnp.eye(N_KV_HEADS, dtype=F32)
    w1r = cmp_w1.reshape(2, 2, CMP_STRIDE, HEAD_DIM, CMP_HIDDEN)
    w1_ext = jnp.einsum('kpmdn,gh->kpmgdhn', w1r, eye).reshape(
        2, 2, CMP_STRIDE * KV_DIM, N_KV_HEADS * CMP_HIDDEN).astype(BF16)
    w2_ext = jnp.einsum('knd,gh->kgnhd', cmp_w2, eye).reshape(
        2, N_KV_HEADS * CMP_HIDDEN, KV_DIM).astype(BF16)
    pos = cmp_pos.reshape(2, 2, CMP_STRIDE, 1, HEAD_DIM)
    pos_ext = jnp.broadcast_to(pos, (2, 2, CMP_STRIDE, N_KV_HEADS, HEAD_DIM)).reshape(
        2, 2, 1, CMP_STRIDE * KV_DIM)
    return pos_ext, w1_ext, w2_ext


def _nsa_tables(seq):
    n_blk = seq // SEL_BLOCK
    n_cmp = (seq - CMP_BLOCK) // CMP_STRIDE + 1
    n_cmp_pad = seq // CMP_STRIDE
    r = SEL_BLOCK // CMP_STRIDE
    c = CMP_BLOCK // CMP_STRIDE
    msel = np.zeros((n_cmp_pad, LANES), np.float32)
    for j in range(n_blk):
        for m in range(r):
            for n in range(c):
                idx = r * j + m + n
                if idx < n_cmp:
                    msel[idx, j] += 1.0
    chunk = min(SEL_CHUNK, seq)
    kblk = np.arange(seq) // SEL_BLOCK
    eexp = (kblk[None, :] == np.arange(LANES)[:, None]).astype(np.float32)
    eexp = eexp.reshape(LANES, seq // chunk, chunk).transpose(1, 0, 2)
    selq = np.zeros((N_KV_HEADS, GQA, GQA * HEAD_DIM, LANES), np.float32)
    for h in range(N_KV_HEADS):
        for g in range(GQA):
            for dd in range(HEAD_DIM):
                selq[h, g, g * HEAD_DIM + dd, h * HEAD_DIM + dd] = 1.0
    return jnp.asarray(msel.T), jnp.asarray(eexp, dtype=BF16), jnp.asarray(selq, dtype=BF16)


def _odd_params(w_in, mu, w2, a2):
    d = w_in.shape[0]
    z64 = jnp.zeros((d, 64), w_in.dtype)
    w_pad = jnp.concatenate([w_in[:, :1536], w_in[:, 1536:1600], z64, w_in[:, 1600:1664], z64,
                             w_in[:, 1664:]], axis=1).astype(BF16)
    m64 = jnp.zeros((64,), mu.dtype)
    mu_pad = jnp.concatenate([mu[:1536], mu[1536:1600], m64, mu[1600:1664], m64, mu[1664:]])[None, :]
    zr = jnp.zeros((64, RWKV_DIM), w2.dtype)
    w2p = jnp.concatenate([w2, zr], axis=0).astype(BF16)
    a2p = jnp.concatenate([a2, zr], axis=0).astype(BF16)
    return w_pad, mu_pad, w2p, a2p


def _head_ones():
    idx = np.arange(RWKV_DIM) // HEAD_DIM
    return jnp.asarray((idx[:, None] == idx[None, :]).astype(np.float32), dtype=BF16)


def kernel(x, c, ada_w, ada_b, norm_mix, norm_ffn, even_w_in, even_cmp_pos, even_cmp_w1, even_cmp_w2,
           even_conv_w, even_w_out, odd_w_in, odd_mu, odd_w0, odd_w2, odd_a0, odd_a2, odd_g2, odd_k_k,
           odd_k_a, odd_r_k, odd_lnx_w, odd_lnx_b, odd_pool_w, odd_pool_scale, odd_w_out,
           router_w, router_b, moe_w_gate, moe_w_up, moe_w_down, final_norm):
    batch, seq, d = x.shape
    n = batch * seq
    depth = ada_w.shape[0]
    x2 = x.reshape(n, d)
    mod = ada_modulation(c, ada_w, ada_b)
    rwt = router_w.T
    rb = router_b.reshape(N_EXPERTS, 1)
    fnorm = final_norm.reshape(1, d)
    cos, sin = _rope_tables(seq)
    msel, eexp, selq = _nsa_tables(seq)
    ones_bd = _head_ones()

    for layer in range(depth):
        m = mod[layer].reshape(batch, 6, 1, d)
        sh1, sc1, g1, sh2, sc2, g2 = (m[:, k] for k in range(6))
        ng_mix = norm_mix[layer].reshape(1, d)
        ng_ffn = norm_ffn[layer].reshape(1, d)
        i = layer // 2
        if layer % 2 == 0:
            (qn, qr, kc, vc, ks, vs, kw, vw, gate, u, bg) = even_in_proj(
                x2, ng_mix, sh1, sc1, _even_w_pad(even_w_in[i]), cos, sin, seq)
            pos_ext, w1_ext, w2_ext = _compress_params(even_cmp_pos[i], even_cmp_w1[i], even_cmp_w2[i])
            kcmp, vcmp = compress_kv(kc, vc, pos_ext, w1_ext, w2_ext, batch, seq)
            o_nsa = nsa_attention(qn, qr, kcmp, vcmp, ks.reshape(batch, seq, KV_DIM),
                                  vs.reshape(batch, seq, KV_DIM), kw.reshape(batch, seq, KV_DIM),
                                  vw.reshape(batch, seq, KV_DIM), gate, selq, msel, eexp, batch, seq)
            x1, hw, ridx, counts = even_out_proj(o_nsa, u, bg, even_conv_w[i], x2, g1,
                                                 even_w_out[i].astype(BF16), ng_ffn, sh2, sc2, rwt, rb, seq)
        else:
            w_pad, mu_pad, w2p, a2p = _odd_params(odd_w_in[i], odd_mu[i], odd_w2[i], odd_a2[i])
            vec = lambda a: a.reshape(1, RWKV_DIM)
            (r, lw, km, v, kn, kb, gg, opool) = odd_in_proj(
                x2, ng_mix, sh1, sc1, w_pad, mu_pad, vec(odd_w0[i]), w2p, vec(odd_a0[i]), a2p,
                odd_g2[i].astype(BF16), vec(odd_k_k[i]), vec(odd_k_a[i]), ones_bd,
                odd_pool_w[i].astype(BF16), vec(odd_pool_scale[i]), seq)
            y = rwkv_scan(r, lw, km, v, kn, kb, batch, seq)
            x1, hw, ridx, counts = odd_out_proj(
                y, r, km, v, gg, opool, vec(odd_r_k[i]), vec(odd_lnx_w[i]), vec(odd_lnx_b[i]), ones_bd,
                x2, g1, odd_w_out[i].astype(BF16), ng_ffn, sh2, sc2, rwt, rb, seq)
        x2 = moe_sparse(hw, ridx, counts, moe_w_gate, moe_w_up, moe_w_down, layer,
                        x1, g2, fnorm, seq, final_norm=(layer == depth - 1))
    return x2.reshape(batch, seq, d)
```

```python
import functools

import jax
import jax.numpy as jnp
import numpy as np
from jax import lax
from jax.experimental import pallas as pl
from jax.experimental.pallas import tpu as pltpu

F32 = jnp.float32
BF16 = jnp.bfloat16
HIGHEST = lax.Precision.HIGHEST

D_MODEL = 1024
DEPTH = 2
HEAD_DIM = 64
ROPE_THETA = 10000.0
NORM_EPS = 1e-6
NEG_INF = -1e30
BIG = 1e9
NSA_DIM = 512
N_KV_HEADS = 2
GQA = 4
KV_DIM = 128
CMP_BLOCK = 32
CMP_STRIDE = 16
CMP_HIDDEN = 256
SEL_BLOCK = 64
N_SEL = 8
N_LOCAL = 2
WINDOW = 512
Q_BLOCK = 128
ATTN_SCALE = HEAD_DIM ** -0.5
CONV_DIM = 512
RWKV_DIM = 512
N_RWKV_HEADS = 8
LNX_EPS = 64e-5
POOL_WINDOWS = (2, 4, 8, 16)
POOL_GROUP = 128
N_EXPERTS = 16
N_EXPERT_GROUPS = 4
EXPERTS_PER_GROUP = 4
D_EXPERT = 512
PAIRS_PER_GROUP = 6
PAIR_LO = (0, 0, 0, 1, 1, 2)
PAIR_HI = (1, 2, 3, 3, 2, 3)
N_CLASSES = N_EXPERT_GROUPS * PAIRS_PER_GROUP
CLASS_ROWS = 32

LANES = 128
SUBLANES = 8
VMEM_LIMIT = 56 * 1024 * 1024

TOK_TILE = 512
MOE_ROW_TILE = 256
MOE_DMA_TILE = 256
SEL_CHUNK = 512
SCAN_CHUNK = 64
SCAN_CHUNKS_PER_STEP = 4
CONV_HALO = 8
POOL_HALO = 16

EVEN_PAD_COLS = 3072
ODD_PAD_COLS = 2432
ODD_RW_COLS = 1920


def _cparams(sem):
    return pltpu.CompilerParams(dimension_semantics=sem, vmem_limit_bytes=VMEM_LIMIT)


def _nt(a, b, precision=None):
    return lax.dot_general(a, b, (((1,), (1,)), ((), ())), preferred_element_type=F32,
                           precision=precision)


def _tn(a, b):
    return lax.dot_general(a, b, (((0,), (0,)), ((), ())), preferred_element_type=F32)


def _mm(a, b, precision=None):
    return jnp.dot(a, b, preferred_element_type=F32, precision=precision)


def _norm_mod(x, g, sh, sc):
    ms = jnp.mean(x * x, axis=-1, keepdims=True)
    return (x * lax.rsqrt(ms + NORM_EPS) * g) * (1.0 + sc) + sh


def _split_sum(x, ones_bf16):
    hi = x.astype(BF16)
    lo = (x - hi.astype(F32)).astype(BF16)
    return _mm(hi, ones_bf16) + _mm(lo, ones_bf16)


def _ada_kernel(c_ref, w_ref, b_ref, o_ref):
    c = c_ref[...]
    cond = c * jax.nn.sigmoid(c)
    o_ref[0] = _mm(cond, w_ref[0], precision=HIGHEST) + b_ref[0]


def ada_modulation(c, ada_w, ada_b):
    depth, d, cols = ada_w.shape
    b = c.shape[0]
    tn = 1536
    return pl.pallas_call(
        _ada_kernel,
        out_shape=jax.ShapeDtypeStruct((depth, b, cols), F32),
        grid=(depth, cols // tn),
        in_specs=[pl.BlockSpec((b, d), lambda l, j: (0, 0)),
                  pl.BlockSpec((1, d, tn), lambda l, j: (l, 0, j)),
                  pl.BlockSpec((1, 1, tn), lambda l, j: (l, 0, j))],
        out_specs=pl.BlockSpec((1, b, tn), lambda l, j: (l, 0, j)),
        compiler_params=_cparams(("parallel", "parallel")),
        name="ada_modulation",
    )(c, ada_w, ada_b.reshape(depth, 1, cols))


def _rope128(t, cos, sin, lane):
    rot = jnp.where((lane % HEAD_DIM) < HEAD_DIM // 2,
                    -pltpu.roll(t, LANES - HEAD_DIM // 2, 1), pltpu.roll(t, HEAD_DIM // 2, 1))
    return t * cos + rot * sin


def _even_in_kernel(x_ref, g_ref, sh_ref, sc_ref, w_ref, cos_ref, sin_ref,
                    qn_ref, qr_ref, kc_ref, vc_ref, ks_ref, vs_ref, kw_ref, vw_ref,
                    gate_ref, u_ref, bg_ref):
    h = _norm_mod(x_ref[...], g_ref[...], sh_ref[0], sc_ref[0])
    proj = _mm(h.astype(BF16), w_ref[...])
    cos = cos_ref[...]
    sin = sin_ref[...]
    lane = lax.broadcasted_iota(jnp.int32, (1, LANES), 1)
    for i in range(NSA_DIM // LANES):
        q = proj[:, i * LANES:(i + 1) * LANES] * ATTN_SCALE
        qn_ref[:, i * LANES:(i + 1) * LANES] = q.astype(BF16)
        qr_ref[:, i * LANES:(i + 1) * LANES] = _rope128(q, cos, sin, lane).astype(BF16)
    o = NSA_DIM
    kc_ref[...] = proj[:, o:o + 128]
    vc_ref[...] = proj[:, o + 128:o + 256]
    ks_ref[...] = _rope128(proj[:, o + 256:o + 384], cos, sin, lane).astype(BF16)
    vs_ref[...] = proj[:, o + 384:o + 512].astype(BF16)
    kw_ref[...] = _rope128(proj[:, o + 512:o + 640], cos, sin, lane).astype(BF16)
    vw_ref[...] = proj[:, o + 640:o + 768].astype(BF16)
    o += 768
    gate_ref[...] = jax.nn.sigmoid(proj[:, o:o + 256])
    o += 256
    xb = proj[:, o:o + 512]
    bg_ref[...] = proj[:, o + 512:o + 1024]
    u_ref[...] = proj[:, o + 1024:o + 1536] * xb


def even_in_proj(x2, g, sh, sc, w_pad, cos, sin, seq):
    n, d = x2.shape
    tm = min(TOK_TILE, seq)
    tpb = seq // tm
    row = lambda i: (i, 0)
    per_b = lambda i: (i // tpb, 0, 0)
    pos = lambda i: (i % tpb, 0)
    outs = [((n, 512), BF16), ((n, 512), BF16), ((n, 128), F32), ((n, 128), F32),
            ((n, 128), BF16), ((n, 128), BF16), ((n, 128), BF16), ((n, 128), BF16),
            ((n, 256), F32), ((n, 512), F32), ((n, 512), F32)]
    return pl.pallas_call(
        _even_in_kernel,
        out_shape=[jax.ShapeDtypeStruct(s, t) for s, t in outs],
        grid=(n // tm,),
        in_specs=[pl.BlockSpec((tm, d), row),
                  pl.BlockSpec((1, d), lambda i: (0, 0)),
                  pl.BlockSpec((1, 1, d), per_b),
                  pl.BlockSpec((1, 1, d), per_b),
                  pl.BlockSpec((d, EVEN_PAD_COLS), lambda i: (0, 0)),
                  pl.BlockSpec((tm, LANES), pos),
                  pl.BlockSpec((tm, LANES), pos)],
        out_specs=[pl.BlockSpec((tm, s[1]), row) for s, _ in outs],
        compiler_params=_cparams(("parallel",)),
        name="even_in_proj",
    )(x2, g, sh, sc, w_pad, cos, sin)


def _compress_kernel(k_ref, v_ref, pos_ref, w1_ref, w2_ref, ko_ref, vo_ref):
    for j, (src, dst) in enumerate(((k_ref, ko_ref), (v_ref, vo_ref))):
        xr = src[0]
        n_rows = xr.shape[0]
        a0 = _mm((xr + pos_ref[j, 0]).astype(BF16), w1_ref[j, 0])
        a1 = _mm((xr + pos_ref[j, 1]).astype(BF16), w1_ref[j, 1])
        hid = a0 + pltpu.roll(a1, n_rows - 1, 0)
        hid = jax.nn.gelu(hid)
        dst[0] = _mm(hid.astype(BF16), w2_ref[j]).astype(BF16)


def compress_kv(kc, vc, pos_ext, w1_ext, w2_ext, batch, seq):
    rows = seq // CMP_STRIDE
    width = CMP_STRIDE * KV_DIM
    kr = kc.reshape(batch, rows, width)
    vr = vc.reshape(batch, rows, width)
    blk = pl.BlockSpec((1, rows, width), lambda b: (b, 0, 0))
    oblk = pl.BlockSpec((1, rows, KV_DIM), lambda b: (b, 0, 0))
    return pl.pallas_call(
        _compress_kernel,
        out_shape=[jax.ShapeDtypeStruct((batch, rows, KV_DIM), BF16)] * 2,
        grid=(batch,),
        in_specs=[blk, blk,
                  pl.BlockSpec(pos_ext.shape, lambda b: (0, 0, 0, 0)),
                  pl.BlockSpec(w1_ext.shape, lambda b: (0, 0, 0, 0)),
                  pl.BlockSpec(w2_ext.shape, lambda b: (0, 0, 0))],
        out_specs=[oblk, oblk],
        compiler_params=_cparams(("parallel",)),
        name="compress_kv",
    )(kr, vr, pos_ext, w1_ext, w2_ext)


def _safe_inv(l):
    return jnp.where(l > 0.0, 1.0 / jnp.where(l > 0.0, l, 1.0), 0.0)


def _nsa_kernel(qn_ref, qr_ref, kc_ref, vc_ref, ks_ref, vs_ref, kw_ref, vw_ref, gate_ref,
                selq_ref, mselt_ref, eexp_ref, o_ref, *, seq, n_sel, sel_chunk, win_len):
    h = pl.program_id(1)
    qt = pl.program_id(2)
    t0 = qt * Q_BLOCK
    n_blk = seq // SEL_BLOCK
    n_cmp_pad = seq // CMP_STRIDE
    rows = GQA * Q_BLOCK
    tpos = t0 + lax.broadcasted_iota(jnp.int32, (1, Q_BLOCK, 1), 1)
    lane = lax.broadcasted_iota(jnp.int32, (1, LANES), 1)
    head_lanes = (lane // HEAD_DIM) == h

    qn = qn_ref[...]
    qr = qr_ref[...]
    qn4 = jnp.concatenate([_mm(qn, selq_ref[0, g]) for g in range(GQA)], axis=0).astype(BF16)
    qr4 = jnp.concatenate([_mm(qr, selq_ref[0, g]) for g in range(GQA)], axis=0).astype(BF16)

    kc = kc_ref[0]
    vc = vc_ref[0]
    cpos = lax.broadcasted_iota(jnp.int32, (1, 1, n_cmp_pad), 2) * CMP_STRIDE + (CMP_BLOCK - 1)
    cmask = cpos <= tpos
    s = jnp.where(cmask, _nt(qn4, kc).reshape(GQA, Q_BLOCK, n_cmp_pad), NEG_INF)
    e = jnp.where(cmask, jnp.exp(s - jnp.max(s, axis=2, keepdims=True)), 0.0)
    p = e * _safe_inv(jnp.sum(e, axis=2, keepdims=True))
    imp = jnp.sum(p, axis=0)
    o_cmp = _mm(p.reshape(rows, n_cmp_pad).astype(BF16), vc)

    pslc = _nt(mselt_ref[...], imp, precision=HIGHEST)[:n_blk]
    tq = t0 + lax.broadcasted_iota(jnp.int32, (1, Q_BLOCK), 1)
    jblk = lax.broadcasted_iota(jnp.int32, (n_blk, 1), 0)
    cur = tq // SEL_BLOCK
    valid = jblk * SEL_BLOCK <= tq
    forced = (jblk == 0) | ((cur - jblk >= 0) & (cur - jblk < N_LOCAL))
    score = jnp.where(forced, BIG, jnp.where(valid, pslc, -BIG))
    rank = jnp.zeros((n_blk, Q_BLOCK), jnp.int32)
    for jp in range(n_blk):
        row = score[jp:jp + 1, :]
        beats = (row > score) | ((row == score) & (jblk > jp))
        rank = rank + beats.astype(jnp.int32)
    sel = jnp.where((rank < n_sel) & (score > -0.5 * BIG), 1.0, 0.0)
    sel_t = jnp.concatenate([sel, jnp.zeros((LANES - n_blk, Q_BLOCK), F32)], axis=0).astype(BF16)

    n_chunks = (t0 + Q_BLOCK - 1) // sel_chunk + 1

    def sel_body(c, carry):
        m, l, acc = carry
        start = pl.multiple_of(c * sel_chunk, sel_chunk)
        kblk = ks_ref[0, pl.ds(start, sel_chunk), :]
        vblk = vs_ref[0, pl.ds(start, sel_chunk), :]
        kpos = start + lax.broadcasted_iota(jnp.int32, (1, 1, sel_chunk), 2)
        mask = (_tn(sel_t, eexp_ref[c]) > 0.5)[None] & (kpos <= tpos)
        s = jnp.where(mask, _nt(qr4, kblk).reshape(GQA, Q_BLOCK, sel_chunk), NEG_INF)
        m_new = jnp.maximum(m, jnp.max(s, axis=2, keepdims=True))
        alpha = jnp.exp(m - m_new)
        p = jnp.exp(s - m_new)
        l_new = alpha * l + jnp.sum(p, axis=2, keepdims=True)
        pv = _mm(p.reshape(rows, sel_chunk).astype(BF16), vblk).reshape(GQA, Q_BLOCK, LANES)
        return m_new, l_new, alpha * acc + pv

    init = (jnp.full((GQA, Q_BLOCK, 1), NEG_INF, F32), jnp.zeros((GQA, Q_BLOCK, 1), F32),
            jnp.zeros((GQA, Q_BLOCK, LANES), F32))
    _, l, acc = lax.fori_loop(0, n_chunks, sel_body, init)
    o_slc = acc * _safe_inv(l)

    ws = pl.multiple_of(jnp.maximum(qt - WINDOW // Q_BLOCK, 0) * Q_BLOCK, Q_BLOCK)
    kwb = kw_ref[0, pl.ds(ws, win_len), :]
    vwb = vw_ref[0, pl.ds(ws, win_len), :]
    diff = tpos - (ws + lax.broadcasted_iota(jnp.int32, (1, 1, win_len), 2))
    wmask = (diff >= 0) & (diff < WINDOW)
    s = jnp.where(wmask, _nt(qr4, kwb).reshape(GQA, Q_BLOCK, win_len), NEG_INF)
    e = jnp.exp(s - jnp.max(s, axis=2, keepdims=True))
    o_win = (_mm(e.reshape(rows, win_len).astype(BF16), vwb).reshape(GQA, Q_BLOCK, LANES)
             * _safe_inv(jnp.sum(e, axis=2, keepdims=True)))

    gate = gate_ref[...]
    o_cmp = o_cmp.reshape(GQA, Q_BLOCK, LANES)
    og = []
    for g in range(GQA):
        o = (gate[:, 3 * g:3 * g + 1] * o_cmp[g] + gate[:, 3 * g + 1:3 * g + 2] * o_slc[g]
             + gate[:, 3 * g + 2:3 * g + 3] * o_win[g])
        og.append(jnp.where(head_lanes, o, pltpu.roll(o, HEAD_DIM, 1)))
    low = lane < HEAD_DIM
    o_ref[:, 0:LANES] = jnp.where(low, og[0], og[1])
    o_ref[:, LANES:2 * LANES] = jnp.where(low, og[2], og[3])


def nsa_attention(qn, qr, kcmp, vcmp, ks, vs, kw, vw, gate, selq, msel, eexp, batch, seq):
    n = batch * seq
    nq = seq // Q_BLOCK
    sel_chunk = min(SEL_CHUNK, seq)
    win_len = min(WINDOW + Q_BLOCK, seq)
    n_sel = min(N_SEL, seq // SEL_BLOCK)
    qspec = pl.BlockSpec((Q_BLOCK, GQA * HEAD_DIM), lambda b, h, q: (b * nq + q, h))
    cspec = pl.BlockSpec((1, seq // CMP_STRIDE, KV_DIM), lambda b, h, q: (b, 0, 0))
    kspec = pl.BlockSpec((1, seq, KV_DIM), lambda b, h, q: (b, 0, 0))
    kern = functools.partial(_nsa_kernel, seq=seq, n_sel=n_sel, sel_chunk=sel_chunk, win_len=win_len)
    return pl.pallas_call(
        kern,
        out_shape=jax.ShapeDtypeStruct((n, NSA_DIM), F32),
        grid=(batch, N_KV_HEADS, nq),
        in_specs=[qspec, qspec, cspec, cspec, kspec, kspec, kspec, kspec,
                  pl.BlockSpec((Q_BLOCK, LANES), lambda b, h, q: (b * nq + q, h)),
                  pl.BlockSpec((1, GQA, GQA * HEAD_DIM, LANES), lambda b, h, q: (h, 0, 0, 0)),
                  pl.BlockSpec(msel.shape, lambda b, h, q: (0, 0)),
                  pl.BlockSpec(eexp.shape, lambda b, h, q: (0, 0, 0))],
        out_specs=qspec,
        compiler_params=_cparams(("parallel", "parallel", "arbitrary")),
        name="nsa_attention",
    )(qn, qr, kcmp, vcmp, ks, vs, kw, vw, gate, selq, msel, eexp)


def _route(h2, rwt_ref, rb_ref):
    rw = rwt_ref[...]
    rw_hi = rw.astype(BF16)
    rw_lo = (rw - rw_hi.astype(F32)).astype(BF16)
    h_hi = h2.astype(BF16)
    h_lo = (h2 - h_hi.astype(F32)).astype(BF16)
    logits = _nt(rw_hi, h_hi) + (_nt(rw_hi, h_lo) + _nt(rw_lo, h_hi))
    scores = jax.nn.sigmoid(logits)
    biased = scores + rb_ref[...]
    rows = [biased[e:e + 1, :] for e in range(N_EXPERTS)]
    srow = [scores[e:e + 1, :] for e in range(N_EXPERTS)]
    gscore = []
    for gi in range(N_EXPERT_GROUPS):
        r = rows[gi * EXPERTS_PER_GROUP:(gi + 1) * EXPERTS_PER_GROUP]
        best = None
        for a in range(EXPERTS_PER_GROUP):
            for b in range(a + 1, EXPERTS_PER_GROUP):
                pair = r[a] + r[b]
                best = pair if best is None else jnp.maximum(best, pair)
        gscore.append(best)
    top_val = gscore[0]
    top_grp = jnp.zeros_like(top_val, dtype=jnp.int32)
    for gi in range(1, N_EXPERT_GROUPS):
        upd = gscore[gi] > top_val
        top_grp = jnp.where(upd, gi, top_grp)
        top_val = jnp.where(upd, gscore[gi], top_val)
    masked = [jnp.where(top_grp == e // EXPERTS_PER_GROUP, rows[e], NEG_INF) for e in range(N_EXPERTS)]
    b1 = masked[0]
    i1 = jnp.zeros_like(top_grp)
    for e in range(1, N_EXPERTS):
        upd = masked[e] > b1
        i1 = jnp.where(upd, e, i1)
        b1 = jnp.where(upd, masked[e], b1)
    b2 = None
    i2 = None
    for e in range(N_EXPERTS):
        v = jnp.where(i1 == e, -jnp.inf, masked[e])
        if b2 is None:
            b2, i2 = v, jnp.zeros_like(top_grp)
        else:
            upd = v > b2
            i2 = jnp.where(upd, e, i2)
            b2 = jnp.where(upd, v, b2)
    s1 = jnp.zeros_like(top_val)
    s2 = jnp.zeros_like(top_val)
    for e in range(N_EXPERTS):
        s1 = s1 + jnp.where(i1 == e, srow[e], 0.0)
        s2 = s2 + jnp.where(i2 == e, srow[e], 0.0)
    tot = s1 + s2
    return i1, i2, s1 / tot, s2 / tot


def _tail(cat_bf16, x_ref, g1_ref, wout_ref, ng_ref, sh2_ref, sc2_ref, rwt_ref, rb_ref, ustrict_ref,
          x1_ref, hw_ref, ridx_ref, cnt_ref, carry_ref):
    y = _mm(cat_bf16, wout_ref[...])
    x1 = x_ref[...] + g1_ref[0] * y
    x1_ref[...] = x1
    h2 = _norm_mod(x1, ng_ref[...], sh2_ref[0], sc2_ref[0])
    tm, d = h2.shape
    hw_ref[:, :d] = h2

    i1, i2, w1, w2 = _route(h2, rwt_ref, rb_ref)
    lo = jnp.minimum(i1, i2) % EXPERTS_PER_GROUP
    hi = jnp.maximum(i1, i2) % EXPERTS_PER_GROUP
    pair = jnp.where(lo == 0, hi - 1, jnp.where(lo == 1, jnp.where(hi == 3, 3, 4), 5))
    cls = (i1 // EXPERTS_PER_GROUP) * PAIRS_PER_GROUP + pair
    w_lo = jnp.where(i1 < i2, w1, w2)
    w_hi = jnp.where(i1 < i2, w2, w1)
    meta_t = jnp.concatenate([w_lo, w_hi, jnp.zeros((LANES - 2, tm), F32)], axis=0)
    hw_ref[:, d:] = meta_t.T

    @pl.when(pl.program_id(0) == 0)
    def _():
        carry_ref[...] = jnp.zeros_like(carry_ref)

    hit = lax.broadcasted_iota(jnp.int32, (CLASS_ROWS, 1), 0) == cls
    cnt = jnp.where(hit, 1.0, 0.0)
    before = _mm(cnt.astype(BF16), ustrict_ref[...]) + carry_ref[:, 0:1]
    carry = carry_ref[...] + jnp.sum(cnt, axis=1, keepdims=True)
    carry_ref[...] = carry
    cnt_ref[...] = carry.astype(jnp.int32)
    ridx_ref[0:1, :] = cls
    ridx_ref[1:2, :] = jnp.sum(jnp.where(hit, before, 0.0), axis=0, keepdims=True).astype(jnp.int32)


def _even_out_kernel(o_ref, u_ref, uh_ref, bg_ref, cw_ref, *tail_refs, tiles_per_seq):
    first = (pl.program_id(0) % tiles_per_seq) == 0
    u = u_ref[...]
    tm = u.shape[0]
    halo = jnp.where(first, 0.0, uh_ref[...])
    ext = jnp.concatenate([halo, u], axis=0)
    u1 = pltpu.roll(ext, 1, 0)[CONV_HALO:]
    u2 = pltpu.roll(ext, 2, 0)[CONV_HALO:]
    cw = cw_ref[...]
    y_conv = bg_ref[...] * (cw[2:3] * u + cw[1:2] * u1 + cw[0:1] * u2)
    cat = jnp.concatenate([o_ref[...], y_conv], axis=1).astype(BF16)
    _tail(cat, *tail_refs)


def _tail_specs(tm, d, tpb):
    row = lambda i: (i, 0)
    per_b = lambda i: (i // tpb, 0, 0)
    const2 = lambda i: (0, 0)
    ins = [pl.BlockSpec((tm, d), row),
           pl.BlockSpec((1, 1, d), per_b),
           pl.BlockSpec((d, d), const2),
           pl.BlockSpec((1, d), const2),
           pl.BlockSpec((1, 1, d), per_b),
           pl.BlockSpec((1, 1, d), per_b),
           pl.BlockSpec((N_EXPERTS, d), const2),
           pl.BlockSpec((N_EXPERTS, 1), const2),
           pl.BlockSpec((tm, tm), const2)]
    outs = [pl.BlockSpec((tm, d), row), pl.BlockSpec((tm, d + LANES), row),
            pl.BlockSpec((2, tm), lambda i: (0, i)),
            pl.BlockSpec((CLASS_ROWS, LANES), const2)]
    scratch = [pltpu.VMEM((CLASS_ROWS, LANES), F32)]
    return ins, outs, scratch


def _tail_out_shapes(n, d):
    return [jax.ShapeDtypeStruct((n, d), F32), jax.ShapeDtypeStruct((n, d + LANES), F32),
            jax.ShapeDtypeStruct((2, n), jnp.int32), jax.ShapeDtypeStruct((CLASS_ROWS, LANES), jnp.int32)]


def _strict_upper(tm):
    return jnp.asarray(np.triu(np.ones((tm, tm), np.float32), 1), dtype=BF16)


def even_out_proj(o_nsa, u, bg, conv_w, x2, g1, w_out, ng, sh2, sc2, rwt, rb, seq):
    n, d = x2.shape
    tm = min(TOK_TILE, seq)
    tpb = seq // tm
    row = lambda i: (i, 0)
    halo = lambda i: (jnp.maximum(i * (tm // CONV_HALO) - 1, 0), 0)
    tin, tout, tscratch = _tail_specs(tm, d, tpb)
    return pl.pallas_call(
        functools.partial(_even_out_kernel, tiles_per_seq=tpb),
        out_shape=_tail_out_shapes(n, d),
        grid=(n // tm,),
        in_specs=[pl.BlockSpec((tm, NSA_DIM), row),
                  pl.BlockSpec((tm, CONV_DIM), row),
                  pl.BlockSpec((CONV_HALO, CONV_DIM), halo),
                  pl.BlockSpec((tm, CONV_DIM), row),
                  pl.BlockSpec(conv_w.shape, lambda i: (0, 0))] + tin,
        out_specs=tout,
        scratch_shapes=tscratch,
        compiler_params=_cparams(("arbitrary",)),
        name="even_out_proj",
    )(o_nsa, u, u, bg, conv_w, x2, g1, w_out, ng, sh2, sc2, rwt, rb, _strict_upper(tm))


def _dispatch_plan(ridx, counts, n):
    cnt = counts[:N_CLASSES, 0]
    padded = (cnt + MOE_ROW_TILE - 1) // MOE_ROW_TILE * MOE_ROW_TILE
    ends = jnp.cumsum(padded)
    starts = ends - padded
    cids = jnp.arange(N_CLASSES, dtype=jnp.int32)[:, None]
    base = jnp.sum(jnp.where(cids == ridx[0][None, :], starts[:, None], 0), axis=0)
    dest = (base + ridx[1]).astype(jnp.int32)
    td = min(MOE_DMA_TILE, n)
    dest3 = dest.reshape(n // td, 1, td)
    n_tiles = n // MOE_ROW_TILE + N_CLASSES
    tile_start = jnp.arange(n_tiles, dtype=jnp.int32) * MOE_ROW_TILE
    tile_class = jnp.minimum(jnp.sum(tile_start[:, None] >= ends[None, :], axis=1), N_CLASSES - 1)
    group_base = (tile_class // PAIRS_PER_GROUP) * EXPERTS_PER_GROUP
    pair = tile_class % PAIRS_PER_GROUP
    tile_lo = (group_base + jnp.asarray(PAIR_LO, jnp.int32)[pair]).astype(jnp.int32)
    tile_hi = (group_base + jnp.asarray(PAIR_HI, jnp.int32)[pair]).astype(jnp.int32)
    n_used = (ends[-1] // MOE_ROW_TILE).reshape(1).astype(jnp.int32)
    last_tile = jnp.where(cnt > 0, ends // MOE_ROW_TILE - 1, -1)
    tail = n_used[0] + jnp.arange(N_CLASSES, dtype=jnp.int32)
    zero_tiles = jnp.concatenate([last_tile, jnp.where(tail < n_tiles, tail, -1)]).astype(jnp.int32)
    return dest3, tile_lo, tile_hi, n_used, zero_tiles, n_tiles


def _dispatch_kernel(ztile_ref, dest_ref, hw_ref, xs_hbm, zbuf, zsem, sem):
    td = hw_ref.shape[0]

    @pl.when(pl.program_id(0) == 0)
    def _():
        zbuf[...] = jnp.zeros_like(zbuf)

        def zero_copy(k):
            start = pl.multiple_of(ztile_ref[k] * MOE_ROW_TILE, MOE_ROW_TILE)
            return pltpu.make_async_copy(zbuf, xs_hbm.at[pl.ds(start, MOE_ROW_TILE)], zsem)

        for k in range(2 * N_CLASSES):
            @pl.when(ztile_ref[k] >= 0)
            def _():
                zero_copy(k).start()
        for k in range(2 * N_CLASSES):
            @pl.when(ztile_ref[k] >= 0)
            def _():
                zero_copy(k).wait()

    for r in range(td):
        pltpu.make_async_copy(hw_ref.at[pl.ds(r, 1)],
                              xs_hbm.at[pl.ds(dest_ref[0, 0, r], 1)], sem).start()
    pltpu.make_async_copy(hw_ref, xs_hbm.at[pl.ds(0, td)], sem).wait()


def moe_dispatch(hw, dest3, zero_tiles, n_tiles):
    n, cols = hw.shape
    td = dest3.shape[2]
    rows = n_tiles * MOE_ROW_TILE
    return pl.pallas_call(
        _dispatch_kernel,
        out_shape=jax.ShapeDtypeStruct((rows, cols), F32),
        grid_spec=pltpu.PrefetchScalarGridSpec(
            num_scalar_prefetch=1,
            grid=(n // td,),
            in_specs=[pl.BlockSpec((1, 1, td), lambda i, z: (i, 0, 0), memory_space=pltpu.SMEM),
                      pl.BlockSpec((td, cols), lambda i, z: (i, 0))],
            out_specs=pl.BlockSpec(memory_space=pl.ANY),
            scratch_shapes=[pltpu.VMEM((MOE_ROW_TILE, cols), F32), pltpu.SemaphoreType.DMA(()),
                            pltpu.SemaphoreType.DMA(())]),
        compiler_params=_cparams(("arbitrary",)),
        name="moe_dispatch",
    )(zero_tiles, dest3, hw)


def _expert_kernel(lo_ref, hi_ref, nused_ref, xs_ref, wg_lo, wu_lo, wd_lo, wg_hi, wu_hi, wd_hi, ys_ref,
                   *wb):
    t = pl.program_id(0)
    prev = jnp.maximum(t - 1, 0)

    for ids, srcs, dsts in ((lo_ref, (wg_lo, wu_lo, wd_lo), wb[:3]), (hi_ref, (wg_hi, wu_hi, wd_hi), wb[3:])):
        @pl.when((t == 0) | (ids[t] != ids[prev]))
        def _():
            for src, dst in zip(srcs, dsts):
                dst[...] = src[0, 0].astype(BF16)

    @pl.when(t < nused_ref[0])
    def _():
        d = xs_ref.shape[1] - LANES
        x = xs_ref[:, :d].astype(BF16)
        meta = xs_ref[:, d:]
        y = None
        for k in range(2):
            a = _mm(x, wb[3 * k][...])
            b = _mm(x, wb[3 * k + 1][...])
            he = (a * jax.nn.sigmoid(a)) * b
            yk = meta[:, k:k + 1] * _mm(he.astype(BF16), wb[3 * k + 2][...])
            y = yk if y is None else y + yk
        ys_ref[...] = y

    @pl.when(t >= nused_ref[0])
    def _():
        ys_ref[...] = jnp.zeros_like(ys_ref)


def moe_experts(xs, tile_lo, tile_hi, n_used, w_gate, w_up, w_down, layer, d):
    rows, cols = xs.shape
    n_tiles = rows // MOE_ROW_TILE
    lo_spec = lambda shape: pl.BlockSpec((1, 1) + shape, lambda t, lo, hi, nu: (layer, lo[t], 0, 0))
    hi_spec = lambda shape: pl.BlockSpec((1, 1) + shape, lambda t, lo, hi, nu: (layer, hi[t], 0, 0))
    shapes = ((d, D_EXPERT), (d, D_EXPERT), (D_EXPERT, d))
    return pl.pallas_call(
        _expert_kernel,
        out_shape=jax.ShapeDtypeStruct((rows, d), F32),
        grid_spec=pltpu.PrefetchScalarGridSpec(
            num_scalar_prefetch=3,
            grid=(n_tiles,),
            in_specs=[pl.BlockSpec((MOE_ROW_TILE, cols),
                                   lambda t, lo, hi, nu: (jnp.minimum(t, nu[0] - 1), 0))]
                     + [lo_spec(s) for s in shapes] + [hi_spec(s) for s in shapes],
            out_specs=pl.BlockSpec((MOE_ROW_TILE, d), lambda t, lo, hi, nu: (t, 0)),
            scratch_shapes=[pltpu.VMEM(s, BF16) for s in shapes + shapes]),
        compiler_params=_cparams(("arbitrary",)),
        name="moe_experts",
    )(tile_lo, tile_hi, n_used, xs, w_gate, w_up, w_down, w_gate, w_up, w_down)


def _combine_kernel(dest_ref, ys_hbm, x_ref, g2_ref, fn_ref, o_ref, buf, sem, *, final_norm):
    tc = x_ref.shape[0]

    for r in range(tc):
        pltpu.make_async_copy(ys_hbm.at[pl.ds(dest_ref[0, 0, r], 1)], buf.at[pl.ds(r, 1)], sem).start()
    pltpu.make_async_copy(ys_hbm.at[pl.ds(0, tc)], buf, sem).wait()
    x = x_ref[...] + g2_ref[0] * buf[...]
    if final_norm:
        ms = jnp.mean(x * x, axis=-1, keepdims=True)
        x = x * lax.rsqrt(ms + NORM_EPS) * fn_ref[...]
    o_ref[...] = x


def moe_combine(ys, dest3, x1, g2, fnorm, seq, final_norm):
    n, d = x1.shape
    tc = dest3.shape[2]
    tpb = seq // tc
    return pl.pallas_call(
        functools.partial(_combine_kernel, final_norm=final_norm),
        out_shape=jax.ShapeDtypeStruct((n, d), F32),
        grid=(n // tc,),
        in_specs=[pl.BlockSpec((1, 1, tc), lambda i: (i, 0, 0), memory_space=pltpu.SMEM),
                  pl.BlockSpec(memory_space=pl.ANY),
                  pl.BlockSpec((tc, d), lambda i: (i, 0)),
                  pl.BlockSpec((1, 1, d), lambda i: (i // tpb, 0, 0)),
                  pl.BlockSpec((1, d), lambda i: (0, 0))],
        out_specs=pl.BlockSpec((tc, d), lambda i: (i, 0)),
        scratch_shapes=[pltpu.VMEM((tc, d), F32), pltpu.SemaphoreType.DMA(())],
        compiler_params=_cparams(("arbitrary",)),
        name="moe_combine",
    )(dest3, ys, x1, g2, fnorm)


def moe_sparse(hw, ridx, counts, w_gate, w_up, w_down, layer, x1, g2, fnorm, seq, final_norm):
    n, d = x1.shape
    dest3, tile_lo, tile_hi, n_used, zero_tiles, n_tiles = _dispatch_plan(ridx, counts, n)
    xs = moe_dispatch(hw, dest3, zero_tiles, n_tiles)
    ys = moe_experts(xs, tile_lo, tile_hi, n_used, w_gate, w_up, w_down, layer, d)
    return moe_combine(ys, dest3, x1, g2, fnorm, seq, final_norm)


def _odd_in_kernel(x_ref, g_ref, sh_ref, sc_ref, w_ref, mu_ref, w0_ref, w2_ref, a0_ref, a2_ref,
                   g2_ref, kk_ref, ka_ref, ones_ref, pw_ref, ps_ref,
                   r_ref, lw_ref, km_ref, v_ref, kn_ref, kb_ref, gg_ref, op_ref,
                   rw_carry, u_carry, *, tiles_per_seq, tm):
    i = pl.program_id(0)
    first = (i % tiles_per_seq) == 0
    h = _norm_mod(x_ref[...], g_ref[...], sh_ref[0], sc_ref[0])
    proj = _mm(h.astype(BF16), w_ref[...])

    rw = proj[:, :ODD_RW_COLS]
    row0 = jnp.where(first, 0.0, rw_carry[0:1, :])
    ridx = lax.broadcasted_iota(jnp.int32, (tm, 1), 0)
    prev = jnp.where(ridx == 0, row0, pltpu.roll(rw, 1, 0))
    rw_carry[0:1, :] = rw[tm - 1:tm, :]
    rw = rw + (prev - rw) * mu_ref[...]

    r = rw[:, 0:512]
    k = rw[:, 512:1024]
    v = rw[:, 1024:1536]
    wl = rw[:, 1536:1664]
    al = rw[:, 1664:1792]
    gl = rw[:, 1792:1920]
    z = -(w0_ref[...] + _mm(jnp.tanh(wl).astype(BF16), w2_ref[...]))
    softplus = jnp.maximum(z, 0.0) + jnp.log1p(jnp.exp(-jnp.abs(z)))
    w_log = -softplus - 0.5
    a = jax.nn.sigmoid(a0_ref[...] + _mm(al.astype(BF16), a2_ref[...]))
    gg_ref[...] = _mm(jax.nn.sigmoid(gl).astype(BF16), g2_ref[...])
    kk0 = k * kk_ref[...]
    ss = _split_sum(kk0 * kk0, ones_ref[...])
    kk = kk0 / jnp.maximum(jnp.sqrt(ss), 1e-12)
    r_ref[...] = r
    lw_ref[...] = -jnp.exp(w_log)
    km_ref[...] = k * (1.0 + (a - 1.0) * ka_ref[...])
    v_ref[...] = v
    kn_ref[...] = kk
    kb_ref[...] = kk * a

    u = proj[:, ODD_RW_COLS:]
    halo = jnp.where(first, 0.0, u_carry[...])
    u_carry[...] = u[tm - POOL_HALO:, :]
    ext = jnp.concatenate([halo, u], axis=0)
    tseq = (i % tiles_per_seq) * tm + ridx
    for gi, win in enumerate(POOL_WINDOWS):
        xg = ext[:, gi * POOL_GROUP:(gi + 1) * POOL_GROUP]
        s = xg
        step = 1
        while step < win:
            s = s + pltpu.roll(s, step, 0)
            step *= 2
        cnt = jnp.minimum(tseq + 1, win).astype(F32)
        pooled = s[POOL_HALO:] / cnt - xg[POOL_HALO:]
        mixed = _mm(pooled.astype(BF16), pw_ref[gi])
        op_ref[:, gi * POOL_GROUP:(gi + 1) * POOL_GROUP] = (
            mixed * ps_ref[:, gi * POOL_GROUP:(gi + 1) * POOL_GROUP])


def odd_in_proj(x2, g, sh, sc, w_pad, mu_pad, w0, w2p, a0, a2p, g2, k_k, k_a, ones_bd, pool_w,
                pool_scale, seq):
    n, d = x2.shape
    tm = min(TOK_TILE, seq)
    tpb = seq // tm
    row = lambda i: (i, 0)
    per_b = lambda i: (i // tpb, 0, 0)
    c2 = lambda i: (0, 0)
    full2 = lambda a: pl.BlockSpec(a.shape, c2)
    return pl.pallas_call(
        functools.partial(_odd_in_kernel, tiles_per_seq=tpb, tm=tm),
        out_shape=[jax.ShapeDtypeStruct((n, RWKV_DIM), F32)] * 8,
        grid=(n // tm,),
        in_specs=[pl.BlockSpec((tm, d), row), pl.BlockSpec((1, d), c2),
                  pl.BlockSpec((1, 1, d), per_b), pl.BlockSpec((1, 1, d), per_b),
                  full2(w_pad), full2(mu_pad), full2(w0), full2(w2p), full2(a0), full2(a2p),
                  full2(g2), full2(k_k), full2(k_a), full2(ones_bd),
                  pl.BlockSpec(pool_w.shape, lambda i: (0, 0, 0)), full2(pool_scale)],
        out_specs=[pl.BlockSpec((tm, RWKV_DIM), row)] * 8,
        scratch_shapes=[pltpu.VMEM((SUBLANES, ODD_RW_COLS), F32),
                        pltpu.VMEM((POOL_HALO, RWKV_DIM), F32)],
        compiler_params=_cparams(("arbitrary",)),
        name="odd_in_proj",
    )(x2, g, sh, sc, w_pad, mu_pad, w0, w2p, a0, a2p, g2, k_k, k_a, ones_bd, pool_w, pool_scale)


def _bmm(a, b):
    return lax.dot_general(a, b, (((2,), (1,)), ((0,), (0,))), preferred_element_type=F32)


def _bnt(a, b):
    return lax.dot_general(a, b, (((2,), (2,)), ((0,), (0,))), preferred_element_type=F32)


def _btn(a, b):
    return lax.dot_general(a, b, (((1,), (1,)), ((0,), (0,))), preferred_element_type=F32)


def _scan_prep_kernel(r_ref, lw_ref, km_ref, v_ref, kn_ref, kb_ref, qe_ref, y0_ref, mt_ref, ct_ref,
                      *, chunk, cb):
    L = chunk
    rows = cb * L
    n_pairs = N_RWKV_HEADS // 2
    two = 2 * L
    rowt = lax.broadcasted_iota(jnp.int32, (rows, 1), 0) % L
    lane = lax.broadcasted_iota(jnp.int32, (1, 1, LANES), 2)
    low = lane < HEAD_DIM
    ri = lax.broadcasted_iota(jnp.int32, (two, two), 0)
    ci = lax.broadcasted_iota(jnp.int32, (two, two), 1)
    same_blk = (ri // L) == (ci // L)
    strict = same_blk & ((ci % L) < (ri % L))
    incl = same_blk & ((ci % L) <= (ri % L))
    li = lax.broadcasted_iota(jnp.int32, (LANES, LANES), 0)
    lj = lax.broadcasted_iota(jnp.int32, (LANES, LANES), 1)
    same_head = (li // HEAD_DIM) == (lj // HEAD_DIM)
    eye = li == lj

    lw = lw_ref[...]
    cum = lw
    step = 1
    while step < L:
        cum = cum + jnp.where(rowt >= step, pltpu.roll(cum, step, 0), 0.0)
        step *= 2

    def to3(x):
        x3 = x.reshape(cb, L, RWKV_DIM)
        return jnp.concatenate([x3[:, :, p * LANES:(p + 1) * LANES] for p in range(n_pairs)], axis=0)

    def stack2(x):
        return jnp.concatenate([jnp.where(low, x, 0.0), jnp.where(low, 0.0, x)], axis=1)

    def fold(x):
        return x[:, :L, :] + x[:, L:, :]

    cum3 = to3(cum)
    lw3 = to3(lw)
    cum_l = cum3[:, L - 1:L, :]
    g_inv = jnp.exp(-cum3)
    g_tail = jnp.exp(cum_l - cum3)
    kb = to3(kb_ref[...])
    km = to3(km_ref[...])
    v = to3(v_ref[...])
    at_s = stack2(-to3(kn_ref[...]) * jnp.exp(cum3 - lw3))
    rt_s = stack2(to3(r_ref[...]) * jnp.exp(cum3))
    v_s = stack2(v).astype(BF16)
    lhs = jnp.concatenate([at_s, rt_s], axis=1).astype(BF16)
    rhs = jnp.concatenate([stack2(kb * g_inv), stack2(km * g_inv)], axis=1).astype(BF16)
    prod = _bnt(lhs, rhs)
    nmat = jnp.where(strict, prod[:, :two, :two], 0.0)
    a_ak = jnp.where(strict, prod[:, :two, two:], 0.0).astype(BF16)
    a_rb = jnp.where(incl, prod[:, two:, :two], 0.0).astype(BF16)
    a_rk = jnp.where(incl, prod[:, two:, two:], 0.0).astype(BF16)

    x = jnp.concatenate([at_s, _bmm(a_ak, v_s)], axis=2)
    npow = nmat
    step = 1
    while step < L:
        nb = npow.astype(BF16)
        x = x + _bmm(nb, x.astype(BF16))
        step *= 2
        if step < L:
            npow = _bmm(nb, nb)
    qy = _bmm(a_rb, x.astype(BF16))
    qe = fold(rt_s + qy[:, :, :LANES])
    y0 = fold(qy[:, :, LANES:] + _bmm(a_rk, v_s))
    wu = fold(x).astype(BF16)
    bwu = _btn((kb * g_tail).astype(BF16), wu)
    kv = _btn((km * g_tail).astype(BF16), v.astype(BF16))
    g_l = jnp.broadcast_to(jnp.exp(cum_l), (n_pairs * cb, LANES, LANES))
    mt = jnp.where(eye, g_l, 0.0) + jnp.where(same_head, bwu[:, :, :LANES], 0.0)
    ct = jnp.where(same_head, bwu[:, :, LANES:] + kv, 0.0)
    for p in range(n_pairs):
        sl = slice(p * LANES, (p + 1) * LANES)
        qe_ref[:, sl] = qe[p * cb:(p + 1) * cb].reshape(rows, LANES)
        y0_ref[:, sl] = y0[p * cb:(p + 1) * cb].reshape(rows, LANES)
        mt_ref[:, p] = mt[p * cb:(p + 1) * cb].astype(BF16)
        ct_ref[:, p] = ct[p * cb:(p + 1) * cb]


def _scan_state_kernel(qe_ref, y0_ref, mt_ref, ct_ref, y_ref, st_ref, *, batch):
    @pl.when(pl.program_id(0) == 0)
    def _():
        st_ref[...] = jnp.zeros_like(st_ref)

    n_pairs = N_RWKV_HEADS // 2
    qe = qe_ref[...]
    qe3 = jnp.concatenate([qe[:, :, p * LANES:(p + 1) * LANES] for p in range(n_pairs)], axis=0)
    st = st_ref[...].astype(BF16)
    y = _bmm(qe3.astype(BF16), st)
    for p in range(n_pairs):
        sl = slice(p * LANES, (p + 1) * LANES)
        y_ref[:, :, sl] = y[p * batch:(p + 1) * batch] + y0_ref[:, :, sl]
    mt = jnp.concatenate([mt_ref[:, 0, p] for p in range(n_pairs)], axis=0)
    ct = jnp.concatenate([ct_ref[:, 0, p] for p in range(n_pairs)], axis=0)
    st_ref[...] = _bmm(mt, st) + ct


def rwkv_scan(r, lw, km, v, kn, kb, batch, seq):
    n = batch * seq
    chunk = min(SCAN_CHUNK, seq)
    nc = seq // chunk
    cb = min(SCAN_CHUNKS_PER_STEP, nc)
    n_pairs = N_RWKV_HEADS // 2
    blk = pl.BlockSpec((cb * chunk, RWKV_DIM), lambda i: (i, 0))
    mblk = pl.BlockSpec((cb, n_pairs, LANES, LANES), lambda i: (i, 0, 0, 0))
    qe, y0, mt, ct = pl.pallas_call(
        functools.partial(_scan_prep_kernel, chunk=chunk, cb=cb),
        out_shape=[jax.ShapeDtypeStruct((n, RWKV_DIM), F32), jax.ShapeDtypeStruct((n, RWKV_DIM), F32),
                   jax.ShapeDtypeStruct((n // chunk, n_pairs, LANES, LANES), BF16),
                   jax.ShapeDtypeStruct((n // chunk, n_pairs, LANES, LANES), F32)],
        grid=(n // (cb * chunk),),
        in_specs=[blk] * 6,
        out_specs=[blk, blk, mblk, mblk],
        compiler_params=_cparams(("parallel",)),
        name="rwkv_scan_prep",
    )(r, lw, km, v, kn, kb)
    sblk = pl.BlockSpec((batch, chunk, RWKV_DIM), lambda c: (0, c, 0))
    smblk = pl.BlockSpec((batch, 1, n_pairs, LANES, LANES), lambda c: (0, c, 0, 0, 0))
    y = pl.pallas_call(
        functools.partial(_scan_state_kernel, batch=batch),
        out_shape=jax.ShapeDtypeStruct((batch, seq, RWKV_DIM), F32),
        grid=(nc,),
        in_specs=[sblk, sblk, smblk, smblk],
        out_specs=sblk,
        scratch_shapes=[pltpu.VMEM((n_pairs * batch, LANES, LANES), F32)],
        compiler_params=_cparams(("arbitrary",)),
        name="rwkv_scan_state",
    )(qe.reshape(batch, seq, RWKV_DIM), y0.reshape(batch, seq, RWKV_DIM),
      mt.reshape(batch, nc, n_pairs, LANES, LANES), ct.reshape(batch, nc, n_pairs, LANES, LANES))
    return y.reshape(n, RWKV_DIM)


def _odd_out_kernel(y_ref, r_ref, km_ref, v_ref, gg_ref, op_ref, rk_ref, lnw_ref, lnb_ref, ones_ref,
                    *tail_refs):
    ones = ones_ref[...]
    inv = 1.0 / HEAD_DIM
    y = y_ref[...]
    mean = _split_sum(y, ones) * inv
    yc = y - mean
    var = _split_sum(yc * yc, ones) * inv
    yn = yc * lax.rsqrt(var + LNX_EPS) * lnw_ref[...] + lnb_ref[...]
    bonus = _split_sum(r_ref[...] * km_ref[...] * rk_ref[...], ones) * v_ref[...]
    o_rwkv = (yn + bonus) * gg_ref[...]
    cat = jnp.concatenate([o_rwkv, op_ref[...]], axis=1).astype(BF16)
    _tail(cat, *tail_refs)


def odd_out_proj(y, r, km, v, gg, opool, r_k, lnx_w, lnx_b, ones_bd, x2, g1, w_out, ng, sh2, sc2,
                 rwt, rb, seq):
    n, d = x2.shape
    tm = min(TOK_TILE, seq)
    tpb = seq // tm
    row = lambda i: (i, 0)
    c2 = lambda i: (0, 0)
    act = pl.BlockSpec((tm, RWKV_DIM), row)
    vec = pl.BlockSpec((1, RWKV_DIM), c2)
    tin, tout, tscratch = _tail_specs(tm, d, tpb)
    return pl.pallas_call(
        _odd_out_kernel,
        out_shape=_tail_out_shapes(n, d),
        grid=(n // tm,),
        in_specs=[act] * 6 + [vec, vec, vec, pl.BlockSpec(ones_bd.shape, c2)] + tin,
        out_specs=tout,
        scratch_shapes=tscratch,
        compiler_params=_cparams(("arbitrary",)),
        name="odd_out_proj",
    )(y, r, km, v, gg, opool, r_k, lnx_w, lnx_b, ones_bd, x2, g1, w_out, ng, sh2, sc2, rwt, rb,
      _strict_upper(tm))


def _rope_tables(seq):
    half = HEAD_DIM // 2
    inv = ROPE_THETA ** (-jnp.arange(half, dtype=F32) / half)
    ang = jnp.arange(seq, dtype=F32)[:, None] * inv[None, :]
    return jnp.tile(jnp.cos(ang), (1, LANES // half)), jnp.tile(jnp.sin(ang), (1, LANES // half))


def _even_w_pad(w_in):
    d = w_in.shape[0]
    q_kv = w_in[:, :NSA_DIM + 6 * KV_DIM]
    gl = w_in[:, NSA_DIM + 6 * KV_DIM:NSA_DIM + 6 * KV_DIM + 24]
    rest = w_in[:, NSA_DIM + 6 * KV_DIM + 24:]
    z = jnp.zeros((d, LANES - 12), w_in.dtype)
    return jnp.concatenate([q_kv, gl[:, :12], z, gl[:, 12:], z, rest], axis=1).astype(BF16)


def _compress_params(cmp_pos, cmp_w1, cmp_w2):
    eye = jnp.eye(N_KV_HEADS, dtype=F32)
    w1r = cmp_w1.reshape(2, 2, CMP_STRIDE, HEAD_DIM, CMP_HIDDEN)
    w1_ext = jnp.einsum('kpmdn,gh->kpmgdhn', w1r, eye).reshape(
        2, 2, CMP_STRIDE * KV_DIM, N_KV_HEADS * CMP_HIDDEN).astype(BF16)
    w2_ext = jnp.einsum('knd,gh->kgnhd', cmp_w2, eye).reshape(
        2, N_KV_HEADS * CMP_HIDDEN, KV_DIM).astype(BF16)
    pos = cmp_pos.reshape(2, 2, CMP_STRIDE, 1, HEAD_DIM)
    pos_ext = jnp.broadcast_to(pos, (2, 2, CMP_STRIDE, N_KV_HEADS, HEAD_DIM)).reshape(
        2, 2, 1, CMP_STRIDE * KV_DIM)
    return pos_ext, w1_ext, w2_ext


def _nsa_tables(seq):
    n_blk = seq // SEL_BLOCK
    n_cmp = (seq - CMP_BLOCK) // CMP_STRIDE + 1
    n_cmp_pad = seq // CMP_STRIDE
    r = SEL_BLOCK // CMP_STRIDE
    c = CMP_BLOCK // CMP_STRIDE
    msel = np.zeros((n_cmp_pad, LANES), np.float32)
    for j in range(n_blk):
        for m in range(r):
            for n in range(c):
                idx = r * j + m + n
                if idx < n_cmp:
                    msel[idx, j] += 1.0
    chunk = min(SEL_CHUNK, seq)
    kblk = np.arange(seq) // SEL_BLOCK
    eexp = (kblk[None, :] == np.arange(LANES)[:, None]).astype(np.float32)
    eexp = eexp.reshape(LANES, seq // chunk, chunk).transpose(1, 0, 2)
    selq = np.zeros((N_KV_HEADS, GQA, GQA * HEAD_DIM, LANES), np.float32)
    for h in range(N_KV_HEADS):
        for g in range(GQA):
            for dd in range(HEAD_DIM):
                selq[h, g, g * HEAD_DIM + dd, h * HEAD_DIM + dd] = 1.0
    return jnp.asarray(msel.T), jnp.asarray(eexp, dtype=BF16), jnp.asarray(selq, dtype=BF16)


def _odd_params(w_in, mu, w2, a2):
    d = w_in.shape[0]
    z64 = jnp.zeros((d, 64), w_in.dtype)
    w_pad = jnp.concatenate([w_in[:, :1536], w_in[:, 1536:1600], z64, w_in[:, 1600:1664], z64,
                             w_in[:, 1664:]], axis=1).astype(BF16)
    m64 = jnp.zeros((64,), mu.dtype)
    mu_pad = jnp.concatenate([mu[:1536], mu[1536:1600], m64, mu[1600:1664], m64, mu[1664:]])[None, :]
    zr = jnp.zeros((64, RWKV_DIM), w2.dtype)
    w2p = jnp.concatenate([w2, zr], axis=0).astype(BF16)
    a2p = jnp.concatenate([a2, zr], axis=0).astype(BF16)
    return w_pad, mu_pad, w2p, a2p


def _head_ones():
    idx = np.arange(RWKV_DIM) // HEAD_DIM
    return jnp.asarray((idx[:, None] == idx[None, :]).astype(np.float32), dtype=BF16)


def kernel(x, c, ada_w, ada_b, norm_mix, norm_ffn, even_w_in, even_cmp_pos, even_cmp_w1, even_cmp_w2,
           even_conv_w, even_w_out, odd_w_in, odd_mu, odd_w0, odd_w2, odd_a0, odd_a2, odd_g2, odd_k_k,
           odd_k_a, odd_r_k, odd_lnx_w, odd_lnx_b, odd_pool_w, odd_pool_scale, odd_w_out,
           router_w, router_b, moe_w_gate, moe_w_up, moe_w_down, final_norm):
    batch, seq, d = x.shape
    n = batch * seq
    depth = ada_w.shape[0]
    x2 = x.reshape(n, d)
    mod = ada_modulation(c, ada_w, ada_b)
    rwt = router_w.T
    rb = router_b.reshape(N_EXPERTS, 1)
    fnorm = final_norm.reshape(1, d)
    cos, sin = _rope_tables(seq)
    msel, eexp, selq = _nsa_tables(seq)
    ones_bd = _head_ones()

    for layer in range(depth):
        m = mod[layer].reshape(batch, 6, 1, d)
        sh1, sc1, g1, sh2, sc2, g2 = (m[:, k] for k in range(6))
        ng_mix = norm_mix[layer].reshape(1, d)
        ng_ffn = norm_ffn[layer].reshape(1, d)
        i = layer // 2
        if layer % 2 == 0:
            (qn, qr, kc, vc, ks, vs, kw, vw, gate, u, bg) = even_in_proj(
                x2, ng_mix, sh1, sc1, _even_w_pad(even_w_in[i]), cos, sin, seq)
            pos_ext, w1_ext, w2_ext = _compress_params(even_cmp_pos[i], even_cmp_w1[i], even_cmp_w2[i])
            kcmp, vcmp = compress_kv(kc, vc, pos_ext, w1_ext, w2_ext, batch, seq)
            o_nsa = nsa_attention(qn, qr, kcmp, vcmp, ks.reshape(batch, seq, KV_DIM),
                                  vs.reshape(batch, seq, KV_DIM), kw.reshape(batch, seq, KV_DIM),
                                  vw.reshape(batch, seq, KV_DIM), gate, selq, msel, eexp, batch, seq)
            x1, hw, ridx, counts = even_out_proj(o_nsa, u, bg, even_conv_w[i], x2, g1,
                                                 even_w_out[i].astype(BF16), ng_ffn, sh2, sc2, rwt, rb, seq)
        else:
            w_pad, mu_pad, w2p, a2p = _odd_params(odd_w_in[i], odd_mu[i], odd_w2[i], odd_a2[i])
            vec = lambda a: a.reshape(1, RWKV_DIM)
            (r, lw, km, v, kn, kb, gg, opool) = odd_in_proj(
                x2, ng_mix, sh1, sc1, w_pad, mu_pad, vec(odd_w0[i]), w2p, vec(odd_a0[i]), a2p,
                odd_g2[i].astype(BF16), vec(odd_k_k[i]), vec(odd_k_a[i]), ones_bd,
                odd_pool_w[i].astype(BF16), vec(odd_pool_scale[i]), seq)
            y = rwkv_scan(r, lw, km, v, kn, kb, batch, seq)
            x1, hw, ridx, counts = odd_out_proj(
                y, r, km, v, gg, opool, vec(odd_r_k[i]), vec(odd_lnx_w[i]), vec(odd_lnx_b[i]), ones_bd,
                x2, g1, odd_w_out[i].astype(BF16), ng_ffn, sh2, sc2, rwt, rb, seq)
        x2 = moe_sparse(hw, ridx, counts, moe_w_gate, moe_w_up, moe_w_down, layer,
                        x1, g2, fnorm, seq, final_norm=(layer == depth - 1))
    return x2.reshape(batch, seq, d)
```

```python
import functools

import jax
import jax.numpy as jnp
import numpy as np
from jax import lax
from jax.experimental import pallas as pl
from jax.experimental.pallas import tpu as pltpu

F32 = jnp.float32
BF16 = jnp.bfloat16
HIGHEST = lax.Precision.HIGHEST

D_MODEL = 1024
DEPTH = 2
HEAD_DIM = 64
ROPE_THETA = 10000.0
NORM_EPS = 1e-6
NEG_INF = -1e30
BIG = 1e9
NSA_DIM = 512
N_KV_HEADS = 2
GQA = 4
KV_DIM = 128
CMP_BLOCK = 32
CMP_STRIDE = 16
CMP_HIDDEN = 256
SEL_BLOCK = 64
N_SEL = 8
N_LOCAL = 2
WINDOW = 512
Q_BLOCK = 128
ATTN_SCALE = HEAD_DIM ** -0.5
CONV_DIM = 512
RWKV_DIM = 512
N_RWKV_HEADS = 8
LNX_EPS = 64e-5
POOL_WINDOWS = (2, 4, 8, 16)
POOL_GROUP = 128
N_EXPERTS = 16
N_EXPERT_GROUPS = 4
EXPERTS_PER_GROUP = 4
D_EXPERT = 512
PAIRS_PER_GROUP = 6
PAIR_LO = (0, 0, 0, 1, 1, 2)
PAIR_HI = (1, 2, 3, 3, 2, 3)
N_CLASSES = N_EXPERT_GROUPS * PAIRS_PER_GROUP
CLASS_ROWS = 32

LANES = 128
SUBLANES = 8
VMEM_LIMIT = 56 * 1024 * 1024

TOK_TILE = 512
MOE_ROW_TILE = 256
MOE_DMA_TILE = 256
SEL_CHUNK = 512
SCAN_CHUNK = 64
SCAN_CHUNKS_PER_STEP = 4
CONV_HALO = 8
POOL_HALO = 16

EVEN_PAD_COLS = 3072
ODD_PAD_COLS = 2432
ODD_RW_COLS = 1920


def _cparams(sem):
    return pltpu.CompilerParams(dimension_semantics=sem, vmem_limit_bytes=VMEM_LIMIT)


def _nt(a, b, precision=None):
    return lax.dot_general(a, b, (((1,), (1,)), ((), ())), preferred_element_type=F32,
                           precision=precision)


def _tn(a, b):
    return lax.dot_general(a, b, (((0,), (0,)), ((), ())), preferred_element_type=F32)


def _mm(a, b, precision=None):
    return jnp.dot(a, b, preferred_element_type=F32, precision=precision)


def _norm_mod(x, g, sh, sc):
    ms = jnp.mean(x * x, axis=-1, keepdims=True)
    return (x * lax.rsqrt(ms + NORM_EPS) * g) * (1.0 + sc) + sh


def _rows_to_tiles(x, ref):
    for s in range(ref.shape[1]):
        ref[:, s, :] = x[:, s * LANES:(s + 1) * LANES]


def _tiles_to_rows(ref):
    return jnp.concatenate([ref[:, s, :] for s in range(ref.shape[1])], axis=1)


def _split_sum(x, ones_bf16):
    hi = x.astype(BF16)
    lo = (x - hi.astype(F32)).astype(BF16)
    return _mm(hi, ones_bf16) + _mm(lo, ones_bf16)


def _ada_kernel(c_ref, w_ref, b_ref, o_ref):
    c = c_ref[...]
    cond = c * jax.nn.sigmoid(c)
    o_ref[0] = _mm(cond, w_ref[0], precision=HIGHEST) + b_ref[0]


def ada_modulation(c, ada_w, ada_b):
    depth, d, cols = ada_w.shape
    b = c.shape[0]
    tn = 1536
    return pl.pallas_call(
        _ada_kernel,
        out_shape=jax.ShapeDtypeStruct((depth, b, cols), F32),
        grid=(depth, cols // tn),
        in_specs=[pl.BlockSpec((b, d), lambda l, j: (0, 0)),
                  pl.BlockSpec((1, d, tn), lambda l, j: (l, 0, j)),
                  pl.BlockSpec((1, 1, tn), lambda l, j: (l, 0, j))],
        out_specs=pl.BlockSpec((1, b, tn), lambda l, j: (l, 0, j)),
        compiler_params=_cparams(("parallel", "parallel")),
        name="ada_modulation",
    )(c, ada_w, ada_b.reshape(depth, 1, cols))


def _rope128(t, cos, sin, lane):
    rot = jnp.where((lane % HEAD_DIM) < HEAD_DIM // 2,
                    -pltpu.roll(t, LANES - HEAD_DIM // 2, 1), pltpu.roll(t, HEAD_DIM // 2, 1))
    return t * cos + rot * sin


def _even_in_kernel(x_ref, g_ref, sh_ref, sc_ref, w_ref, cos_ref, sin_ref,
                    qn_ref, qr_ref, kc_ref, vc_ref, ks_ref, vs_ref, kw_ref, vw_ref,
                    gate_ref, u_ref, bg_ref):
    h = _norm_mod(x_ref[...], g_ref[...], sh_ref[0], sc_ref[0])
    proj = _mm(h.astype(BF16), w_ref[...])
    cos = cos_ref[...]
    sin = sin_ref[...]
    lane = lax.broadcasted_iota(jnp.int32, (1, LANES), 1)
    for i in range(NSA_DIM // LANES):
        q = proj[:, i * LANES:(i + 1) * LANES] * ATTN_SCALE
        qn_ref[:, i * LANES:(i + 1) * LANES] = q.astype(BF16)
        qr_ref[:, i * LANES:(i + 1) * LANES] = _rope128(q, cos, sin, lane).astype(BF16)
    o = NSA_DIM
    kc_ref[...] = proj[:, o:o + 128]
    vc_ref[...] = proj[:, o + 128:o + 256]
    ks_ref[...] = _rope128(proj[:, o + 256:o + 384], cos, sin, lane).astype(BF16)
    vs_ref[...] = proj[:, o + 384:o + 512].astype(BF16)
    kw_ref[...] = _rope128(proj[:, o + 512:o + 640], cos, sin, lane).astype(BF16)
    vw_ref[...] = proj[:, o + 640:o + 768].astype(BF16)
    o += 768
    gate_ref[...] = jax.nn.sigmoid(proj[:, o:o + 256])
    o += 256
    xb = proj[:, o:o + 512]
    bg_ref[...] = proj[:, o + 512:o + 1024]
    u_ref[...] = proj[:, o + 1024:o + 1536] * xb


def even_in_proj(x2, g, sh, sc, w_pad, cos, sin, seq):
    n, d = x2.shape
    tm = min(TOK_TILE, seq)
    tpb = seq // tm
    row = lambda i: (i, 0)
    per_b = lambda i: (i // tpb, 0, 0)
    pos = lambda i: (i % tpb, 0)
    outs = [((n, 512), BF16), ((n, 512), BF16), ((n, 128), F32), ((n, 128), F32),
            ((n, 128), BF16), ((n, 128), BF16), ((n, 128), BF16), ((n, 128), BF16),
            ((n, 256), F32), ((n, 512), F32), ((n, 512), F32)]
    return pl.pallas_call(
        _even_in_kernel,
        out_shape=[jax.ShapeDtypeStruct(s, t) for s, t in outs],
        grid=(n // tm,),
        in_specs=[pl.BlockSpec((tm, d), row),
                  pl.BlockSpec((1, d), lambda i: (0, 0)),
                  pl.BlockSpec((1, 1, d), per_b),
                  pl.BlockSpec((1, 1, d), per_b),
                  pl.BlockSpec((d, EVEN_PAD_COLS), lambda i: (0, 0)),
                  pl.BlockSpec((tm, LANES), pos),
                  pl.BlockSpec((tm, LANES), pos)],
        out_specs=[pl.BlockSpec((tm, s[1]), row) for s, _ in outs],
        compiler_params=_cparams(("parallel",)),
        name="even_in_proj",
    )(x2, g, sh, sc, w_pad, cos, sin)


def _compress_kernel(k_ref, v_ref, pos_ref, w1_ref, w2_ref, ko_ref, vo_ref):
    for j, (src, dst) in enumerate(((k_ref, ko_ref), (v_ref, vo_ref))):
        xr = src[0]
        n_rows = xr.shape[0]
        a0 = _mm((xr + pos_ref[j, 0]).astype(BF16), w1_ref[j, 0])
        a1 = _mm((xr + pos_ref[j, 1]).astype(BF16), w1_ref[j, 1])
        hid = a0 + pltpu.roll(a1, n_rows - 1, 0)
        hid = jax.nn.gelu(hid)
        dst[0] = _mm(hid.astype(BF16), w2_ref[j]).astype(BF16)


def compress_kv(kc, vc, pos_ext, w1_ext, w2_ext, batch, seq):
    rows = seq // CMP_STRIDE
    width = CMP_STRIDE * KV_DIM
    kr = kc.reshape(batch, rows, width)
    vr = vc.reshape(batch, rows, width)
    blk = pl.BlockSpec((1, rows, width), lambda b: (b, 0, 0))
    oblk = pl.BlockSpec((1, rows, KV_DIM), lambda b: (b, 0, 0))
    return pl.pallas_call(
        _compress_kernel,
        out_shape=[jax.ShapeDtypeStruct((batch, rows, KV_DIM), BF16)] * 2,
        grid=(batch,),
        in_specs=[blk, blk,
                  pl.BlockSpec(pos_ext.shape, lambda b: (0, 0, 0, 0)),
                  pl.BlockSpec(w1_ext.shape, lambda b: (0, 0, 0, 0)),
                  pl.BlockSpec(w2_ext.shape, lambda b: (0, 0, 0))],
        out_specs=[oblk, oblk],
        compiler_params=_cparams(("parallel",)),
        name="compress_kv",
    )(kr, vr, pos_ext, w1_ext, w2_ext)


def _safe_inv(l):
    return jnp.where(l > 0.0, 1.0 / jnp.where(l > 0.0, l, 1.0), 0.0)


def _nsa_kernel(qn_ref, qr_ref, kc_ref, vc_ref, ks_ref, vs_ref, kw_ref, vw_ref, gate_ref,
                selq_ref, mselt_ref, eexp_ref, o_ref, *, seq, n_sel, sel_chunk, win_len):
    h = pl.program_id(1)
    qt = pl.program_id(2)
    t0 = qt * Q_BLOCK
    n_blk = seq // SEL_BLOCK
    n_cmp_pad = seq // CMP_STRIDE
    rows = GQA * Q_BLOCK
    tpos = t0 + lax.broadcasted_iota(jnp.int32, (1, Q_BLOCK, 1), 1)
    lane = lax.broadcasted_iota(jnp.int32, (1, LANES), 1)
    head_lanes = (lane // HEAD_DIM) == h

    qn = qn_ref[...]
    qr = qr_ref[...]
    qn4 = jnp.concatenate([_mm(qn, selq_ref[0, g]) for g in range(GQA)], axis=0).astype(BF16)
    qr4 = jnp.concatenate([_mm(qr, selq_ref[0, g]) for g in range(GQA)], axis=0).astype(BF16)

    kc = kc_ref[0]
    vc = vc_ref[0]
    cpos = lax.broadcasted_iota(jnp.int32, (1, 1, n_cmp_pad), 2) * CMP_STRIDE + (CMP_BLOCK - 1)
    cmask = cpos <= tpos
    s = jnp.where(cmask, _nt(qn4, kc).reshape(GQA, Q_BLOCK, n_cmp_pad), NEG_INF)
    e = jnp.where(cmask, jnp.exp(s - jnp.max(s, axis=2, keepdims=True)), 0.0)
    p = e * _safe_inv(jnp.sum(e, axis=2, keepdims=True))
    imp = jnp.sum(p, axis=0)
    o_cmp = _mm(p.reshape(rows, n_cmp_pad).astype(BF16), vc)

    pslc = _nt(mselt_ref[...], imp, precision=HIGHEST)[:n_blk]
    tq = t0 + lax.broadcasted_iota(jnp.int32, (1, Q_BLOCK), 1)
    jblk = lax.broadcasted_iota(jnp.int32, (n_blk, 1), 0)
    cur = tq // SEL_BLOCK
    valid = jblk * SEL_BLOCK <= tq
    forced = (jblk == 0) | ((cur - jblk >= 0) & (cur - jblk < N_LOCAL))
    score = jnp.where(forced, BIG, jnp.where(valid, pslc, -BIG))
    rank = jnp.zeros((n_blk, Q_BLOCK), jnp.int32)
    for jp in range(n_blk):
        row = score[jp:jp + 1, :]
        beats = (row > score) | ((row == score) & (jblk > jp))
        rank = rank + beats.astype(jnp.int32)
    sel = jnp.where((rank < n_sel) & (score > -0.5 * BIG), 1.0, 0.0)
    sel_t = jnp.concatenate([sel, jnp.zeros((LANES - n_blk, Q_BLOCK), F32)], axis=0).astype(BF16)

    n_chunks = (t0 + Q_BLOCK - 1) // sel_chunk + 1

    def sel_body(c, carry):
        m, l, acc = carry
        start = pl.multiple_of(c * sel_chunk, sel_chunk)
        kblk = ks_ref[0, pl.ds(start, sel_chunk), :]
        vblk = vs_ref[0, pl.ds(start, sel_chunk), :]
        kpos = start + lax.broadcasted_iota(jnp.int32, (1, 1, sel_chunk), 2)
        mask = (_tn(sel_t, eexp_ref[c]) > 0.5)[None] & (kpos <= tpos)
        s = jnp.where(mask, _nt(qr4, kblk).reshape(GQA, Q_BLOCK, sel_chunk), NEG_INF)
        m_new = jnp.maximum(m, jnp.max(s, axis=2, keepdims=True))
        alpha = jnp.exp(m - m_new)
        p = jnp.exp(s - m_new)
        l_new = alpha * l + jnp.sum(p, axis=2, keepdims=True)
        pv = _mm(p.reshape(rows, sel_chunk).astype(BF16), vblk).reshape(GQA, Q_BLOCK, LANES)
        return m_new, l_new, alpha * acc + pv

    init = (jnp.full((GQA, Q_BLOCK, 1), NEG_INF, F32), jnp.zeros((GQA, Q_BLOCK, 1), F32),
            jnp.zeros((GQA, Q_BLOCK, LANES), F32))
    _, l, acc = lax.fori_loop(0, n_chunks, sel_body, init)
    o_slc = acc * _safe_inv(l)

    ws = pl.multiple_of(jnp.maximum(qt - WINDOW // Q_BLOCK, 0) * Q_BLOCK, Q_BLOCK)
    kwb = kw_ref[0, pl.ds(ws, win_len), :]
    vwb = vw_ref[0, pl.ds(ws, win_len), :]
    diff = tpos - (ws + lax.broadcasted_iota(jnp.int32, (1, 1, win_len), 2))
    wmask = (diff >= 0) & (diff < WINDOW)
    s = jnp.where(wmask, _nt(qr4, kwb).reshape(GQA, Q_BLOCK, win_len), NEG_INF)
    e = jnp.exp(s - jnp.max(s, axis=2, keepdims=True))
    o_win = (_mm(e.reshape(rows, win_len).astype(BF16), vwb).reshape(GQA, Q_BLOCK, LANES)
             * _safe_inv(jnp.sum(e, axis=2, keepdims=True)))

    gate = gate_ref[...]
    o_cmp = o_cmp.reshape(GQA, Q_BLOCK, LANES)
    og = []
    for g in range(GQA):
        o = (gate[:, 3 * g:3 * g + 1] * o_cmp[g] + gate[:, 3 * g + 1:3 * g + 2] * o_slc[g]
             + gate[:, 3 * g + 2:3 * g + 3] * o_win[g])
        og.append(jnp.where(head_lanes, o, pltpu.roll(o, HEAD_DIM, 1)))
    low = lane < HEAD_DIM
    o_ref[:, 0:LANES] = jnp.where(low, og[0], og[1])
    o_ref[:, LANES:2 * LANES] = jnp.where(low, og[2], og[3])


def nsa_attention(qn, qr, kcmp, vcmp, ks, vs, kw, vw, gate, selq, msel, eexp, batch, seq):
    n = batch * seq
    nq = seq // Q_BLOCK
    sel_chunk = min(SEL_CHUNK, seq)
    win_len = min(WINDOW + Q_BLOCK, seq)
    n_sel = min(N_SEL, seq // SEL_BLOCK)
    qspec = pl.BlockSpec((Q_BLOCK, GQA * HEAD_DIM), lambda b, h, q: (b * nq + q, h))
    cspec = pl.BlockSpec((1, seq // CMP_STRIDE, KV_DIM), lambda b, h, q: (b, 0, 0))
    kspec = pl.BlockSpec((1, seq, KV_DIM), lambda b, h, q: (b, 0, 0))
    kern = functools.partial(_nsa_kernel, seq=seq, n_sel=n_sel, sel_chunk=sel_chunk, win_len=win_len)
    return pl.pallas_call(
        kern,
        out_shape=jax.ShapeDtypeStruct((n, NSA_DIM), F32),
        grid=(batch, N_KV_HEADS, nq),
        in_specs=[qspec, qspec, cspec, cspec, kspec, kspec, kspec, kspec,
                  pl.BlockSpec((Q_BLOCK, LANES), lambda b, h, q: (b * nq + q, h)),
                  pl.BlockSpec((1, GQA, GQA * HEAD_DIM, LANES), lambda b, h, q: (h, 0, 0, 0)),
                  pl.BlockSpec(msel.shape, lambda b, h, q: (0, 0)),
                  pl.BlockSpec(eexp.shape, lambda b, h, q: (0, 0, 0))],
        out_specs=qspec,
        compiler_params=_cparams(("parallel", "parallel", "arbitrary")),
        name="nsa_attention",
    )(qn, qr, kcmp, vcmp, ks, vs, kw, vw, gate, selq, msel, eexp)


def _route(h2, rwt_ref, rb_ref):
    rw = rwt_ref[...]
    rw_hi = rw.astype(BF16)
    rw_lo = (rw - rw_hi.astype(F32)).astype(BF16)
    h_hi = h2.astype(BF16)
    h_lo = (h2 - h_hi.astype(F32)).astype(BF16)
    logits = _nt(rw_hi, h_hi) + (_nt(rw_hi, h_lo) + _nt(rw_lo, h_hi))
    scores = jax.nn.sigmoid(logits)
    biased = scores + rb_ref[...]
    rows = [biased[e:e + 1, :] for e in range(N_EXPERTS)]
    srow = [scores[e:e + 1, :] for e in range(N_EXPERTS)]
    gscore = []
    for gi in range(N_EXPERT_GROUPS):
        r = rows[gi * EXPERTS_PER_GROUP:(gi + 1) * EXPERTS_PER_GROUP]
        best = None
        for a in range(EXPERTS_PER_GROUP):
            for b in range(a + 1, EXPERTS_PER_GROUP):
                pair = r[a] + r[b]
                best = pair if best is None else jnp.maximum(best, pair)
        gscore.append(best)
    top_val = gscore[0]
    top_grp = jnp.zeros_like(top_val, dtype=jnp.int32)
    for gi in range(1, N_EXPERT_GROUPS):
        upd = gscore[gi] > top_val
        top_grp = jnp.where(upd, gi, top_grp)
        top_val = jnp.where(upd, gscore[gi], top_val)
    masked = [jnp.where(top_grp == e // EXPERTS_PER_GROUP, rows[e], NEG_INF) for e in range(N_EXPERTS)]
    b1 = masked[0]
    i1 = jnp.zeros_like(top_grp)
    for e in range(1, N_EXPERTS):
        upd = masked[e] > b1
        i1 = jnp.where(upd, e, i1)
        b1 = jnp.where(upd, masked[e], b1)
    b2 = None
    i2 = None
    for e in range(N_EXPERTS):
        v = jnp.where(i1 == e, -jnp.inf, masked[e])
        if b2 is None:
            b2, i2 = v, jnp.zeros_like(top_grp)
        else:
            upd = v > b2
            i2 = jnp.where(upd, e, i2)
            b2 = jnp.where(upd, v, b2)
    s1 = jnp.zeros_like(top_val)
    s2 = jnp.zeros_like(top_val)
    for e in range(N_EXPERTS):
        s1 = s1 + jnp.where(i1 == e, srow[e], 0.0)
        s2 = s2 + jnp.where(i2 == e, srow[e], 0.0)
    tot = s1 + s2
    return i1, i2, s1 / tot, s2 / tot


def _tail(cat_bf16, x_ref, g1_ref, wout_ref, ng_ref, sh2_ref, sc2_ref, rwt_ref, rb_ref, ustrict_ref,
          x1_ref, hrec_ref, meta_ref, ridx_ref, cnt_ref, carry_ref):
    y = _mm(cat_bf16, wout_ref[...])
    x1 = x_ref[...] + g1_ref[0] * y
    x1_ref[...] = x1
    h2 = _norm_mod(x1, ng_ref[...], sh2_ref[0], sc2_ref[0])
    tm, d = h2.shape
    _rows_to_tiles(h2, hrec_ref)

    i1, i2, w1, w2 = _route(h2, rwt_ref, rb_ref)
    lo = jnp.minimum(i1, i2) % EXPERTS_PER_GROUP
    hi = jnp.maximum(i1, i2) % EXPERTS_PER_GROUP
    pair = jnp.where(lo == 0, hi - 1, jnp.where(lo == 1, jnp.where(hi == 3, 3, 4), 5))
    cls = (i1 // EXPERTS_PER_GROUP) * PAIRS_PER_GROUP + pair
    w_lo = jnp.where(i1 < i2, w1, w2)
    w_hi = jnp.where(i1 < i2, w2, w1)
    meta_t = jnp.concatenate([w_lo, w_hi, jnp.zeros((LANES - 2, tm), F32)], axis=0)
    meta_ref[...] = meta_t.T

    @pl.when(pl.program_id(0) == 0)
    def _():
        carry_ref[...] = jnp.zeros_like(carry_ref)

    hit = lax.broadcasted_iota(jnp.int32, (CLASS_ROWS, 1), 0) == cls
    cnt = jnp.where(hit, 1.0, 0.0)
    before = _mm(cnt.astype(BF16), ustrict_ref[...]) + carry_ref[:, 0:1]
    carry = carry_ref[...] + jnp.sum(cnt, axis=1, keepdims=True)
    carry_ref[...] = carry
    cnt_ref[...] = carry.astype(jnp.int32)
    ridx_ref[0:1, :] = cls
    ridx_ref[1:2, :] = jnp.sum(jnp.where(hit, before, 0.0), axis=0, keepdims=True).astype(jnp.int32)


def _even_out_kernel(o_ref, u_ref, uh_ref, bg_ref, cw_ref, *tail_refs, tiles_per_seq):
    first = (pl.program_id(0) % tiles_per_seq) == 0
    u = u_ref[...]
    tm = u.shape[0]
    halo = jnp.where(first, 0.0, uh_ref[...])
    ext = jnp.concatenate([halo, u], axis=0)
    u1 = pltpu.roll(ext, 1, 0)[CONV_HALO:]
    u2 = pltpu.roll(ext, 2, 0)[CONV_HALO:]
    cw = cw_ref[...]
    y_conv = bg_ref[...] * (cw[2:3] * u + cw[1:2] * u1 + cw[0:1] * u2)
    cat = jnp.concatenate([o_ref[...], y_conv], axis=1).astype(BF16)
    _tail(cat, *tail_refs)


def _tail_specs(tm, d, tpb):
    row = lambda i: (i, 0)
    per_b = lambda i: (i // tpb, 0, 0)
    const2 = lambda i: (0, 0)
    ins = [pl.BlockSpec((tm, d), row),
           pl.BlockSpec((1, 1, d), per_b),
           pl.BlockSpec((d, d), const2),
           pl.BlockSpec((1, d), const2),
           pl.BlockSpec((1, 1, d), per_b),
           pl.BlockSpec((1, 1, d), per_b),
           pl.BlockSpec((N_EXPERTS, d), const2),
           pl.BlockSpec((N_EXPERTS, 1), const2),
           pl.BlockSpec((tm, tm), const2)]
    outs = [pl.BlockSpec((tm, d), row),
            pl.BlockSpec((tm, d // LANES, LANES), lambda i: (i, 0, 0)),
            pl.BlockSpec((tm, LANES), row),
            pl.BlockSpec((2, tm), lambda i: (0, i)),
            pl.BlockSpec((CLASS_ROWS, LANES), const2)]
    scratch = [pltpu.VMEM((CLASS_ROWS, LANES), F32)]
    return ins, outs, scratch


def _tail_out_shapes(n, d):
    return [jax.ShapeDtypeStruct((n, d), F32), jax.ShapeDtypeStruct((n, d // LANES, LANES), F32),
            jax.ShapeDtypeStruct((n, LANES), F32),
            jax.ShapeDtypeStruct((2, n), jnp.int32), jax.ShapeDtypeStruct((CLASS_ROWS, LANES), jnp.int32)]


def _strict_upper(tm):
    return jnp.asarray(np.triu(np.ones((tm, tm), np.float32), 1), dtype=BF16)


def even_out_proj(o_nsa, u, bg, conv_w, x2, g1, w_out, ng, sh2, sc2, rwt, rb, seq):
    n, d = x2.shape
    tm = min(TOK_TILE, seq)
    tpb = seq // tm
    row = lambda i: (i, 0)
    halo = lambda i: (jnp.maximum(i * (tm // CONV_HALO) - 1, 0), 0)
    tin, tout, tscratch = _tail_specs(tm, d, tpb)
    return pl.pallas_call(
        functools.partial(_even_out_kernel, tiles_per_seq=tpb),
        out_shape=_tail_out_shapes(n, d),
        grid=(n // tm,),
        in_specs=[pl.BlockSpec((tm, NSA_DIM), row),
                  pl.BlockSpec((tm, CONV_DIM), row),
                  pl.BlockSpec((CONV_HALO, CONV_DIM), halo),
                  pl.BlockSpec((tm, CONV_DIM), row),
                  pl.BlockSpec(conv_w.shape, lambda i: (0, 0))] + tin,
        out_specs=tout,
        scratch_shapes=tscratch,
        compiler_params=_cparams(("arbitrary",)),
        name="even_out_proj",
    )(o_nsa, u, u, bg, conv_w, x2, g1, w_out, ng, sh2, sc2, rwt, rb, _strict_upper(tm))


def _dispatch_plan(ridx, counts, n):
    cnt = counts[:N_CLASSES, 0]
    padded = (cnt + MOE_ROW_TILE - 1) // MOE_ROW_TILE * MOE_ROW_TILE
    ends = jnp.cumsum(padded)
    starts = ends - padded
    cids = jnp.arange(N_CLASSES, dtype=jnp.int32)[:, None]
    base = jnp.sum(jnp.where(cids == ridx[0][None, :], starts[:, None], 0), axis=0)
    dest = (base + ridx[1]).astype(jnp.int32)
    td = min(MOE_DMA_TILE, n)
    dest3 = dest.reshape(n // td, 1, td)
    n_tiles = n // MOE_ROW_TILE + N_CLASSES
    tile_start = jnp.arange(n_tiles, dtype=jnp.int32) * MOE_ROW_TILE
    tile_class = jnp.minimum(jnp.sum(tile_start[:, None] >= ends[None, :], axis=1), N_CLASSES - 1)
    group_base = (tile_class // PAIRS_PER_GROUP) * EXPERTS_PER_GROUP
    pair = tile_class % PAIRS_PER_GROUP
    tile_lo = (group_base + jnp.asarray(PAIR_LO, jnp.int32)[pair]).astype(jnp.int32)
    tile_hi = (group_base + jnp.asarray(PAIR_HI, jnp.int32)[pair]).astype(jnp.int32)
    n_used = (ends[-1] // MOE_ROW_TILE).reshape(1).astype(jnp.int32)
    last_tile = jnp.where(cnt > 0, ends // MOE_ROW_TILE - 1, -1)
    tail = n_used[0] + jnp.arange(N_CLASSES, dtype=jnp.int32)
    zero_tiles = jnp.concatenate([last_tile, jnp.where(tail < n_tiles, tail, -1)]).astype(jnp.int32)
    return dest3, tile_lo, tile_hi, n_used, zero_tiles, n_tiles


def _dispatch_kernel(ztile_ref, dest_ref, hrec_ref, meta_ref, xs_hbm, xm_hbm, zrec, zmeta, zsem, sem):
    td = hrec_ref.shape[0]

    @pl.when(pl.program_id(0) == 0)
    def _():
        zrec[...] = jnp.zeros_like(zrec)
        zmeta[...] = jnp.zeros_like(zmeta)

        def zero_copies(k):
            start = pl.multiple_of(ztile_ref[k] * MOE_ROW_TILE, MOE_ROW_TILE)
            return (pltpu.make_async_copy(zrec, xs_hbm.at[pl.ds(start, MOE_ROW_TILE)], zsem),
                    pltpu.make_async_copy(zmeta, xm_hbm.at[pl.ds(start, MOE_ROW_TILE)], zsem))

        for k in range(2 * N_CLASSES):
            @pl.when(ztile_ref[k] >= 0)
            def _():
                for c in zero_copies(k):
                    c.start()
        for k in range(2 * N_CLASSES):
            @pl.when(ztile_ref[k] >= 0)
            def _():
                for c in zero_copies(k):
                    c.wait()

    for r in range(td):
        dst = dest_ref[0, 0, r]
        pltpu.make_async_copy(hrec_ref.at[pl.ds(r, 1)], xs_hbm.at[pl.ds(dst, 1)], sem).start(
            priority=r % 2)
        pltpu.make_async_copy(meta_ref.at[pl.ds(r, 1)], xm_hbm.at[pl.ds(dst, 1)], sem).start(
            priority=(r + 1) % 2)
    pltpu.make_async_copy(hrec_ref, xs_hbm.at[pl.ds(0, td)], sem).wait()
    pltpu.make_async_copy(meta_ref, xm_hbm.at[pl.ds(0, td)], sem).wait()


def moe_dispatch(hrec, meta, dest3, zero_tiles, n_tiles):
    n, sub, lanes = hrec.shape
    td = dest3.shape[2]
    rows = n_tiles * MOE_ROW_TILE
    return pl.pallas_call(
        _dispatch_kernel,
        out_shape=[jax.ShapeDtypeStruct((rows, sub, lanes), F32),
                   jax.ShapeDtypeStruct((rows, lanes), F32)],
        grid_spec=pltpu.PrefetchScalarGridSpec(
            num_scalar_prefetch=1,
            grid=(n // td,),
            in_specs=[pl.BlockSpec((1, 1, td), lambda i, z: (i, 0, 0), memory_space=pltpu.SMEM),
                      pl.BlockSpec((td, sub, lanes), lambda i, z: (i, 0, 0)),
                      pl.BlockSpec((td, lanes), lambda i, z: (i, 0))],
            out_specs=[pl.BlockSpec(memory_space=pl.ANY), pl.BlockSpec(memory_space=pl.ANY)],
            scratch_shapes=[pltpu.VMEM((MOE_ROW_TILE, sub, lanes), F32),
                            pltpu.VMEM((MOE_ROW_TILE, lanes), F32),
                            pltpu.SemaphoreType.DMA(()), pltpu.SemaphoreType.DMA(())]),
        compiler_params=_cparams(("arbitrary",)),
        name="moe_dispatch",
    )(zero_tiles, dest3, hrec, meta)


def _expert_kernel(lo_ref, hi_ref, nused_ref, xs_ref, xm_ref, wg_lo, wu_lo, wd_lo, wg_hi, wu_hi, wd_hi,
                   ys_ref, *wb):
    t = pl.program_id(0)
    prev = jnp.maximum(t - 1, 0)

    for ids, srcs, dsts in ((lo_ref, (wg_lo, wu_lo, wd_lo), wb[:3]), (hi_ref, (wg_hi, wu_hi, wd_hi), wb[3:])):
        @pl.when((t == 0) | (ids[t] != ids[prev]))
        def _():
            for src, dst in zip(srcs, dsts):
                dst[...] = src[0, 0].astype(BF16)

    @pl.when(t < nused_ref[0])
    def _():
        x = _tiles_to_rows(xs_ref).astype(BF16)
        meta = xm_ref[...]
        y = None
        for k in range(2):
            a = _mm(x, wb[3 * k][...])
            b = _mm(x, wb[3 * k + 1][...])
            he = (a * jax.nn.sigmoid(a)) * b
            yk = meta[:, k:k + 1] * _mm(he.astype(BF16), wb[3 * k + 2][...])
            y = yk if y is None else y + yk
        _rows_to_tiles(y, ys_ref)

    @pl.when(t >= nused_ref[0])
    def _():
        ys_ref[...] = jnp.zeros_like(ys_ref)


def moe_experts(xs, xm, tile_lo, tile_hi, n_used, w_gate, w_up, w_down, layer, d):
    rows, sub, lanes = xs.shape
    n_tiles = rows // MOE_ROW_TILE
    used = lambda t, lo, hi, nu: jnp.minimum(t, nu[0] - 1)
    lo_spec = lambda shape: pl.BlockSpec((1, 1) + shape, lambda t, lo, hi, nu: (layer, lo[t], 0, 0))
    hi_spec = lambda shape: pl.BlockSpec((1, 1) + shape, lambda t, lo, hi, nu: (layer, hi[t], 0, 0))
    shapes = ((d, D_EXPERT), (d, D_EXPERT), (D_EXPERT, d))
    return pl.pallas_call(
        _expert_kernel,
        out_shape=jax.ShapeDtypeStruct((rows, sub, lanes), F32),
        grid_spec=pltpu.PrefetchScalarGridSpec(
            num_scalar_prefetch=3,
            grid=(n_tiles,),
            in_specs=[pl.BlockSpec((MOE_ROW_TILE, sub, lanes),
                                   lambda t, lo, hi, nu: (used(t, lo, hi, nu), 0, 0)),
                      pl.BlockSpec((MOE_ROW_TILE, lanes),
                                   lambda t, lo, hi, nu: (used(t, lo, hi, nu), 0))]
                     + [lo_spec(s) for s in shapes] + [hi_spec(s) for s in shapes],
            out_specs=pl.BlockSpec((MOE_ROW_TILE, sub, lanes), lambda t, lo, hi, nu: (t, 0, 0)),
            scratch_shapes=[pltpu.VMEM(s, BF16) for s in shapes + shapes]),
        compiler_params=_cparams(("arbitrary",)),
        name="moe_experts",
    )(tile_lo, tile_hi, n_used, xs, xm, w_gate, w_up, w_down, w_gate, w_up, w_down)


def _combine_kernel(dest_ref, ys_hbm, x_ref, g2_ref, fn_ref, o_ref, buf, sem, *, final_norm):
    tc = x_ref.shape[0]

    for r in range(tc):
        pltpu.make_async_copy(ys_hbm.at[pl.ds(dest_ref[0, 0, r], 1)], buf.at[pl.ds(r, 1)], sem).start(
            priority=r % 2)
    pltpu.make_async_copy(ys_hbm.at[pl.ds(0, tc)], buf, sem).wait()
    x = x_ref[...] + g2_ref[0] * _tiles_to_rows(buf)
    if final_norm:
        ms = jnp.mean(x * x, axis=-1, keepdims=True)
        x = x * lax.rsqrt(ms + NORM_EPS) * fn_ref[...]
    o_ref[...] = x


def moe_combine(ys, dest3, x1, g2, fnorm, seq, final_norm):
    n, d = x1.shape
    tc = dest3.shape[2]
    tpb = seq // tc
    return pl.pallas_call(
        functools.partial(_combine_kernel, final_norm=final_norm),
        out_shape=jax.ShapeDtypeStruct((n, d), F32),
        grid=(n // tc,),
        in_specs=[pl.BlockSpec((1, 1, tc), lambda i: (i, 0, 0), memory_space=pltpu.SMEM),
                  pl.BlockSpec(memory_space=pl.ANY),
                  pl.BlockSpec((tc, d), lambda i: (i, 0)),
                  pl.BlockSpec((1, 1, d), lambda i: (i // tpb, 0, 0)),
                  pl.BlockSpec((1, d), lambda i: (0, 0))],
        out_specs=pl.BlockSpec((tc, d), lambda i: (i, 0)),
        scratch_shapes=[pltpu.VMEM((tc, d // LANES, LANES), F32), pltpu.SemaphoreType.DMA(())],
        compiler_params=_cparams(("arbitrary",)),
        name="moe_combine",
    )(dest3, ys, x1, g2, fnorm)


def moe_sparse(route, w_gate, w_up, w_down, layer, x1, g2, fnorm, seq, final_norm):
    hrec, meta, ridx, counts = route
    n, d = x1.shape
    dest3, tile_lo, tile_hi, n_used, zero_tiles, n_tiles = _dispatch_plan(ridx, counts, n)
    xs, xm = moe_dispatch(hrec, meta, dest3, zero_tiles, n_tiles)
    ys = moe_experts(xs, xm, tile_lo, tile_hi, n_used, w_gate, w_up, w_down, layer, d)
    return moe_combine(ys, dest3, x1, g2, fnorm, seq, final_norm)


def _odd_in_kernel(x_ref, g_ref, sh_ref, sc_ref, w_ref, mu_ref, w0_ref, w2_ref, a0_ref, a2_ref,
                   g2_ref, kk_ref, ka_ref, ones_ref, pw_ref, ps_ref,
                   r_ref, lw_ref, km_ref, v_ref, kn_ref, kb_ref, gg_ref, op_ref,
                   rw_carry, u_carry, *, tiles_per_seq, tm):
    i = pl.program_id(0)
    first = (i % tiles_per_seq) == 0
    h = _norm_mod(x_ref[...], g_ref[...], sh_ref[0], sc_ref[0])
    proj = _mm(h.astype(BF16), w_ref[...])

    rw = proj[:, :ODD_RW_COLS]
    row0 = jnp.where(first, 0.0, rw_carry[0:1, :])
    ridx = lax.broadcasted_iota(jnp.int32, (tm, 1), 0)
    prev = jnp.where(ridx == 0, row0, pltpu.roll(rw, 1, 0))
    rw_carry[0:1, :] = rw[tm - 1:tm, :]
    rw = rw + (prev - rw) * mu_ref[...]

    r = rw[:, 0:512]
    k = rw[:, 512:1024]
    v = rw[:, 1024:1536]
    wl = rw[:, 1536:1664]
    al = rw[:, 1664:1792]
    gl = rw[:, 1792:1920]
    z = -(w0_ref[...] + _mm(jnp.tanh(wl).astype(BF16), w2_ref[...]))
    softplus = jnp.maximum(z, 0.0) + jnp.log1p(jnp.exp(-jnp.abs(z)))
    w_log = -softplus - 0.5
    a = jax.nn.sigmoid(a0_ref[...] + _mm(al.astype(BF16), a2_ref[...]))
    gg_ref[...] = _mm(jax.nn.sigmoid(gl).astype(BF16), g2_ref[...])
    kk0 = k * kk_ref[...]
    ss = _split_sum(kk0 * kk0, ones_ref[...])
    kk = kk0 / jnp.maximum(jnp.sqrt(ss), 1e-12)
    r_ref[...] = r
    lw_ref[...] = -jnp.exp(w_log)
    km_ref[...] = k * (1.0 + (a - 1.0) * ka_ref[...])
    v_ref[...] = v
    kn_ref[...] = kk
    kb_ref[...] = kk * a

    u = proj[:, ODD_RW_COLS:]
    halo = jnp.where(first, 0.0, u_carry[...])
    u_carry[...] = u[tm - POOL_HALO:, :]
    ext = jnp.concatenate([halo, u], axis=0)
    tseq = (i % tiles_per_seq) * tm + ridx
    for gi, win in enumerate(POOL_WINDOWS):
        xg = ext[:, gi * POOL_GROUP:(gi + 1) * POOL_GROUP]
        s = xg
        step = 1
        while step < win:
            s = s + pltpu.roll(s, step, 0)
            step *= 2
        cnt = jnp.minimum(tseq + 1, win).astype(F32)
        pooled = s[POOL_HALO:] / cnt - xg[POOL_HALO:]
        mixed = _mm(pooled.astype(BF16), pw_ref[gi])
        op_ref[:, gi * POOL_GROUP:(gi + 1) * POOL_GROUP] = (
            mixed * ps_ref[:, gi * POOL_GROUP:(gi + 1) * POOL_GROUP])


def odd_in_proj(x2, g, sh, sc, w_pad, mu_pad, w0, w2p, a0, a2p, g2, k_k, k_a, ones_bd, pool_w,
                pool_scale, seq):
    n, d = x2.shape
    tm = min(TOK_TILE, seq)
    tpb = seq // tm
    row = lambda i: (i, 0)
    per_b = lambda i: (i // tpb, 0, 0)
    c2 = lambda i: (0, 0)
    full2 = lambda a: pl.BlockSpec(a.shape, c2)
    return pl.pallas_call(
        functools.partial(_odd_in_kernel, tiles_per_seq=tpb, tm=tm),
        out_shape=[jax.ShapeDtypeStruct((n, RWKV_DIM), F32)] * 8,
        grid=(n // tm,),
        in_specs=[pl.BlockSpec((tm, d), row), pl.BlockSpec((1, d), c2),
                  pl.BlockSpec((1, 1, d), per_b), pl.BlockSpec((1, 1, d), per_b),
                  full2(w_pad), full2(mu_pad), full2(w0), full2(w2p), full2(a0), full2(a2p),
                  full2(g2), full2(k_k), full2(k_a), full2(ones_bd),
                  pl.BlockSpec(pool_w.shape, lambda i: (0, 0, 0)), full2(pool_scale)],
        out_specs=[pl.BlockSpec((tm, RWKV_DIM), row)] * 8,
        scratch_shapes=[pltpu.VMEM((SUBLANES, ODD_RW_COLS), F32),
                        pltpu.VMEM((POOL_HALO, RWKV_DIM), F32)],
        compiler_params=_cparams(("arbitrary",)),
        name="odd_in_proj",
    )(x2, g, sh, sc, w_pad, mu_pad, w0, w2p, a0, a2p, g2, k_k, k_a, ones_bd, pool_w, pool_scale)


def _bmm(a, b):
    return lax.dot_general(a, b, (((2,), (1,)), ((0,), (0,))), preferred_element_type=F32)


def _bnt(a, b):
    return lax.dot_general(a, b, (((2,), (2,)), ((0,), (0,))), preferred_element_type=F32)


def _btn(a, b):
    return lax.dot_general(a, b, (((1,), (1,)), ((0,), (0,))), preferred_element_type=F32)


def _scan_prep_kernel(r_ref, lw_ref, km_ref, v_ref, kn_ref, kb_ref, qe_ref, y0_ref, mt_ref, ct_ref,
                      *, chunk, cb):
    L = chunk
    rows = cb * L
    n_pairs = N_RWKV_HEADS // 2
    two = 2 * L
    rowt = lax.broadcasted_iota(jnp.int32, (rows, 1), 0) % L
    lane = lax.broadcasted_iota(jnp.int32, (1, 1, LANES), 2)
    low = lane < HEAD_DIM
    ri = lax.broadcasted_iota(jnp.int32, (two, two), 0)
    ci = lax.broadcasted_iota(jnp.int32, (two, two), 1)
    same_blk = (ri // L) == (ci // L)
    strict = same_blk & ((ci % L) < (ri % L))
    incl = same_blk & ((ci % L) <= (ri % L))
    li = lax.broadcasted_iota(jnp.int32, (LANES, LANES), 0)
    lj = lax.broadcasted_iota(jnp.int32, (LANES, LANES), 1)
    same_head = (li // HEAD_DIM) == (lj // HEAD_DIM)
    eye = li == lj

    lw = lw_ref[...]
    cum = lw
    step = 1
    while step < L:
        cum = cum + jnp.where(rowt >= step, pltpu.roll(cum, step, 0), 0.0)
        step *= 2

    def to3(x):
        x3 = x.reshape(cb, L, RWKV_DIM)
        return jnp.concatenate([x3[:, :, p * LANES:(p + 1) * LANES] for p in range(n_pairs)], axis=0)

    def stack2(x):
        return jnp.concatenate([jnp.where(low, x, 0.0), jnp.where(low, 0.0, x)], axis=1)

    def fold(x):
        return x[:, :L, :] + x[:, L:, :]

    cum3 = to3(cum)
    lw3 = to3(lw)
    cum_l = cum3[:, L - 1:L, :]
    g_inv = jnp.exp(-cum3)
    g_tail = jnp.exp(cum_l - cum3)
    kb = to3(kb_ref[...])
    km = to3(km_ref[...])
    v = to3(v_ref[...])
    at_s = stack2(-to3(kn_ref[...]) * jnp.exp(cum3 - lw3))
    rt_s = stack2(to3(r_ref[...]) * jnp.exp(cum3))
    v_s = stack2(v).astype(BF16)
    lhs = jnp.concatenate([at_s, rt_s], axis=1).astype(BF16)
    rhs = jnp.concatenate([stack2(kb * g_inv), stack2(km * g_inv)], axis=1).astype(BF16)
    prod = _bnt(lhs, rhs)
    nmat = jnp.where(strict, prod[:, :two, :two], 0.0)
    a_ak = jnp.where(strict, prod[:, :two, two:], 0.0).astype(BF16)
    a_rb = jnp.where(incl, prod[:, two:, :two], 0.0).astype(BF16)
    a_rk = jnp.where(incl, prod[:, two:, two:], 0.0).astype(BF16)

    x = jnp.concatenate([at_s, _bmm(a_ak, v_s)], axis=2)
    npow = nmat
    step = 1
    while step < L:
        nb = npow.astype(BF16)
        x = x + _bmm(nb, x.astype(BF16))
        step *= 2
        if step < L:
            npow = _bmm(nb, nb)
    qy = _bmm(a_rb, x.astype(BF16))
    qe = fold(rt_s + qy[:, :, :LANES])
    y0 = fold(qy[:, :, LANES:] + _bmm(a_rk, v_s))
    wu = fold(x).astype(BF16)
    bwu = _btn((kb * g_tail).astype(BF16), wu)
    kv = _btn((km * g_tail).astype(BF16), v.astype(BF16))
    g_l = jnp.broadcast_to(jnp.exp(cum_l), (n_pairs * cb, LANES, LANES))
    mt = jnp.where(eye, g_l, 0.0) + jnp.where(same_head, bwu[:, :, :LANES], 0.0)
    ct = jnp.where(same_head, bwu[:, :, LANES:] + kv, 0.0)
    for p in range(n_pairs):
        sl = slice(p * LANES, (p + 1) * LANES)
        qe_ref[:, sl] = qe[p * cb:(p + 1) * cb].reshape(rows, LANES)
        y0_ref[:, sl] = y0[p * cb:(p + 1) * cb].reshape(rows, LANES)
        mt_ref[:, p] = mt[p * cb:(p + 1) * cb].astype(BF16)
        ct_ref[:, p] = ct[p * cb:(p + 1) * cb]


def _scan_state_kernel(qe_ref, y0_ref, mt_ref, ct_ref, y_ref, st_ref, *, batch):
    @pl.when(pl.program_id(0) == 0)
    def _():
        st_ref[...] = jnp.zeros_like(st_ref)

    n_pairs = N_RWKV_HEADS // 2
    qe = qe_ref[...]
    qe3 = jnp.concatenate([qe[:, :, p * LANES:(p + 1) * LANES] for p in range(n_pairs)], axis=0)
    st = st_ref[...].astype(BF16)
    y = _bmm(qe3.astype(BF16), st)
    for p in range(n_pairs):
        sl = slice(p * LANES, (p + 1) * LANES)
        y_ref[:, :, sl] = y[p * batch:(p + 1) * batch] + y0_ref[:, :, sl]
    mt = jnp.concatenate([mt_ref[:, 0, p] for p in range(n_pairs)], axis=0)
    ct = jnp.concatenate([ct_ref[:, 0, p] for p in range(n_pairs)], axis=0)
    st_ref[...] = _bmm(mt, st) + ct


def rwkv_scan(r, lw, km, v, kn, kb, batch, seq):
    n = batch * seq
    chunk = min(SCAN_CHUNK, seq)
    nc = seq // chunk
    cb = min(SCAN_CHUNKS_PER_STEP, nc)
    n_pairs = N_RWKV_HEADS // 2
    blk = pl.BlockSpec((cb * chunk, RWKV_DIM), lambda i: (i, 0))
    mblk = pl.BlockSpec((cb, n_pairs, LANES, LANES), lambda i: (i, 0, 0, 0))
    qe, y0, mt, ct = pl.pallas_call(
        functools.partial(_scan_prep_kernel, chunk=chunk, cb=cb),
        out_shape=[jax.ShapeDtypeStruct((n, RWKV_DIM), F32), jax.ShapeDtypeStruct((n, RWKV_DIM), F32),
                   jax.ShapeDtypeStruct((n // chunk, n_pairs, LANES, LANES), BF16),
                   jax.ShapeDtypeStruct((n // chunk, n_pairs, LANES, LANES), F32)],
        grid=(n // (cb * chunk),),
        in_specs=[blk] * 6,
        out_specs=[blk, blk, mblk, mblk],
        compiler_params=_cparams(("parallel",)),
        name="rwkv_scan_prep",
    )(r, lw, km, v, kn, kb)
    sblk = pl.BlockSpec((batch, chunk, RWKV_DIM), lambda c: (0, c, 0))
    smblk = pl.BlockSpec((batch, 1, n_pairs, LANES, LANES), lambda c: (0, c, 0, 0, 0))
    y = pl.pallas_call(
        functools.partial(_scan_state_kernel, batch=batch),
        out_shape=jax.ShapeDtypeStruct((batch, seq, RWKV_DIM), F32),
        grid=(nc,),
        in_specs=[sblk, sblk, smblk, smblk],
        out_specs=sblk,
        scratch_shapes=[pltpu.VMEM((n_pairs * batch, LANES, LANES), F32)],
        compiler_params=_cparams(("arbitrary",)),
        name="rwkv_scan_state",
    )(qe.reshape(batch, seq, RWKV_DIM), y0.reshape(batch, seq, RWKV_DIM),
      mt.reshape(batch, nc, n_pairs, LANES, LANES), ct.reshape(batch, nc, n_pairs, LANES, LANES))
    return y.reshape(n, RWKV_DIM)


def _odd_out_kernel(y_ref, r_ref, km_ref, v_ref, gg_ref, op_ref, rk_ref, lnw_ref, lnb_ref, ones_ref,
                    *tail_refs):
    ones = ones_ref[...]
    inv = 1.0 / HEAD_DIM
    y = y_ref[...]
    mean = _split_sum(y, ones) * inv
    yc = y - mean
    var = _split_sum(yc * yc, ones) * inv
    yn = yc * lax.rsqrt(var + LNX_EPS) * lnw_ref[...] + lnb_ref[...]
    bonus = _split_sum(r_ref[...] * km_ref[...] * rk_ref[...], ones) * v_ref[...]
    o_rwkv = (yn + bonus) * gg_ref[...]
    cat = jnp.concatenate([o_rwkv, op_ref[...]], axis=1).astype(BF16)
    _tail(cat, *tail_refs)


def odd_out_proj(y, r, km, v, gg, opool, r_k, lnx_w, lnx_b, ones_bd, x2, g1, w_out, ng, sh2, sc2,
                 rwt, rb, seq):
    n, d = x2.shape
    tm = min(TOK_TILE, seq)
    tpb = seq // tm
    row = lambda i: (i, 0)
    c2 = lambda i: (0, 0)
    act = pl.BlockSpec((tm, RWKV_DIM), row)
    vec = pl.BlockSpec((1, RWKV_DIM), c2)
    tin, tout, tscratch = _tail_specs(tm, d, tpb)
    return pl.pallas_call(
        _odd_out_kernel,
        out_shape=_tail_out_shapes(n, d),
        grid=(n // tm,),
        in_specs=[act] * 6 + [vec, vec, vec, pl.BlockSpec(ones_bd.shape, c2)] + tin,
        out_specs=tout,
        scratch_shapes=tscratch,
        compiler_params=_cparams(("arbitrary",)),
        name="odd_out_proj",
    )(y, r, km, v, gg, opool, r_k, lnx_w, lnx_b, ones_bd, x2, g1, w_out, ng, sh2, sc2, rwt, rb,
      _strict_upper(tm))


def _rope_tables(seq):
    half = HEAD_DIM // 2
    inv = ROPE_THETA ** (-jnp.arange(half, dtype=F32) / half)
    ang = jnp.arange(seq, dtype=F32)[:, None] * inv[None, :]
    return jnp.tile(jnp.cos(ang), (1, LANES // half)), jnp.tile(jnp.sin(ang), (1, LANES // half))


def _even_w_pad(w_in):
    d = w_in.shape[0]
    q_kv = w_in[:, :NSA_DIM + 6 * KV_DIM]
    gl = w_in[:, NSA_DIM + 6 * KV_DIM:NSA_DIM + 6 * KV_DIM + 24]
    rest = w_in[:, NSA_DIM + 6 * KV_DIM + 24:]
    z = jnp.zeros((d, LANES - 12), w_in.dtype)
    return jnp.concatenate([q_kv, gl[:, :12], z, gl[:, 12:], z, rest], axis=1).astype(BF16)


def _compress_params(cmp_pos, cmp_w1, cmp_w2):
    eye = jnp.eye(N_KV_HEADS, dtype=F32)
    w1r = cmp_w1.reshape(2, 2, CMP_STRIDE, HEAD_DIM, CMP_HIDDEN)
    w1_ext = jnp.einsum('kpmdn,gh->kpmgdhn', w1r, eye).reshape(
        2, 2, CMP_STRIDE * KV_DIM, N_KV_HEADS * CMP_HIDDEN).astype(BF16)
    w2_ext = jnp.einsum('knd,gh->kgnhd', cmp_w2, eye).reshape(
        2, N_KV_HEADS * CMP_HIDDEN, KV_DIM).astype(BF16)
    pos = cmp_pos.reshape(2, 2, CMP_STRIDE, 1, HEAD_DIM)
    pos_ext = jnp.broadcast_to(pos, (2, 2, CMP_STRIDE, N_KV_HEADS, HEAD_DIM)).reshape(
        2, 2, 1, CMP_STRIDE * KV_DIM)
    return pos_ext, w1_ext, w2_ext


def _nsa_tables(seq):
    n_blk = seq // SEL_BLOCK
    n_cmp = (seq - CMP_BLOCK) // CMP_STRIDE + 1
    n_cmp_pad = seq // CMP_STRIDE
    r = SEL_BLOCK // CMP_STRIDE
    c = CMP_BLOCK // CMP_STRIDE
    msel = np.zeros((n_cmp_pad, LANES), np.float32)
    for j in range(n_blk):
        for m in range(r):
            for n in range(c):
                idx = r * j + m + n
                if idx < n_cmp:
                    msel[idx, j] += 1.0
    chunk = min(SEL_CHUNK, seq)
    kblk = np.arange(seq) // SEL_BLOCK
    eexp = (kblk[None, :] == np.arange(LANES)[:, None]).astype(np.float32)
    eexp = eexp.reshape(LANES, seq // chunk, chunk).transpose(1, 0, 2)
    selq = np.zeros((N_KV_HEADS, GQA, GQA * HEAD_DIM, LANES), np.float32)
    for h in range(N_KV_HEADS):
        for g in range(GQA):
            for dd in range(HEAD_DIM):
                selq[h, g, g * HEAD_DIM + dd, h * HEAD_DIM + dd] = 1.0
    return jnp.asarray(msel.T), jnp.asarray(eexp, dtype=BF16), jnp.asarray(selq, dtype=BF16)


def _odd_params(w_in, mu, w2, a2):
    d = w_in.shape[0]
    z64 = jnp.zeros((d, 64), w_in.dtype)
    w_pad = jnp.concatenate([w_in[:, :1536], w_in[:, 1536:1600], z64, w_in[:, 1600:1664], z64,
                             w_in[:, 1664:]], axis=1).astype(BF16)
    m64 = jnp.zeros((64,), mu.dtype)
    mu_pad = jnp.concatenate([mu[:1536], mu[1536:1600], m64, mu[1600:1664], m64, mu[1664:]])[None, :]
    zr = jnp.zeros((64, RWKV_DIM), w2.dtype)
    w2p = jnp.concatenate([w2, zr], axis=0).astype(BF16)
    a2p = jnp.concatenate([a2, zr], axis=0).astype(BF16)
    return w_pad, mu_pad, w2p, a2p


def _head_ones():
    idx = np.arange(RWKV_DIM) // HEAD_DIM
    return jnp.asarray((idx[:, None] == idx[None, :]).astype(np.float32), dtype=BF16)


def kernel(x, c, ada_w, ada_b, norm_mix, norm_ffn, even_w_in, even_cmp_pos, even_cmp_w1, even_cmp_w2,
           even_conv_w, even_w_out, odd_w_in, odd_mu, odd_w0, odd_w2, odd_a0, odd_a2, odd_g2, odd_k_k,
           odd_k_a, odd_r_k, odd_lnx_w, odd_lnx_b, odd_pool_w, odd_pool_scale, odd_w_out,
           router_w, router_b, moe_w_gate, moe_w_up, moe_w_down, final_norm):
    batch, seq, d = x.shape
    n = batch * seq
    depth = ada_w.shape[0]
    x2 = x.reshape(n, d)
    mod = ada_modulation(c, ada_w, ada_b)
    rwt = router_w.T
    rb = router_b.reshape(N_EXPERTS, 1)
    fnorm = final_norm.reshape(1, d)
    cos, sin = _rope_tables(seq)
    msel, eexp, selq = _nsa_tables(seq)
    ones_bd = _head_ones()

    for layer in range(depth):
        m = mod[layer].reshape(batch, 6, 1, d)
        sh1, sc1, g1, sh2, sc2, g2 = (m[:, k] for k in range(6))
        ng_mix = norm_mix[layer].reshape(1, d)
        ng_ffn = norm_ffn[layer].reshape(1, d)
        i = layer // 2
        if layer % 2 == 0:
            (qn, qr, kc, vc, ks, vs, kw, vw, gate, u, bg) = even_in_proj(
                x2, ng_mix, sh1, sc1, _even_w_pad(even_w_in[i]), cos, sin, seq)
            pos_ext, w1_ext, w2_ext = _compress_params(even_cmp_pos[i], even_cmp_w1[i], even_cmp_w2[i])
            kcmp, vcmp = compress_kv(kc, vc, pos_ext, w1_ext, w2_ext, batch, seq)
            o_nsa = nsa_attention(qn, qr, kcmp, vcmp, ks.reshape(batch, seq, KV_DIM),
                                  vs.reshape(batch, seq, KV_DIM), kw.reshape(batch, seq, KV_DIM),
                                  vw.reshape(batch, seq, KV_DIM), gate, selq, msel, eexp, batch, seq)
            x1, *route = even_out_proj(o_nsa, u, bg, even_conv_w[i], x2, g1,
                                       even_w_out[i].astype(BF16), ng_ffn, sh2, sc2, rwt, rb, seq)
        else:
            w_pad, mu_pad, w2p, a2p = _odd_params(odd_w_in[i], odd_mu[i], odd_w2[i], odd_a2[i])
            vec = lambda a: a.reshape(1, RWKV_DIM)
            (r, lw, km, v, kn, kb, gg, opool) = odd_in_proj(
                x2, ng_mix, sh1, sc1, w_pad, mu_pad, vec(odd_w0[i]), w2p, vec(odd_a0[i]), a2p,
                odd_g2[i].astype(BF16), vec(odd_k_k[i]), vec(odd_k_a[i]), ones_bd,
                odd_pool_w[i].astype(BF16), vec(odd_pool_scale[i]), seq)
            y = rwkv_scan(r, lw, km, v, kn, kb, batch, seq)
            x1, *route = odd_out_proj(
                y, r, km, v, gg, opool, vec(odd_r_k[i]), vec(odd_lnx_w[i]), vec(odd_lnx_b[i]), ones_bd,
                x2, g1, odd_w_out[i].astype(BF16), ng_ffn, sh2, sc2, rwt, rb, seq)
        x2 = moe_sparse(route, moe_w_gate, moe_w_up, moe_w_down, layer,
                        x1, g2, fnorm, seq, final_norm=(layer == depth - 1))
    return x2.reshape(batch, seq, d)
```

```python
import functools

import jax
import jax.numpy as jnp
import numpy as np
from jax import lax
from jax.experimental import pallas as pl
from jax.experimental.pallas import tpu as pltpu

F32 = jnp.float32
BF16 = jnp.bfloat16
HIGHEST = lax.Precision.HIGHEST

D_MODEL = 1024
DEPTH = 2
HEAD_DIM = 64
ROPE_THETA = 10000.0
NORM_EPS = 1e-6
NEG_INF = -1e30
BIG = 1e9
NSA_DIM = 512
N_KV_HEADS = 2
GQA = 4
KV_DIM = 128
CMP_BLOCK = 32
CMP_STRIDE = 16
CMP_HIDDEN = 256
SEL_BLOCK = 64
N_SEL = 8
N_LOCAL = 2
WINDOW = 512
Q_BLOCK = 128
ATTN_SCALE = HEAD_DIM ** -0.5
CONV_DIM = 512
RWKV_DIM = 512
N_RWKV_HEADS = 8
LNX_EPS = 64e-5
POOL_WINDOWS = (2, 4, 8, 16)
POOL_GROUP = 128
N_EXPERTS = 16
N_EXPERT_GROUPS = 4
EXPERTS_PER_GROUP = 4
D_EXPERT = 512
PAIRS_PER_GROUP = 6
PAIR_LO = (0, 0, 0, 1, 1, 2)
PAIR_HI = (1, 2, 3, 3, 2, 3)
N_CLASSES = N_EXPERT_GROUPS * PAIRS_PER_GROUP
CLASS_ROWS = 32

LANES = 128
SUBLANES = 8
VMEM_LIMIT = 56 * 1024 * 1024

TOK_TILE = 512
MOE_ROW_TILE = 256
MOE_DMA_TILE = 512
SEL_CHUNK = 512
SCAN_CHUNK = 64
SCAN_CHUNKS_PER_STEP = 4
CONV_HALO = 8
POOL_HALO = 16

EVEN_PAD_COLS = 3072
ODD_PAD_COLS = 2432
ODD_RW_COLS = 1920


def _cparams(sem):
    return pltpu.CompilerParams(dimension_semantics=sem, vmem_limit_bytes=VMEM_LIMIT)


def _nt(a, b, precision=None):
    return lax.dot_general(a, b, (((1,), (1,)), ((), ())), preferred_element_type=F32,
                           precision=precision)


def _mm(a, b, precision=None):
    return jnp.dot(a, b, preferred_element_type=F32, precision=precision)


def _norm_mod(x, g, sh, sc):
    ms = jnp.mean(x * x, axis=-1, keepdims=True)
    return (x * lax.rsqrt(ms + NORM_EPS) * g) * (1.0 + sc) + sh


def _split_sum(x, ones_bf16):
    hi = x.astype(BF16)
    lo = (x - hi.astype(F32)).astype(BF16)
    return _mm(hi, ones_bf16) + _mm(lo, ones_bf16)


def _ada_kernel(c_ref, w_ref, b_ref, o_ref):
    c = c_ref[...]
    cond = c * jax.nn.sigmoid(c)
    o_ref[0] = _mm(cond, w_ref[0], precision=HIGHEST) + b_ref[0]


def ada_modulation(c, ada_w, ada_b):
    depth, d, cols = ada_w.shape
    b = c.shape[0]
    tn = 1536
    return pl.pallas_call(
        _ada_kernel,
        out_shape=jax.ShapeDtypeStruct((depth, b, cols), F32),
        grid=(depth, cols // tn),
        in_specs=[pl.BlockSpec((b, d), lambda l, j: (0, 0)),
                  pl.BlockSpec((1, d, tn), lambda l, j: (l, 0, j)),
                  pl.BlockSpec((1, 1, tn), lambda l, j: (l, 0, j))],
        out_specs=pl.BlockSpec((1, b, tn), lambda l, j: (l, 0, j)),
        compiler_params=_cparams(("parallel", "parallel")),
        name="ada_modulation",
    )(c, ada_w, ada_b.reshape(depth, 1, cols))


def _rope128(t, cos, sin, lane):
    rot = jnp.where((lane % HEAD_DIM) < HEAD_DIM // 2,
                    -pltpu.roll(t, LANES - HEAD_DIM // 2, 1), pltpu.roll(t, HEAD_DIM // 2, 1))
    return t * cos + rot * sin


def _even_in_kernel(x_ref, g_ref, sh_ref, sc_ref, w_ref, cos_ref, sin_ref,
                    qn_ref, qr_ref, kc_ref, vc_ref, ks_ref, vs_ref, kw_ref, vw_ref,
                    gate_ref, u_ref, bg_ref, *, tiles_per_seq):
    h = _norm_mod(x_ref[...], g_ref[...], sh_ref[0], sc_ref[0])
    tm = h.shape[0]
    proj = _mm(h.astype(BF16), w_ref[...])
    cos = cos_ref[...]
    sin = sin_ref[...]
    lane = lax.broadcasted_iota(jnp.int32, (1, LANES), 1)
    low = lane < HEAD_DIM
    for i in range(NSA_DIM // LANES):
        q = proj[:, i * LANES:(i + 1) * LANES] * ATTN_SCALE
        qn_ref[:, i * LANES:(i + 1) * LANES] = q.astype(BF16)
        qr_ref[:, i * LANES:(i + 1) * LANES] = _rope128(q, cos, sin, lane).astype(BF16)
    o = NSA_DIM
    kc_ref[...] = proj[:, o:o + 128]
    vc_ref[...] = proj[:, o + 128:o + 256]
    pos = (pl.program_id(0) % tiles_per_seq) * tm + lax.broadcasted_iota(jnp.int32, (tm, 1), 0)
    blk = pos // SEL_BLOCK
    ks = _rope128(proj[:, o + 256:o + 384], cos, sin, lane)
    ks_ref[:, 0:LANES] = jnp.where(low, ks, jnp.where(lane - HEAD_DIM == blk, 1.0, 0.0)).astype(BF16)
    ks_ref[:, LANES:2 * LANES] = jnp.where(low, jnp.where(lane == blk, 1.0, 0.0), ks).astype(BF16)
    vs = proj[:, o + 384:o + 512]
    vs_ref[:, 0:LANES] = jnp.where(low, vs, 1.0).astype(BF16)
    vs_ref[:, LANES:2 * LANES] = jnp.where(low, 1.0, vs).astype(BF16)
    kw_ref[...] = _rope128(proj[:, o + 512:o + 640], cos, sin, lane).astype(BF16)
    vw = proj[:, o + 640:o + 768]
    vw_ref[:, 0:LANES] = jnp.where(low, vw, 1.0).astype(BF16)
    vw_ref[:, LANES:2 * LANES] = jnp.where(low, 1.0, vw).astype(BF16)
    o += 768
    gate_ref[...] = jax.nn.sigmoid(proj[:, o:o + 256])
    o += 256
    xb = proj[:, o:o + 512]
    bg_ref[...] = proj[:, o + 512:o + 1024]
    u_ref[...] = proj[:, o + 1024:o + 1536] * xb


def even_in_proj(x2, g, sh, sc, w_pad, cos, sin, seq):
    n, d = x2.shape
    tm = min(TOK_TILE, seq)
    tpb = seq // tm
    row = lambda i: (i, 0)
    per_b = lambda i: (i // tpb, 0, 0)
    pos = lambda i: (i % tpb, 0)
    outs = [((n, 512), BF16), ((n, 512), BF16), ((n, 128), F32), ((n, 128), F32),
            ((n, 256), BF16), ((n, 256), BF16), ((n, 128), BF16), ((n, 256), BF16),
            ((n, 256), F32), ((n, 512), F32), ((n, 512), F32)]
    return pl.pallas_call(
        functools.partial(_even_in_kernel, tiles_per_seq=tpb),
        out_shape=[jax.ShapeDtypeStruct(s, t) for s, t in outs],
        grid=(n // tm,),
        in_specs=[pl.BlockSpec((tm, d), row),
                  pl.BlockSpec((1, d), lambda i: (0, 0)),
                  pl.BlockSpec((1, 1, d), per_b),
                  pl.BlockSpec((1, 1, d), per_b),
                  pl.BlockSpec((d, EVEN_PAD_COLS), lambda i: (0, 0)),
                  pl.BlockSpec((tm, LANES), pos),
                  pl.BlockSpec((tm, LANES), pos)],
        out_specs=[pl.BlockSpec((tm, s[1]), row) for s, _ in outs],
        compiler_params=_cparams(("parallel",)),
        name="even_in_proj",
    )(x2, g, sh, sc, w_pad, cos, sin)


def _compress_kernel(k_ref, v_ref, pos_ref, w1_ref, w2_ref, ko_ref, vo_ref):
    for j, (src, dst) in enumerate(((k_ref, ko_ref), (v_ref, vo_ref))):
        xr = src[0]
        n_rows = xr.shape[0]
        a0 = _mm((xr + pos_ref[j, 0]).astype(BF16), w1_ref[j, 0])
        a1 = _mm((xr + pos_ref[j, 1]).astype(BF16), w1_ref[j, 1])
        hid = a0 + pltpu.roll(a1, n_rows - 1, 0)
        hid = jax.nn.gelu(hid)
        dst[0] = _mm(hid.astype(BF16), w2_ref[j]).astype(BF16)


def compress_kv(kc, vc, pos_ext, w1_ext, w2_ext, batch, seq):
    rows = seq // CMP_STRIDE
    width = CMP_STRIDE * KV_DIM
    kr = kc.reshape(batch, rows, width)
    vr = vc.reshape(batch, rows, width)
    blk = pl.BlockSpec((1, rows, width), lambda b: (b, 0, 0))
    oblk = pl.BlockSpec((1, rows, KV_DIM), lambda b: (b, 0, 0))
    return pl.pallas_call(
        _compress_kernel,
        out_shape=[jax.ShapeDtypeStruct((batch, rows, KV_DIM), BF16)] * 2,
        grid=(batch,),
        in_specs=[blk, blk,
                  pl.BlockSpec(pos_ext.shape, lambda b: (0, 0, 0, 0)),
                  pl.BlockSpec(w1_ext.shape, lambda b: (0, 0, 0, 0)),
                  pl.BlockSpec(w2_ext.shape, lambda b: (0, 0, 0))],
        out_specs=[oblk, oblk],
        compiler_params=_cparams(("parallel",)),
        name="compress_kv",
    )(kr, vr, pos_ext, w1_ext, w2_ext)


def _safe_inv(l):
    return jnp.where(l > 0.0, 1.0 / jnp.where(l > 0.0, l, 1.0), 0.0)


def _nsa_kernel(qn_ref, qr_ref, kc_ref, vc_ref, ks_ref, vs_ref, kw_ref, vw_ref, gate_ref,
                selq_ref, mselt_ref, o_ref, *, seq, n_sel, sel_chunk, win_len):
    h = pl.program_id(1)
    qt = pl.program_id(2)
    t0 = qt * Q_BLOCK
    n_blk = seq // SEL_BLOCK
    n_cmp_pad = seq // CMP_STRIDE
    rows = GQA * Q_BLOCK
    tpos = t0 + lax.broadcasted_iota(jnp.int32, (1, Q_BLOCK, 1), 1)
    lane = lax.broadcasted_iota(jnp.int32, (1, LANES), 1)
    head_lanes = (lane // HEAD_DIM) == h

    def normalise(acc):
        return acc * _safe_inv(pltpu.roll(acc, HEAD_DIM, 1))

    qn = qn_ref[...]
    qr = qr_ref[...]
    qn4 = jnp.concatenate([_mm(qn, selq_ref[0, g]) for g in range(GQA)], axis=0).astype(BF16)
    qr4f = jnp.concatenate([_mm(qr, selq_ref[0, g]) for g in range(GQA)], axis=0)
    qr4 = qr4f.astype(BF16)

    kc = kc_ref[0]
    vc = vc_ref[0]
    cpos = lax.broadcasted_iota(jnp.int32, (1, 1, n_cmp_pad), 2) * CMP_STRIDE + (CMP_BLOCK - 1)
    cmask = cpos <= tpos
    s = jnp.where(cmask, _nt(qn4, kc).reshape(GQA, Q_BLOCK, n_cmp_pad), NEG_INF)
    e = jnp.where(cmask, jnp.exp(s - jnp.max(s, axis=2, keepdims=True)), 0.0)
    p = e * _safe_inv(jnp.sum(e, axis=2, keepdims=True))
    imp = jnp.sum(p, axis=0)
    o_cmp = _mm(p.reshape(rows, n_cmp_pad).astype(BF16), vc)

    pslc = _nt(mselt_ref[...], imp, precision=HIGHEST)[:n_blk]
    tq = t0 + lax.broadcasted_iota(jnp.int32, (1, Q_BLOCK), 1)
    jblk = lax.broadcasted_iota(jnp.int32, (n_blk, 1), 0)
    cur = tq // SEL_BLOCK
    valid = jblk * SEL_BLOCK <= tq
    forced = (jblk == 0) | ((cur - jblk >= 0) & (cur - jblk < N_LOCAL))
    score = jnp.where(forced, BIG, jnp.where(valid, pslc, -BIG))
    rank = jnp.zeros((n_blk, Q_BLOCK), jnp.int32)
    for jp in range(n_blk):
        row = score[jp:jp + 1, :]
        beats = (row > score) | ((row == score) & (jblk > jp))
        rank = rank + beats.astype(jnp.int32)
    selb = jnp.where((rank < n_sel) & (score > -0.5 * BIG), 0.0, NEG_INF)
    selb_q = jnp.concatenate([selb, jnp.zeros((LANES - n_blk, Q_BLOCK), F32)], axis=0).T
    bias = jnp.where(h == 0, pltpu.roll(selb_q, HEAD_DIM, 1), selb_q)
    qs4 = (qr4f + jnp.concatenate([bias] * GQA, axis=0)).astype(BF16)

    def chunk_scores(c):
        start = pl.multiple_of(c * sel_chunk, sel_chunk)
        kblk = ks_ref[0, pl.ds(start, sel_chunk), :]
        vblk = vs_ref[0, pl.ds(start, sel_chunk), :]
        return start, _nt(qs4, kblk).reshape(GQA, Q_BLOCK, sel_chunk), vblk

    def online_update(carry, s, vblk):
        m, acc = carry
        m_new = jnp.maximum(m, jnp.max(s, axis=2, keepdims=True))
        alpha = jnp.exp(m - m_new)
        p = jnp.exp(s - m_new)
        pv = _mm(p.reshape(rows, sel_chunk).astype(BF16), vblk).reshape(GQA, Q_BLOCK, LANES)
        return m_new, alpha * acc + pv

    def sel_body(c, carry):
        _, s, vblk = chunk_scores(c)
        return online_update(carry, s, vblk)

    diag_chunk = (t0 + Q_BLOCK - 1) // sel_chunk
    init = (jnp.full((GQA, Q_BLOCK, 1), NEG_INF, F32), jnp.zeros((GQA, Q_BLOCK, LANES), F32))
    carry = lax.fori_loop(0, diag_chunk, sel_body, init)
    start, s, vblk = chunk_scores(diag_chunk)
    kpos = start + lax.broadcasted_iota(jnp.int32, (1, 1, sel_chunk), 2)
    _, acc = online_update(carry, jnp.where(kpos <= tpos, s, NEG_INF), vblk)
    o_slc = normalise(acc.reshape(rows, LANES))

    ws = pl.multiple_of(jnp.maximum(qt - WINDOW // Q_BLOCK, 0) * Q_BLOCK, Q_BLOCK)
    kwb = kw_ref[0, pl.ds(ws, win_len), :]
    vwb = vw_ref[0, pl.ds(ws, win_len), :]
    diff = tpos - (ws + lax.broadcasted_iota(jnp.int32, (1, 1, win_len), 2))
    wmask = (diff >= 0) & (diff < WINDOW)
    s = jnp.where(wmask, _nt(qr4, kwb).reshape(GQA, Q_BLOCK, win_len), NEG_INF)
    e = jnp.exp(s - jnp.max(s, axis=2, keepdims=True))
    o_win = normalise(_mm(e.reshape(rows, win_len).astype(BF16), vwb))

    gate = gate_ref[...]
    og = []
    for g in range(GQA):
        sl = slice(g * Q_BLOCK, (g + 1) * Q_BLOCK)
        o = (gate[:, 3 * g:3 * g + 1] * o_cmp[sl] + gate[:, 3 * g + 1:3 * g + 2] * o_slc[sl]
             + gate[:, 3 * g + 2:3 * g + 3] * o_win[sl])
        og.append(jnp.where(head_lanes, o, pltpu.roll(o, HEAD_DIM, 1)))
    low = lane < HEAD_DIM
    o_ref[:, 0:LANES] = jnp.where(low, og[0], og[1])
    o_ref[:, LANES:2 * LANES] = jnp.where(low, og[2], og[3])


def nsa_attention(qn, qr, kcmp, vcmp, ks, vs, kw, vw, gate, selq, mselt, batch, seq):
    n = batch * seq
    nq = seq // Q_BLOCK
    sel_chunk = min(SEL_CHUNK, seq)
    win_len = min(WINDOW + Q_BLOCK, seq)
    n_sel = min(N_SEL, seq // SEL_BLOCK)
    qspec = pl.BlockSpec((Q_BLOCK, GQA * HEAD_DIM), lambda b, h, q: (b * nq + q, h))
    cspec = pl.BlockSpec((1, seq // CMP_STRIDE, KV_DIM), lambda b, h, q: (b, 0, 0))
    both = pl.BlockSpec((1, seq, KV_DIM), lambda b, h, q: (b, 0, 0))
    mine = pl.BlockSpec((1, seq, KV_DIM), lambda b, h, q: (b, 0, h))
    kern = functools.partial(_nsa_kernel, seq=seq, n_sel=n_sel, sel_chunk=sel_chunk, win_len=win_len)
    return pl.pallas_call(
        kern,
        out_shape=jax.ShapeDtypeStruct((n, NSA_DIM), F32),
        grid=(batch, N_KV_HEADS, nq),
        in_specs=[qspec, qspec, cspec, cspec, mine, mine, both, mine,
                  pl.BlockSpec((Q_BLOCK, LANES), lambda b, h, q: (b * nq + q, h)),
                  pl.BlockSpec((1, GQA, GQA * HEAD_DIM, LANES), lambda b, h, q: (h, 0, 0, 0)),
                  pl.BlockSpec(mselt.shape, lambda b, h, q: (0, 0))],
        out_specs=qspec,
        compiler_params=_cparams(("parallel", "parallel", "arbitrary")),
        name="nsa_attention",
    )(qn, qr, kcmp, vcmp, ks.reshape(batch, seq, 2 * KV_DIM), vs.reshape(batch, seq, 2 * KV_DIM),
      kw.reshape(batch, seq, KV_DIM), vw.reshape(batch, seq, 2 * KV_DIM), gate, selq, mselt)


def _route(h2, rwt_ref, rb_ref):
    rw = rwt_ref[...]
    rw_hi = rw.astype(BF16)
    rw_lo = (rw - rw_hi.astype(F32)).astype(BF16)
    h_hi = h2.astype(BF16)
    h_lo = (h2 - h_hi.astype(F32)).astype(BF16)
    logits = _nt(rw_hi, h_hi) + (_nt(rw_hi, h_lo) + _nt(rw_lo, h_hi))
    scores = jax.nn.sigmoid(logits)
    biased = scores + rb_ref[...]
    rows = [biased[e:e + 1, :] for e in range(N_EXPERTS)]
    srow = [scores[e:e + 1, :] for e in range(N_EXPERTS)]
    gscore = []
    for gi in range(N_EXPERT_GROUPS):
        r = rows[gi * EXPERTS_PER_GROUP:(gi + 1) * EXPERTS_PER_GROUP]
        best = None
        for a in range(EXPERTS_PER_GROUP):
            for b in range(a + 1, EXPERTS_PER_GROUP):
                pair = r[a] + r[b]
                best = pair if best is None else jnp.maximum(best, pair)
        gscore.append(best)
    top_val = gscore[0]
    top_grp = jnp.zeros_like(top_val, dtype=jnp.int32)
    for gi in range(1, N_EXPERT_GROUPS):
        upd = gscore[gi] > top_val
        top_grp = jnp.where(upd, gi, top_grp)
        top_val = jnp.where(upd, gscore[gi], top_val)
    masked = [jnp.where(top_grp == e // EXPERTS_PER_GROUP, rows[e], NEG_INF) for e in range(N_EXPERTS)]
    b1 = masked[0]
    i1 = jnp.zeros_like(top_grp)
    for e in range(1, N_EXPERTS):
        upd = masked[e] > b1
        i1 = jnp.where(upd, e, i1)
        b1 = jnp.where(upd, masked[e], b1)
    b2 = None
    i2 = None
    for e in range(N_EXPERTS):
        v = jnp.where(i1 == e, -jnp.inf, masked[e])
        if b2 is None:
            b2, i2 = v, jnp.zeros_like(top_grp)
        else:
            upd = v > b2
            i2 = jnp.where(upd, e, i2)
            b2 = jnp.where(upd, v, b2)
    s1 = jnp.zeros_like(top_val)
    s2 = jnp.zeros_like(top_val)
    for e in range(N_EXPERTS):
        s1 = s1 + jnp.where(i1 == e, srow[e], 0.0)
        s2 = s2 + jnp.where(i2 == e, srow[e], 0.0)
    tot = s1 + s2
    return i1, i2, s1 / tot, s2 / tot


def _tail(cat_bf16, x_ref, g1_ref, wout_ref, ng_ref, sh2_ref, sc2_ref, rwt_ref, rb_ref, ustrict_ref,
          x1_ref, hw_ref, ridx_ref, cnt_ref, carry_ref):
    y = _mm(cat_bf16, wout_ref[...])
    x1 = x_ref[...] + g1_ref[0] * y
    x1_ref[...] = x1
    h2 = _norm_mod(x1, ng_ref[...], sh2_ref[0], sc2_ref[0])
    tm, d = h2.shape
    hw_ref[:, :d] = h2

    i1, i2, w1, w2 = _route(h2, rwt_ref, rb_ref)
    lo = jnp.minimum(i1, i2) % EXPERTS_PER_GROUP
    hi = jnp.maximum(i1, i2) % EXPERTS_PER_GROUP
    pair = jnp.where(lo == 0, hi - 1, jnp.where(lo == 1, jnp.where(hi == 3, 3, 4), 5))
    cls = (i1 // EXPERTS_PER_GROUP) * PAIRS_PER_GROUP + pair
    w_lo = jnp.where(i1 < i2, w1, w2)
    w_hi = jnp.where(i1 < i2, w2, w1)
    meta_t = jnp.concatenate([w_lo, w_hi, jnp.zeros((LANES - 2, tm), F32)], axis=0)
    hw_ref[:, d:] = meta_t.T

    @pl.when(pl.program_id(0) == 0)
    def _():
        carry_ref[...] = jnp.zeros_like(carry_ref)

    hit = lax.broadcasted_iota(jnp.int32, (CLASS_ROWS, 1), 0) == cls
    cnt = jnp.where(hit, 1.0, 0.0)
    before = _mm(cnt.astype(BF16), ustrict_ref[...]) + carry_ref[:, 0:1]
    carry = carry_ref[...] + jnp.sum(cnt, axis=1, keepdims=True)
    carry_ref[...] = carry
    cnt_ref[...] = carry.astype(jnp.int32)
    ridx_ref[0:1, :] = cls
    ridx_ref[1:2, :] = jnp.sum(jnp.where(hit, before, 0.0), axis=0, keepdims=True).astype(jnp.int32)


def _even_out_kernel(o_ref, u_ref, uh_ref, bg_ref, cw_ref, *tail_refs, tiles_per_seq):
    first = (pl.program_id(0) % tiles_per_seq) == 0
    u = u_ref[...]
    halo = jnp.where(first, 0.0, uh_ref[...])
    ext = jnp.concatenate([halo, u], axis=0)
    u1 = pltpu.roll(ext, 1, 0)[CONV_HALO:]
    u2 = pltpu.roll(ext, 2, 0)[CONV_HALO:]
    cw = cw_ref[...]
    y_conv = bg_ref[...] * (cw[2:3] * u + cw[1:2] * u1 + cw[0:1] * u2)
    cat = jnp.concatenate([o_ref[...], y_conv], axis=1).astype(BF16)
    _tail(cat, *tail_refs)


def _tail_specs(tm, d, tpb):
    row = lambda i: (i, 0)
    per_b = lambda i: (i // tpb, 0, 0)
    const2 = lambda i: (0, 0)
    ins = [pl.BlockSpec((tm, d), row),
           pl.BlockSpec((1, 1, d), per_b),
           pl.BlockSpec((d, d), const2),
           pl.BlockSpec((1, d), const2),
           pl.BlockSpec((1, 1, d), per_b),
           pl.BlockSpec((1, 1, d), per_b),
           pl.BlockSpec((N_EXPERTS, d), const2),
           pl.BlockSpec((N_EXPERTS, 1), const2),
           pl.BlockSpec((tm, tm), const2)]
    outs = [pl.BlockSpec((tm, d), row), pl.BlockSpec((tm, d + LANES), row),
            pl.BlockSpec((2, tm), lambda i: (0, i)),
            pl.BlockSpec((CLASS_ROWS, LANES), const2)]
    scratch = [pltpu.VMEM((CLASS_ROWS, LANES), F32)]
    return ins, outs, scratch


def _tail_out_shapes(n, d):
    return [jax.ShapeDtypeStruct((n, d), F32), jax.ShapeDtypeStruct((n, d + LANES), F32),
            jax.ShapeDtypeStruct((2, n), jnp.int32), jax.ShapeDtypeStruct((CLASS_ROWS, LANES), jnp.int32)]


def _strict_upper(tm):
    return jnp.asarray(np.triu(np.ones((tm, tm), np.float32), 1), dtype=BF16)


def even_out_proj(o_nsa, u, bg, conv_w, x2, g1, w_out, ng, sh2, sc2, rwt, rb, seq):
    n, d = x2.shape
    tm = min(TOK_TILE, seq)
    tpb = seq // tm
    row = lambda i: (i, 0)
    halo = lambda i: (jnp.maximum(i * (tm // CONV_HALO) - 1, 0), 0)
    tin, tout, tscratch = _tail_specs(tm, d, tpb)
    return pl.pallas_call(
        functools.partial(_even_out_kernel, tiles_per_seq=tpb),
        out_shape=_tail_out_shapes(n, d),
        grid=(n // tm,),
        in_specs=[pl.BlockSpec((tm, NSA_DIM), row),
                  pl.BlockSpec((tm, CONV_DIM), row),
                  pl.BlockSpec((CONV_HALO, CONV_DIM), halo),
                  pl.BlockSpec((tm, CONV_DIM), row),
                  pl.BlockSpec(conv_w.shape, lambda i: (0, 0))] + tin,
        out_specs=tout,
        scratch_shapes=tscratch,
        compiler_params=_cparams(("arbitrary",)),
        name="even_out_proj",
    )(o_nsa, u, u, bg, conv_w, x2, g1, w_out, ng, sh2, sc2, rwt, rb, _strict_upper(tm))


def _dispatch_plan(ridx, counts, n):
    cnt = counts[:N_CLASSES, 0]
    padded = (cnt + MOE_ROW_TILE - 1) // MOE_ROW_TILE * MOE_ROW_TILE
    ends = jnp.cumsum(padded)
    starts = ends - padded
    cids = jnp.arange(N_CLASSES, dtype=jnp.int32)[:, None]
    base = jnp.sum(jnp.where(cids == ridx[0][None, :], starts[:, None], 0), axis=0)
    dest = (base + ridx[1]).astype(jnp.int32)
    td = min(MOE_DMA_TILE, n)
    dest3 = dest.reshape(n // td, 1, td)
    n_tiles = n // MOE_ROW_TILE + N_CLASSES
    tile_start = jnp.arange(n_tiles, dtype=jnp.int32) * MOE_ROW_TILE
    tile_class = jnp.minimum(jnp.sum(tile_start[:, None] >= ends[None, :], axis=1), N_CLASSES - 1)
    group_base = (tile_class // PAIRS_PER_GROUP) * EXPERTS_PER_GROUP
    pair = tile_class % PAIRS_PER_GROUP
    tile_lo = (group_base + jnp.asarray(PAIR_LO, jnp.int32)[pair]).astype(jnp.int32)
    tile_hi = (group_base + jnp.asarray(PAIR_HI, jnp.int32)[pair]).astype(jnp.int32)
    n_used = (ends[-1] // MOE_ROW_TILE).reshape(1).astype(jnp.int32)
    last_tile = jnp.where(cnt > 0, ends // MOE_ROW_TILE - 1, -1)
    tail = n_used[0] + jnp.arange(N_CLASSES, dtype=jnp.int32)
    zero_tiles = jnp.concatenate([last_tile, jnp.where(tail < n_tiles, tail, -1)]).astype(jnp.int32)
    return dest3, tile_lo, tile_hi, n_used, zero_tiles, n_tiles


def _dispatch_kernel(ztile_ref, dest_ref, hw_ref, xs_hbm, zbuf, zsem, sem):
    td = hw_ref.shape[0]

    @pl.when(pl.program_id(0) == 0)
    def _():
        zbuf[...] = jnp.zeros_like(zbuf)

        def zero_copy(k):
            start = pl.multiple_of(ztile_ref[k] * MOE_ROW_TILE, MOE_ROW_TILE)
            return pltpu.make_async_copy(zbuf, xs_hbm.at[pl.ds(start, MOE_ROW_TILE)], zsem)

        for k in range(2 * N_CLASSES):
            @pl.when(ztile_ref[k] >= 0)
            def _():
                zero_copy(k).start()
        for k in range(2 * N_CLASSES):
            @pl.when(ztile_ref[k] >= 0)
            def _():
                zero_copy(k).wait()

    for r in range(td):
        pltpu.make_async_copy(hw_ref.at[pl.ds(r, 1)],
                              xs_hbm.at[pl.ds(dest_ref[0, 0, r], 1)], sem).start()
    pltpu.make_async_copy(hw_ref, xs_hbm.at[pl.ds(0, td)], sem).wait()


def moe_dispatch(hw, dest3, zero_tiles, n_tiles):
    n, cols = hw.shape
    td = dest3.shape[2]
    rows = n_tiles * MOE_ROW_TILE
    return pl.pallas_call(
        _dispatch_kernel,
        out_shape=jax.ShapeDtypeStruct((rows, cols), F32),
        grid_spec=pltpu.PrefetchScalarGridSpec(
            num_scalar_prefetch=1,
            grid=(n // td,),
            in_specs=[pl.BlockSpec((1, 1, td), lambda i, z: (i, 0, 0), memory_space=pltpu.SMEM),
                      pl.BlockSpec((td, cols), lambda i, z: (i, 0))],
            out_specs=pl.BlockSpec(memory_space=pl.ANY),
            scratch_shapes=[pltpu.VMEM((MOE_ROW_TILE, cols), F32), pltpu.SemaphoreType.DMA(()),
                            pltpu.SemaphoreType.DMA(())]),
        compiler_params=_cparams(("arbitrary",)),
        name="moe_dispatch",
    )(zero_tiles, dest3, hw)


def _expert_kernel(lo_ref, hi_ref, nused_ref, xs_ref, wg_lo, wu_lo, wd_lo, wg_hi, wu_hi, wd_hi, ys_ref,
                   *wb):
    t = pl.program_id(0)
    prev = jnp.maximum(t - 1, 0)

    for ids, srcs, dsts in ((lo_ref, (wg_lo, wu_lo, wd_lo), wb[:3]), (hi_ref, (wg_hi, wu_hi, wd_hi), wb[3:])):
        @pl.when((t == 0) | (ids[t] != ids[prev]))
        def _():
            for src, dst in zip(srcs, dsts):
                dst[...] = src[0, 0].astype(BF16)

    @pl.when(t < nused_ref[0])
    def _():
        d = xs_ref.shape[1] - LANES
        x = xs_ref[:, :d].astype(BF16)
        meta = xs_ref[:, d:]
        y = None
        for k in range(2):
            a = _mm(x, wb[3 * k][...])
            b = _mm(x, wb[3 * k + 1][...])
            he = (a * jax.nn.sigmoid(a)) * b
            yk = meta[:, k:k + 1] * _mm(he.astype(BF16), wb[3 * k + 2][...])
            y = yk if y is None else y + yk
        ys_ref[...] = y

    @pl.when(t >= nused_ref[0])
    def _():
        ys_ref[...] = jnp.zeros_like(ys_ref)


def moe_experts(xs, tile_lo, tile_hi, n_used, w_gate, w_up, w_down, layer, d):
    rows, cols = xs.shape
    n_tiles = rows // MOE_ROW_TILE
    lo_spec = lambda shape: pl.BlockSpec((1, 1) + shape, lambda t, lo, hi, nu: (layer, lo[t], 0, 0))
    hi_spec = lambda shape: pl.BlockSpec((1, 1) + shape, lambda t, lo, hi, nu: (layer, hi[t], 0, 0))
    shapes = ((d, D_EXPERT), (d, D_EXPERT), (D_EXPERT, d))
    return pl.pallas_call(
        _expert_kernel,
        out_shape=jax.ShapeDtypeStruct((rows, d), F32),
        grid_spec=pltpu.PrefetchScalarGridSpec(
            num_scalar_prefetch=3,
            grid=(n_tiles,),
            in_specs=[pl.BlockSpec((MOE_ROW_TILE, cols),
                                   lambda t, lo, hi, nu: (jnp.minimum(t, nu[0] - 1), 0))]
                     + [lo_spec(s) for s in shapes] + [hi_spec(s) for s in shapes],
            out_specs=pl.BlockSpec((MOE_ROW_TILE, d), lambda t, lo, hi, nu: (t, 0)),
            scratch_shapes=[pltpu.VMEM(s, BF16) for s in shapes + shapes]),
        compiler_params=_cparams(("arbitrary",)),
        name="moe_experts",
    )(tile_lo, tile_hi, n_used, xs, w_gate, w_up, w_down, w_gate, w_up, w_down)


def _combine_kernel(dest_ref, ys_hbm, x_ref, g2_ref, fn_ref, o_ref, buf, sem, *, final_norm):
    tc = x_ref.shape[0]

    for r in range(tc):
        pltpu.make_async_copy(ys_hbm.at[pl.ds(dest_ref[0, 0, r], 1)], buf.at[pl.ds(r, 1)], sem).start()
    pltpu.make_async_copy(ys_hbm.at[pl.ds(0, tc)], buf, sem).wait()
    x = x_ref[...] + g2_ref[0] * buf[...]
    if final_norm:
        ms = jnp.mean(x * x, axis=-1, keepdims=True)
        x = x * lax.rsqrt(ms + NORM_EPS) * fn_ref[...]
    o_ref[...] = x


def moe_combine(ys, dest3, x1, g2, fnorm, seq, final_norm):
    n, d = x1.shape
    tc = dest3.shape[2]
    tpb = seq // tc
    return pl.pallas_call(
        functools.partial(_combine_kernel, final_norm=final_norm),
        out_shape=jax.ShapeDtypeStruct((n, d), F32),
        grid=(n // tc,),
        in_specs=[pl.BlockSpec((1, 1, tc), lambda i: (i, 0, 0), memory_space=pltpu.SMEM),
                  pl.BlockSpec(memory_space=pl.ANY),
                  pl.BlockSpec((tc, d), lambda i: (i, 0)),
                  pl.BlockSpec((1, 1, d), lambda i: (i // tpb, 0, 0)),
                  pl.BlockSpec((1, d), lambda i: (0, 0))],
        out_specs=pl.BlockSpec((tc, d), lambda i: (i, 0)),
        scratch_shapes=[pltpu.VMEM((tc, d), F32), pltpu.SemaphoreType.DMA(())],
        compiler_params=_cparams(("arbitrary",)),
        name="moe_combine",
    )(dest3, ys, x1, g2, fnorm)


def moe_sparse(hw, ridx, counts, w_gate, w_up, w_down, layer, x1, g2, fnorm, seq, final_norm):
    n, d = x1.shape
    dest3, tile_lo, tile_hi, n_used, zero_tiles, n_tiles = _dispatch_plan(ridx, counts, n)
    xs = moe_dispatch(hw, dest3, zero_tiles, n_tiles)
    ys = moe_experts(xs, tile_lo, tile_hi, n_used, w_gate, w_up, w_down, layer, d)
    return moe_combine(ys, dest3, x1, g2, fnorm, seq, final_norm)


def _odd_in_kernel(x_ref, g_ref, sh_ref, sc_ref, w_ref, mu_ref, w0_ref, w2_ref, a0_ref, a2_ref,
                   g2_ref, kk_ref, ka_ref, ones_ref, pw_ref, ps_ref,
                   r_ref, lw_ref, km_ref, v_ref, kn_ref, kb_ref, gg_ref, op_ref,
                   rw_carry, u_carry, *, tiles_per_seq, tm):
    i = pl.program_id(0)
    first = (i % tiles_per_seq) == 0
    h = _norm_mod(x_ref[...], g_ref[...], sh_ref[0], sc_ref[0])
    proj = _mm(h.astype(BF16), w_ref[...])

    rw = proj[:, :ODD_RW_COLS]
    row0 = jnp.where(first, 0.0, rw_carry[0:1, :])
    ridx = lax.broadcasted_iota(jnp.int32, (tm, 1), 0)
    prev = jnp.where(ridx == 0, row0, pltpu.roll(rw, 1, 0))
    rw_carry[0:1, :] = rw[tm - 1:tm, :]
    rw = rw + (prev - rw) * mu_ref[...]

    r = rw[:, 0:512]
    k = rw[:, 512:1024]
    v = rw[:, 1024:1536]
    wl = rw[:, 1536:1664]
    al = rw[:, 1664:1792]
    gl = rw[:, 1792:1920]
    z = -(w0_ref[...] + _mm(jnp.tanh(wl).astype(BF16), w2_ref[...]))
    softplus = jnp.maximum(z, 0.0) + jnp.log1p(jnp.exp(-jnp.abs(z)))
    w_log = -softplus - 0.5
    a = jax.nn.sigmoid(a0_ref[...] + _mm(al.astype(BF16), a2_ref[...]))
    gg_ref[...] = _mm(jax.nn.sigmoid(gl).astype(BF16), g2_ref[...])
    kk0 = k * kk_ref[...]
    ss = _split_sum(kk0 * kk0, ones_ref[...])
    kk = kk0 / jnp.maximum(jnp.sqrt(ss), 1e-12)
    r_ref[...] = r
    lw_ref[...] = -jnp.exp(w_log)
    km_ref[...] = k * (1.0 + (a - 1.0) * ka_ref[...])
    v_ref[...] = v
    kn_ref[...] = kk
    kb_ref[...] = kk * a

    u = proj[:, ODD_RW_COLS:]
    halo = jnp.where(first, 0.0, u_carry[...])
    u_carry[...] = u[tm - POOL_HALO:, :]
    ext = jnp.concatenate([halo, u], axis=0)
    tseq = (i % tiles_per_seq) * tm + ridx
    for gi, win in enumerate(POOL_WINDOWS):
        xg = ext[:, gi * POOL_GROUP:(gi + 1) * POOL_GROUP]
        s = xg
        step = 1
        while step < win:
            s = s + pltpu.roll(s, step, 0)
            step *= 2
        cnt = jnp.minimum(tseq + 1, win).astype(F32)
        pooled = s[POOL_HALO:] / cnt - xg[POOL_HALO:]
        mixed = _mm(pooled.astype(BF16), pw_ref[gi])
        op_ref[:, gi * POOL_GROUP:(gi + 1) * POOL_GROUP] = (
            mixed * ps_ref[:, gi * POOL_GROUP:(gi + 1) * POOL_GROUP])


def odd_in_proj(x2, g, sh, sc, w_pad, mu_pad, w0, w2p, a0, a2p, g2, k_k, k_a, ones_bd, pool_w,
                pool_scale, seq):
    n, d = x2.shape
    tm = min(TOK_TILE, seq)
    tpb = seq // tm
    row = lambda i: (i, 0)
    per_b = lambda i: (i // tpb, 0, 0)
    c2 = lambda i: (0, 0)
    full2 = lambda a: pl.BlockSpec(a.shape, c2)
    return pl.pallas_call(
        functools.partial(_odd_in_kernel, tiles_per_seq=tpb, tm=tm),
        out_shape=[jax.ShapeDtypeStruct((n, RWKV_DIM), F32)] * 8,
        grid=(n // tm,),
        in_specs=[pl.BlockSpec((tm, d), row), pl.BlockSpec((1, d), c2),
                  pl.BlockSpec((1, 1, d), per_b), pl.BlockSpec((1, 1, d), per_b),
                  full2(w_pad), full2(mu_pad), full2(w0), full2(w2p), full2(a0), full2(a2p),
                  full2(g2), full2(k_k), full2(k_a), full2(ones_bd),
                  pl.BlockSpec(pool_w.shape, lambda i: (0, 0, 0)), full2(pool_scale)],
        out_specs=[pl.BlockSpec((tm, RWKV_DIM), row)] * 8,
        scratch_shapes=[pltpu.VMEM((SUBLANES, ODD_RW_COLS), F32),
                        pltpu.VMEM((POOL_HALO, RWKV_DIM), F32)],
        compiler_params=_cparams(("arbitrary",)),
        name="odd_in_proj",
    )(x2, g, sh, sc, w_pad, mu_pad, w0, w2p, a0, a2p, g2, k_k, k_a, ones_bd, pool_w, pool_scale)


def _bmm(a, b):
    return lax.dot_general(a, b, (((2,), (1,)), ((0,), (0,))), preferred_element_type=F32)


def _bnt(a, b):
    return lax.dot_general(a, b, (((2,), (2,)), ((0,), (0,))), preferred_element_type=F32)


def _btn(a, b):
    return lax.dot_general(a, b, (((1,), (1,)), ((0,), (0,))), preferred_element_type=F32)


def _scan_prep_kernel(r_ref, lw_ref, km_ref, v_ref, kn_ref, kb_ref, qe_ref, y0_ref, mt_ref, ct_ref,
                      *, chunk, cb):
    L = chunk
    rows = cb * L
    n_pairs = N_RWKV_HEADS // 2
    two = 2 * L
    rowt = lax.broadcasted_iota(jnp.int32, (rows, 1), 0) % L
    lane = lax.broadcasted_iota(jnp.int32, (1, 1, LANES), 2)
    low = lane < HEAD_DIM
    ri = lax.broadcasted_iota(jnp.int32, (two, two), 0)
    ci = lax.broadcasted_iota(jnp.int32, (two, two), 1)
    same_blk = (ri // L) == (ci // L)
    strict = same_blk & ((ci % L) < (ri % L))
    incl = same_blk & ((ci % L) <= (ri % L))
    li = lax.broadcasted_iota(jnp.int32, (LANES, LANES), 0)
    lj = lax.broadcasted_iota(jnp.int32, (LANES, LANES), 1)
    same_head = (li // HEAD_DIM) == (lj // HEAD_DIM)
    eye = li == lj

    lw = lw_ref[...]
    cum = lw
    step = 1
    while step < L:
        cum = cum + jnp.where(rowt >= step, pltpu.roll(cum, step, 0), 0.0)
        step *= 2

    def to3(x):
        x3 = x.reshape(cb, L, RWKV_DIM)
        return jnp.concatenate([x3[:, :, p * LANES:(p + 1) * LANES] for p in range(n_pairs)], axis=0)

    def stack2(x):
        return jnp.concatenate([jnp.where(low, x, 0.0), jnp.where(low, 0.0, x)], axis=1)

    def fold(x):
        return x[:, :L, :] + x[:, L:, :]

    cum3 = to3(cum)
    lw3 = to3(lw)
    cum_l = cum3[:, L - 1:L, :]
    g_inv = jnp.exp(-cum3)
    g_tail = jnp.exp(cum_l - cum3)
    kb = to3(kb_ref[...])
    km = to3(km_ref[...])
    v = to3(v_ref[...])
    at_s = stack2(-to3(kn_ref[...]) * jnp.exp(cum3 - lw3))
    rt_s = stack2(to3(r_ref[...]) * jnp.exp(cum3))
    v_s = stack2(v).astype(BF16)
    lhs = jnp.concatenate([at_s, rt_s], axis=1).astype(BF16)
    rhs = jnp.concatenate([stack2(kb * g_inv), stack2(km * g_inv)], axis=1).astype(BF16)
    prod = _bnt(lhs, rhs)
    nmat = jnp.where(strict, prod[:, :two, :two], 0.0)
    a_ak = jnp.where(strict, prod[:, :two, two:], 0.0).astype(BF16)
    a_rb = jnp.where(incl, prod[:, two:, :two], 0.0).astype(BF16)
    a_rk = jnp.where(incl, prod[:, two:, two:], 0.0).astype(BF16)

    x = jnp.concatenate([at_s, _bmm(a_ak, v_s)], axis=2)
    npow = nmat
    step = 1
    while step < L:
        nb = npow.astype(BF16)
        x = x + _bmm(nb, x.astype(BF16))
        step *= 2
        if step < L:
            npow = _bmm(nb, nb)
    qy = _bmm(a_rb, x.astype(BF16))
    qe = fold(rt_s + qy[:, :, :LANES])
    y0 = fold(qy[:, :, LANES:] + _bmm(a_rk, v_s))
    wu = fold(x).astype(BF16)
    bwu = _btn((kb * g_tail).astype(BF16), wu)
    kv = _btn((km * g_tail).astype(BF16), v.astype(BF16))
    g_l = jnp.broadcast_to(jnp.exp(cum_l), (n_pairs * cb, LANES, LANES))
    mt = jnp.where(eye, g_l, 0.0) + jnp.where(same_head, bwu[:, :, :LANES], 0.0)
    ct = jnp.where(same_head, bwu[:, :, LANES:] + kv, 0.0)
    for p in range(n_pairs):
        sl = slice(p * LANES, (p + 1) * LANES)
        qe_ref[:, sl] = qe[p * cb:(p + 1) * cb].reshape(rows, LANES)
        y0_ref[:, sl] = y0[p * cb:(p + 1) * cb].reshape(rows, LANES)
        mt_ref[:, p] = mt[p * cb:(p + 1) * cb].astype(BF16)
        ct_ref[:, p] = ct[p * cb:(p + 1) * cb]


def _scan_state_kernel(qe_ref, y0_ref, mt_ref, ct_ref, y_ref, st_ref, *, batch):
    @pl.when(pl.program_id(0) == 0)
    def _():
        st_ref[...] = jnp.zeros_like(st_ref)

    n_pairs = N_RWKV_HEADS // 2
    qe = qe_ref[...]
    qe3 = jnp.concatenate([qe[:, :, p * LANES:(p + 1) * LANES] for p in range(n_pairs)], axis=0)
    st = st_ref[...].astype(BF16)
    y = _bmm(qe3.astype(BF16), st)
    for p in range(n_pairs):
        sl = slice(p * LANES, (p + 1) * LANES)
        y_ref[:, :, sl] = y[p * batch:(p + 1) * batch] + y0_ref[:, :, sl]
    mt = jnp.concatenate([mt_ref[:, 0, p] for p in range(n_pairs)], axis=0)
    ct = jnp.concatenate([ct_ref[:, 0, p] for p in range(n_pairs)], axis=0)
    st_ref[...] = _bmm(mt, st) + ct


def rwkv_scan(r, lw, km, v, kn, kb, batch, seq):
    n = batch * seq
    chunk = min(SCAN_CHUNK, seq)
    nc = seq // chunk
    cb = min(SCAN_CHUNKS_PER_STEP, nc)
    n_pairs = N_RWKV_HEADS // 2
    blk = pl.BlockSpec((cb * chunk, RWKV_DIM), lambda i: (i, 0))
    mblk = pl.BlockSpec((cb, n_pairs, LANES, LANES), lambda i: (i, 0, 0, 0))
    qe, y0, mt, ct = pl.pallas_call(
        functools.partial(_scan_prep_kernel, chunk=chunk, cb=cb),
        out_shape=[jax.ShapeDtypeStruct((n, RWKV_DIM), F32), jax.ShapeDtypeStruct((n, RWKV_DIM), F32),
                   jax.ShapeDtypeStruct((n // chunk, n_pairs, LANES, LANES), BF16),
                   jax.ShapeDtypeStruct((n // chunk, n_pairs, LANES, LANES), F32)],
        grid=(n // (cb * chunk),),
        in_specs=[blk] * 6,
        out_specs=[blk, blk, mblk, mblk],
        compiler_params=_cparams(("parallel",)),
        name="rwkv_scan_prep",
    )(r, lw, km, v, kn, kb)
    sblk = pl.BlockSpec((batch, chunk, RWKV_DIM), lambda c: (0, c, 0))
    smblk = pl.BlockSpec((batch, 1, n_pairs, LANES, LANES), lambda c: (0, c, 0, 0, 0))
    y = pl.pallas_call(
        functools.partial(_scan_state_kernel, batch=batch),
        out_shape=jax.ShapeDtypeStruct((batch, seq, RWKV_DIM), F32),
        grid=(nc,),
        in_specs=[sblk, sblk, smblk, smblk],
        out_specs=sblk,
        scratch_shapes=[pltpu.VMEM((n_pairs * batch, LANES, LANES), F32)],
        compiler_params=_cparams(("arbitrary",)),
        name="rwkv_scan_state",
    )(qe.reshape(batch, seq, RWKV_DIM), y0.reshape(batch, seq, RWKV_DIM),
      mt.reshape(batch, nc, n_pairs, LANES, LANES), ct.reshape(batch, nc, n_pairs, LANES, LANES))
    return y.reshape(n, RWKV_DIM)


def _odd_out_kernel(y_ref, r_ref, km_ref, v_ref, gg_ref, op_ref, rk_ref, lnw_ref, lnb_ref, ones_ref,
                    *tail_refs):
    ones = ones_ref[...]
    inv = 1.0 / HEAD_DIM
    y = y_ref[...]
    mean = _split_sum(y, ones) * inv
    yc = y - mean
    var = _split_sum(yc * yc, ones) * inv
    yn = yc * lax.rsqrt(var + LNX_EPS) * lnw_ref[...] + lnb_ref[...]
    bonus = _split_sum(r_ref[...] * km_ref[...] * rk_ref[...], ones) * v_ref[...]
    o_rwkv = (yn + bonus) * gg_ref[...]
    cat = jnp.concatenate([o_rwkv, op_ref[...]], axis=1).astype(BF16)
    _tail(cat, *tail_refs)


def odd_out_proj(y, r, km, v, gg, opool, r_k, lnx_w, lnx_b, ones_bd, x2, g1, w_out, ng, sh2, sc2,
                 rwt, rb, seq):
    n, d = x2.shape
    tm = min(TOK_TILE, seq)
    tpb = seq // tm
    row = lambda i: (i, 0)
    c2 = lambda i: (0, 0)
    act = pl.BlockSpec((tm, RWKV_DIM), row)
    vec = pl.BlockSpec((1, RWKV_DIM), c2)
    tin, tout, tscratch = _tail_specs(tm, d, tpb)
    return pl.pallas_call(
        _odd_out_kernel,
        out_shape=_tail_out_shapes(n, d),
        grid=(n // tm,),
        in_specs=[act] * 6 + [vec, vec, vec, pl.BlockSpec(ones_bd.shape, c2)] + tin,
        out_specs=tout,
        scratch_shapes=tscratch,
        compiler_params=_cparams(("arbitrary",)),
        name="odd_out_proj",
    )(y, r, km, v, gg, opool, r_k, lnx_w, lnx_b, ones_bd, x2, g1, w_out, ng, sh2, sc2, rwt, rb,
      _strict_upper(tm))


def _rope_tables(seq):
    half = HEAD_DIM // 2
    inv = ROPE_THETA ** (-jnp.arange(half, dtype=F32) / half)
    ang = jnp.arange(seq, dtype=F32)[:, None] * inv[None, :]
    return jnp.tile(jnp.cos(ang), (1, LANES // half)), jnp.tile(jnp.sin(ang), (1, LANES // half))


def _even_w_pad(w_in):
    d = w_in.shape[0]
    q_kv = w_in[:, :NSA_DIM + 6 * KV_DIM]
    gl = w_in[:, NSA_DIM + 6 * KV_DIM:NSA_DIM + 6 * KV_DIM + 24]
    rest = w_in[:, NSA_DIM + 6 * KV_DIM + 24:]
    z = jnp.zeros((d, LANES - 12), w_in.dtype)
    return jnp.concatenate([q_kv, gl[:, :12], z, gl[:, 12:], z, rest], axis=1).astype(BF16)


def _compress_params(cmp_pos, cmp_w1, cmp_w2):
    eye = jnp.eye(N_KV_HEADS, dtype=F32)
    w1r = cmp_w1.reshape(2, 2, CMP_STRIDE, HEAD_DIM, CMP_HIDDEN)
    w1_ext = jnp.einsum('kpmdn,gh->kpmgdhn', w1r, eye).reshape(
        2, 2, CMP_STRIDE * KV_DIM, N_KV_HEADS * CMP_HIDDEN).astype(BF16)
    w2_ext = jnp.einsum('knd,gh->kgnhd', cmp_w2, eye).reshape(
        2, N_KV_HEADS * CMP_HIDDEN, KV_DIM).astype(BF16)
    pos = cmp_pos.reshape(2, 2, CMP_STRIDE, 1, HEAD_DIM)
    pos_ext = jnp.broadcast_to(pos, (2, 2, CMP_STRIDE, N_KV_HEADS, HEAD_DIM)).reshape(
        2, 2, 1, CMP_STRIDE * KV_DIM)
    return pos_ext, w1_ext, w2_ext


def _nsa_tables(seq):
    n_blk = seq // SEL_BLOCK
    n_cmp = (seq - CMP_BLOCK) // CMP_STRIDE + 1
    n_cmp_pad = seq // CMP_STRIDE
    r = SEL_BLOCK // CMP_STRIDE
    c = CMP_BLOCK // CMP_STRIDE
    msel = np.zeros((n_cmp_pad, LANES), np.float32)
    for j in range(n_blk):
        for m in range(r):
            for n in range(c):
                idx = r * j + m + n
                if idx < n_cmp:
                    msel[idx, j] += 1.0
    selq = np.zeros((N_KV_HEADS, GQA, GQA * HEAD_DIM, LANES), np.float32)
    for h in range(N_KV_HEADS):
        for g in range(GQA):
            for dd in range(HEAD_DIM):
                selq[h, g, g * HEAD_DIM + dd, h * HEAD_DIM + dd] = 1.0
    return jnp.asarray(msel.T), jnp.asarray(selq, dtype=BF16)


def _odd_params(w_in, mu, w2, a2):
    d = w_in.shape[0]
    z64 = jnp.zeros((d, 64), w_in.dtype)
    w_pad = jnp.concatenate([w_in[:, :1536], w_in[:, 1536:1600], z64, w_in[:, 1600:1664], z64,
                             w_in[:, 1664:]], axis=1).astype(BF16)
    m64 = jnp.zeros((64,), mu.dtype)
    mu_pad = jnp.concatenate([mu[:1536], mu[1536:1600], m64, mu[1600:1664], m64, mu[1664:]])[None, :]
    zr = jnp.zeros((64, RWKV_DIM), w2.dtype)
    w2p = jnp.concatenate([w2, zr], axis=0).astype(BF16)
    a2p = jnp.concatenate([a2, zr], axis=0).astype(BF16)
    return w_pad, mu_pad, w2p, a2p


def _head_ones():
    idx = np.arange(RWKV_DIM) // HEAD_DIM
    return jnp.asarray((idx[:, None] == idx[None, :]).astype(np.float32), dtype=BF16)


def kernel(x, c, ada_w, ada_b, norm_mix, norm_ffn, even_w_in, even_cmp_pos, even_cmp_w1, even_cmp_w2,
           even_conv_w, even_w_out, odd_w_in, odd_mu, odd_w0, odd_w2, odd_a0, odd_a2, odd_g2, odd_k_k,
           odd_k_a, odd_r_k, odd_lnx_w, odd_lnx_b, odd_pool_w, odd_pool_scale, odd_w_out,
           router_w, router_b, moe_w_gate, moe_w_up, moe_w_down, final_norm):
    batch, seq, d = x.shape
    n = batch * seq
    depth = ada_w.shape[0]
    x2 = x.reshape(n, d)
    mod = ada_modulation(c, ada_w, ada_b)
    rwt = router_w.T
    rb = router_b.reshape(N_EXPERTS, 1)
    fnorm = final_norm.reshape(1, d)
    cos, sin = _rope_tables(seq)
    mselt, selq = _nsa_tables(seq)
    ones_bd = _head_ones()

    for layer in range(depth):
        m = mod[layer].reshape(batch, 6, 1, d)
        sh1, sc1, g1, sh2, sc2, g2 = (m[:, k] for k in range(6))
        ng_mix = norm_mix[layer].reshape(1, d)
        ng_ffn = norm_ffn[layer].reshape(1, d)
        i = layer // 2
        if layer % 2 == 0:
            (qn, qr, kc, vc, ks, vs, kw, vw, gate, u, bg) = even_in_proj(
                x2, ng_mix, sh1, sc1, _even_w_pad(even_w_in[i]), cos, sin, seq)
            pos_ext, w1_ext, w2_ext = _compress_params(even_cmp_pos[i], even_cmp_w1[i], even_cmp_w2[i])
            kcmp, vcmp = compress_kv(kc, vc, pos_ext, w1_ext, w2_ext, batch, seq)
            o_nsa = nsa_attention(qn, qr, kcmp, vcmp, ks, vs, kw, vw, gate, selq, mselt, batch, seq)
            x1, hw, ridx, counts = even_out_proj(o_nsa, u, bg, even_conv_w[i], x2, g1,
                                                 even_w_out[i].astype(BF16), ng_ffn, sh2, sc2, rwt, rb, seq)
        else:
            w_pad, mu_pad, w2p, a2p = _odd_params(odd_w_in[i], odd_mu[i], odd_w2[i], odd_a2[i])
            vec = lambda a: a.reshape(1, RWKV_DIM)
            (r, lw, km, v, kn, kb, gg, opool) = odd_in_proj(
                x2, ng_mix, sh1, sc1, w_pad, mu_pad, vec(odd_w0[i]), w2p, vec(odd_a0[i]), a2p,
                odd_g2[i].astype(BF16), vec(odd_k_k[i]), vec(odd_k_a[i]), ones_bd,
                odd_pool_w[i].astype(BF16), vec(odd_pool_scale[i]), seq)
            y = rwkv_scan(r, lw, km, v, kn, kb, batch, seq)
            x1, hw, ridx, counts = odd_out_proj(
                y, r, km, v, gg, opool, vec(odd_r_k[i]), vec(odd_lnx_w[i]), vec(odd_lnx_b[i]), ones_bd,
                x2, g1, odd_w_out[i].astype(BF16), ng_ffn, sh2, sc2, rwt, rb, seq)
        x2 = moe_sparse(hw, ridx, counts, moe_w_gate, moe_w_up, moe_w_down, layer,
                        x1, g2, fnorm, seq, final_norm=(layer == depth - 1))
    return x2.reshape(batch, seq, d)
```

```python
import functools

import jax
import jax.numpy as jnp
import numpy as np
from jax import lax
from jax.experimental import pallas as pl
from jax.experimental.pallas import tpu as pltpu

F32 = jnp.float32
BF16 = jnp.bfloat16
HIGHEST = lax.Precision.HIGHEST

D_MODEL = 1024
DEPTH = 2
HEAD_DIM = 64
ROPE_THETA = 10000.0
NORM_EPS = 1e-6
NEG_INF = -1e30
BIG = 1e9
NSA_DIM = 512
N_KV_HEADS = 2
GQA = 4
KV_DIM = 128
CMP_BLOCK = 32
CMP_STRIDE = 16
CMP_HIDDEN = 256
SEL_BLOCK = 64
N_SEL = 8
N_LOCAL = 2
WINDOW = 512
NSA_Q_TILE = 256
ATTN_SCALE = HEAD_DIM ** -0.5
CONV_DIM = 512
RWKV_DIM = 512
N_RWKV_HEADS = 8
LNX_EPS = 64e-5
POOL_WINDOWS = (2, 4, 8, 16)
POOL_GROUP = 128
N_EXPERTS = 16
N_EXPERT_GROUPS = 4
EXPERTS_PER_GROUP = 4
D_EXPERT = 512
PAIRS_PER_GROUP = 6
PAIR_LO = (0, 0, 0, 1, 1, 2)
PAIR_HI = (1, 2, 3, 3, 2, 3)
N_CLASSES = N_EXPERT_GROUPS * PAIRS_PER_GROUP
CLASS_ROWS = 32

LANES = 128
SUBLANES = 8
VMEM_LIMIT = 56 * 1024 * 1024

TOK_TILE = 512
MOE_ROW_TILE = 256
MOE_DMA_TILE = 1024
SEL_CHUNK = 512
SCAN_CHUNK = 64
SCAN_CHUNKS_PER_STEP = 4
CONV_HALO = 8
POOL_HALO = 16

EVEN_PAD_COLS = 3072
ODD_PAD_COLS = 2432
ODD_RW_COLS = 1920


def _cparams(sem):
    return pltpu.CompilerParams(dimension_semantics=sem, vmem_limit_bytes=VMEM_LIMIT)


def _nt(a, b, precision=None):
    return lax.dot_general(a, b, (((1,), (1,)), ((), ())), preferred_element_type=F32,
                           precision=precision)


def _mm(a, b, precision=None):
    return jnp.dot(a, b, preferred_element_type=F32, precision=precision)


def _norm_mod(x, g, sh, sc):
    ms = jnp.mean(x * x, axis=-1, keepdims=True)
    return (x * lax.rsqrt(ms + NORM_EPS) * g) * (1.0 + sc) + sh


def _split_sum(x, ones_bf16):
    hi = x.astype(BF16)
    lo = (x - hi.astype(F32)).astype(BF16)
    return _mm(hi, ones_bf16) + _mm(lo, ones_bf16)


def _ada_kernel(c_ref, w_ref, b_ref, o_ref):
    c = c_ref[...]
    cond = c * jax.nn.sigmoid(c)
    o_ref[0] = _mm(cond, w_ref[0], precision=HIGHEST) + b_ref[0]


def ada_modulation(c, ada_w, ada_b):
    depth, d, cols = ada_w.shape
    b = c.shape[0]
    tn = 1536
    return pl.pallas_call(
        _ada_kernel,
        out_shape=jax.ShapeDtypeStruct((depth, b, cols), F32),
        grid=(depth, cols // tn),
        in_specs=[pl.BlockSpec((b, d), lambda l, j: (0, 0)),
                  pl.BlockSpec((1, d, tn), lambda l, j: (l, 0, j)),
                  pl.BlockSpec((1, 1, tn), lambda l, j: (l, 0, j))],
        out_specs=pl.BlockSpec((1, b, tn), lambda l, j: (l, 0, j)),
        compiler_params=_cparams(("parallel", "parallel")),
        name="ada_modulation",
    )(c, ada_w, ada_b.reshape(depth, 1, cols))


def _rope128(t, cos, sin, lane):
    rot = jnp.where((lane % HEAD_DIM) < HEAD_DIM // 2,
                    -pltpu.roll(t, LANES - HEAD_DIM // 2, 1), pltpu.roll(t, HEAD_DIM // 2, 1))
    return t * cos + rot * sin


def _even_in_kernel(x_ref, g_ref, sh_ref, sc_ref, w_ref, cos_ref, sin_ref,
                    qn_ref, qr_ref, kc_ref, vc_ref, ks_ref, vs_ref, kw_ref, vw_ref,
                    gate_ref, u_ref, bg_ref, *, tiles_per_seq):
    h = _norm_mod(x_ref[...], g_ref[...], sh_ref[0], sc_ref[0])
    tm = h.shape[0]
    proj = _mm(h.astype(BF16), w_ref[...])
    cos = cos_ref[...]
    sin = sin_ref[...]
    lane = lax.broadcasted_iota(jnp.int32, (1, LANES), 1)
    low = lane < HEAD_DIM
    for i in range(NSA_DIM // LANES):
        q = proj[:, i * LANES:(i + 1) * LANES] * ATTN_SCALE
        qn_ref[:, i * LANES:(i + 1) * LANES] = q.astype(BF16)
        qr_ref[:, i * LANES:(i + 1) * LANES] = _rope128(q, cos, sin, lane).astype(BF16)
    o = NSA_DIM
    kc_ref[...] = proj[:, o:o + 128]
    vc_ref[...] = proj[:, o + 128:o + 256]
    pos = (pl.program_id(0) % tiles_per_seq) * tm + lax.broadcasted_iota(jnp.int32, (tm, 1), 0)
    blk = pos // SEL_BLOCK
    ks = _rope128(proj[:, o + 256:o + 384], cos, sin, lane)
    ks_ref[:, 0:LANES] = jnp.where(low, ks, jnp.where(lane - HEAD_DIM == blk, 1.0, 0.0)).astype(BF16)
    ks_ref[:, LANES:2 * LANES] = jnp.where(low, jnp.where(lane == blk, 1.0, 0.0), ks).astype(BF16)
    vs = proj[:, o + 384:o + 512]
    vs_ref[:, 0:LANES] = jnp.where(low, vs, 1.0).astype(BF16)
    vs_ref[:, LANES:2 * LANES] = jnp.where(low, 1.0, vs).astype(BF16)
    kw_ref[...] = _rope128(proj[:, o + 512:o + 640], cos, sin, lane).astype(BF16)
    vw = proj[:, o + 640:o + 768]
    vw_ref[:, 0:LANES] = jnp.where(low, vw, 1.0).astype(BF16)
    vw_ref[:, LANES:2 * LANES] = jnp.where(low, 1.0, vw).astype(BF16)
    o += 768
    gate_ref[...] = jax.nn.sigmoid(proj[:, o:o + 256])
    o += 256
    xb = proj[:, o:o + 512]
    bg_ref[...] = proj[:, o + 512:o + 1024]
    u_ref[...] = proj[:, o + 1024:o + 1536] * xb


def even_in_proj(x2, g, sh, sc, w_pad, cos, sin, seq):
    n, d = x2.shape
    tm = min(TOK_TILE, seq)
    tpb = seq // tm
    row = lambda i: (i, 0)
    per_b = lambda i: (i // tpb, 0, 0)
    pos = lambda i: (i % tpb, 0)
    outs = [((n, 512), BF16), ((n, 512), BF16), ((n, 128), F32), ((n, 128), F32),
            ((n, 256), BF16), ((n, 256), BF16), ((n, 128), BF16), ((n, 256), BF16),
            ((n, 256), F32), ((n, 512), F32), ((n, 512), F32)]
    return pl.pallas_call(
        functools.partial(_even_in_kernel, tiles_per_seq=tpb),
        out_shape=[jax.ShapeDtypeStruct(s, t) for s, t in outs],
        grid=(n // tm,),
        in_specs=[pl.BlockSpec((tm, d), row),
                  pl.BlockSpec((1, d), lambda i: (0, 0)),
                  pl.BlockSpec((1, 1, d), per_b),
                  pl.BlockSpec((1, 1, d), per_b),
                  pl.BlockSpec((d, EVEN_PAD_COLS), lambda i: (0, 0)),
                  pl.BlockSpec((tm, LANES), pos),
                  pl.BlockSpec((tm, LANES), pos)],
        out_specs=[pl.BlockSpec((tm, s[1]), row) for s, _ in outs],
        compiler_params=_cparams(("parallel",)),
        name="even_in_proj",
    )(x2, g, sh, sc, w_pad, cos, sin)


def _compress_kernel(k_ref, v_ref, pos_ref, w1_ref, w2_ref, ko_ref, vo_ref):
    for j, (src, dst) in enumerate(((k_ref, ko_ref), (v_ref, vo_ref))):
        xr = src[0]
        n_rows = xr.shape[0]
        a0 = _mm((xr + pos_ref[j, 0]).astype(BF16), w1_ref[j, 0])
        a1 = _mm((xr + pos_ref[j, 1]).astype(BF16), w1_ref[j, 1])
        hid = a0 + pltpu.roll(a1, n_rows - 1, 0)
        hid = jax.nn.gelu(hid)
        dst[0] = _mm(hid.astype(BF16), w2_ref[j]).astype(BF16)


def compress_kv(kc, vc, pos_ext, w1_ext, w2_ext, batch, seq):
    rows = seq // CMP_STRIDE
    width = CMP_STRIDE * KV_DIM
    kr = kc.reshape(batch, rows, width)
    vr = vc.reshape(batch, rows, width)
    blk = pl.BlockSpec((1, rows, width), lambda b: (b, 0, 0))
    oblk = pl.BlockSpec((1, rows, KV_DIM), lambda b: (b, 0, 0))
    return pl.pallas_call(
        _compress_kernel,
        out_shape=[jax.ShapeDtypeStruct((batch, rows, KV_DIM), BF16)] * 2,
        grid=(batch,),
        in_specs=[blk, blk,
                  pl.BlockSpec(pos_ext.shape, lambda b: (0, 0, 0, 0)),
                  pl.BlockSpec(w1_ext.shape, lambda b: (0, 0, 0, 0)),
                  pl.BlockSpec(w2_ext.shape, lambda b: (0, 0, 0))],
        out_specs=[oblk, oblk],
        compiler_params=_cparams(("parallel",)),
        name="compress_kv",
    )(kr, vr, pos_ext, w1_ext, w2_ext)


def _safe_inv(l):
    return jnp.where(l > 0.0, 1.0 / jnp.where(l > 0.0, l, 1.0), 0.0)


def _nsa_kernel(qn_ref, qr_ref, kc_ref, vc_ref, ks_ref, vs_ref, kw_ref, vw_ref, gate_ref,
                selq_ref, mselt_ref, o_ref, *, seq, n_sel, sel_chunk, win_len):
    h = pl.program_id(1)
    qt = pl.program_id(2)
    t0 = qt * NSA_Q_TILE
    n_blk = seq // SEL_BLOCK
    n_cmp_pad = seq // CMP_STRIDE
    rows = GQA * NSA_Q_TILE
    tpos = t0 + lax.broadcasted_iota(jnp.int32, (1, NSA_Q_TILE, 1), 1)
    lane = lax.broadcasted_iota(jnp.int32, (1, LANES), 1)
    head_lanes = (lane // HEAD_DIM) == h

    def normalise(acc):
        return acc * _safe_inv(pltpu.roll(acc, HEAD_DIM, 1))

    qn = qn_ref[...]
    qr = qr_ref[...]
    qn4 = jnp.concatenate([_mm(qn, selq_ref[0, g]) for g in range(GQA)], axis=0).astype(BF16)
    qr4f = jnp.concatenate([_mm(qr, selq_ref[0, g]) for g in range(GQA)], axis=0)
    qr4 = qr4f.astype(BF16)

    kc = kc_ref[0]
    vc = vc_ref[0]
    cpos = lax.broadcasted_iota(jnp.int32, (1, 1, n_cmp_pad), 2) * CMP_STRIDE + (CMP_BLOCK - 1)
    cmask = cpos <= tpos
    s = jnp.where(cmask, _nt(qn4, kc).reshape(GQA, NSA_Q_TILE, n_cmp_pad), NEG_INF)
    e = jnp.where(cmask, jnp.exp(s - jnp.max(s, axis=2, keepdims=True)), 0.0)
    p = e * _safe_inv(jnp.sum(e, axis=2, keepdims=True))
    imp = jnp.sum(p, axis=0)
    o_cmp = _mm(p.reshape(rows, n_cmp_pad).astype(BF16), vc)

    pslc = _nt(mselt_ref[...], imp, precision=HIGHEST)[:n_blk]
    tq = t0 + lax.broadcasted_iota(jnp.int32, (1, NSA_Q_TILE), 1)
    jblk = lax.broadcasted_iota(jnp.int32, (n_blk, 1), 0)
    cur = tq // SEL_BLOCK
    valid = jblk * SEL_BLOCK <= tq
    forced = (jblk == 0) | ((cur - jblk >= 0) & (cur - jblk < N_LOCAL))
    score = jnp.where(forced, BIG, jnp.where(valid, pslc, -BIG))
    rank = jnp.zeros((n_blk, NSA_Q_TILE), jnp.int32)
    for jp in range(n_blk):
        row = score[jp:jp + 1, :]
        beats = (row > score) | ((row == score) & (jblk > jp))
        rank = rank + beats.astype(jnp.int32)
    selb = jnp.where((rank < n_sel) & (score > -0.5 * BIG), 0.0, NEG_INF)
    selb_q = jnp.concatenate([selb, jnp.zeros((LANES - n_blk, NSA_Q_TILE), F32)], axis=0).T
    bias = jnp.where(h == 0, pltpu.roll(selb_q, HEAD_DIM, 1), selb_q)
    qs4 = (qr4f + jnp.concatenate([bias] * GQA, axis=0)).astype(BF16)

    def chunk_scores(c):
        start = pl.multiple_of(c * sel_chunk, sel_chunk)
        kblk = ks_ref[0, pl.ds(start, sel_chunk), :]
        vblk = vs_ref[0, pl.ds(start, sel_chunk), :]
        return start, _nt(qs4, kblk).reshape(GQA, NSA_Q_TILE, sel_chunk), vblk

    def online_update(carry, s, vblk):
        m, acc = carry
        m_new = jnp.maximum(m, jnp.max(s, axis=2, keepdims=True))
        alpha = jnp.exp(m - m_new)
        p = jnp.exp(s - m_new)
        pv = _mm(p.reshape(rows, sel_chunk).astype(BF16), vblk).reshape(GQA, NSA_Q_TILE, LANES)
        return m_new, alpha * acc + pv

    def sel_body(c, carry):
        _, s, vblk = chunk_scores(c)
        return online_update(carry, s, vblk)

    diag_chunk = (t0 + NSA_Q_TILE - 1) // sel_chunk
    init = (jnp.full((GQA, NSA_Q_TILE, 1), NEG_INF, F32), jnp.zeros((GQA, NSA_Q_TILE, LANES), F32))
    carry = lax.fori_loop(0, diag_chunk, sel_body, init)
    start, s, vblk = chunk_scores(diag_chunk)
    kpos = start + lax.broadcasted_iota(jnp.int32, (1, 1, sel_chunk), 2)
    _, acc = online_update(carry, jnp.where(kpos <= tpos, s, NEG_INF), vblk)
    o_slc = normalise(acc.reshape(rows, LANES))

    ws = pl.multiple_of(jnp.maximum(qt - WINDOW // NSA_Q_TILE, 0) * NSA_Q_TILE, NSA_Q_TILE)
    kwb = kw_ref[0, pl.ds(ws, win_len), :]
    vwb = vw_ref[0, pl.ds(ws, win_len), :]
    diff = tpos - (ws + lax.broadcasted_iota(jnp.int32, (1, 1, win_len), 2))
    wmask = (diff >= 0) & (diff < WINDOW)
    s = jnp.where(wmask, _nt(qr4, kwb).reshape(GQA, NSA_Q_TILE, win_len), NEG_INF)
    e = jnp.exp(s - jnp.max(s, axis=2, keepdims=True))
    o_win = normalise(_mm(e.reshape(rows, win_len).astype(BF16), vwb))

    gate = gate_ref[...]
    og = []
    for g in range(GQA):
        sl = slice(g * NSA_Q_TILE, (g + 1) * NSA_Q_TILE)
        o = (gate[:, 3 * g:3 * g + 1] * o_cmp[sl] + gate[:, 3 * g + 1:3 * g + 2] * o_slc[sl]
             + gate[:, 3 * g + 2:3 * g + 3] * o_win[sl])
        og.append(jnp.where(head_lanes, o, pltpu.roll(o, HEAD_DIM, 1)))
    low = lane < HEAD_DIM
    o_ref[:, 0:LANES] = jnp.where(low, og[0], og[1])
    o_ref[:, LANES:2 * LANES] = jnp.where(low, og[2], og[3])


def nsa_attention(qn, qr, kcmp, vcmp, ks, vs, kw, vw, gate, selq, mselt, batch, seq):
    n = batch * seq
    nq = seq // NSA_Q_TILE
    sel_chunk = min(SEL_CHUNK, seq)
    win_len = min(WINDOW + NSA_Q_TILE, seq)
    n_sel = min(N_SEL, seq // SEL_BLOCK)
    qspec = pl.BlockSpec((NSA_Q_TILE, GQA * HEAD_DIM), lambda b, h, q: (b * nq + q, h))
    cspec = pl.BlockSpec((1, seq // CMP_STRIDE, KV_DIM), lambda b, h, q: (b, 0, 0))
    both = pl.BlockSpec((1, seq, KV_DIM), lambda b, h, q: (b, 0, 0))
    mine = pl.BlockSpec((1, seq, KV_DIM), lambda b, h, q: (b, 0, h))
    kern = functools.partial(_nsa_kernel, seq=seq, n_sel=n_sel, sel_chunk=sel_chunk, win_len=win_len)
    return pl.pallas_call(
        kern,
        out_shape=jax.ShapeDtypeStruct((n, NSA_DIM), F32),
        grid=(batch, N_KV_HEADS, nq),
        in_specs=[qspec, qspec, cspec, cspec, mine, mine, both, mine,
                  pl.BlockSpec((NSA_Q_TILE, LANES), lambda b, h, q: (b * nq + q, h)),
                  pl.BlockSpec((1, GQA, GQA * HEAD_DIM, LANES), lambda b, h, q: (h, 0, 0, 0)),
                  pl.BlockSpec(mselt.shape, lambda b, h, q: (0, 0))],
        out_specs=qspec,
        compiler_params=_cparams(("parallel", "parallel", "arbitrary")),
        name="nsa_attention",
    )(qn, qr, kcmp, vcmp, ks.reshape(batch, seq, 2 * KV_DIM), vs.reshape(batch, seq, 2 * KV_DIM),
      kw.reshape(batch, seq, KV_DIM), vw.reshape(batch, seq, 2 * KV_DIM), gate, selq, mselt)


def _route(h2, rwt_ref, rb_ref):
    rw = rwt_ref[...]
    rw_hi = rw.astype(BF16)
    rw_lo = (rw - rw_hi.astype(F32)).astype(BF16)
    h_hi = h2.astype(BF16)
    h_lo = (h2 - h_hi.astype(F32)).astype(BF16)
    logits = _nt(rw_hi, h_hi) + (_nt(rw_hi, h_lo) + _nt(rw_lo, h_hi))
    scores = jax.nn.sigmoid(logits)
    biased = scores + rb_ref[...]
    rows = [biased[e:e + 1, :] for e in range(N_EXPERTS)]
    srow = [scores[e:e + 1, :] for e in range(N_EXPERTS)]
    gscore = []
    for gi in range(N_EXPERT_GROUPS):
        r = rows[gi * EXPERTS_PER_GROUP:(gi + 1) * EXPERTS_PER_GROUP]
        best = None
        for a in range(EXPERTS_PER_GROUP):
            for b in range(a + 1, EXPERTS_PER_GROUP):
                pair = r[a] + r[b]
                best = pair if best is None else jnp.maximum(best, pair)
        gscore.append(best)
    top_val = gscore[0]
    top_grp = jnp.zeros_like(top_val, dtype=jnp.int32)
    for gi in range(1, N_EXPERT_GROUPS):
        upd = gscore[gi] > top_val
        top_grp = jnp.where(upd, gi, top_grp)
        top_val = jnp.where(upd, gscore[gi], top_val)
    masked = [jnp.where(top_grp == e // EXPERTS_PER_GROUP, rows[e], NEG_INF) for e in range(N_EXPERTS)]
    b1 = masked[0]
    i1 = jnp.zeros_like(top_grp)
    for e in range(1, N_EXPERTS):
        upd = masked[e] > b1
        i1 = jnp.where(upd, e, i1)
        b1 = jnp.where(upd, masked[e], b1)
    b2 = None
    i2 = None
    for e in range(N_EXPERTS):
        v = jnp.where(i1 == e, -jnp.inf, masked[e])
        if b2 is None:
            b2, i2 = v, jnp.zeros_like(top_grp)
        else:
            upd = v > b2
            i2 = jnp.where(upd, e, i2)
            b2 = jnp.where(upd, v, b2)
    s1 = jnp.zeros_like(top_val)
    s2 = jnp.zeros_like(top_val)
    for e in range(N_EXPERTS):
        s1 = s1 + jnp.where(i1 == e, srow[e], 0.0)
        s2 = s2 + jnp.where(i2 == e, srow[e], 0.0)
    tot = s1 + s2
    return i1, i2, s1 / tot, s2 / tot


def _tail(cat_bf16, x_ref, g1_ref, wout_ref, ng_ref, sh2_ref, sc2_ref, rwt_ref, rb_ref, ustrict_ref,
          x1_ref, hw_ref, ridx_ref, cnt_ref, carry_ref):
    y = _mm(cat_bf16, wout_ref[...])
    x1 = x_ref[...] + g1_ref[0] * y
    x1_ref[...] = x1
    h2 = _norm_mod(x1, ng_ref[...], sh2_ref[0], sc2_ref[0])
    tm, d = h2.shape
    hw_ref[:, :d] = h2

    i1, i2, w1, w2 = _route(h2, rwt_ref, rb_ref)
    lo = jnp.minimum(i1, i2) % EXPERTS_PER_GROUP
    hi = jnp.maximum(i1, i2) % EXPERTS_PER_GROUP
    pair = jnp.where(lo == 0, hi - 1, jnp.where(lo == 1, jnp.where(hi == 3, 3, 4), 5))
    cls = (i1 // EXPERTS_PER_GROUP) * PAIRS_PER_GROUP + pair
    w_lo = jnp.where(i1 < i2, w1, w2)
    w_hi = jnp.where(i1 < i2, w2, w1)
    meta_t = jnp.concatenate([w_lo, w_hi, jnp.zeros((LANES - 2, tm), F32)], axis=0)
    hw_ref[:, d:] = meta_t.T

    @pl.when(pl.program_id(0) == 0)
    def _():
        carry_ref[...] = jnp.zeros_like(carry_ref)

    hit = lax.broadcasted_iota(jnp.int32, (CLASS_ROWS, 1), 0) == cls
    cnt = jnp.where(hit, 1.0, 0.0)
    before = _mm(cnt.astype(BF16), ustrict_ref[...]) + carry_ref[:, 0:1]
    carry = carry_ref[...] + jnp.sum(cnt, axis=1, keepdims=True)
    carry_ref[...] = carry
    cnt_ref[...] = carry.astype(jnp.int32)
    ridx_ref[0:1, :] = cls
    ridx_ref[1:2, :] = jnp.sum(jnp.where(hit, before, 0.0), axis=0, keepdims=True).astype(jnp.int32)


def _even_out_kernel(o_ref, u_ref, uh_ref, bg_ref, cw_ref, *tail_refs, tiles_per_seq):
    first = (pl.program_id(0) % tiles_per_seq) == 0
    u = u_ref[...]
    halo = jnp.where(first, 0.0, uh_ref[...])
    ext = jnp.concatenate([halo, u], axis=0)
    u1 = pltpu.roll(ext, 1, 0)[CONV_HALO:]
    u2 = pltpu.roll(ext, 2, 0)[CONV_HALO:]
    cw = cw_ref[...]
    y_conv = bg_ref[...] * (cw[2:3] * u + cw[1:2] * u1 + cw[0:1] * u2)
    cat = jnp.concatenate([o_ref[...], y_conv], axis=1).astype(BF16)
    _tail(cat, *tail_refs)


def _tail_specs(tm, d, tpb):
    row = lambda i: (i, 0)
    per_b = lambda i: (i // tpb, 0, 0)
    const2 = lambda i: (0, 0)
    ins = [pl.BlockSpec((tm, d), row),
           pl.BlockSpec((1, 1, d), per_b),
           pl.BlockSpec((d, d), const2),
           pl.BlockSpec((1, d), const2),
           pl.BlockSpec((1, 1, d), per_b),
           pl.BlockSpec((1, 1, d), per_b),
           pl.BlockSpec((N_EXPERTS, d), const2),
           pl.BlockSpec((N_EXPERTS, 1), const2),
           pl.BlockSpec((tm, tm), const2)]
    outs = [pl.BlockSpec((tm, d), row), pl.BlockSpec((tm, d + LANES), row),
            pl.BlockSpec((2, tm), lambda i: (0, i)),
            pl.BlockSpec((CLASS_ROWS, LANES), const2)]
    scratch = [pltpu.VMEM((CLASS_ROWS, LANES), F32)]
    return ins, outs, scratch


def _tail_out_shapes(n, d):
    return [jax.ShapeDtypeStruct((n, d), F32), jax.ShapeDtypeStruct((n, d + LANES), F32),
            jax.ShapeDtypeStruct((2, n), jnp.int32), jax.ShapeDtypeStruct((CLASS_ROWS, LANES), jnp.int32)]


def _strict_upper(tm):
    return jnp.asarray(np.triu(np.ones((tm, tm), np.float32), 1), dtype=BF16)


def even_out_proj(o_nsa, u, bg, conv_w, x2, g1, w_out, ng, sh2, sc2, rwt, rb, seq):
    n, d = x2.shape
    tm = min(TOK_TILE, seq)
    tpb = seq // tm
    row = lambda i: (i, 0)
    halo = lambda i: (jnp.maximum(i * (tm // CONV_HALO) - 1, 0), 0)
    tin, tout, tscratch = _tail_specs(tm, d, tpb)
    return pl.pallas_call(
        functools.partial(_even_out_kernel, tiles_per_seq=tpb),
        out_shape=_tail_out_shapes(n, d),
        grid=(n // tm,),
        in_specs=[pl.BlockSpec((tm, NSA_DIM), row),
                  pl.BlockSpec((tm, CONV_DIM), row),
                  pl.BlockSpec((CONV_HALO, CONV_DIM), halo),
                  pl.BlockSpec((tm, CONV_DIM), row),
                  pl.BlockSpec(conv_w.shape, lambda i: (0, 0))] + tin,
        out_specs=tout,
        scratch_shapes=tscratch,
        compiler_params=_cparams(("arbitrary",)),
        name="even_out_proj",
    )(o_nsa, u, u, bg, conv_w, x2, g1, w_out, ng, sh2, sc2, rwt, rb, _strict_upper(tm))


def _dispatch_plan(ridx, counts, n):
    cnt = counts[:N_CLASSES, 0]
    padded = (cnt + MOE_ROW_TILE - 1) // MOE_ROW_TILE * MOE_ROW_TILE
    ends = jnp.cumsum(padded)
    starts = ends - padded
    cids = jnp.arange(N_CLASSES, dtype=jnp.int32)[:, None]
    base = jnp.sum(jnp.where(cids == ridx[0][None, :], starts[:, None], 0), axis=0)
    dest = (base + ridx[1]).astype(jnp.int32)
    td = min(MOE_DMA_TILE, n)
    dest3 = dest.reshape(n // td, 1, td)
    n_tiles = n // MOE_ROW_TILE + N_CLASSES
    tile_start = jnp.arange(n_tiles, dtype=jnp.int32) * MOE_ROW_TILE
    tile_class = jnp.minimum(jnp.sum(tile_start[:, None] >= ends[None, :], axis=1), N_CLASSES - 1)
    group_base = (tile_class // PAIRS_PER_GROUP) * EXPERTS_PER_GROUP
    pair = tile_class % PAIRS_PER_GROUP
    tile_lo = (group_base + jnp.asarray(PAIR_LO, jnp.int32)[pair]).astype(jnp.int32)
    tile_hi = (group_base + jnp.asarray(PAIR_HI, jnp.int32)[pair]).astype(jnp.int32)
    n_used = (ends[-1] // MOE_ROW_TILE).reshape(1).astype(jnp.int32)
    last_tile = jnp.where(cnt > 0, ends // MOE_ROW_TILE - 1, -1)
    tail = n_used[0] + jnp.arange(N_CLASSES, dtype=jnp.int32)
    zero_tiles = jnp.concatenate([last_tile, jnp.where(tail < n_tiles, tail, -1)]).astype(jnp.int32)
    return dest3, tile_lo, tile_hi, n_used, zero_tiles, n_tiles


def _dispatch_kernel(ztile_ref, dest_ref, hw_ref, xs_hbm, zbuf, zsem, sem):
    td = hw_ref.shape[0]

    @pl.when(pl.program_id(0) == 0)
    def _():
        zbuf[...] = jnp.zeros_like(zbuf)

        def zero_copy(k):
            start = pl.multiple_of(ztile_ref[k] * MOE_ROW_TILE, MOE_ROW_TILE)
            return pltpu.make_async_copy(zbuf, xs_hbm.at[pl.ds(start, MOE_ROW_TILE)], zsem)

        for k in range(2 * N_CLASSES):
            @pl.when(ztile_ref[k] >= 0)
            def _():
                zero_copy(k).start()
        for k in range(2 * N_CLASSES):
            @pl.when(ztile_ref[k] >= 0)
            def _():
                zero_copy(k).wait()

    for r in range(td):
        pltpu.make_async_copy(hw_ref.at[pl.ds(r, 1)],
                              xs_hbm.at[pl.ds(dest_ref[0, 0, r], 1)], sem).start()
    pltpu.make_async_copy(hw_ref, xs_hbm.at[pl.ds(0, td)], sem).wait()


def moe_dispatch(hw, dest3, zero_tiles, n_tiles):
    n, cols = hw.shape
    td = dest3.shape[2]
    rows = n_tiles * MOE_ROW_TILE
    return pl.pallas_call(
        _dispatch_kernel,
        out_shape=jax.ShapeDtypeStruct((rows, cols), F32),
        grid_spec=pltpu.PrefetchScalarGridSpec(
            num_scalar_prefetch=1,
            grid=(n // td,),
            in_specs=[pl.BlockSpec((1, 1, td), lambda i, z: (i, 0, 0), memory_space=pltpu.SMEM),
                      pl.BlockSpec((td, cols), lambda i, z: (i, 0))],
            out_specs=pl.BlockSpec(memory_space=pl.ANY),
            scratch_shapes=[pltpu.VMEM((MOE_ROW_TILE, cols), F32), pltpu.SemaphoreType.DMA(()),
                            pltpu.SemaphoreType.DMA(())]),
        compiler_params=_cparams(("arbitrary",)),
        name="moe_dispatch",
    )(zero_tiles, dest3, hw)


def _expert_kernel(lo_ref, hi_ref, nused_ref, xs_ref, wg_lo, wu_lo, wd_lo, wg_hi, wu_hi, wd_hi, ys_ref,
                   *wb):
    t = pl.program_id(0)
    prev = jnp.maximum(t - 1, 0)

    for ids, srcs, dsts in ((lo_ref, (wg_lo, wu_lo, wd_lo), wb[:3]), (hi_ref, (wg_hi, wu_hi, wd_hi), wb[3:])):
        @pl.when((t == 0) | (ids[t] != ids[prev]))
        def _():
            for src, dst in zip(srcs, dsts):
                dst[...] = src[0, 0].astype(BF16)

    @pl.when(t < nused_ref[0])
    def _():
        d = xs_ref.shape[1] - LANES
        x = xs_ref[:, :d].astype(BF16)
        meta = xs_ref[:, d:]
        y = None
        for k in range(2):
            a = _mm(x, wb[3 * k][...])
            b = _mm(x, wb[3 * k + 1][...])
            he = (a * jax.nn.sigmoid(a)) * b
            yk = meta[:, k:k + 1] * _mm(he.astype(BF16), wb[3 * k + 2][...])
            y = yk if y is None else y + yk
        ys_ref[...] = y

    @pl.when(t >= nused_ref[0])
    def _():
        ys_ref[...] = jnp.zeros_like(ys_ref)


def moe_experts(xs, tile_lo, tile_hi, n_used, w_gate, w_up, w_down, layer, d):
    rows, cols = xs.shape
    n_tiles = rows // MOE_ROW_TILE
    lo_spec = lambda shape: pl.BlockSpec((1, 1) + shape, lambda t, lo, hi, nu: (layer, lo[t], 0, 0))
    hi_spec = lambda shape: pl.BlockSpec((1, 1) + shape, lambda t, lo, hi, nu: (layer, hi[t], 0, 0))
    shapes = ((d, D_EXPERT), (d, D_EXPERT), (D_EXPERT, d))
    return pl.pallas_call(
        _expert_kernel,
        out_shape=jax.ShapeDtypeStruct((rows, d), F32),
        grid_spec=pltpu.PrefetchScalarGridSpec(
            num_scalar_prefetch=3,
            grid=(n_tiles,),
            in_specs=[pl.BlockSpec((MOE_ROW_TILE, cols),
                                   lambda t, lo, hi, nu: (jnp.minimum(t, nu[0] - 1), 0))]
                     + [lo_spec(s) for s in shapes] + [hi_spec(s) for s in shapes],
            out_specs=pl.BlockSpec((MOE_ROW_TILE, d), lambda t, lo, hi, nu: (t, 0)),
            scratch_shapes=[pltpu.VMEM(s, BF16) for s in shapes + shapes]),
        compiler_params=_cparams(("arbitrary",)),
        name="moe_experts",
    )(tile_lo, tile_hi, n_used, xs, w_gate, w_up, w_down, w_gate, w_up, w_down)


def _combine_kernel(dest_ref, ys_hbm, x_ref, g2_ref, fn_ref, o_ref, buf, sem, *, final_norm):
    tc = x_ref.shape[0]

    for r in range(tc):
        pltpu.make_async_copy(ys_hbm.at[pl.ds(dest_ref[0, 0, r], 1)], buf.at[pl.ds(r, 1)], sem).start()
    pltpu.make_async_copy(ys_hbm.at[pl.ds(0, tc)], buf, sem).wait()
    x = x_ref[...] + g2_ref[0] * buf[...]
    if final_norm:
        ms = jnp.mean(x * x, axis=-1, keepdims=True)
        x = x * lax.rsqrt(ms + NORM_EPS) * fn_ref[...]
    o_ref[...] = x


def moe_combine(ys, dest3, x1, g2, fnorm, seq, final_norm):
    n, d = x1.shape
    tc = dest3.shape[2]
    tpb = seq // tc
    return pl.pallas_call(
        functools.partial(_combine_kernel, final_norm=final_norm),
        out_shape=jax.ShapeDtypeStruct((n, d), F32),
        grid=(n // tc,),
        in_specs=[pl.BlockSpec((1, 1, tc), lambda i: (i, 0, 0), memory_space=pltpu.SMEM),
                  pl.BlockSpec(memory_space=pl.ANY),
                  pl.BlockSpec((tc, d), lambda i: (i, 0)),
                  pl.BlockSpec((1, 1, d), lambda i: (i // tpb, 0, 0)),
                  pl.BlockSpec((1, d), lambda i: (0, 0))],
        out_specs=pl.BlockSpec((tc, d), lambda i: (i, 0)),
        scratch_shapes=[pltpu.VMEM((tc, d), F32), pltpu.SemaphoreType.DMA(())],
        compiler_params=_cparams(("arbitrary",)),
        name="moe_combine",
    )(dest3, ys, x1, g2, fnorm)


def moe_sparse(hw, ridx, counts, w_gate, w_up, w_down, layer, x1, g2, fnorm, seq, final_norm):
    n, d = x1.shape
    dest3, tile_lo, tile_hi, n_used, zero_tiles, n_tiles = _dispatch_plan(ridx, counts, n)
    xs = moe_dispatch(hw, dest3, zero_tiles, n_tiles)
    ys = moe_experts(xs, tile_lo, tile_hi, n_used, w_gate, w_up, w_down, layer, d)
    return moe_combine(ys, dest3, x1, g2, fnorm, seq, final_norm)


def _odd_in_kernel(x_ref, g_ref, sh_ref, sc_ref, w_ref, mu_ref, w0_ref, w2_ref, a0_ref, a2_ref,
                   g2_ref, kk_ref, ka_ref, ones_ref, pw_ref, ps_ref,
                   r_ref, lw_ref, km_ref, v_ref, kn_ref, kb_ref, gg_ref, op_ref,
                   rw_carry, u_carry, *, tiles_per_seq, tm):
    i = pl.program_id(0)
    first = (i % tiles_per_seq) == 0
    h = _norm_mod(x_ref[...], g_ref[...], sh_ref[0], sc_ref[0])
    proj = _mm(h.astype(BF16), w_ref[...])

    rw = proj[:, :ODD_RW_COLS]
    row0 = jnp.where(first, 0.0, rw_carry[0:1, :])
    ridx = lax.broadcasted_iota(jnp.int32, (tm, 1), 0)
    prev = jnp.where(ridx == 0, row0, pltpu.roll(rw, 1, 0))
    rw_carry[0:1, :] = rw[tm - 1:tm, :]
    rw = rw + (prev - rw) * mu_ref[...]

    r = rw[:, 0:512]
    k = rw[:, 512:1024]
    v = rw[:, 1024:1536]
    wl = rw[:, 1536:1664]
    al = rw[:, 1664:1792]
    gl = rw[:, 1792:1920]
    z = -(w0_ref[...] + _mm(jnp.tanh(wl).astype(BF16), w2_ref[...]))
    softplus = jnp.maximum(z, 0.0) + jnp.log1p(jnp.exp(-jnp.abs(z)))
    w_log = -softplus - 0.5
    a = jax.nn.sigmoid(a0_ref[...] + _mm(al.astype(BF16), a2_ref[...]))
    gg_ref[...] = _mm(jax.nn.sigmoid(gl).astype(BF16), g2_ref[...])
    kk0 = k * kk_ref[...]
    ss = _split_sum(kk0 * kk0, ones_ref[...])
    kk = kk0 / jnp.maximum(jnp.sqrt(ss), 1e-12)
    r_ref[...] = r
    lw_ref[...] = -jnp.exp(w_log)
    km_ref[...] = k * (1.0 + (a - 1.0) * ka_ref[...])
    v_ref[...] = v
    kn_ref[...] = kk
    kb_ref[...] = kk * a

    u = proj[:, ODD_RW_COLS:]
    halo = jnp.where(first, 0.0, u_carry[...])
    u_carry[...] = u[tm - POOL_HALO:, :]
    ext = jnp.concatenate([halo, u], axis=0)
    tseq = (i % tiles_per_seq) * tm + ridx
    for gi, win in enumerate(POOL_WINDOWS):
        xg = ext[:, gi * POOL_GROUP:(gi + 1) * POOL_GROUP]
        s = xg
        step = 1
        while step < win:
            s = s + pltpu.roll(s, step, 0)
            step *= 2
        cnt = jnp.minimum(tseq + 1, win).astype(F32)
        pooled = s[POOL_HALO:] / cnt - xg[POOL_HALO:]
        mixed = _mm(pooled.astype(BF16), pw_ref[gi])
        op_ref[:, gi * POOL_GROUP:(gi + 1) * POOL_GROUP] = (
            mixed * ps_ref[:, gi * POOL_GROUP:(gi + 1) * POOL_GROUP])


def odd_in_proj(x2, g, sh, sc, w_pad, mu_pad, w0, w2p, a0, a2p, g2, k_k, k_a, ones_bd, pool_w,
                pool_scale, seq):
    n, d = x2.shape
    tm = min(TOK_TILE, seq)
    tpb = seq // tm
    row = lambda i: (i, 0)
    per_b = lambda i: (i // tpb, 0, 0)
    c2 = lambda i: (0, 0)
    full2 = lambda a: pl.BlockSpec(a.shape, c2)
    return pl.pallas_call(
        functools.partial(_odd_in_kernel, tiles_per_seq=tpb, tm=tm),
        out_shape=[jax.ShapeDtypeStruct((n, RWKV_DIM), F32)] * 8,
        grid=(n // tm,),
        in_specs=[pl.BlockSpec((tm, d), row), pl.BlockSpec((1, d), c2),
                  pl.BlockSpec((1, 1, d), per_b), pl.BlockSpec((1, 1, d), per_b),
                  full2(w_pad), full2(mu_pad), full2(w0), full2(w2p), full2(a0), full2(a2p),
                  full2(g2), full2(k_k), full2(k_a), full2(ones_bd),
                  pl.BlockSpec(pool_w.shape, lambda i: (0, 0, 0)), full2(pool_scale)],
        out_specs=[pl.BlockSpec((tm, RWKV_DIM), row)] * 8,
        scratch_shapes=[pltpu.VMEM((SUBLANES, ODD_RW_COLS), F32),
                        pltpu.VMEM((POOL_HALO, RWKV_DIM), F32)],
        compiler_params=_cparams(("arbitrary",)),
        name="odd_in_proj",
    )(x2, g, sh, sc, w_pad, mu_pad, w0, w2p, a0, a2p, g2, k_k, k_a, ones_bd, pool_w, pool_scale)


def _bmm(a, b):
    return lax.dot_general(a, b, (((2,), (1,)), ((0,), (0,))), preferred_element_type=F32)


def _bnt(a, b):
    return lax.dot_general(a, b, (((2,), (2,)), ((0,), (0,))), preferred_element_type=F32)


def _btn(a, b):
    return lax.dot_general(a, b, (((1,), (1,)), ((0,), (0,))), preferred_element_type=F32)


def _scan_prep_kernel(r_ref, lw_ref, km_ref, v_ref, kn_ref, kb_ref, qe_ref, y0_ref, mt_ref, ct_ref,
                      *, chunk, cb):
    L = chunk
    rows = cb * L
    n_pairs = N_RWKV_HEADS // 2
    two = 2 * L
    rowt = lax.broadcasted_iota(jnp.int32, (rows, 1), 0) % L
    lane = lax.broadcasted_iota(jnp.int32, (1, 1, LANES), 2)
    low = lane < HEAD_DIM
    ri = lax.broadcasted_iota(jnp.int32, (two, two), 0)
    ci = lax.broadcasted_iota(jnp.int32, (two, two), 1)
    same_blk = (ri // L) == (ci // L)
    strict = same_blk & ((ci % L) < (ri % L))
    incl = same_blk & ((ci % L) <= (ri % L))
    li = lax.broadcasted_iota(jnp.int32, (LANES, LANES), 0)
    lj = lax.broadcasted_iota(jnp.int32, (LANES, LANES), 1)
    same_head = (li // HEAD_DIM) == (lj // HEAD_DIM)
    eye = li == lj

    lw = lw_ref[...]
    cum = lw
    step = 1
    while step < L:
        cum = cum + jnp.where(rowt >= step, pltpu.roll(cum, step, 0), 0.0)
        step *= 2

    def to3(x):
        x3 = x.reshape(cb, L, RWKV_DIM)
        return jnp.concatenate([x3[:, :, p * LANES:(p + 1) * LANES] for p in range(n_pairs)], axis=0)

    def stack2(x):
        return jnp.concatenate([jnp.where(low, x, 0.0), jnp.where(low, 0.0, x)], axis=1)

    def fold(x):
        return x[:, :L, :] + x[:, L:, :]

    cum3 = to3(cum)
    lw3 = to3(lw)
    cum_l = cum3[:, L - 1:L, :]
    g_inv = jnp.exp(-cum3)
    g_tail = jnp.exp(cum_l - cum3)
    kb = to3(kb_ref[...])
    km = to3(km_ref[...])
    v = to3(v_ref[...])
    at_s = stack2(-to3(kn_ref[...]) * jnp.exp(cum3 - lw3))
    rt_s = stack2(to3(r_ref[...]) * jnp.exp(cum3))
    v_s = stack2(v).astype(BF16)
    lhs = jnp.concatenate([at_s, rt_s], axis=1).astype(BF16)
    rhs = jnp.concatenate([stack2(kb * g_inv), stack2(km * g_inv)], axis=1).astype(BF16)
    prod = _bnt(lhs, rhs)
    nmat = jnp.where(strict, prod[:, :two, :two], 0.0)
    a_ak = jnp.where(strict, prod[:, :two, two:], 0.0).astype(BF16)
    a_rb = jnp.where(incl, prod[:, two:, :two], 0.0).astype(BF16)
    a_rk = jnp.where(incl, prod[:, two:, two:], 0.0).astype(BF16)

    x = jnp.concatenate([at_s, _bmm(a_ak, v_s)], axis=2)
    npow = nmat
    step = 1
    while step < L:
        nb = npow.astype(BF16)
        x = x + _bmm(nb, x.astype(BF16))
        step *= 2
        if step < L:
            npow = _bmm(nb, nb)
    qy = _bmm(a_rb, x.astype(BF16))
    qe = fold(rt_s + qy[:, :, :LANES])
    y0 = fold(qy[:, :, LANES:] + _bmm(a_rk, v_s))
    wu = fold(x).astype(BF16)
    bwu = _btn((kb * g_tail).astype(BF16), wu)
    kv = _btn((km * g_tail).astype(BF16), v.astype(BF16))
    g_l = jnp.broadcast_to(jnp.exp(cum_l), (n_pairs * cb, LANES, LANES))
    mt = jnp.where(eye, g_l, 0.0) + jnp.where(same_head, bwu[:, :, :LANES], 0.0)
    ct = jnp.where(same_head, bwu[:, :, LANES:] + kv, 0.0)
    for p in range(n_pairs):
        sl = slice(p * LANES, (p + 1) * LANES)
        qe_ref[:, sl] = qe[p * cb:(p + 1) * cb].reshape(rows, LANES)
        y0_ref[:, sl] = y0[p * cb:(p + 1) * cb].reshape(rows, LANES)
        mt_ref[:, p] = mt[p * cb:(p + 1) * cb].astype(BF16)
        ct_ref[:, p] = ct[p * cb:(p + 1) * cb]


def _scan_state_kernel(qe_ref, y0_ref, mt_ref, ct_ref, y_ref, st_ref, *, batch):
    @pl.when(pl.program_id(0) == 0)
    def _():
        st_ref[...] = jnp.zeros_like(st_ref)

    n_pairs = N_RWKV_HEADS // 2
    qe = qe_ref[...]
    qe3 = jnp.concatenate([qe[:, :, p * LANES:(p + 1) * LANES] for p in range(n_pairs)], axis=0)
    st = st_ref[...].astype(BF16)
    y = _bmm(qe3.astype(BF16), st)
    for p in range(n_pairs):
        sl = slice(p * LANES, (p + 1) * LANES)
        y_ref[:, :, sl] = y[p * batch:(p + 1) * batch] + y0_ref[:, :, sl]
    mt = jnp.concatenate([mt_ref[:, 0, p] for p in range(n_pairs)], axis=0)
    ct = jnp.concatenate([ct_ref[:, 0, p] for p in range(n_pairs)], axis=0)
    st_ref[...] = _bmm(mt, st) + ct


def rwkv_scan(r, lw, km, v, kn, kb, batch, seq):
    n = batch * seq
    chunk = min(SCAN_CHUNK, seq)
    nc = seq // chunk
    cb = min(SCAN_CHUNKS_PER_STEP, nc)
    n_pairs = N_RWKV_HEADS // 2
    blk = pl.BlockSpec((cb * chunk, RWKV_DIM), lambda i: (i, 0))
    mblk = pl.BlockSpec((cb, n_pairs, LANES, LANES), lambda i: (i, 0, 0, 0))
    qe, y0, mt, ct = pl.pallas_call(
        functools.partial(_scan_prep_kernel, chunk=chunk, cb=cb),
        out_shape=[jax.ShapeDtypeStruct((n, RWKV_DIM), F32), jax.ShapeDtypeStruct((n, RWKV_DIM), F32),
                   jax.ShapeDtypeStruct((n // chunk, n_pairs, LANES, LANES), BF16),
                   jax.ShapeDtypeStruct((n // chunk, n_pairs, LANES, LANES), F32)],
        grid=(n // (cb * chunk),),
        in_specs=[blk] * 6,
        out_specs=[blk, blk, mblk, mblk],
        compiler_params=_cparams(("parallel",)),
        name="rwkv_scan_prep",
    )(r, lw, km, v, kn, kb)
    sblk = pl.BlockSpec((batch, chunk, RWKV_DIM), lambda c: (0, c, 0))
    smblk = pl.BlockSpec((batch, 1, n_pairs, LANES, LANES), lambda c: (0, c, 0, 0, 0))
    y = pl.pallas_call(
        functools.partial(_scan_state_kernel, batch=batch),
        out_shape=jax.ShapeDtypeStruct((batch, seq, RWKV_DIM), F32),
        grid=(nc,),
        in_specs=[sblk, sblk, smblk, smblk],
        out_specs=sblk,
        scratch_shapes=[pltpu.VMEM((n_pairs * batch, LANES, LANES), F32)],
        compiler_params=_cparams(("arbitrary",)),
        name="rwkv_scan_state",
    )(qe.reshape(batch, seq, RWKV_DIM), y0.reshape(batch, seq, RWKV_DIM),
      mt.reshape(batch, nc, n_pairs, LANES, LANES), ct.reshape(batch, nc, n_pairs, LANES, LANES))
    return y.reshape(n, RWKV_DIM)


def _odd_out_kernel(y_ref, r_ref, km_ref, v_ref, gg_ref, op_ref, rk_ref, lnw_ref, lnb_ref, ones_ref,
                    *tail_refs):
    ones = ones_ref[...]
    inv = 1.0 / HEAD_DIM
    y = y_ref[...]
    mean = _split_sum(y, ones) * inv
    yc = y - mean
    var = _split_sum(yc * yc, ones) * inv
    yn = yc * lax.rsqrt(var + LNX_EPS) * lnw_ref[...] + lnb_ref[...]
    bonus = _split_sum(r_ref[...] * km_ref[...] * rk_ref[...], ones) * v_ref[...]
    o_rwkv = (yn + bonus) * gg_ref[...]
    cat = jnp.concatenate([o_rwkv, op_ref[...]], axis=1).astype(BF16)
    _tail(cat, *tail_refs)


def odd_out_proj(y, r, km, v, gg, opool, r_k, lnx_w, lnx_b, ones_bd, x2, g1, w_out, ng, sh2, sc2,
                 rwt, rb, seq):
    n, d = x2.shape
    tm = min(TOK_TILE, seq)
    tpb = seq // tm
    row = lambda i: (i, 0)
    c2 = lambda i: (0, 0)
    act = pl.BlockSpec((tm, RWKV_DIM), row)
    vec = pl.BlockSpec((1, RWKV_DIM), c2)
    tin, tout, tscratch = _tail_specs(tm, d, tpb)
    return pl.pallas_call(
        _odd_out_kernel,
        out_shape=_tail_out_shapes(n, d),
        grid=(n // tm,),
        in_specs=[act] * 6 + [vec, vec, vec, pl.BlockSpec(ones_bd.shape, c2)] + tin,
        out_specs=tout,
        scratch_shapes=tscratch,
        compiler_params=_cparams(("arbitrary",)),
        name="odd_out_proj",
    )(y, r, km, v, gg, opool, r_k, lnx_w, lnx_b, ones_bd, x2, g1, w_out, ng, sh2, sc2, rwt, rb,
      _strict_upper(tm))


def _rope_tables(seq):
    half = HEAD_DIM // 2
    inv = ROPE_THETA ** (-jnp.arange(half, dtype=F32) / half)
    ang = jnp.arange(seq, dtype=F32)[:, None] * inv[None, :]
    return jnp.tile(jnp.cos(ang), (1, LANES // half)), jnp.tile(jnp.sin(ang), (1, LANES // half))


def _even_w_pad(w_in):
    d = w_in.shape[0]
    q_kv = w_in[:, :NSA_DIM + 6 * KV_DIM]
    gl = w_in[:, NSA_DIM + 6 * KV_DIM:NSA_DIM + 6 * KV_DIM + 24]
    rest = w_in[:, NSA_DIM + 6 * KV_DIM + 24:]
    z = jnp.zeros((d, LANES - 12), w_in.dtype)
    return jnp.concatenate([q_kv, gl[:, :12], z, gl[:, 12:], z, rest], axis=1).astype(BF16)


def _compress_params(cmp_pos, cmp_w1, cmp_w2):
    eye = jnp.eye(N_KV_HEADS, dtype=F32)
    w1r = cmp_w1.reshape(2, 2, CMP_STRIDE, HEAD_DIM, CMP_HIDDEN)
    w1_ext = jnp.einsum('kpmdn,gh->kpmgdhn', w1r, eye).reshape(
        2, 2, CMP_STRIDE * KV_DIM, N_KV_HEADS * CMP_HIDDEN).astype(BF16)
    w2_ext = jnp.einsum('knd,gh->kgnhd', cmp_w2, eye).reshape(
        2, N_KV_HEADS * CMP_HIDDEN, KV_DIM).astype(BF16)
    pos = cmp_pos.reshape(2, 2, CMP_STRIDE, 1, HEAD_DIM)
    pos_ext = jnp.broadcast_to(pos, (2, 2, CMP_STRIDE, N_KV_HEADS, HEAD_DIM)).reshape(
        2, 2, 1, CMP_STRIDE * KV_DIM)
    return pos_ext, w1_ext, w2_ext


def _nsa_tables(seq):
    n_blk = seq // SEL_BLOCK
    n_cmp = (seq - CMP_BLOCK) // CMP_STRIDE + 1
    n_cmp_pad = seq // CMP_STRIDE
    r = SEL_BLOCK // CMP_STRIDE
    c = CMP_BLOCK // CMP_STRIDE
    msel = np.zeros((n_cmp_pad, LANES), np.float32)
    for j in range(n_blk):
        for m in range(r):
            for n in range(c):
                idx = r * j + m + n
                if idx < n_cmp:
                    msel[idx, j] += 1.0
    selq = np.zeros((N_KV_HEADS, GQA, GQA * HEAD_DIM, LANES), np.float32)
    for h in range(N_KV_HEADS):
        for g in range(GQA):
            for dd in range(HEAD_DIM):
                selq[h, g, g * HEAD_DIM + dd, h * HEAD_DIM + dd] = 1.0
    return jnp.asarray(msel.T), jnp.asarray(selq, dtype=BF16)


def _odd_params(w_in, mu, w2, a2):
    d = w_in.shape[0]
    z64 = jnp.zeros((d, 64), w_in.dtype)
    w_pad = jnp.concatenate([w_in[:, :1536], w_in[:, 1536:1600], z64, w_in[:, 1600:1664], z64,
                             w_in[:, 1664:]], axis=1).astype(BF16)
    m64 = jnp.zeros((64,), mu.dtype)
    mu_pad = jnp.concatenate([mu[:1536], mu[1536:1600], m64, mu[1600:1664], m64, mu[1664:]])[None, :]
    zr = jnp.zeros((64, RWKV_DIM), w2.dtype)
    w2p = jnp.concatenate([w2, zr], axis=0).astype(BF16)
    a2p = jnp.concatenate([a2, zr], axis=0).astype(BF16)
    return w_pad, mu_pad, w2p, a2p


def _head_ones():
    idx = np.arange(RWKV_DIM) // HEAD_DIM
    return jnp.asarray((idx[:, None] == idx[None, :]).astype(np.float32), dtype=BF16)


def kernel(x, c, ada_w, ada_b, norm_mix, norm_ffn, even_w_in, even_cmp_pos, even_cmp_w1, even_cmp_w2,
           even_conv_w, even_w_out, odd_w_in, odd_mu, odd_w0, odd_w2, odd_a0, odd_a2, odd_g2, odd_k_k,
           odd_k_a, odd_r_k, odd_lnx_w, odd_lnx_b, odd_pool_w, odd_pool_scale, odd_w_out,
           router_w, router_b, moe_w_gate, moe_w_up, moe_w_down, final_norm):
    batch, seq, d = x.shape
    n = batch * seq
    depth = ada_w.shape[0]
    x2 = x.reshape(n, d)
    mod = ada_modulation(c, ada_w, ada_b)
    rwt = router_w.T
    rb = router_b.reshape(N_EXPERTS, 1)
    fnorm = final_norm.reshape(1, d)
    cos, sin = _rope_tables(seq)
    mselt, selq = _nsa_tables(seq)
    ones_bd = _head_ones()

    for layer in range(depth):
        m = mod[layer].reshape(batch, 6, 1, d)
        sh1, sc1, g1, sh2, sc2, g2 = (m[:, k] for k in range(6))
        ng_mix = norm_mix[layer].reshape(1, d)
        ng_ffn = norm_ffn[layer].reshape(1, d)
        i = layer // 2
        if layer % 2 == 0:
            (qn, qr, kc, vc, ks, vs, kw, vw, gate, u, bg) = even_in_proj(
                x2, ng_mix, sh1, sc1, _even_w_pad(even_w_in[i]), cos, sin, seq)
            pos_ext, w1_ext, w2_ext = _compress_params(even_cmp_pos[i], even_cmp_w1[i], even_cmp_w2[i])
            kcmp, vcmp = compress_kv(kc, vc, pos_ext, w1_ext, w2_ext, batch, seq)
            o_nsa = nsa_attention(qn, qr, kcmp, vcmp, ks, vs, kw, vw, gate, selq, mselt, batch, seq)
            x1, hw, ridx, counts = even_out_proj(o_nsa, u, bg, even_conv_w[i], x2, g1,
                                                 even_w_out[i].astype(BF16), ng_ffn, sh2, sc2, rwt, rb, seq)
        else:
            w_pad, mu_pad, w2p, a2p = _odd_params(odd_w_in[i], odd_mu[i], odd_w2[i], odd_a2[i])
            vec = lambda a: a.reshape(1, RWKV_DIM)
            (r, lw, km, v, kn, kb, gg, opool) = odd_in_proj(
                x2, ng_mix, sh1, sc1, w_pad, mu_pad, vec(odd_w0[i]), w2p, vec(odd_a0[i]), a2p,
                odd_g2[i].astype(BF16), vec(odd_k_k[i]), vec(odd_k_a[i]), ones_bd,
                odd_pool_w[i].astype(BF16), vec(odd_pool_scale[i]), seq)
            y = rwkv_scan(r, lw, km, v, kn, kb, batch, seq)
            x1, hw, ridx, counts = odd_out_proj(
                y, r, km, v, gg, opool, vec(odd_r_k[i]), vec(odd_lnx_w[i]), vec(odd_lnx_b[i]), ones_bd,
                x2, g1, odd_w_out[i].astype(BF16), ng_ffn, sh2, sc2, rwt, rb, seq)
        x2 = moe_sparse(hw, ridx, counts, moe_w_gate, moe_w_up, moe_w_down, layer,
                        x1, g2, fnorm, seq, final_norm=(layer == depth - 1))
    return x2.reshape(batch, seq, d)
```

```python
import functools

import jax
import jax.numpy as jnp
import numpy as np
from jax import lax
from jax.experimental import pallas as pl
from jax.experimental.pallas import tpu as pltpu

F32 = jnp.float32
BF16 = jnp.bfloat16
HIGHEST = lax.Precision.HIGHEST

D_MODEL = 1024
DEPTH = 2
HEAD_DIM = 64
ROPE_THETA = 10000.0
NORM_EPS = 1e-6
NEG_INF = -1e30
BIG = 1e9
NSA_DIM = 512
N_KV_HEADS = 2
GQA = 4
KV_DIM = 128
CMP_BLOCK = 32
CMP_STRIDE = 16
CMP_HIDDEN = 256
SEL_BLOCK = 64
N_SEL = 8
N_LOCAL = 2
WINDOW = 512
NSA_Q_TILE = 256
ATTN_SCALE = HEAD_DIM ** -0.5
LOG2_E = 1.4426950408889634
CONV_DIM = 512
RWKV_DIM = 512
N_RWKV_HEADS = 8
LNX_EPS = 64e-5
POOL_WINDOWS = (2, 4, 8, 16)
POOL_GROUP = 128
N_EXPERTS = 16
N_EXPERT_GROUPS = 4
EXPERTS_PER_GROUP = 4
D_EXPERT = 512
PAIRS_PER_GROUP = 6
PAIR_LO = (0, 0, 0, 1, 1, 2)
PAIR_HI = (1, 2, 3, 3, 2, 3)
N_CLASSES = N_EXPERT_GROUPS * PAIRS_PER_GROUP
CLASS_ROWS = 32

LANES = 128
SUBLANES = 8
VMEM_LIMIT = 56 * 1024 * 1024

TOK_TILE = 512
MOE_ROW_TILE = 256
MOE_DMA_TILE = 2048
SEL_CHUNK = 512
SCAN_CHUNK = 64
SCAN_CHUNKS_PER_STEP = 4
CONV_HALO = 8
POOL_HALO = 16

EVEN_PAD_COLS = 3072
ODD_PAD_COLS = 2432
ODD_RW_COLS = 1920


def _cparams(sem):
    return pltpu.CompilerParams(dimension_semantics=sem, vmem_limit_bytes=VMEM_LIMIT)


def _nt(a, b, precision=None):
    return lax.dot_general(a, b, (((1,), (1,)), ((), ())), preferred_element_type=F32,
                           precision=precision)


def _mm(a, b, precision=None):
    return jnp.dot(a, b, preferred_element_type=F32, precision=precision)


def _norm_mod(x, g, sh, sc):
    ms = jnp.mean(x * x, axis=-1, keepdims=True)
    return (x * lax.rsqrt(ms + NORM_EPS) * g) * (1.0 + sc) + sh


def _split_sum(x, ones_bf16):
    hi = x.astype(BF16)
    lo = (x - hi.astype(F32)).astype(BF16)
    return _mm(hi, ones_bf16) + _mm(lo, ones_bf16)


def _ada_kernel(c_ref, w_ref, b_ref, o_ref):
    c = c_ref[...]
    cond = c * jax.nn.sigmoid(c)
    o_ref[0] = _mm(cond, w_ref[0], precision=HIGHEST) + b_ref[0]


def ada_modulation(c, ada_w, ada_b):
    depth, d, cols = ada_w.shape
    b = c.shape[0]
    tn = 1536
    return pl.pallas_call(
        _ada_kernel,
        out_shape=jax.ShapeDtypeStruct((depth, b, cols), F32),
        grid=(depth, cols // tn),
        in_specs=[pl.BlockSpec((b, d), lambda l, j: (0, 0)),
                  pl.BlockSpec((1, d, tn), lambda l, j: (l, 0, j)),
                  pl.BlockSpec((1, 1, tn), lambda l, j: (l, 0, j))],
        out_specs=pl.BlockSpec((1, b, tn), lambda l, j: (l, 0, j)),
        compiler_params=_cparams(("parallel", "parallel")),
        name="ada_modulation",
    )(c, ada_w, ada_b.reshape(depth, 1, cols))


def _rope128(t, cos, sin, lane):
    rot = jnp.where((lane % HEAD_DIM) < HEAD_DIM // 2,
                    -pltpu.roll(t, LANES - HEAD_DIM // 2, 1), pltpu.roll(t, HEAD_DIM // 2, 1))
    return t * cos + rot * sin


def _even_in_kernel(x_ref, g_ref, sh_ref, sc_ref, w_ref, cos_ref, sin_ref,
                    qn_ref, qr_ref, kc_ref, vc_ref, ks_ref, vs_ref, kw_ref, vw_ref,
                    gate_ref, u_ref, bg_ref, *, tiles_per_seq):
    h = _norm_mod(x_ref[...], g_ref[...], sh_ref[0], sc_ref[0])
    tm = h.shape[0]
    proj = _mm(h.astype(BF16), w_ref[...])
    cos = cos_ref[...]
    sin = sin_ref[...]
    lane = lax.broadcasted_iota(jnp.int32, (1, LANES), 1)
    low = lane < HEAD_DIM
    for i in range(NSA_DIM // LANES):
        q = proj[:, i * LANES:(i + 1) * LANES] * (ATTN_SCALE * LOG2_E)
        qn_ref[:, i * LANES:(i + 1) * LANES] = q.astype(BF16)
        qr_ref[:, i * LANES:(i + 1) * LANES] = _rope128(q, cos, sin, lane).astype(BF16)
    o = NSA_DIM
    kc_ref[...] = proj[:, o:o + 128]
    vc_ref[...] = proj[:, o + 128:o + 256]
    pos = (pl.program_id(0) % tiles_per_seq) * tm + lax.broadcasted_iota(jnp.int32, (tm, 1), 0)
    blk = pos // SEL_BLOCK
    ks = _rope128(proj[:, o + 256:o + 384], cos, sin, lane)
    ks_ref[:, 0:LANES] = jnp.where(low, ks, jnp.where(lane - HEAD_DIM == blk, 1.0, 0.0)).astype(BF16)
    ks_ref[:, LANES:2 * LANES] = jnp.where(low, jnp.where(lane == blk, 1.0, 0.0), ks).astype(BF16)
    vs = proj[:, o + 384:o + 512]
    vs_ref[:, 0:LANES] = jnp.where(low, vs, 1.0).astype(BF16)
    vs_ref[:, LANES:2 * LANES] = jnp.where(low, 1.0, vs).astype(BF16)
    kw_ref[...] = _rope128(proj[:, o + 512:o + 640], cos, sin, lane).astype(BF16)
    vw = proj[:, o + 640:o + 768]
    vw_ref[:, 0:LANES] = jnp.where(low, vw, 1.0).astype(BF16)
    vw_ref[:, LANES:2 * LANES] = jnp.where(low, 1.0, vw).astype(BF16)
    o += 768
    gate_ref[...] = jax.nn.sigmoid(proj[:, o:o + 256])
    o += 256
    xb = proj[:, o:o + 512]
    bg_ref[...] = proj[:, o + 512:o + 1024]
    u_ref[...] = proj[:, o + 1024:o + 1536] * xb


def even_in_proj(x2, g, sh, sc, w_pad, cos, sin, seq):
    n, d = x2.shape
    tm = min(TOK_TILE, seq)
    tpb = seq // tm
    row = lambda i: (i, 0)
    per_b = lambda i: (i // tpb, 0, 0)
    pos = lambda i: (i % tpb, 0)
    outs = [((n, 512), BF16), ((n, 512), BF16), ((n, 128), F32), ((n, 128), F32),
            ((n, 256), BF16), ((n, 256), BF16), ((n, 128), BF16), ((n, 256), BF16),
            ((n, 256), F32), ((n, 512), F32), ((n, 512), F32)]
    return pl.pallas_call(
        functools.partial(_even_in_kernel, tiles_per_seq=tpb),
        out_shape=[jax.ShapeDtypeStruct(s, t) for s, t in outs],
        grid=(n // tm,),
        in_specs=[pl.BlockSpec((tm, d), row),
                  pl.BlockSpec((1, d), lambda i: (0, 0)),
                  pl.BlockSpec((1, 1, d), per_b),
                  pl.BlockSpec((1, 1, d), per_b),
                  pl.BlockSpec((d, EVEN_PAD_COLS), lambda i: (0, 0)),
                  pl.BlockSpec((tm, LANES), pos),
                  pl.BlockSpec((tm, LANES), pos)],
        out_specs=[pl.BlockSpec((tm, s[1]), row) for s, _ in outs],
        compiler_params=_cparams(("parallel",)),
        name="even_in_proj",
    )(x2, g, sh, sc, w_pad, cos, sin)


def _compress_kernel(k_ref, v_ref, pos_ref, w1_ref, w2_ref, ko_ref, vo_ref):
    for j, (src, dst) in enumerate(((k_ref, ko_ref), (v_ref, vo_ref))):
        xr = src[0]
        n_rows = xr.shape[0]
        a0 = _mm((xr + pos_ref[j, 0]).astype(BF16), w1_ref[j, 0])
        a1 = _mm((xr + pos_ref[j, 1]).astype(BF16), w1_ref[j, 1])
        hid = a0 + pltpu.roll(a1, n_rows - 1, 0)
        hid = jax.nn.gelu(hid)
        dst[0] = _mm(hid.astype(BF16), w2_ref[j]).astype(BF16)


def compress_kv(kc, vc, pos_ext, w1_ext, w2_ext, batch, seq):
    rows = seq // CMP_STRIDE
    width = CMP_STRIDE * KV_DIM
    kr = kc.reshape(batch, rows, width)
    vr = vc.reshape(batch, rows, width)
    blk = pl.BlockSpec((1, rows, width), lambda b: (b, 0, 0))
    oblk = pl.BlockSpec((1, rows, KV_DIM), lambda b: (b, 0, 0))
    return pl.pallas_call(
        _compress_kernel,
        out_shape=[jax.ShapeDtypeStruct((batch, rows, KV_DIM), BF16)] * 2,
        grid=(batch,),
        in_specs=[blk, blk,
                  pl.BlockSpec(pos_ext.shape, lambda b: (0, 0, 0, 0)),
                  pl.BlockSpec(w1_ext.shape, lambda b: (0, 0, 0, 0)),
                  pl.BlockSpec(w2_ext.shape, lambda b: (0, 0, 0))],
        out_specs=[oblk, oblk],
        compiler_params=_cparams(("parallel",)),
        name="compress_kv",
    )(kr, vr, pos_ext, w1_ext, w2_ext)


def _safe_inv(l):
    return jnp.where(l > 0.0, 1.0 / jnp.where(l > 0.0, l, 1.0), 0.0)


def _nsa_kernel(qn_ref, qr_ref, kc_ref, vc_ref, ks_ref, vs_ref, kw_ref, vw_ref, gate_ref,
                selq_ref, mselt_ref, o_ref, *, seq, n_sel, sel_chunk, win_len):
    h = pl.program_id(1)
    qt = pl.program_id(2)
    t0 = qt * NSA_Q_TILE
    n_blk = seq // SEL_BLOCK
    n_cmp_pad = seq // CMP_STRIDE
    rows = GQA * NSA_Q_TILE
    tpos = t0 + lax.broadcasted_iota(jnp.int32, (1, NSA_Q_TILE, 1), 1)
    lane = lax.broadcasted_iota(jnp.int32, (1, LANES), 1)
    head_lanes = (lane // HEAD_DIM) == h

    def normalise(acc):
        return acc * _safe_inv(pltpu.roll(acc, HEAD_DIM, 1))

    qn = qn_ref[...]
    qr = qr_ref[...]
    qn4 = jnp.concatenate([_mm(qn, selq_ref[0, g]) for g in range(GQA)], axis=0).astype(BF16)
    qr4f = jnp.concatenate([_mm(qr, selq_ref[0, g]) for g in range(GQA)], axis=0)
    qr4 = qr4f.astype(BF16)

    kc = kc_ref[0]
    vc = vc_ref[0]
    cpos = lax.broadcasted_iota(jnp.int32, (1, 1, n_cmp_pad), 2) * CMP_STRIDE + (CMP_BLOCK - 1)
    cmask = cpos <= tpos
    s = jnp.where(cmask, _nt(qn4, kc).reshape(GQA, NSA_Q_TILE, n_cmp_pad), NEG_INF)
    e = jnp.where(cmask, jnp.exp2(s - jnp.max(s, axis=2, keepdims=True)), 0.0)
    p = e * _safe_inv(jnp.sum(e, axis=2, keepdims=True))
    imp = jnp.sum(p, axis=0)
    o_cmp = _mm(p.reshape(rows, n_cmp_pad).astype(BF16), vc)

    pslc = _nt(mselt_ref[...], imp, precision=HIGHEST)[:n_blk]
    tq = t0 + lax.broadcasted_iota(jnp.int32, (1, NSA_Q_TILE), 1)
    jblk = lax.broadcasted_iota(jnp.int32, (n_blk, 1), 0)
    cur = tq // SEL_BLOCK
    valid = jblk * SEL_BLOCK <= tq
    forced = (jblk == 0) | ((cur - jblk >= 0) & (cur - jblk < N_LOCAL))
    score = jnp.where(forced, BIG, jnp.where(valid, pslc, -BIG))
    rank = jnp.zeros((n_blk, NSA_Q_TILE), jnp.int32)
    for jp in range(n_blk):
        row = score[jp:jp + 1, :]
        beats = (row > score) | ((row == score) & (jblk > jp))
        rank = rank + beats.astype(jnp.int32)
    selb = jnp.where((rank < n_sel) & (score > -0.5 * BIG), 0.0, NEG_INF)
    selb_q = jnp.concatenate([selb, jnp.zeros((LANES - n_blk, NSA_Q_TILE), F32)], axis=0).T
    bias = jnp.where(h == 0, pltpu.roll(selb_q, HEAD_DIM, 1), selb_q)
    qs4 = (qr4f + jnp.concatenate([bias] * GQA, axis=0)).astype(BF16)

    def chunk_scores(c):
        start = pl.multiple_of(c * sel_chunk, sel_chunk)
        kblk = ks_ref[0, pl.ds(start, sel_chunk), :]
        vblk = vs_ref[0, pl.ds(start, sel_chunk), :]
        return start, _nt(qs4, kblk).reshape(GQA, NSA_Q_TILE, sel_chunk), vblk

    def online_update(carry, s, vblk):
        m, acc = carry
        m_new = jnp.maximum(m, jnp.max(s, axis=2, keepdims=True))
        alpha = jnp.exp2(m - m_new)
        p = jnp.exp2(s - m_new)
        pv = _mm(p.reshape(rows, sel_chunk).astype(BF16), vblk).reshape(GQA, NSA_Q_TILE, LANES)
        return m_new, alpha * acc + pv

    def sel_body(c, carry):
        _, s, vblk = chunk_scores(c)
        return online_update(carry, s, vblk)

    diag_chunk = (t0 + NSA_Q_TILE - 1) // sel_chunk
    init = (jnp.full((GQA, NSA_Q_TILE, 1), NEG_INF, F32), jnp.zeros((GQA, NSA_Q_TILE, LANES), F32))
    carry = lax.fori_loop(0, diag_chunk, sel_body, init)
    start, s, vblk = chunk_scores(diag_chunk)
    kpos = start + lax.broadcasted_iota(jnp.int32, (1, 1, sel_chunk), 2)
    _, acc = online_update(carry, jnp.where(kpos <= tpos, s, NEG_INF), vblk)
    o_slc = normalise(acc.reshape(rows, LANES))

    ws = pl.multiple_of(jnp.maximum(qt - WINDOW // NSA_Q_TILE, 0) * NSA_Q_TILE, NSA_Q_TILE)
    kwb = kw_ref[0, pl.ds(ws, win_len), :]
    vwb = vw_ref[0, pl.ds(ws, win_len), :]
    diff = tpos - (ws + lax.broadcasted_iota(jnp.int32, (1, 1, win_len), 2))
    wmask = (diff >= 0) & (diff < WINDOW)
    s = jnp.where(wmask, _nt(qr4, kwb).reshape(GQA, NSA_Q_TILE, win_len), NEG_INF)
    e = jnp.exp2(s - jnp.max(s, axis=2, keepdims=True))
    o_win = normalise(_mm(e.reshape(rows, win_len).astype(BF16), vwb))

    gate = gate_ref[...]
    og = []
    for g in range(GQA):
        sl = slice(g * NSA_Q_TILE, (g + 1) * NSA_Q_TILE)
        o = (gate[:, 3 * g:3 * g + 1] * o_cmp[sl] + gate[:, 3 * g + 1:3 * g + 2] * o_slc[sl]
             + gate[:, 3 * g + 2:3 * g + 3] * o_win[sl])
        og.append(jnp.where(head_lanes, o, pltpu.roll(o, HEAD_DIM, 1)))
    low = lane < HEAD_DIM
    o_ref[:, 0:LANES] = jnp.where(low, og[0], og[1])
    o_ref[:, LANES:2 * LANES] = jnp.where(low, og[2], og[3])


def nsa_attention(qn, qr, kcmp, vcmp, ks, vs, kw, vw, gate, selq, mselt, batch, seq):
    n = batch * seq
    nq = seq // NSA_Q_TILE
    sel_chunk = min(SEL_CHUNK, seq)
    win_len = min(WINDOW + NSA_Q_TILE, seq)
    n_sel = min(N_SEL, seq // SEL_BLOCK)
    qspec = pl.BlockSpec((NSA_Q_TILE, GQA * HEAD_DIM), lambda b, h, q: (b * nq + q, h))
    cspec = pl.BlockSpec((1, seq // CMP_STRIDE, KV_DIM), lambda b, h, q: (b, 0, 0))
    both = pl.BlockSpec((1, seq, KV_DIM), lambda b, h, q: (b, 0, 0))
    mine = pl.BlockSpec((1, seq, KV_DIM), lambda b, h, q: (b, 0, h))
    kern = functools.partial(_nsa_kernel, seq=seq, n_sel=n_sel, sel_chunk=sel_chunk, win_len=win_len)
    return pl.pallas_call(
        kern,
        out_shape=jax.ShapeDtypeStruct((n, NSA_DIM), F32),
        grid=(batch, N_KV_HEADS, nq),
        in_specs=[qspec, qspec, cspec, cspec, mine, mine, both, mine,
                  pl.BlockSpec((NSA_Q_TILE, LANES), lambda b, h, q: (b * nq + q, h)),
                  pl.BlockSpec((1, GQA, GQA * HEAD_DIM, LANES), lambda b, h, q: (h, 0, 0, 0)),
                  pl.BlockSpec(mselt.shape, lambda b, h, q: (0, 0))],
        out_specs=qspec,
        compiler_params=_cparams(("parallel", "parallel", "arbitrary")),
        name="nsa_attention",
    )(qn, qr, kcmp, vcmp, ks.reshape(batch, seq, 2 * KV_DIM), vs.reshape(batch, seq, 2 * KV_DIM),
      kw.reshape(batch, seq, KV_DIM), vw.reshape(batch, seq, 2 * KV_DIM), gate, selq, mselt)


def _route(h2, rwt_ref, rb_ref):
    rw = rwt_ref[...]
    rw_hi = rw.astype(BF16)
    rw_lo = (rw - rw_hi.astype(F32)).astype(BF16)
    h_hi = h2.astype(BF16)
    h_lo = (h2 - h_hi.astype(F32)).astype(BF16)
    logits = _nt(rw_hi, h_hi) + (_nt(rw_hi, h_lo) + _nt(rw_lo, h_hi))
    scores = jax.nn.sigmoid(logits)
    biased = scores + rb_ref[...]
    rows = [biased[e:e + 1, :] for e in range(N_EXPERTS)]
    srow = [scores[e:e + 1, :] for e in range(N_EXPERTS)]
    gscore = []
    for gi in range(N_EXPERT_GROUPS):
        r = rows[gi * EXPERTS_PER_GROUP:(gi + 1) * EXPERTS_PER_GROUP]
        best = None
        for a in range(EXPERTS_PER_GROUP):
            for b in range(a + 1, EXPERTS_PER_GROUP):
                pair = r[a] + r[b]
                best = pair if best is None else jnp.maximum(best, pair)
        gscore.append(best)
    top_val = gscore[0]
    top_grp = jnp.zeros_like(top_val, dtype=jnp.int32)
    for gi in range(1, N_EXPERT_GROUPS):
        upd = gscore[gi] > top_val
        top_grp = jnp.where(upd, gi, top_grp)
        top_val = jnp.where(upd, gscore[gi], top_val)
    masked = [jnp.where(top_grp == e // EXPERTS_PER_GROUP, rows[e], NEG_INF) for e in range(N_EXPERTS)]
    b1 = masked[0]
    i1 = jnp.zeros_like(top_grp)
    for e in range(1, N_EXPERTS):
        upd = masked[e] > b1
        i1 = jnp.where(upd, e, i1)
        b1 = jnp.where(upd, masked[e], b1)
    b2 = None
    i2 = None
    for e in range(N_EXPERTS):
        v = jnp.where(i1 == e, -jnp.inf, masked[e])
        if b2 is None:
            b2, i2 = v, jnp.zeros_like(top_grp)
        else:
            upd = v > b2
            i2 = jnp.where(upd, e, i2)
            b2 = jnp.where(upd, v, b2)
    s1 = jnp.zeros_like(top_val)
    s2 = jnp.zeros_like(top_val)
    for e in range(N_EXPERTS):
        s1 = s1 + jnp.where(i1 == e, srow[e], 0.0)
        s2 = s2 + jnp.where(i2 == e, srow[e], 0.0)
    tot = s1 + s2
    return i1, i2, s1 / tot, s2 / tot


def _tail(cat_bf16, x_ref, g1_ref, wout_ref, ng_ref, sh2_ref, sc2_ref, rwt_ref, rb_ref, ustrict_ref,
          x1_ref, hw_ref, ridx_ref, cnt_ref, carry_ref):
    y = _mm(cat_bf16, wout_ref[...])
    x1 = x_ref[...] + g1_ref[0] * y
    x1_ref[...] = x1
    h2 = _norm_mod(x1, ng_ref[...], sh2_ref[0], sc2_ref[0])
    tm, d = h2.shape
    hw_ref[:, :d] = h2

    i1, i2, w1, w2 = _route(h2, rwt_ref, rb_ref)
    lo = jnp.minimum(i1, i2) % EXPERTS_PER_GROUP
    hi = jnp.maximum(i1, i2) % EXPERTS_PER_GROUP
    pair = jnp.where(lo == 0, hi - 1, jnp.where(lo == 1, jnp.where(hi == 3, 3, 4), 5))
    cls = (i1 // EXPERTS_PER_GROUP) * PAIRS_PER_GROUP + pair
    w_lo = jnp.where(i1 < i2, w1, w2)
    w_hi = jnp.where(i1 < i2, w2, w1)
    meta_t = jnp.concatenate([w_lo, w_hi, jnp.zeros((LANES - 2, tm), F32)], axis=0)
    hw_ref[:, d:] = meta_t.T

    @pl.when(pl.program_id(0) == 0)
    def _():
        carry_ref[...] = jnp.zeros_like(carry_ref)

    hit = lax.broadcasted_iota(jnp.int32, (CLASS_ROWS, 1), 0) == cls
    cnt = jnp.where(hit, 1.0, 0.0)
    before = _mm(cnt.astype(BF16), ustrict_ref[...]) + carry_ref[:, 0:1]
    carry = carry_ref[...] + jnp.sum(cnt, axis=1, keepdims=True)
    carry_ref[...] = carry
    cnt_ref[...] = carry.astype(jnp.int32)
    ridx_ref[0:1, :] = cls
    ridx_ref[1:2, :] = jnp.sum(jnp.where(hit, before, 0.0), axis=0, keepdims=True).astype(jnp.int32)


def _even_out_kernel(o_ref, u_ref, uh_ref, bg_ref, cw_ref, *tail_refs, tiles_per_seq):
    first = (pl.program_id(0) % tiles_per_seq) == 0
    u = u_ref[...]
    halo = jnp.where(first, 0.0, uh_ref[...])
    ext = jnp.concatenate([halo, u], axis=0)
    u1 = pltpu.roll(ext, 1, 0)[CONV_HALO:]
    u2 = pltpu.roll(ext, 2, 0)[CONV_HALO:]
    cw = cw_ref[...]
    y_conv = bg_ref[...] * (cw[2:3] * u + cw[1:2] * u1 + cw[0:1] * u2)
    cat = jnp.concatenate([o_ref[...], y_conv], axis=1).astype(BF16)
    _tail(cat, *tail_refs)


def _tail_specs(tm, d, tpb):
    row = lambda i: (i, 0)
    per_b = lambda i: (i // tpb, 0, 0)
    const2 = lambda i: (0, 0)
    ins = [pl.BlockSpec((tm, d), row),
           pl.BlockSpec((1, 1, d), per_b),
           pl.BlockSpec((d, d), const2),
           pl.BlockSpec((1, d), const2),
           pl.BlockSpec((1, 1, d), per_b),
           pl.BlockSpec((1, 1, d), per_b),
           pl.BlockSpec((N_EXPERTS, d), const2),
           pl.BlockSpec((N_EXPERTS, 1), const2),
           pl.BlockSpec((tm, tm), const2)]
    outs = [pl.BlockSpec((tm, d), row), pl.BlockSpec((tm, d + LANES), row),
            pl.BlockSpec((2, tm), lambda i: (0, i)),
            pl.BlockSpec((CLASS_ROWS, LANES), const2)]
    scratch = [pltpu.VMEM((CLASS_ROWS, LANES), F32)]
    return ins, outs, scratch


def _tail_out_shapes(n, d):
    return [jax.ShapeDtypeStruct((n, d), F32), jax.ShapeDtypeStruct((n, d + LANES), F32),
            jax.ShapeDtypeStruct((2, n), jnp.int32), jax.ShapeDtypeStruct((CLASS_ROWS, LANES), jnp.int32)]


def _strict_upper(tm):
    return jnp.asarray(np.triu(np.ones((tm, tm), np.float32), 1), dtype=BF16)


def even_out_proj(o_nsa, u, bg, conv_w, x2, g1, w_out, ng, sh2, sc2, rwt, rb, seq):
    n, d = x2.shape
    tm = min(TOK_TILE, seq)
    tpb = seq // tm
    row = lambda i: (i, 0)
    halo = lambda i: (jnp.maximum(i * (tm // CONV_HALO) - 1, 0), 0)
    tin, tout, tscratch = _tail_specs(tm, d, tpb)
    return pl.pallas_call(
        functools.partial(_even_out_kernel, tiles_per_seq=tpb),
        out_shape=_tail_out_shapes(n, d),
        grid=(n // tm,),
        in_specs=[pl.BlockSpec((tm, NSA_DIM), row),
                  pl.BlockSpec((tm, CONV_DIM), row),
                  pl.BlockSpec((CONV_HALO, CONV_DIM), halo),
                  pl.BlockSpec((tm, CONV_DIM), row),
                  pl.BlockSpec(conv_w.shape, lambda i: (0, 0))] + tin,
        out_specs=tout,
        scratch_shapes=tscratch,
        compiler_params=_cparams(("arbitrary",)),
        name="even_out_proj",
    )(o_nsa, u, u, bg, conv_w, x2, g1, w_out, ng, sh2, sc2, rwt, rb, _strict_upper(tm))


def _dispatch_plan(ridx, counts, n, seq):
    cnt = counts[:N_CLASSES, 0]
    padded = (cnt + MOE_ROW_TILE - 1) // MOE_ROW_TILE * MOE_ROW_TILE
    ends = jnp.cumsum(padded)
    starts = ends - padded
    cids = jnp.arange(N_CLASSES, dtype=jnp.int32)[:, None]
    base = jnp.sum(jnp.where(cids == ridx[0][None, :], starts[:, None], 0), axis=0)
    dest = (base + ridx[1]).astype(jnp.int32)
    td = min(MOE_DMA_TILE, seq)
    dest3 = dest.reshape(n // td, 1, td)
    n_tiles = n // MOE_ROW_TILE + N_CLASSES
    tile_start = jnp.arange(n_tiles, dtype=jnp.int32) * MOE_ROW_TILE
    tile_class = jnp.minimum(jnp.sum(tile_start[:, None] >= ends[None, :], axis=1), N_CLASSES - 1)
    group_base = (tile_class // PAIRS_PER_GROUP) * EXPERTS_PER_GROUP
    pair = tile_class % PAIRS_PER_GROUP
    tile_lo = (group_base + jnp.asarray(PAIR_LO, jnp.int32)[pair]).astype(jnp.int32)
    tile_hi = (group_base + jnp.asarray(PAIR_HI, jnp.int32)[pair]).astype(jnp.int32)
    n_used = (ends[-1] // MOE_ROW_TILE).reshape(1).astype(jnp.int32)
    last_tile = jnp.where(cnt > 0, ends // MOE_ROW_TILE - 1, -1)
    tail = n_used[0] + jnp.arange(N_CLASSES, dtype=jnp.int32)
    zero_tiles = jnp.concatenate([last_tile, jnp.where(tail < n_tiles, tail, -1)]).astype(jnp.int32)
    return dest3, tile_lo, tile_hi, n_used, zero_tiles, n_tiles


def _dispatch_kernel(ztile_ref, dest_ref, hw_ref, xs_hbm, zbuf, zsem, sem):
    td = hw_ref.shape[0]

    @pl.when(pl.program_id(0) == 0)
    def _():
        zbuf[...] = jnp.zeros_like(zbuf)

        def zero_copy(k):
            start = pl.multiple_of(ztile_ref[k] * MOE_ROW_TILE, MOE_ROW_TILE)
            return pltpu.make_async_copy(zbuf, xs_hbm.at[pl.ds(start, MOE_ROW_TILE)], zsem)

        for k in range(2 * N_CLASSES):
            @pl.when(ztile_ref[k] >= 0)
            def _():
                zero_copy(k).start()
        for k in range(2 * N_CLASSES):
            @pl.when(ztile_ref[k] >= 0)
            def _():
                zero_copy(k).wait()

    for r in range(td):
        pltpu.make_async_copy(hw_ref.at[pl.ds(r, 1)],
                              xs_hbm.at[pl.ds(dest_ref[0, 0, r], 1)], sem).start()
    pltpu.make_async_copy(hw_ref, xs_hbm.at[pl.ds(0, td)], sem).wait()


def moe_dispatch(hw, dest3, zero_tiles, n_tiles):
    n, cols = hw.shape
    td = dest3.shape[2]
    rows = n_tiles * MOE_ROW_TILE
    return pl.pallas_call(
        _dispatch_kernel,
        out_shape=jax.ShapeDtypeStruct((rows, cols), F32),
        grid_spec=pltpu.PrefetchScalarGridSpec(
            num_scalar_prefetch=1,
            grid=(n // td,),
            in_specs=[pl.BlockSpec((1, 1, td), lambda i, z: (i, 0, 0), memory_space=pltpu.SMEM),
                      pl.BlockSpec((td, cols), lambda i, z: (i, 0))],
            out_specs=pl.BlockSpec(memory_space=pl.ANY),
            scratch_shapes=[pltpu.VMEM((MOE_ROW_TILE, cols), F32), pltpu.SemaphoreType.DMA(()),
                            pltpu.SemaphoreType.DMA(())]),
        compiler_params=_cparams(("arbitrary",)),
        name="moe_dispatch",
    )(zero_tiles, dest3, hw)


def _expert_kernel(lo_ref, hi_ref, nused_ref, xs_ref, wg_lo, wu_lo, wd_lo, wg_hi, wu_hi, wd_hi, ys_ref,
                   *wb):
    t = pl.program_id(0)
    prev = jnp.maximum(t - 1, 0)

    for ids, srcs, dsts in ((lo_ref, (wg_lo, wu_lo, wd_lo), wb[:3]), (hi_ref, (wg_hi, wu_hi, wd_hi), wb[3:])):
        @pl.when((t == 0) | (ids[t] != ids[prev]))
        def _():
            for src, dst in zip(srcs, dsts):
                dst[...] = src[0, 0].astype(BF16)

    @pl.when(t < nused_ref[0])
    def _():
        d = xs_ref.shape[1] - LANES
        x = xs_ref[:, :d].astype(BF16)
        meta = xs_ref[:, d:]
        y = None
        for k in range(2):
            a = _mm(x, wb[3 * k][...])
            b = _mm(x, wb[3 * k + 1][...])
            he = (a * jax.nn.sigmoid(a)) * b
            yk = meta[:, k:k + 1] * _mm(he.astype(BF16), wb[3 * k + 2][...])
            y = yk if y is None else y + yk
        ys_ref[...] = y

    @pl.when(t >= nused_ref[0])
    def _():
        ys_ref[...] = jnp.zeros_like(ys_ref)


def moe_experts(xs, tile_lo, tile_hi, n_used, w_gate, w_up, w_down, layer, d):
    rows, cols = xs.shape
    n_tiles = rows // MOE_ROW_TILE
    lo_spec = lambda shape: pl.BlockSpec((1, 1) + shape, lambda t, lo, hi, nu: (layer, lo[t], 0, 0))
    hi_spec = lambda shape: pl.BlockSpec((1, 1) + shape, lambda t, lo, hi, nu: (layer, hi[t], 0, 0))
    shapes = ((d, D_EXPERT), (d, D_EXPERT), (D_EXPERT, d))
    return pl.pallas_call(
        _expert_kernel,
        out_shape=jax.ShapeDtypeStruct((rows, d), F32),
        grid_spec=pltpu.PrefetchScalarGridSpec(
            num_scalar_prefetch=3,
            grid=(n_tiles,),
            in_specs=[pl.BlockSpec((MOE_ROW_TILE, cols),
                                   lambda t, lo, hi, nu: (jnp.minimum(t, nu[0] - 1), 0))]
                     + [lo_spec(s) for s in shapes] + [hi_spec(s) for s in shapes],
            out_specs=pl.BlockSpec((MOE_ROW_TILE, d), lambda t, lo, hi, nu: (t, 0)),
            scratch_shapes=[pltpu.VMEM(s, BF16) for s in shapes + shapes]),
        compiler_params=_cparams(("arbitrary",)),
        name="moe_experts",
    )(tile_lo, tile_hi, n_used, xs, w_gate, w_up, w_down, w_gate, w_up, w_down)


def _combine_kernel(dest_ref, ys_hbm, x_ref, g2_ref, fn_ref, o_ref, buf, sem, *, final_norm):
    tc = x_ref.shape[0]

    for r in range(tc):
        pltpu.make_async_copy(ys_hbm.at[pl.ds(dest_ref[0, 0, r], 1)], buf.at[pl.ds(r, 1)], sem).start()
    pltpu.make_async_copy(ys_hbm.at[pl.ds(0, tc)], buf, sem).wait()
    x = x_ref[...] + g2_ref[0] * buf[...]
    if final_norm:
        ms = jnp.mean(x * x, axis=-1, keepdims=True)
        x = x * lax.rsqrt(ms + NORM_EPS) * fn_ref[...]
    o_ref[...] = x


def moe_combine(ys, dest3, x1, g2, fnorm, seq, final_norm):
    n, d = x1.shape
    tc = dest3.shape[2]
    tpb = seq // tc
    return pl.pallas_call(
        functools.partial(_combine_kernel, final_norm=final_norm),
        out_shape=jax.ShapeDtypeStruct((n, d), F32),
        grid=(n // tc,),
        in_specs=[pl.BlockSpec((1, 1, tc), lambda i: (i, 0, 0), memory_space=pltpu.SMEM),
                  pl.BlockSpec(memory_space=pl.ANY),
                  pl.BlockSpec((tc, d), lambda i: (i, 0)),
                  pl.BlockSpec((1, 1, d), lambda i: (i // tpb, 0, 0)),
                  pl.BlockSpec((1, d), lambda i: (0, 0))],
        out_specs=pl.BlockSpec((tc, d), lambda i: (i, 0)),
        scratch_shapes=[pltpu.VMEM((tc, d), F32), pltpu.SemaphoreType.DMA(())],
        compiler_params=_cparams(("arbitrary",)),
        name="moe_combine",
    )(dest3, ys, x1, g2, fnorm)


def moe_sparse(hw, ridx, counts, w_gate, w_up, w_down, layer, x1, g2, fnorm, seq, final_norm):
    n, d = x1.shape
    dest3, tile_lo, tile_hi, n_used, zero_tiles, n_tiles = _dispatch_plan(ridx, counts, n, seq)
    xs = moe_dispatch(hw, dest3, zero_tiles, n_tiles)
    ys = moe_experts(xs, tile_lo, tile_hi, n_used, w_gate, w_up, w_down, layer, d)
    return moe_combine(ys, dest3, x1, g2, fnorm, seq, final_norm)


def _odd_in_kernel(x_ref, g_ref, sh_ref, sc_ref, w_ref, mu_ref, w0_ref, w2_ref, a0_ref, a2_ref,
                   g2_ref, kk_ref, ka_ref, ones_ref, pw_ref, ps_ref,
                   r_ref, lw_ref, km_ref, v_ref, kn_ref, kb_ref, gg_ref, op_ref,
                   rw_carry, u_carry, *, tiles_per_seq, tm):
    i = pl.program_id(0)
    first = (i % tiles_per_seq) == 0
    h = _norm_mod(x_ref[...], g_ref[...], sh_ref[0], sc_ref[0])
    proj = _mm(h.astype(BF16), w_ref[...])

    rw = proj[:, :ODD_RW_COLS]
    row0 = jnp.where(first, 0.0, rw_carry[0:1, :])
    ridx = lax.broadcasted_iota(jnp.int32, (tm, 1), 0)
    prev = jnp.where(ridx == 0, row0, pltpu.roll(rw, 1, 0))
    rw_carry[0:1, :] = rw[tm - 1:tm, :]
    rw = rw + (prev - rw) * mu_ref[...]

    r = rw[:, 0:512]
    k = rw[:, 512:1024]
    v = rw[:, 1024:1536]
    wl = rw[:, 1536:1664]
    al = rw[:, 1664:1792]
    gl = rw[:, 1792:1920]
    z = -(w0_ref[...] + _mm(jnp.tanh(wl).astype(BF16), w2_ref[...]))
    softplus = jnp.maximum(z, 0.0) + jnp.log1p(jnp.exp(-jnp.abs(z)))
    w_log = -softplus - 0.5
    a = jax.nn.sigmoid(a0_ref[...] + _mm(al.astype(BF16), a2_ref[...]))
    gg_ref[...] = _mm(jax.nn.sigmoid(gl).astype(BF16), g2_ref[...])
    kk0 = k * kk_ref[...]
    ss = _split_sum(kk0 * kk0, ones_ref[...])
    kk = kk0 / jnp.maximum(jnp.sqrt(ss), 1e-12)
    r_ref[...] = r
    lw_ref[...] = -jnp.exp(w_log)
    km_ref[...] = k * (1.0 + (a - 1.0) * ka_ref[...])
    v_ref[...] = v
    kn_ref[...] = kk
    kb_ref[...] = kk * a

    u = proj[:, ODD_RW_COLS:]
    halo = jnp.where(first, 0.0, u_carry[...])
    u_carry[...] = u[tm - POOL_HALO:, :]
    ext = jnp.concatenate([halo, u], axis=0)
    tseq = (i % tiles_per_seq) * tm + ridx
    for gi, win in enumerate(POOL_WINDOWS):
        xg = ext[:, gi * POOL_GROUP:(gi + 1) * POOL_GROUP]
        s = xg
        step = 1
        while step < win:
            s = s + pltpu.roll(s, step, 0)
            step *= 2
        cnt = jnp.minimum(tseq + 1, win).astype(F32)
        pooled = s[POOL_HALO:] / cnt - xg[POOL_HALO:]
        mixed = _mm(pooled.astype(BF16), pw_ref[gi])
        op_ref[:, gi * POOL_GROUP:(gi + 1) * POOL_GROUP] = (
            mixed * ps_ref[:, gi * POOL_GROUP:(gi + 1) * POOL_GROUP])


def odd_in_proj(x2, g, sh, sc, w_pad, mu_pad, w0, w2p, a0, a2p, g2, k_k, k_a, ones_bd, pool_w,
                pool_scale, seq):
    n, d = x2.shape
    tm = min(TOK_TILE, seq)
    tpb = seq // tm
    row = lambda i: (i, 0)
    per_b = lambda i: (i // tpb, 0, 0)
    c2 = lambda i: (0, 0)
    full2 = lambda a: pl.BlockSpec(a.shape, c2)
    return pl.pallas_call(
        functools.partial(_odd_in_kernel, tiles_per_seq=tpb, tm=tm),
        out_shape=[jax.ShapeDtypeStruct((n, RWKV_DIM), F32)] * 8,
        grid=(n // tm,),
        in_specs=[pl.BlockSpec((tm, d), row), pl.BlockSpec((1, d), c2),
                  pl.BlockSpec((1, 1, d), per_b), pl.BlockSpec((1, 1, d), per_b),
                  full2(w_pad), full2(mu_pad), full2(w0), full2(w2p), full2(a0), full2(a2p),
                  full2(g2), full2(k_k), full2(k_a), full2(ones_bd),
                  pl.BlockSpec(pool_w.shape, lambda i: (0, 0, 0)), full2(pool_scale)],
        out_specs=[pl.BlockSpec((tm, RWKV_DIM), row)] * 8,
        scratch_shapes=[pltpu.VMEM((SUBLANES, ODD_RW_COLS), F32),
                        pltpu.VMEM((POOL_HALO, RWKV_DIM), F32)],
        compiler_params=_cparams(("arbitrary",)),
        name="odd_in_proj",
    )(x2, g, sh, sc, w_pad, mu_pad, w0, w2p, a0, a2p, g2, k_k, k_a, ones_bd, pool_w, pool_scale)


def _bmm(a, b):
    return lax.dot_general(a, b, (((2,), (1,)), ((0,), (0,))), preferred_element_type=F32)


def _bnt(a, b):
    return lax.dot_general(a, b, (((2,), (2,)), ((0,), (0,))), preferred_element_type=F32)


def _btn(a, b):
    return lax.dot_general(a, b, (((1,), (1,)), ((0,), (0,))), preferred_element_type=F32)


def _scan_prep_kernel(r_ref, lw_ref, km_ref, v_ref, kn_ref, kb_ref, qe_ref, y0_ref, mt_ref, ct_ref,
                      *, chunk, cb):
    L = chunk
    rows = cb * L
    n_pairs = N_RWKV_HEADS // 2
    two = 2 * L
    rowt = lax.broadcasted_iota(jnp.int32, (rows, 1), 0) % L
    lane = lax.broadcasted_iota(jnp.int32, (1, 1, LANES), 2)
    low = lane < HEAD_DIM
    ri = lax.broadcasted_iota(jnp.int32, (two, two), 0)
    ci = lax.broadcasted_iota(jnp.int32, (two, two), 1)
    same_blk = (ri // L) == (ci // L)
    strict = same_blk & ((ci % L) < (ri % L))
    incl = same_blk & ((ci % L) <= (ri % L))
    li = lax.broadcasted_iota(jnp.int32, (LANES, LANES), 0)
    lj = lax.broadcasted_iota(jnp.int32, (LANES, LANES), 1)
    same_head = (li // HEAD_DIM) == (lj // HEAD_DIM)
    eye = li == lj

    lw = lw_ref[...]
    cum = lw
    step = 1
    while step < L:
        cum = cum + jnp.where(rowt >= step, pltpu.roll(cum, step, 0), 0.0)
        step *= 2

    def to3(x):
        x3 = x.reshape(cb, L, RWKV_DIM)
        return jnp.concatenate([x3[:, :, p * LANES:(p + 1) * LANES] for p in range(n_pairs)], axis=0)

    def stack2(x):
        return jnp.concatenate([jnp.where(low, x, 0.0), jnp.where(low, 0.0, x)], axis=1)

    def fold(x):
        return x[:, :L, :] + x[:, L:, :]

    cum3 = to3(cum)
    lw3 = to3(lw)
    cum_l = cum3[:, L - 1:L, :]
    g_inv = jnp.exp(-cum3)
    g_tail = jnp.exp(cum_l - cum3)
    kb = to3(kb_ref[...])
    km = to3(km_ref[...])
    v = to3(v_ref[...])
    at_s = stack2(-to3(kn_ref[...]) * jnp.exp(cum3 - lw3))
    rt_s = stack2(to3(r_ref[...]) * jnp.exp(cum3))
    v_s = stack2(v).astype(BF16)
    lhs = jnp.concatenate([at_s, rt_s], axis=1).astype(BF16)
    rhs = jnp.concatenate([stack2(kb * g_inv), stack2(km * g_inv)], axis=1).astype(BF16)
    prod = _bnt(lhs, rhs)
    nmat = jnp.where(strict, prod[:, :two, :two], 0.0)
    a_ak = jnp.where(strict, prod[:, :two, two:], 0.0).astype(BF16)
    a_rb = jnp.where(incl, prod[:, two:, :two], 0.0).astype(BF16)
    a_rk = jnp.where(incl, prod[:, two:, two:], 0.0).astype(BF16)

    x = jnp.concatenate([at_s, _bmm(a_ak, v_s)], axis=2)
    npow = nmat
    step = 1
    while step < L:
        nb = npow.astype(BF16)
        x = x + _bmm(nb, x.astype(BF16))
        step *= 2
        if step < L:
            npow = _bmm(nb, nb)
    qy = _bmm(a_rb, x.astype(BF16))
    qe = fold(rt_s + qy[:, :, :LANES])
    y0 = fold(qy[:, :, LANES:] + _bmm(a_rk, v_s))
    wu = fold(x).astype(BF16)
    bwu = _btn((kb * g_tail).astype(BF16), wu)
    kv = _btn((km * g_tail).astype(BF16), v.astype(BF16))
    g_l = jnp.broadcast_to(jnp.exp(cum_l), (n_pairs * cb, LANES, LANES))
    mt = jnp.where(eye, g_l, 0.0) + jnp.where(same_head, bwu[:, :, :LANES], 0.0)
    ct = jnp.where(same_head, bwu[:, :, LANES:] + kv, 0.0)
    for p in range(n_pairs):
        sl = slice(p * LANES, (p + 1) * LANES)
        qe_ref[:, sl] = qe[p * cb:(p + 1) * cb].reshape(rows, LANES)
        y0_ref[:, sl] = y0[p * cb:(p + 1) * cb].reshape(rows, LANES)
        mt_ref[:, p] = mt[p * cb:(p + 1) * cb].astype(BF16)
        ct_ref[:, p] = ct[p * cb:(p + 1) * cb]


def _scan_state_kernel(qe_ref, y0_ref, mt_ref, ct_ref, y_ref, st_ref, *, batch):
    @pl.when(pl.program_id(0) == 0)
    def _():
        st_ref[...] = jnp.zeros_like(st_ref)

    n_pairs = N_RWKV_HEADS // 2
    qe = qe_ref[...]
    qe3 = jnp.concatenate([qe[:, :, p * LANES:(p + 1) * LANES] for p in range(n_pairs)], axis=0)
    st = st_ref[...].astype(BF16)
    y = _bmm(qe3.astype(BF16), st)
    for p in range(n_pairs):
        sl = slice(p * LANES, (p + 1) * LANES)
        y_ref[:, :, sl] = y[p * batch:(p + 1) * batch] + y0_ref[:, :, sl]
    mt = jnp.concatenate([mt_ref[:, 0, p] for p in range(n_pairs)], axis=0)
    ct = jnp.concatenate([ct_ref[:, 0, p] for p in range(n_pairs)], axis=0)
    st_ref[...] = _bmm(mt, st) + ct


def rwkv_scan(r, lw, km, v, kn, kb, batch, seq):
    n = batch * seq
    chunk = min(SCAN_CHUNK, seq)
    nc = seq // chunk
    cb = min(SCAN_CHUNKS_PER_STEP, nc)
    n_pairs = N_RWKV_HEADS // 2
    blk = pl.BlockSpec((cb * chunk, RWKV_DIM), lambda i: (i, 0))
    mblk = pl.BlockSpec((cb, n_pairs, LANES, LANES), lambda i: (i, 0, 0, 0))
    qe, y0, mt, ct = pl.pallas_call(
        functools.partial(_scan_prep_kernel, chunk=chunk, cb=cb),
        out_shape=[jax.ShapeDtypeStruct((n, RWKV_DIM), F32), jax.ShapeDtypeStruct((n, RWKV_DIM), F32),
                   jax.ShapeDtypeStruct((n // chunk, n_pairs, LANES, LANES), BF16),
                   jax.ShapeDtypeStruct((n // chunk, n_pairs, LANES, LANES), F32)],
        grid=(n // (cb * chunk),),
        in_specs=[blk] * 6,
        out_specs=[blk, blk, mblk, mblk],
        compiler_params=_cparams(("parallel",)),
        name="rwkv_scan_prep",
    )(r, lw, km, v, kn, kb)
    sblk = pl.BlockSpec((batch, chunk, RWKV_DIM), lambda c: (0, c, 0))
    smblk = pl.BlockSpec((batch, 1, n_pairs, LANES, LANES), lambda c: (0, c, 0, 0, 0))
    y = pl.pallas_call(
        functools.partial(_scan_state_kernel, batch=batch),
        out_shape=jax.ShapeDtypeStruct((batch, seq, RWKV_DIM), F32),
        grid=(nc,),
        in_specs=[sblk, sblk, smblk, smblk],
        out_specs=sblk,
        scratch_shapes=[pltpu.VMEM((n_pairs * batch, LANES, LANES), F32)],
        compiler_params=_cparams(("arbitrary",)),
        name="rwkv_scan_state",
    )(qe.reshape(batch, seq, RWKV_DIM), y0.reshape(batch, seq, RWKV_DIM),
      mt.reshape(batch, nc, n_pairs, LANES, LANES), ct.reshape(batch, nc, n_pairs, LANES, LANES))
    return y.reshape(n, RWKV_DIM)


def _odd_out_kernel(y_ref, r_ref, km_ref, v_ref, gg_ref, op_ref, rk_ref, lnw_ref, lnb_ref, ones_ref,
                    *tail_refs):
    ones = ones_ref[...]
    inv = 1.0 / HEAD_DIM
    y = y_ref[...]
    mean = _split_sum(y, ones) * inv
    yc = y - mean
    var = _split_sum(yc * yc, ones) * inv
    yn = yc * lax.rsqrt(var + LNX_EPS) * lnw_ref[...] + lnb_ref[...]
    bonus = _split_sum(r_ref[...] * km_ref[...] * rk_ref[...], ones) * v_ref[...]
    o_rwkv = (yn + bonus) * gg_ref[...]
    cat = jnp.concatenate([o_rwkv, op_ref[...]], axis=1).astype(BF16)
    _tail(cat, *tail_refs)


def odd_out_proj(y, r, km, v, gg, opool, r_k, lnx_w, lnx_b, ones_bd, x2, g1, w_out, ng, sh2, sc2,
                 rwt, rb, seq):
    n, d = x2.shape
    tm = min(TOK_TILE, seq)
    tpb = seq // tm
    row = lambda i: (i, 0)
    c2 = lambda i: (0, 0)
    act = pl.BlockSpec((tm, RWKV_DIM), row)
    vec = pl.BlockSpec((1, RWKV_DIM), c2)
    tin, tout, tscratch = _tail_specs(tm, d, tpb)
    return pl.pallas_call(
        _odd_out_kernel,
        out_shape=_tail_out_shapes(n, d),
        grid=(n // tm,),
        in_specs=[act] * 6 + [vec, vec, vec, pl.BlockSpec(ones_bd.shape, c2)] + tin,
        out_specs=tout,
        scratch_shapes=tscratch,
        compiler_params=_cparams(("arbitrary",)),
        name="odd_out_proj",
    )(y, r, km, v, gg, opool, r_k, lnx_w, lnx_b, ones_bd, x2, g1, w_out, ng, sh2, sc2, rwt, rb,
      _strict_upper(tm))


def _rope_tables(seq):
    half = HEAD_DIM // 2
    inv = ROPE_THETA ** (-jnp.arange(half, dtype=F32) / half)
    ang = jnp.arange(seq, dtype=F32)[:, None] * inv[None, :]
    return jnp.tile(jnp.cos(ang), (1, LANES // half)), jnp.tile(jnp.sin(ang), (1, LANES // half))


def _even_w_pad(w_in):
    d = w_in.shape[0]
    q_kv = w_in[:, :NSA_DIM + 6 * KV_DIM]
    gl = w_in[:, NSA_DIM + 6 * KV_DIM:NSA_DIM + 6 * KV_DIM + 24]
    rest = w_in[:, NSA_DIM + 6 * KV_DIM + 24:]
    z = jnp.zeros((d, LANES - 12), w_in.dtype)
    return jnp.concatenate([q_kv, gl[:, :12], z, gl[:, 12:], z, rest], axis=1).astype(BF16)


def _compress_params(cmp_pos, cmp_w1, cmp_w2):
    eye = jnp.eye(N_KV_HEADS, dtype=F32)
    w1r = cmp_w1.reshape(2, 2, CMP_STRIDE, HEAD_DIM, CMP_HIDDEN)
    w1_ext = jnp.einsum('kpmdn,gh->kpmgdhn', w1r, eye).reshape(
        2, 2, CMP_STRIDE * KV_DIM, N_KV_HEADS * CMP_HIDDEN).astype(BF16)
    w2_ext = jnp.einsum('knd,gh->kgnhd', cmp_w2, eye).reshape(
        2, N_KV_HEADS * CMP_HIDDEN, KV_DIM).astype(BF16)
    pos = cmp_pos.reshape(2, 2, CMP_STRIDE, 1, HEAD_DIM)
    pos_ext = jnp.broadcast_to(pos, (2, 2, CMP_STRIDE, N_KV_HEADS, HEAD_DIM)).reshape(
        2, 2, 1, CMP_STRIDE * KV_DIM)
    return pos_ext, w1_ext, w2_ext


def _nsa_tables(seq):
    n_blk = seq // SEL_BLOCK
    n_cmp = (seq - CMP_BLOCK) // CMP_STRIDE + 1
    n_cmp_pad = seq // CMP_STRIDE
    r = SEL_BLOCK // CMP_STRIDE
    c = CMP_BLOCK // CMP_STRIDE
    msel = np.zeros((n_cmp_pad, LANES), np.float32)
    for j in range(n_blk):
        for m in range(r):
            for n in range(c):
                idx = r * j + m + n
                if idx < n_cmp:
                    msel[idx, j] += 1.0
    selq = np.zeros((N_KV_HEADS, GQA, GQA * HEAD_DIM, LANES), np.float32)
    for h in range(N_KV_HEADS):
        for g in range(GQA):
            for dd in range(HEAD_DIM):
                selq[h, g, g * HEAD_DIM + dd, h * HEAD_DIM + dd] = 1.0
    return jnp.asarray(msel.T), jnp.asarray(selq, dtype=BF16)


def _odd_params(w_in, mu, w2, a2):
    d = w_in.shape[0]
    z64 = jnp.zeros((d, 64), w_in.dtype)
    w_pad = jnp.concatenate([w_in[:, :1536], w_in[:, 1536:1600], z64, w_in[:, 1600:1664], z64,
                             w_in[:, 1664:]], axis=1).astype(BF16)
    m64 = jnp.zeros((64,), mu.dtype)
    mu_pad = jnp.concatenate([mu[:1536], mu[1536:1600], m64, mu[1600:1664], m64, mu[1664:]])[None, :]
    zr = jnp.zeros((64, RWKV_DIM), w2.dtype)
    w2p = jnp.concatenate([w2, zr], axis=0).astype(BF16)
    a2p = jnp.concatenate([a2, zr], axis=0).astype(BF16)
    return w_pad, mu_pad, w2p, a2p


def _head_ones():
    idx = np.arange(RWKV_DIM) // HEAD_DIM
    return jnp.asarray((idx[:, None] == idx[None, :]).astype(np.float32), dtype=BF16)


def kernel(x, c, ada_w, ada_b, norm_mix, norm_ffn, even_w_in, even_cmp_pos, even_cmp_w1, even_cmp_w2,
           even_conv_w, even_w_out, odd_w_in, odd_mu, odd_w0, odd_w2, odd_a0, odd_a2, odd_g2, odd_k_k,
           odd_k_a, odd_r_k, odd_lnx_w, odd_lnx_b, odd_pool_w, odd_pool_scale, odd_w_out,
           router_w, router_b, moe_w_gate, moe_w_up, moe_w_down, final_norm):
    batch, seq, d = x.shape
    n = batch * seq
    depth = ada_w.shape[0]
    x2 = x.reshape(n, d)
    mod = ada_modulation(c, ada_w, ada_b)
    rwt = router_w.T
    rb = router_b.reshape(N_EXPERTS, 1)
    fnorm = final_norm.reshape(1, d)
    cos, sin = _rope_tables(seq)
    mselt, selq = _nsa_tables(seq)
    ones_bd = _head_ones()

    for layer in range(depth):
        m = mod[layer].reshape(batch, 6, 1, d)
        sh1, sc1, g1, sh2, sc2, g2 = (m[:, k] for k in range(6))
        ng_mix = norm_mix[layer].reshape(1, d)
        ng_ffn = norm_ffn[layer].reshape(1, d)
        i = layer // 2
        if layer % 2 == 0:
            (qn, qr, kc, vc, ks, vs, kw, vw, gate, u, bg) = even_in_proj(
                x2, ng_mix, sh1, sc1, _even_w_pad(even_w_in[i]), cos, sin, seq)
            pos_ext, w1_ext, w2_ext = _compress_params(even_cmp_pos[i], even_cmp_w1[i], even_cmp_w2[i])
            kcmp, vcmp = compress_kv(kc, vc, pos_ext, w1_ext, w2_ext, batch, seq)
            o_nsa = nsa_attention(qn, qr, kcmp, vcmp, ks, vs, kw, vw, gate, selq, mselt, batch, seq)
            x1, hw, ridx, counts = even_out_proj(o_nsa, u, bg, even_conv_w[i], x2, g1,
                                                 even_w_out[i].astype(BF16), ng_ffn, sh2, sc2, rwt, rb, seq)
        else:
            w_pad, mu_pad, w2p, a2p = _odd_params(odd_w_in[i], odd_mu[i], odd_w2[i], odd_a2[i])
            vec = lambda a: a.reshape(1, RWKV_DIM)
            (r, lw, km, v, kn, kb, gg, opool) = odd_in_proj(
                x2, ng_mix, sh1, sc1, w_pad, mu_pad, vec(odd_w0[i]), w2p, vec(odd_a0[i]), a2p,
                odd_g2[i].astype(BF16), vec(odd_k_k[i]), vec(odd_k_a[i]), ones_bd,
                odd_pool_w[i].astype(BF16), vec(odd_pool_scale[i]), seq)
            y = rwkv_scan(r, lw, km, v, kn, kb, batch, seq)
            x1, hw, ridx, counts = odd_out_proj(
                y, r, km, v, gg, opool, vec(odd_r_k[i]), vec(odd_lnx_w[i]), vec(odd_lnx_b[i]), ones_bd,
                x2, g1, odd_w_out[i].astype(BF16), ng_ffn, sh2, sc2, rwt, rb, seq)
        x2 = moe_sparse(hw, ridx, counts, moe_w_gate, moe_w_up, moe_w_down, layer,
                        x1, g2, fnorm, seq, final_norm=(layer == depth - 1))
    return x2.reshape(batch, seq, d)
```

```python
import functools

import jax
import jax.numpy as jnp
import numpy as np
from jax import lax
from jax.experimental import pallas as pl
from jax.experimental.pallas import tpu as pltpu

F32 = jnp.float32
BF16 = jnp.bfloat16
HIGHEST = lax.Precision.HIGHEST

D_MODEL = 1024
DEPTH = 2
HEAD_DIM = 64
ROPE_THETA = 10000.0
NORM_EPS = 1e-6
NEG_INF = -1e30
BIG = 1e9
NSA_DIM = 512
N_KV_HEADS = 2
GQA = 4
KV_DIM = 128
CMP_BLOCK = 32
CMP_STRIDE = 16
CMP_HIDDEN = 256
SEL_BLOCK = 64
N_SEL = 8
N_LOCAL = 2
WINDOW = 512
NSA_Q_TILE = 256
ATTN_SCALE = HEAD_DIM ** -0.5
LOG2_E = 1.4426950408889634
CONV_DIM = 512
RWKV_DIM = 512
N_RWKV_HEADS = 8
LNX_EPS = 64e-5
POOL_WINDOWS = (2, 4, 8, 16)
POOL_GROUP = 128
N_EXPERTS = 16
N_EXPERT_GROUPS = 4
EXPERTS_PER_GROUP = 4
D_EXPERT = 512
PAIRS_PER_GROUP = 6
PAIR_LO = (0, 0, 0, 1, 1, 2)
PAIR_HI = (1, 2, 3, 3, 2, 3)
N_CLASSES = N_EXPERT_GROUPS * PAIRS_PER_GROUP
CLASS_ROWS = 32

LANES = 128
SUBLANES = 8
VMEM_LIMIT = 56 * 1024 * 1024

TOK_TILE = 512
MOE_ROW_TILE = 256
MOE_DMA_TILE = 2048
SEL_CHUNK = 512
SCAN_CHUNK = 64
SCAN_CHUNKS_PER_STEP = 4
CONV_HALO = 8
POOL_HALO = 16

EVEN_PAD_COLS = 3072
ODD_PAD_COLS = 2432
ODD_RW_COLS = 1920


def _cparams(sem):
    return pltpu.CompilerParams(dimension_semantics=sem, vmem_limit_bytes=VMEM_LIMIT)


def _nt(a, b, precision=None):
    return lax.dot_general(a, b, (((1,), (1,)), ((), ())), preferred_element_type=F32,
                           precision=precision)


def _mm(a, b, precision=None):
    return jnp.dot(a, b, preferred_element_type=F32, precision=precision)


def _norm_mod(x, g, sh, sc):
    ms = jnp.mean(x * x, axis=-1, keepdims=True)
    return (x * lax.rsqrt(ms + NORM_EPS) * g) * (1.0 + sc) + sh


def _split_sum(x, ones_bf16):
    hi = x.astype(BF16)
    lo = (x - hi.astype(F32)).astype(BF16)
    return _mm(hi, ones_bf16) + _mm(lo, ones_bf16)


def _ada_kernel(c_ref, w_ref, b_ref, o_ref):
    c = c_ref[...]
    cond = c * jax.nn.sigmoid(c)
    o_ref[0] = _mm(cond, w_ref[0], precision=HIGHEST) + b_ref[0]


def ada_modulation(c, ada_w, ada_b):
    depth, d, cols = ada_w.shape
    b = c.shape[0]
    tn = 1536
    return pl.pallas_call(
        _ada_kernel,
        out_shape=jax.ShapeDtypeStruct((depth, b, cols), F32),
        grid=(depth, cols // tn),
        in_specs=[pl.BlockSpec((b, d), lambda l, j: (0, 0)),
                  pl.BlockSpec((1, d, tn), lambda l, j: (l, 0, j)),
                  pl.BlockSpec((1, 1, tn), lambda l, j: (l, 0, j))],
        out_specs=pl.BlockSpec((1, b, tn), lambda l, j: (l, 0, j)),
        compiler_params=_cparams(("parallel", "parallel")),
        name="ada_modulation",
    )(c, ada_w, ada_b.reshape(depth, 1, cols))


def _rope128(t, cos, sin, lane):
    rot = jnp.where((lane % HEAD_DIM) < HEAD_DIM // 2,
                    -pltpu.roll(t, LANES - HEAD_DIM // 2, 1), pltpu.roll(t, HEAD_DIM // 2, 1))
    return t * cos + rot * sin


def _even_in_kernel(x_ref, g_ref, sh_ref, sc_ref, w_ref, cos_ref, sin_ref,
                    qn_ref, qr_ref, kc_ref, vc_ref, ks_ref, vs_ref, kw_ref, vw_ref,
                    gate_ref, u_ref, bg_ref, *, tiles_per_seq):
    h = _norm_mod(x_ref[...], g_ref[...], sh_ref[0], sc_ref[0])
    tm = h.shape[0]
    proj = _mm(h.astype(BF16), w_ref[...])
    cos = cos_ref[...]
    sin = sin_ref[...]
    lane = lax.broadcasted_iota(jnp.int32, (1, LANES), 1)
    low = lane < HEAD_DIM
    for i in range(NSA_DIM // LANES):
        q = proj[:, i * LANES:(i + 1) * LANES] * (ATTN_SCALE * LOG2_E)
        qn_ref[:, i * LANES:(i + 1) * LANES] = q.astype(BF16)
        qr_ref[:, i * LANES:(i + 1) * LANES] = _rope128(q, cos, sin, lane).astype(BF16)
    o = NSA_DIM
    kc_ref[...] = proj[:, o:o + 128]
    vc_ref[...] = proj[:, o + 128:o + 256]
    pos = (pl.program_id(0) % tiles_per_seq) * tm + lax.broadcasted_iota(jnp.int32, (tm, 1), 0)
    blk = pos // SEL_BLOCK
    ks = _rope128(proj[:, o + 256:o + 384], cos, sin, lane)
    ks_ref[:, 0:LANES] = jnp.where(low, ks, jnp.where(lane - HEAD_DIM == blk, 1.0, 0.0)).astype(BF16)
    ks_ref[:, LANES:2 * LANES] = jnp.where(low, jnp.where(lane == blk, 1.0, 0.0), ks).astype(BF16)
    vs = proj[:, o + 384:o + 512]
    vs_ref[:, 0:LANES] = jnp.where(low, vs, 1.0).astype(BF16)
    vs_ref[:, LANES:2 * LANES] = jnp.where(low, 1.0, vs).astype(BF16)
    kw_ref[...] = _rope128(proj[:, o + 512:o + 640], cos, sin, lane).astype(BF16)
    vw = proj[:, o + 640:o + 768]
    vw_ref[:, 0:LANES] = jnp.where(low, vw, 1.0).astype(BF16)
    vw_ref[:, LANES:2 * LANES] = jnp.where(low, 1.0, vw).astype(BF16)
    o += 768
    gate_ref[...] = jax.nn.sigmoid(proj[:, o:o + 256])
    o += 256
    xb = proj[:, o:o + 512]
    bg_ref[...] = proj[:, o + 512:o + 1024]
    u_ref[...] = proj[:, o + 1024:o + 1536] * xb


def even_in_proj(x2, g, sh, sc, w_pad, cos, sin, seq):
    n, d = x2.shape
    tm = min(TOK_TILE, seq)
    tpb = seq // tm
    row = lambda i: (i, 0)
    per_b = lambda i: (i // tpb, 0, 0)
    pos = lambda i: (i % tpb, 0)
    outs = [((n, 512), BF16), ((n, 512), BF16), ((n, 128), F32), ((n, 128), F32),
            ((n, 256), BF16), ((n, 256), BF16), ((n, 128), BF16), ((n, 256), BF16),
            ((n, 256), F32), ((n, 512), F32), ((n, 512), F32)]
    return pl.pallas_call(
        functools.partial(_even_in_kernel, tiles_per_seq=tpb),
        out_shape=[jax.ShapeDtypeStruct(s, t) for s, t in outs],
        grid=(n // tm,),
        in_specs=[pl.BlockSpec((tm, d), row),
                  pl.BlockSpec((1, d), lambda i: (0, 0)),
                  pl.BlockSpec((1, 1, d), per_b),
                  pl.BlockSpec((1, 1, d), per_b),
                  pl.BlockSpec((d, EVEN_PAD_COLS), lambda i: (0, 0)),
                  pl.BlockSpec((tm, LANES), pos),
                  pl.BlockSpec((tm, LANES), pos)],
        out_specs=[pl.BlockSpec((tm, s[1]), row) for s, _ in outs],
        compiler_params=_cparams(("parallel",)),
        name="even_in_proj",
    )(x2, g, sh, sc, w_pad, cos, sin)


def _compress_kernel(k_ref, v_ref, pos_ref, w1_ref, w2_ref, ko_ref, vo_ref):
    for j, (src, dst) in enumerate(((k_ref, ko_ref), (v_ref, vo_ref))):
        xr = src[0]
        n_rows = xr.shape[0]
        a0 = _mm((xr + pos_ref[j, 0]).astype(BF16), w1_ref[j, 0])
        a1 = _mm((xr + pos_ref[j, 1]).astype(BF16), w1_ref[j, 1])
        hid = a0 + pltpu.roll(a1, n_rows - 1, 0)
        hid = jax.nn.gelu(hid)
        dst[0] = _mm(hid.astype(BF16), w2_ref[j]).astype(BF16)


def compress_kv(kc, vc, pos_ext, w1_ext, w2_ext, batch, seq):
    rows = seq // CMP_STRIDE
    width = CMP_STRIDE * KV_DIM
    kr = kc.reshape(batch, rows, width)
    vr = vc.reshape(batch, rows, width)
    blk = pl.BlockSpec((1, rows, width), lambda b: (b, 0, 0))
    oblk = pl.BlockSpec((1, rows, KV_DIM), lambda b: (b, 0, 0))
    return pl.pallas_call(
        _compress_kernel,
        out_shape=[jax.ShapeDtypeStruct((batch, rows, KV_DIM), BF16)] * 2,
        grid=(batch,),
        in_specs=[blk, blk,
                  pl.BlockSpec(pos_ext.shape, lambda b: (0, 0, 0, 0)),
                  pl.BlockSpec(w1_ext.shape, lambda b: (0, 0, 0, 0)),
                  pl.BlockSpec(w2_ext.shape, lambda b: (0, 0, 0))],
        out_specs=[oblk, oblk],
        compiler_params=_cparams(("parallel",)),
        name="compress_kv",
    )(kr, vr, pos_ext, w1_ext, w2_ext)


def _safe_inv(l):
    return jnp.where(l > 0.0, 1.0 / jnp.where(l > 0.0, l, 1.0), 0.0)


def _nsa_kernel(qn_ref, qr_ref, kc_ref, vc_ref, ks_ref, vs_ref, kw_ref, vw_ref, gate_ref,
                selq_ref, mselt_ref, o_ref, *, seq, n_sel, sel_chunk, win_len):
    h = pl.program_id(1)
    qt = pl.program_id(2)
    t0 = qt * NSA_Q_TILE
    n_blk = seq // SEL_BLOCK
    n_cmp_pad = seq // CMP_STRIDE
    rows = GQA * NSA_Q_TILE
    tpos = t0 + lax.broadcasted_iota(jnp.int32, (1, NSA_Q_TILE, 1), 1)
    lane = lax.broadcasted_iota(jnp.int32, (1, LANES), 1)
    head_lanes = (lane // HEAD_DIM) == h

    def normalise(acc):
        return acc * _safe_inv(pltpu.roll(acc, HEAD_DIM, 1))

    qn = qn_ref[...]
    qr = qr_ref[...]
    qn4 = jnp.concatenate([_mm(qn, selq_ref[0, g]) for g in range(GQA)], axis=0).astype(BF16)
    qr4f = jnp.concatenate([_mm(qr, selq_ref[0, g]) for g in range(GQA)], axis=0)
    qr4 = qr4f.astype(BF16)

    kc = kc_ref[0]
    vc = vc_ref[0]
    cpos = lax.broadcasted_iota(jnp.int32, (1, 1, n_cmp_pad), 2) * CMP_STRIDE + (CMP_BLOCK - 1)
    cmask = cpos <= tpos
    s = jnp.where(cmask, _nt(qn4, kc).reshape(GQA, NSA_Q_TILE, n_cmp_pad), NEG_INF)
    e = jnp.where(cmask, jnp.exp2(s - jnp.max(s, axis=2, keepdims=True)), 0.0)
    p = e * _safe_inv(jnp.sum(e, axis=2, keepdims=True))
    imp = jnp.sum(p, axis=0)
    o_cmp = _mm(p.reshape(rows, n_cmp_pad).astype(BF16), vc)

    pslc = _nt(mselt_ref[...], imp, precision=HIGHEST)[:n_blk]
    tq = t0 + lax.broadcasted_iota(jnp.int32, (1, NSA_Q_TILE), 1)
    jblk = lax.broadcasted_iota(jnp.int32, (n_blk, 1), 0)
    cur = tq // SEL_BLOCK
    valid = jblk * SEL_BLOCK <= tq
    forced = (jblk == 0) | ((cur - jblk >= 0) & (cur - jblk < N_LOCAL))
    score = jnp.where(forced, BIG, jnp.where(valid, pslc, -BIG))
    rank = jnp.zeros((n_blk, NSA_Q_TILE), jnp.int32)
    for jp in range(n_blk):
        row = score[jp:jp + 1, :]
        beats = (row > score) | ((row == score) & (jblk > jp))
        rank = rank + beats.astype(jnp.int32)
    selb = jnp.where((rank < n_sel) & (score > -0.5 * BIG), 0.0, NEG_INF)
    selb_q = jnp.concatenate([selb, jnp.zeros((LANES - n_blk, NSA_Q_TILE), F32)], axis=0).T
    bias = jnp.where(h == 0, pltpu.roll(selb_q, HEAD_DIM, 1), selb_q)
    qs4 = (qr4f + jnp.concatenate([bias] * GQA, axis=0)).astype(BF16)

    def chunk_scores(c):
        start = pl.multiple_of(c * sel_chunk, sel_chunk)
        kblk = ks_ref[0, pl.ds(start, sel_chunk), :]
        vblk = vs_ref[0, pl.ds(start, sel_chunk), :]
        return start, _nt(qs4, kblk).reshape(GQA, NSA_Q_TILE, sel_chunk), vblk

    def online_update(carry, s, vblk):
        m, acc = carry
        m_new = jnp.maximum(m, jnp.max(s, axis=2, keepdims=True))
        alpha = jnp.exp2(m - m_new)
        p = jnp.exp2(s - m_new)
        pv = _mm(p.reshape(rows, sel_chunk).astype(BF16), vblk).reshape(GQA, NSA_Q_TILE, LANES)
        return m_new, alpha * acc + pv

    def sel_body(c, carry):
        _, s, vblk = chunk_scores(c)
        return online_update(carry, s, vblk)

    diag_chunk = (t0 + NSA_Q_TILE - 1) // sel_chunk
    init = (jnp.full((GQA, NSA_Q_TILE, 1), NEG_INF, F32), jnp.zeros((GQA, NSA_Q_TILE, LANES), F32))
    carry = lax.fori_loop(0, diag_chunk, sel_body, init)
    start, s, vblk = chunk_scores(diag_chunk)
    kpos = start + lax.broadcasted_iota(jnp.int32, (1, 1, sel_chunk), 2)
    _, acc = online_update(carry, jnp.where(kpos <= tpos, s, NEG_INF), vblk)
    o_slc = normalise(acc.reshape(rows, LANES))

    ws = pl.multiple_of(jnp.maximum(qt - WINDOW // NSA_Q_TILE, 0) * NSA_Q_TILE, NSA_Q_TILE)
    kwb = kw_ref[0, pl.ds(ws, win_len), :]
    vwb = vw_ref[0, pl.ds(ws, win_len), :]
    diff = tpos - (ws + lax.broadcasted_iota(jnp.int32, (1, 1, win_len), 2))
    wmask = (diff >= 0) & (diff < WINDOW)
    s = jnp.where(wmask, _nt(qr4, kwb).reshape(GQA, NSA_Q_TILE, win_len), NEG_INF)
    e = jnp.exp2(s - jnp.max(s, axis=2, keepdims=True))
    o_win = normalise(_mm(e.reshape(rows, win_len).astype(BF16), vwb))

    gate = gate_ref[...]
    og = []
    for g in range(GQA):
        sl = slice(g * NSA_Q_TILE, (g + 1) * NSA_Q_TILE)
        o = (gate[:, 3 * g:3 * g + 1] * o_cmp[sl] + gate[:, 3 * g + 1:3 * g + 2] * o_slc[sl]
             + gate[:, 3 * g + 2:3 * g + 3] * o_win[sl])
        og.append(jnp.where(head_lanes, o, pltpu.roll(o, HEAD_DIM, 1)))
    low = lane < HEAD_DIM
    o_ref[:, 0:LANES] = jnp.where(low, og[0], og[1])
    o_ref[:, LANES:2 * LANES] = jnp.where(low, og[2], og[3])


def nsa_attention(qn, qr, kcmp, vcmp, ks, vs, kw, vw, gate, selq, mselt, batch, seq):
    n = batch * seq
    nq = seq // NSA_Q_TILE
    sel_chunk = min(SEL_CHUNK, seq)
    win_len = min(WINDOW + NSA_Q_TILE, seq)
    n_sel = min(N_SEL, seq // SEL_BLOCK)
    qspec = pl.BlockSpec((NSA_Q_TILE, GQA * HEAD_DIM), lambda b, h, q: (b * nq + q, h))
    cspec = pl.BlockSpec((1, seq // CMP_STRIDE, KV_DIM), lambda b, h, q: (b, 0, 0))
    both = pl.BlockSpec((1, seq, KV_DIM), lambda b, h, q: (b, 0, 0))
    mine = pl.BlockSpec((1, seq, KV_DIM), lambda b, h, q: (b, 0, h))
    kern = functools.partial(_nsa_kernel, seq=seq, n_sel=n_sel, sel_chunk=sel_chunk, win_len=win_len)
    return pl.pallas_call(
        kern,
        out_shape=jax.ShapeDtypeStruct((n, NSA_DIM), F32),
        grid=(batch, N_KV_HEADS, nq),
        in_specs=[qspec, qspec, cspec, cspec, mine, mine, both, mine,
                  pl.BlockSpec((NSA_Q_TILE, LANES), lambda b, h, q: (b * nq + q, h)),
                  pl.BlockSpec((1, GQA, GQA * HEAD_DIM, LANES), lambda b, h, q: (h, 0, 0, 0)),
                  pl.BlockSpec(mselt.shape, lambda b, h, q: (0, 0))],
        out_specs=qspec,
        compiler_params=_cparams(("parallel", "parallel", "arbitrary")),
        name="nsa_attention",
    )(qn, qr, kcmp, vcmp, ks.reshape(batch, seq, 2 * KV_DIM), vs.reshape(batch, seq, 2 * KV_DIM),
      kw.reshape(batch, seq, KV_DIM), vw.reshape(batch, seq, 2 * KV_DIM), gate, selq, mselt)


def _route(h2, rwt_ref, rb_ref):
    rw = rwt_ref[...]
    rw_hi = rw.astype(BF16)
    rw_lo = (rw - rw_hi.astype(F32)).astype(BF16)
    h_hi = h2.astype(BF16)
    h_lo = (h2 - h_hi.astype(F32)).astype(BF16)
    logits = _nt(rw_hi, h_hi) + (_nt(rw_hi, h_lo) + _nt(rw_lo, h_hi))
    scores = jax.nn.sigmoid(logits)
    biased = scores + rb_ref[...]
    rows = [biased[e:e + 1, :] for e in range(N_EXPERTS)]
    srow = [scores[e:e + 1, :] for e in range(N_EXPERTS)]
    gscore = []
    for gi in range(N_EXPERT_GROUPS):
        r = rows[gi * EXPERTS_PER_GROUP:(gi + 1) * EXPERTS_PER_GROUP]
        best = None
        for a in range(EXPERTS_PER_GROUP):
            for b in range(a + 1, EXPERTS_PER_GROUP):
                pair = r[a] + r[b]
                best = pair if best is None else jnp.maximum(best, pair)
        gscore.append(best)
    top_val = gscore[0]
    top_grp = jnp.zeros_like(top_val, dtype=jnp.int32)
    for gi in range(1, N_EXPERT_GROUPS):
        upd = gscore[gi] > top_val
        top_grp = jnp.where(upd, gi, top_grp)
        top_val = jnp.where(upd, gscore[gi], top_val)
    masked = [jnp.where(top_grp == e // EXPERTS_PER_GROUP, rows[e], NEG_INF) for e in range(N_EXPERTS)]
    b1 = masked[0]
    i1 = jnp.zeros_like(top_grp)
    for e in range(1, N_EXPERTS):
        upd = masked[e] > b1
        i1 = jnp.where(upd, e, i1)
        b1 = jnp.where(upd, masked[e], b1)
    b2 = None
    i2 = None
    for e in range(N_EXPERTS):
        v = jnp.where(i1 == e, -jnp.inf, masked[e])
        if b2 is None:
            b2, i2 = v, jnp.zeros_like(top_grp)
        else:
            upd = v > b2
            i2 = jnp.where(upd, e, i2)
            b2 = jnp.where(upd, v, b2)
    s1 = jnp.zeros_like(top_val)
    s2 = jnp.zeros_like(top_val)
    for e in range(N_EXPERTS):
        s1 = s1 + jnp.where(i1 == e, srow[e], 0.0)
        s2 = s2 + jnp.where(i2 == e, srow[e], 0.0)
    tot = s1 + s2
    return i1, i2, s1 / tot, s2 / tot


def _tail(cat_bf16, x_ref, g1_ref, wout_ref, ng_ref, sh2_ref, sc2_ref, rwt_ref, rb_ref, ustrict_ref,
          x1_ref, hw_ref, ridx_ref, cnt_ref, carry_ref):
    y = _mm(cat_bf16, wout_ref[...])
    x1 = x_ref[...] + g1_ref[0] * y
    x1_ref[...] = x1
    h2 = _norm_mod(x1, ng_ref[...], sh2_ref[0], sc2_ref[0])
    tm, d = h2.shape
    hw_ref[:, :d] = h2

    i1, i2, w1, w2 = _route(h2, rwt_ref, rb_ref)
    lo = jnp.minimum(i1, i2) % EXPERTS_PER_GROUP
    hi = jnp.maximum(i1, i2) % EXPERTS_PER_GROUP
    pair = jnp.where(lo == 0, hi - 1, jnp.where(lo == 1, jnp.where(hi == 3, 3, 4), 5))
    cls = (i1 // EXPERTS_PER_GROUP) * PAIRS_PER_GROUP + pair
    w_lo = jnp.where(i1 < i2, w1, w2)
    w_hi = jnp.where(i1 < i2, w2, w1)
    meta_t = jnp.concatenate([w_lo, w_hi, jnp.zeros((LANES - 2, tm), F32)], axis=0)
    hw_ref[:, d:] = meta_t.T

    @pl.when(pl.program_id(0) == 0)
    def _():
        carry_ref[...] = jnp.zeros_like(carry_ref)

    hit = lax.broadcasted_iota(jnp.int32, (CLASS_ROWS, 1), 0) == cls
    cnt = jnp.where(hit, 1.0, 0.0)
    before = _mm(cnt.astype(BF16), ustrict_ref[...]) + carry_ref[:, 0:1]
    carry = carry_ref[...] + jnp.sum(cnt, axis=1, keepdims=True)
    carry_ref[...] = carry
    cnt_ref[...] = carry.astype(jnp.int32)
    ridx_ref[0:1, :] = cls
    ridx_ref[1:2, :] = jnp.sum(jnp.where(hit, before, 0.0), axis=0, keepdims=True).astype(jnp.int32)


def _even_out_kernel(o_ref, u_ref, uh_ref, bg_ref, cw_ref, *tail_refs, tiles_per_seq):
    first = (pl.program_id(0) % tiles_per_seq) == 0
    u = u_ref[...]
    halo = jnp.where(first, 0.0, uh_ref[...])
    ext = jnp.concatenate([halo, u], axis=0)
    u1 = pltpu.roll(ext, 1, 0)[CONV_HALO:]
    u2 = pltpu.roll(ext, 2, 0)[CONV_HALO:]
    cw = cw_ref[...]
    y_conv = bg_ref[...] * (cw[2:3] * u + cw[1:2] * u1 + cw[0:1] * u2)
    cat = jnp.concatenate([o_ref[...], y_conv], axis=1).astype(BF16)
    _tail(cat, *tail_refs)


def _tail_specs(tm, d, tpb):
    row = lambda i: (i, 0)
    per_b = lambda i: (i // tpb, 0, 0)
    const2 = lambda i: (0, 0)
    ins = [pl.BlockSpec((tm, d), row),
           pl.BlockSpec((1, 1, d), per_b),
           pl.BlockSpec((d, d), const2),
           pl.BlockSpec((1, d), const2),
           pl.BlockSpec((1, 1, d), per_b),
           pl.BlockSpec((1, 1, d), per_b),
           pl.BlockSpec((N_EXPERTS, d), const2),
           pl.BlockSpec((N_EXPERTS, 1), const2),
           pl.BlockSpec((tm, tm), const2)]
    outs = [pl.BlockSpec((tm, d), row), pl.BlockSpec((tm, d + LANES), row),
            pl.BlockSpec((2, tm), lambda i: (0, i)),
            pl.BlockSpec((CLASS_ROWS, LANES), const2)]
    scratch = [pltpu.VMEM((CLASS_ROWS, LANES), F32)]
    return ins, outs, scratch


def _tail_out_shapes(n, d):
    return [jax.ShapeDtypeStruct((n, d), F32), jax.ShapeDtypeStruct((n, d + LANES), F32),
            jax.ShapeDtypeStruct((2, n), jnp.int32), jax.ShapeDtypeStruct((CLASS_ROWS, LANES), jnp.int32)]


def _strict_upper(tm):
    return jnp.asarray(np.triu(np.ones((tm, tm), np.float32), 1), dtype=BF16)


def even_out_proj(o_nsa, u, bg, conv_w, x2, g1, w_out, ng, sh2, sc2, rwt, rb, seq):
    n, d = x2.shape
    tm = min(TOK_TILE, seq)
    tpb = seq // tm
    row = lambda i: (i, 0)
    halo = lambda i: (jnp.maximum(i * (tm // CONV_HALO) - 1, 0), 0)
    tin, tout, tscratch = _tail_specs(tm, d, tpb)
    return pl.pallas_call(
        functools.partial(_even_out_kernel, tiles_per_seq=tpb),
        out_shape=_tail_out_shapes(n, d),
        grid=(n // tm,),
        in_specs=[pl.BlockSpec((tm, NSA_DIM), row),
                  pl.BlockSpec((tm, CONV_DIM), row),
                  pl.BlockSpec((CONV_HALO, CONV_DIM), halo),
                  pl.BlockSpec((tm, CONV_DIM), row),
                  pl.BlockSpec(conv_w.shape, lambda i: (0, 0))] + tin,
        out_specs=tout,
        scratch_shapes=tscratch,
        compiler_params=_cparams(("arbitrary",)),
        name="even_out_proj",
    )(o_nsa, u, u, bg, conv_w, x2, g1, w_out, ng, sh2, sc2, rwt, rb, _strict_upper(tm))


def _dispatch_plan(ridx, counts, n, seq):
    cnt = counts[:N_CLASSES, 0]
    padded = (cnt + MOE_ROW_TILE - 1) // MOE_ROW_TILE * MOE_ROW_TILE
    ends = jnp.cumsum(padded)
    starts = ends - padded
    cids = jnp.arange(N_CLASSES, dtype=jnp.int32)[:, None]
    base = jnp.sum(jnp.where(cids == ridx[0][None, :], starts[:, None], 0), axis=0)
    dest = (base + ridx[1]).astype(jnp.int32)
    td = min(MOE_DMA_TILE, seq)
    dest3 = dest.reshape(n // td, 1, td)
    n_tiles = n // MOE_ROW_TILE + N_CLASSES
    tile_start = jnp.arange(n_tiles, dtype=jnp.int32) * MOE_ROW_TILE
    tile_class = jnp.minimum(jnp.sum(tile_start[:, None] >= ends[None, :], axis=1), N_CLASSES - 1)
    group_base = (tile_class // PAIRS_PER_GROUP) * EXPERTS_PER_GROUP
    pair = tile_class % PAIRS_PER_GROUP
    tile_lo = (group_base + jnp.asarray(PAIR_LO, jnp.int32)[pair]).astype(jnp.int32)
    tile_hi = (group_base + jnp.asarray(PAIR_HI, jnp.int32)[pair]).astype(jnp.int32)
    n_used = (ends[-1] // MOE_ROW_TILE).reshape(1).astype(jnp.int32)
    last_tile = jnp.where(cnt > 0, ends // MOE_ROW_TILE - 1, -1)
    tail = n_used[0] + jnp.arange(N_CLASSES, dtype=jnp.int32)
    zero_tiles = jnp.concatenate([last_tile, jnp.where(tail < n_tiles, tail, -1)]).astype(jnp.int32)
    return dest3, tile_lo, tile_hi, n_used, zero_tiles, n_tiles


def _dispatch_kernel(ztile_ref, dest_ref, hw_ref, xs_hbm, zbuf, zsem, sem):
    td = hw_ref.shape[0]

    @pl.when(pl.program_id(0) == 0)
    def _():
        zbuf[...] = jnp.zeros_like(zbuf)

        def zero_copy(k):
            start = pl.multiple_of(ztile_ref[k] * MOE_ROW_TILE, MOE_ROW_TILE)
            return pltpu.make_async_copy(zbuf, xs_hbm.at[pl.ds(start, MOE_ROW_TILE)], zsem)

        for k in range(2 * N_CLASSES):
            @pl.when(ztile_ref[k] >= 0)
            def _():
                zero_copy(k).start()
        for k in range(2 * N_CLASSES):
            @pl.when(ztile_ref[k] >= 0)
            def _():
                zero_copy(k).wait()

    for r in range(td):
        pltpu.make_async_copy(hw_ref.at[pl.ds(r, 1)],
                              xs_hbm.at[pl.ds(dest_ref[0, 0, r], 1)], sem).start()
    pltpu.make_async_copy(hw_ref, xs_hbm.at[pl.ds(0, td)], sem).wait()


def moe_dispatch(hw, dest3, zero_tiles, n_tiles):
    n, cols = hw.shape
    td = dest3.shape[2]
    rows = n_tiles * MOE_ROW_TILE
    return pl.pallas_call(
        _dispatch_kernel,
        out_shape=jax.ShapeDtypeStruct((rows, cols), F32),
        grid_spec=pltpu.PrefetchScalarGridSpec(
            num_scalar_prefetch=1,
            grid=(n // td,),
            in_specs=[pl.BlockSpec((1, 1, td), lambda i, z: (i, 0, 0), memory_space=pltpu.SMEM),
                      pl.BlockSpec((td, cols), lambda i, z: (i, 0))],
            out_specs=pl.BlockSpec(memory_space=pl.ANY),
            scratch_shapes=[pltpu.VMEM((MOE_ROW_TILE, cols), F32), pltpu.SemaphoreType.DMA(()),
                            pltpu.SemaphoreType.DMA(())]),
        compiler_params=_cparams(("arbitrary",)),
        name="moe_dispatch",
    )(zero_tiles, dest3, hw)


def _expert_kernel(lo_ref, hi_ref, nused_ref, xs_ref, wg_lo, wu_lo, wd_lo, wg_hi, wu_hi, wd_hi, ys_ref,
                   *wb):
    t = pl.program_id(0)
    prev = jnp.maximum(t - 1, 0)

    for ids, srcs, dsts in ((lo_ref, (wg_lo, wu_lo, wd_lo), wb[:3]), (hi_ref, (wg_hi, wu_hi, wd_hi), wb[3:])):
        @pl.when((t == 0) | (ids[t] != ids[prev]))
        def _():
            for src, dst in zip(srcs, dsts):
                dst[...] = src[0, 0].astype(BF16)

    @pl.when(t < nused_ref[0])
    def _():
        d = xs_ref.shape[1] - LANES
        x = xs_ref[:, :d].astype(BF16)
        meta = xs_ref[:, d:]
        y = None
        for k in range(2):
            a = _mm(x, wb[3 * k][...])
            b = _mm(x, wb[3 * k + 1][...])
            he = (a * jax.nn.sigmoid(a)) * b
            yk = meta[:, k:k + 1] * _mm(he.astype(BF16), wb[3 * k + 2][...])
            y = yk if y is None else y + yk
        ys_ref[...] = y

    @pl.when(t >= nused_ref[0])
    def _():
        ys_ref[...] = jnp.zeros_like(ys_ref)


def moe_experts(xs, tile_lo, tile_hi, n_used, w_gate, w_up, w_down, layer, d):
    rows, cols = xs.shape
    n_tiles = rows // MOE_ROW_TILE
    lo_spec = lambda shape: pl.BlockSpec((1, 1) + shape, lambda t, lo, hi, nu: (layer, lo[t], 0, 0))
    hi_spec = lambda shape: pl.BlockSpec((1, 1) + shape, lambda t, lo, hi, nu: (layer, hi[t], 0, 0))
    shapes = ((d, D_EXPERT), (d, D_EXPERT), (D_EXPERT, d))
    return pl.pallas_call(
        _expert_kernel,
        out_shape=jax.ShapeDtypeStruct((rows, d), F32),
        grid_spec=pltpu.PrefetchScalarGridSpec(
            num_scalar_prefetch=3,
            grid=(n_tiles,),
            in_specs=[pl.BlockSpec((MOE_ROW_TILE, cols),
                                   lambda t, lo, hi, nu: (jnp.minimum(t, nu[0] - 1), 0))]
                     + [lo_spec(s) for s in shapes] + [hi_spec(s) for s in shapes],
            out_specs=pl.BlockSpec((MOE_ROW_TILE, d), lambda t, lo, hi, nu: (t, 0)),
            scratch_shapes=[pltpu.VMEM(s, BF16) for s in shapes + shapes]),
        compiler_params=_cparams(("arbitrary",)),
        name="moe_experts",
    )(tile_lo, tile_hi, n_used, xs, w_gate, w_up, w_down, w_gate, w_up, w_down)


def _combine_kernel(dest_ref, ys_hbm, x_ref, g2_ref, fn_ref, o_ref, buf, sem, *, final_norm):
    tc = x_ref.shape[0]

    for r in range(tc):
        pltpu.make_async_copy(ys_hbm.at[pl.ds(dest_ref[0, 0, r], 1)], buf.at[pl.ds(r, 1)], sem).start()
    pltpu.make_async_copy(ys_hbm.at[pl.ds(0, tc)], buf, sem).wait()
    x = x_ref[...] + g2_ref[0] * buf[...]
    if final_norm:
        ms = jnp.mean(x * x, axis=-1, keepdims=True)
        x = x * lax.rsqrt(ms + NORM_EPS) * fn_ref[...]
    o_ref[...] = x


def moe_combine(ys, dest3, x1, g2, fnorm, seq, final_norm):
    n, d = x1.shape
    tc = dest3.shape[2]
    tpb = seq // tc
    return pl.pallas_call(
        functools.partial(_combine_kernel, final_norm=final_norm),
        out_shape=jax.ShapeDtypeStruct((n, d), F32),
        grid=(n // tc,),
        in_specs=[pl.BlockSpec((1, 1, tc), lambda i: (i, 0, 0), memory_space=pltpu.SMEM),
                  pl.BlockSpec(memory_space=pl.ANY),
                  pl.BlockSpec((tc, d), lambda i: (i, 0)),
                  pl.BlockSpec((1, 1, d), lambda i: (i // tpb, 0, 0)),
                  pl.BlockSpec((1, d), lambda i: (0, 0))],
        out_specs=pl.BlockSpec((tc, d), lambda i: (i, 0)),
        scratch_shapes=[pltpu.VMEM((tc, d), F32), pltpu.SemaphoreType.DMA(())],
        compiler_params=_cparams(("arbitrary",)),
        name="moe_combine",
    )(dest3, ys, x1, g2, fnorm)


def moe_routed_experts(hw, ridx, counts, w_gate, w_up, w_down, layer, seq):
    n = hw.shape[0]
    d = hw.shape[1] - LANES
    dest3, tile_lo, tile_hi, n_used, zero_tiles, n_tiles = _dispatch_plan(ridx, counts, n, seq)
    xs = moe_dispatch(hw, dest3, zero_tiles, n_tiles)
    ys = moe_experts(xs, tile_lo, tile_hi, n_used, w_gate, w_up, w_down, layer, d)
    return ys, dest3


def _odd_in_kernel(dcur_ref, dnxt_ref, ys_hbm, x1_ref, gm_ref, g_ref, sh_ref, sc_ref, w_ref, mu_ref,
                   w0_ref, w2_ref, a0_ref, a2_ref, g2_ref, kk_ref, ka_ref, ones_ref, pw_ref, ps_ref,
                   x_ref, r_ref, lw_ref, km_ref, v_ref, kn_ref, kb_ref, gg_ref, op_ref,
                   rw_carry, u_carry, ybuf, ysem, *, tiles_per_seq, tm, n_steps):
    i = pl.program_id(0)
    first = (i % tiles_per_seq) == 0
    slot = i % 2

    def gather(dest_ref, to_slot):
        for r in range(tm):
            pltpu.make_async_copy(ys_hbm.at[pl.ds(dest_ref[0, 0, r], 1)],
                                  ybuf.at[to_slot, pl.ds(r, 1)], ysem.at[to_slot]).start()

    @pl.when(i == 0)
    def _():
        gather(dcur_ref, 0)

    @pl.when(i + 1 < n_steps)
    def _():
        gather(dnxt_ref, 1 - slot)

    pltpu.make_async_copy(ys_hbm.at[pl.ds(0, tm)], ybuf.at[slot], ysem.at[slot]).wait()
    x = x1_ref[...] + gm_ref[0] * ybuf[slot]
    x_ref[...] = x
    h = _norm_mod(x, g_ref[...], sh_ref[0], sc_ref[0])
    proj = _mm(h.astype(BF16), w_ref[...])

    rw = proj[:, :ODD_RW_COLS]
    row0 = jnp.where(first, 0.0, rw_carry[0:1, :])
    ridx = lax.broadcasted_iota(jnp.int32, (tm, 1), 0)
    prev = jnp.where(ridx == 0, row0, pltpu.roll(rw, 1, 0))
    rw_carry[0:1, :] = rw[tm - 1:tm, :]
    rw = rw + (prev - rw) * mu_ref[...]

    r = rw[:, 0:512]
    k = rw[:, 512:1024]
    v = rw[:, 1024:1536]
    wl = rw[:, 1536:1664]
    al = rw[:, 1664:1792]
    gl = rw[:, 1792:1920]
    z = -(w0_ref[...] + _mm(jnp.tanh(wl).astype(BF16), w2_ref[...]))
    softplus = jnp.maximum(z, 0.0) + jnp.log1p(jnp.exp(-jnp.abs(z)))
    w_log = -softplus - 0.5
    a = jax.nn.sigmoid(a0_ref[...] + _mm(al.astype(BF16), a2_ref[...]))
    gg_ref[...] = _mm(jax.nn.sigmoid(gl).astype(BF16), g2_ref[...])
    kk0 = k * kk_ref[...]
    ss = _split_sum(kk0 * kk0, ones_ref[...])
    kk = kk0 / jnp.maximum(jnp.sqrt(ss), 1e-12)
    r_ref[...] = r
    lw_ref[...] = -jnp.exp(w_log)
    km_ref[...] = k * (1.0 + (a - 1.0) * ka_ref[...])
    v_ref[...] = v
    kn_ref[...] = kk
    kb_ref[...] = kk * a

    u = proj[:, ODD_RW_COLS:]
    halo = jnp.where(first, 0.0, u_carry[...])
    u_carry[...] = u[tm - POOL_HALO:, :]
    ext = jnp.concatenate([halo, u], axis=0)
    tseq = (i % tiles_per_seq) * tm + ridx
    for gi, win in enumerate(POOL_WINDOWS):
        xg = ext[:, gi * POOL_GROUP:(gi + 1) * POOL_GROUP]
        s = xg
        step = 1
        while step < win:
            s = s + pltpu.roll(s, step, 0)
            step *= 2
        cnt = jnp.minimum(tseq + 1, win).astype(F32)
        pooled = s[POOL_HALO:] / cnt - xg[POOL_HALO:]
        mixed = _mm(pooled.astype(BF16), pw_ref[gi])
        op_ref[:, gi * POOL_GROUP:(gi + 1) * POOL_GROUP] = (
            mixed * ps_ref[:, gi * POOL_GROUP:(gi + 1) * POOL_GROUP])


def odd_in_proj(pending, g, sh, sc, w_pad, mu_pad, w0, w2p, a0, a2p, g2, k_k, k_a, ones_bd, pool_w,
                pool_scale, seq):
    ys, dest, x1, gate_moe = pending
    n, d = x1.shape
    tm = min(TOK_TILE, seq)
    tpb = seq // tm
    n_steps = n // tm
    row = lambda i: (i, 0)
    per_b = lambda i: (i // tpb, 0, 0)
    c2 = lambda i: (0, 0)
    full2 = lambda a: pl.BlockSpec(a.shape, c2)
    dest3 = dest.reshape(n_steps, 1, tm)
    return pl.pallas_call(
        functools.partial(_odd_in_kernel, tiles_per_seq=tpb, tm=tm, n_steps=n_steps),
        out_shape=[jax.ShapeDtypeStruct((n, d), F32)] + [jax.ShapeDtypeStruct((n, RWKV_DIM), F32)] * 8,
        grid=(n_steps,),
        in_specs=[pl.BlockSpec((1, 1, tm), lambda i: (i, 0, 0), memory_space=pltpu.SMEM),
                  pl.BlockSpec((1, 1, tm), lambda i: (jnp.minimum(i + 1, n_steps - 1), 0, 0),
                               memory_space=pltpu.SMEM),
                  pl.BlockSpec(memory_space=pl.ANY),
                  pl.BlockSpec((tm, d), row), pl.BlockSpec((1, 1, d), per_b),
                  pl.BlockSpec((1, d), c2),
                  pl.BlockSpec((1, 1, d), per_b), pl.BlockSpec((1, 1, d), per_b),
                  full2(w_pad), full2(mu_pad), full2(w0), full2(w2p), full2(a0), full2(a2p),
                  full2(g2), full2(k_k), full2(k_a), full2(ones_bd),
                  pl.BlockSpec(pool_w.shape, lambda i: (0, 0, 0)), full2(pool_scale)],
        out_specs=[pl.BlockSpec((tm, d), row)] + [pl.BlockSpec((tm, RWKV_DIM), row)] * 8,
        scratch_shapes=[pltpu.VMEM((SUBLANES, ODD_RW_COLS), F32),
                        pltpu.VMEM((POOL_HALO, RWKV_DIM), F32),
                        pltpu.VMEM((2, tm, d), F32), pltpu.SemaphoreType.DMA((2,))],
        compiler_params=_cparams(("arbitrary",)),
        name="odd_in_proj",
    )(dest3, dest3, ys, x1, gate_moe, g, sh, sc, w_pad, mu_pad, w0, w2p, a0, a2p, g2, k_k, k_a, ones_bd,
      pool_w, pool_scale)


def _bmm(a, b):
    return lax.dot_general(a, b, (((2,), (1,)), ((0,), (0,))), preferred_element_type=F32)


def _bnt(a, b):
    return lax.dot_general(a, b, (((2,), (2,)), ((0,), (0,))), preferred_element_type=F32)


def _btn(a, b):
    return lax.dot_general(a, b, (((1,), (1,)), ((0,), (0,))), preferred_element_type=F32)


def _scan_prep_kernel(r_ref, lw_ref, km_ref, v_ref, kn_ref, kb_ref, qe_ref, y0_ref, mt_ref, ct_ref,
                      *, chunk, cb):
    L = chunk
    rows = cb * L
    n_pairs = N_RWKV_HEADS // 2
    two = 2 * L
    rowt = lax.broadcasted_iota(jnp.int32, (rows, 1), 0) % L
    lane = lax.broadcasted_iota(jnp.int32, (1, 1, LANES), 2)
    low = lane < HEAD_DIM
    ri = lax.broadcasted_iota(jnp.int32, (two, two), 0)
    ci = lax.broadcasted_iota(jnp.int32, (two, two), 1)
    same_blk = (ri // L) == (ci // L)
    strict = same_blk & ((ci % L) < (ri % L))
    incl = same_blk & ((ci % L) <= (ri % L))
    li = lax.broadcasted_iota(jnp.int32, (LANES, LANES), 0)
    lj = lax.broadcasted_iota(jnp.int32, (LANES, LANES), 1)
    same_head = (li // HEAD_DIM) == (lj // HEAD_DIM)
    eye = li == lj

    lw = lw_ref[...]
    cum = lw
    step = 1
    while step < L:
        cum = cum + jnp.where(rowt >= step, pltpu.roll(cum, step, 0), 0.0)
        step *= 2

    def to3(x):
        x3 = x.reshape(cb, L, RWKV_DIM)
        return jnp.concatenate([x3[:, :, p * LANES:(p + 1) * LANES] for p in range(n_pairs)], axis=0)

    def stack2(x):
        return jnp.concatenate([jnp.where(low, x, 0.0), jnp.where(low, 0.0, x)], axis=1)

    def fold(x):
        return x[:, :L, :] + x[:, L:, :]

    cum3 = to3(cum)
    lw3 = to3(lw)
    cum_l = cum3[:, L - 1:L, :]
    g_inv = jnp.exp(-cum3)
    g_tail = jnp.exp(cum_l - cum3)
    kb = to3(kb_ref[...])
    km = to3(km_ref[...])
    v = to3(v_ref[...])
    at_s = stack2(-to3(kn_ref[...]) * jnp.exp(cum3 - lw3))
    rt_s = stack2(to3(r_ref[...]) * jnp.exp(cum3))
    v_s = stack2(v).astype(BF16)
    lhs = jnp.concatenate([at_s, rt_s], axis=1).astype(BF16)
    rhs = jnp.concatenate([stack2(kb * g_inv), stack2(km * g_inv)], axis=1).astype(BF16)
    prod = _bnt(lhs, rhs)
    nmat = jnp.where(strict, prod[:, :two, :two], 0.0)
    a_ak = jnp.where(strict, prod[:, :two, two:], 0.0).astype(BF16)
    a_rb = jnp.where(incl, prod[:, two:, :two], 0.0).astype(BF16)
    a_rk = jnp.where(incl, prod[:, two:, two:], 0.0).astype(BF16)

    x = jnp.concatenate([at_s, _bmm(a_ak, v_s)], axis=2)
    npow = nmat
    step = 1
    while step < L:
        nb = npow.astype(BF16)
        x = x + _bmm(nb, x.astype(BF16))
        step *= 2
        if step < L:
            npow = _bmm(nb, nb)
    qy = _bmm(a_rb, x.astype(BF16))
    qe = fold(rt_s + qy[:, :, :LANES])
    y0 = fold(qy[:, :, LANES:] + _bmm(a_rk, v_s))
    wu = fold(x).astype(BF16)
    bwu = _btn((kb * g_tail).astype(BF16), wu)
    kv = _btn((km * g_tail).astype(BF16), v.astype(BF16))
    g_l = jnp.broadcast_to(jnp.exp(cum_l), (n_pairs * cb, LANES, LANES))
    mt = jnp.where(eye, g_l, 0.0) + jnp.where(same_head, bwu[:, :, :LANES], 0.0)
    ct = jnp.where(same_head, bwu[:, :, LANES:] + kv, 0.0)
    for p in range(n_pairs):
        sl = slice(p * LANES, (p + 1) * LANES)
        qe_ref[:, sl] = qe[p * cb:(p + 1) * cb].reshape(rows, LANES)
        y0_ref[:, sl] = y0[p * cb:(p + 1) * cb].reshape(rows, LANES)
        mt_ref[:, p] = mt[p * cb:(p + 1) * cb].astype(BF16)
        ct_ref[:, p] = ct[p * cb:(p + 1) * cb]


def _scan_state_kernel(qe_ref, y0_ref, mt_ref, ct_ref, y_ref, st_ref, *, batch):
    @pl.when(pl.program_id(0) == 0)
    def _():
        st_ref[...] = jnp.zeros_like(st_ref)

    n_pairs = N_RWKV_HEADS // 2
    qe = qe_ref[...]
    qe3 = jnp.concatenate([qe[:, :, p * LANES:(p + 1) * LANES] for p in range(n_pairs)], axis=0)
    st = st_ref[...].astype(BF16)
    y = _bmm(qe3.astype(BF16), st)
    for p in range(n_pairs):
        sl = slice(p * LANES, (p + 1) * LANES)
        y_ref[:, :, sl] = y[p * batch:(p + 1) * batch] + y0_ref[:, :, sl]
    mt = jnp.concatenate([mt_ref[:, 0, p] for p in range(n_pairs)], axis=0)
    ct = jnp.concatenate([ct_ref[:, 0, p] for p in range(n_pairs)], axis=0)
    st_ref[...] = _bmm(mt, st) + ct


def rwkv_scan(r, lw, km, v, kn, kb, batch, seq):
    n = batch * seq
    chunk = min(SCAN_CHUNK, seq)
    nc = seq // chunk
    cb = min(SCAN_CHUNKS_PER_STEP, nc)
    n_pairs = N_RWKV_HEADS // 2
    blk = pl.BlockSpec((cb * chunk, RWKV_DIM), lambda i: (i, 0))
    mblk = pl.BlockSpec((cb, n_pairs, LANES, LANES), lambda i: (i, 0, 0, 0))
    qe, y0, mt, ct = pl.pallas_call(
        functools.partial(_scan_prep_kernel, chunk=chunk, cb=cb),
        out_shape=[jax.ShapeDtypeStruct((n, RWKV_DIM), F32), jax.ShapeDtypeStruct((n, RWKV_DIM), F32),
                   jax.ShapeDtypeStruct((n // chunk, n_pairs, LANES, LANES), BF16),
                   jax.ShapeDtypeStruct((n // chunk, n_pairs, LANES, LANES), F32)],
        grid=(n // (cb * chunk),),
        in_specs=[blk] * 6,
        out_specs=[blk, blk, mblk, mblk],
        compiler_params=_cparams(("parallel",)),
        name="rwkv_scan_prep",
    )(r, lw, km, v, kn, kb)
    sblk = pl.BlockSpec((batch, chunk, RWKV_DIM), lambda c: (0, c, 0))
    smblk = pl.BlockSpec((batch, 1, n_pairs, LANES, LANES), lambda c: (0, c, 0, 0, 0))
    y = pl.pallas_call(
        functools.partial(_scan_state_kernel, batch=batch),
        out_shape=jax.ShapeDtypeStruct((batch, seq, RWKV_DIM), F32),
        grid=(nc,),
        in_specs=[sblk, sblk, smblk, smblk],
        out_specs=sblk,
        scratch_shapes=[pltpu.VMEM((n_pairs * batch, LANES, LANES), F32)],
        compiler_params=_cparams(("arbitrary",)),
        name="rwkv_scan_state",
    )(qe.reshape(batch, seq, RWKV_DIM), y0.reshape(batch, seq, RWKV_DIM),
      mt.reshape(batch, nc, n_pairs, LANES, LANES), ct.reshape(batch, nc, n_pairs, LANES, LANES))
    return y.reshape(n, RWKV_DIM)


def _odd_out_kernel(y_ref, r_ref, km_ref, v_ref, gg_ref, op_ref, rk_ref, lnw_ref, lnb_ref, ones_ref,
                    *tail_refs):
    ones = ones_ref[...]
    inv = 1.0 / HEAD_DIM
    y = y_ref[...]
    mean = _split_sum(y, ones) * inv
    yc = y - mean
    var = _split_sum(yc * yc, ones) * inv
    yn = yc * lax.rsqrt(var + LNX_EPS) * lnw_ref[...] + lnb_ref[...]
    bonus = _split_sum(r_ref[...] * km_ref[...] * rk_ref[...], ones) * v_ref[...]
    o_rwkv = (yn + bonus) * gg_ref[...]
    cat = jnp.concatenate([o_rwkv, op_ref[...]], axis=1).astype(BF16)
    _tail(cat, *tail_refs)


def odd_out_proj(y, r, km, v, gg, opool, r_k, lnx_w, lnx_b, ones_bd, x2, g1, w_out, ng, sh2, sc2,
                 rwt, rb, seq):
    n, d = x2.shape
    tm = min(TOK_TILE, seq)
    tpb = seq // tm
    row = lambda i: (i, 0)
    c2 = lambda i: (0, 0)
    act = pl.BlockSpec((tm, RWKV_DIM), row)
    vec = pl.BlockSpec((1, RWKV_DIM), c2)
    tin, tout, tscratch = _tail_specs(tm, d, tpb)
    return pl.pallas_call(
        _odd_out_kernel,
        out_shape=_tail_out_shapes(n, d),
        grid=(n // tm,),
        in_specs=[act] * 6 + [vec, vec, vec, pl.BlockSpec(ones_bd.shape, c2)] + tin,
        out_specs=tout,
        scratch_shapes=tscratch,
        compiler_params=_cparams(("arbitrary",)),
        name="odd_out_proj",
    )(y, r, km, v, gg, opool, r_k, lnx_w, lnx_b, ones_bd, x2, g1, w_out, ng, sh2, sc2, rwt, rb,
      _strict_upper(tm))


def _rope_tables(seq):
    half = HEAD_DIM // 2
    inv = ROPE_THETA ** (-jnp.arange(half, dtype=F32) / half)
    ang = jnp.arange(seq, dtype=F32)[:, None] * inv[None, :]
    return jnp.tile(jnp.cos(ang), (1, LANES // half)), jnp.tile(jnp.sin(ang), (1, LANES // half))


def _even_w_pad(w_in):
    d = w_in.shape[0]
    q_kv = w_in[:, :NSA_DIM + 6 * KV_DIM]
    gl = w_in[:, NSA_DIM + 6 * KV_DIM:NSA_DIM + 6 * KV_DIM + 24]
    rest = w_in[:, NSA_DIM + 6 * KV_DIM + 24:]
    z = jnp.zeros((d, LANES - 12), w_in.dtype)
    return jnp.concatenate([q_kv, gl[:, :12], z, gl[:, 12:], z, rest], axis=1).astype(BF16)


def _compress_params(cmp_pos, cmp_w1, cmp_w2):
    eye = jnp.eye(N_KV_HEADS, dtype=F32)
    w1r = cmp_w1.reshape(2, 2, CMP_STRIDE, HEAD_DIM, CMP_HIDDEN)
    w1_ext = jnp.einsum('kpmdn,gh->kpmgdhn', w1r, eye).reshape(
        2, 2, CMP_STRIDE * KV_DIM, N_KV_HEADS * CMP_HIDDEN).astype(BF16)
    w2_ext = jnp.einsum('knd,gh->kgnhd', cmp_w2, eye).reshape(
        2, N_KV_HEADS * CMP_HIDDEN, KV_DIM).astype(BF16)
    pos = cmp_pos.reshape(2, 2, CMP_STRIDE, 1, HEAD_DIM)
    pos_ext = jnp.broadcast_to(pos, (2, 2, CMP_STRIDE, N_KV_HEADS, HEAD_DIM)).reshape(
        2, 2, 1, CMP_STRIDE * KV_DIM)
    return pos_ext, w1_ext, w2_ext


def _nsa_tables(seq):
    n_blk = seq // SEL_BLOCK
    n_cmp = (seq - CMP_BLOCK) // CMP_STRIDE + 1
    n_cmp_pad = seq // CMP_STRIDE
    r = SEL_BLOCK // CMP_STRIDE
    c = CMP_BLOCK // CMP_STRIDE
    msel = np.zeros((n_cmp_pad, LANES), np.float32)
    for j in range(n_blk):
        for m in range(r):
            for n in range(c):
                idx = r * j + m + n
                if idx < n_cmp:
                    msel[idx, j] += 1.0
    selq = np.zeros((N_KV_HEADS, GQA, GQA * HEAD_DIM, LANES), np.float32)
    for h in range(N_KV_HEADS):
        for g in range(GQA):
            for dd in range(HEAD_DIM):
                selq[h, g, g * HEAD_DIM + dd, h * HEAD_DIM + dd] = 1.0
    return jnp.asarray(msel.T), jnp.asarray(selq, dtype=BF16)


def _odd_params(w_in, mu, w2, a2):
    d = w_in.shape[0]
    z64 = jnp.zeros((d, 64), w_in.dtype)
    w_pad = jnp.concatenate([w_in[:, :1536], w_in[:, 1536:1600], z64, w_in[:, 1600:1664], z64,
                             w_in[:, 1664:]], axis=1).astype(BF16)
    m64 = jnp.zeros((64,), mu.dtype)
    mu_pad = jnp.concatenate([mu[:1536], mu[1536:1600], m64, mu[1600:1664], m64, mu[1664:]])[None, :]
    zr = jnp.zeros((64, RWKV_DIM), w2.dtype)
    w2p = jnp.concatenate([w2, zr], axis=0).astype(BF16)
    a2p = jnp.concatenate([a2, zr], axis=0).astype(BF16)
    return w_pad, mu_pad, w2p, a2p


def _head_ones():
    idx = np.arange(RWKV_DIM) // HEAD_DIM
    return jnp.asarray((idx[:, None] == idx[None, :]).astype(np.float32), dtype=BF16)


def kernel(x, c, ada_w, ada_b, norm_mix, norm_ffn, even_w_in, even_cmp_pos, even_cmp_w1, even_cmp_w2,
           even_conv_w, even_w_out, odd_w_in, odd_mu, odd_w0, odd_w2, odd_a0, odd_a2, odd_g2, odd_k_k,
           odd_k_a, odd_r_k, odd_lnx_w, odd_lnx_b, odd_pool_w, odd_pool_scale, odd_w_out,
           router_w, router_b, moe_w_gate, moe_w_up, moe_w_down, final_norm):
    batch, seq, d = x.shape
    n = batch * seq
    depth = ada_w.shape[0]
    x2 = x.reshape(n, d)
    mod = ada_modulation(c, ada_w, ada_b)
    rwt = router_w.T
    rb = router_b.reshape(N_EXPERTS, 1)
    fnorm = final_norm.reshape(1, d)
    cos, sin = _rope_tables(seq)
    mselt, selq = _nsa_tables(seq)
    ones_bd = _head_ones()

    pending = None
    for layer in range(depth):
        m = mod[layer].reshape(batch, 6, 1, d)
        sh1, sc1, g1, sh2, sc2, g2 = (m[:, k] for k in range(6))
        ng_mix = norm_mix[layer].reshape(1, d)
        ng_ffn = norm_ffn[layer].reshape(1, d)
        i = layer // 2
        if layer % 2 == 0:
            (qn, qr, kc, vc, ks, vs, kw, vw, gate, u, bg) = even_in_proj(
                x2, ng_mix, sh1, sc1, _even_w_pad(even_w_in[i]), cos, sin, seq)
            pos_ext, w1_ext, w2_ext = _compress_params(even_cmp_pos[i], even_cmp_w1[i], even_cmp_w2[i])
            kcmp, vcmp = compress_kv(kc, vc, pos_ext, w1_ext, w2_ext, batch, seq)
            o_nsa = nsa_attention(qn, qr, kcmp, vcmp, ks, vs, kw, vw, gate, selq, mselt, batch, seq)
            x1, hw, ridx, counts = even_out_proj(o_nsa, u, bg, even_conv_w[i], x2, g1,
                                                 even_w_out[i].astype(BF16), ng_ffn, sh2, sc2, rwt, rb, seq)
        else:
            w_pad, mu_pad, w2p, a2p = _odd_params(odd_w_in[i], odd_mu[i], odd_w2[i], odd_a2[i])
            vec = lambda a: a.reshape(1, RWKV_DIM)
            (x2, r, lw, km, v, kn, kb, gg, opool) = odd_in_proj(
                pending, ng_mix, sh1, sc1, w_pad, mu_pad, vec(odd_w0[i]), w2p, vec(odd_a0[i]), a2p,
                odd_g2[i].astype(BF16), vec(odd_k_k[i]), vec(odd_k_a[i]), ones_bd,
                odd_pool_w[i].astype(BF16), vec(odd_pool_scale[i]), seq)
            y = rwkv_scan(r, lw, km, v, kn, kb, batch, seq)
            x1, hw, ridx, counts = odd_out_proj(
                y, r, km, v, gg, opool, vec(odd_r_k[i]), vec(odd_lnx_w[i]), vec(odd_lnx_b[i]), ones_bd,
                x2, g1, odd_w_out[i].astype(BF16), ng_ffn, sh2, sc2, rwt, rb, seq)
        ys, dest3 = moe_routed_experts(hw, ridx, counts, moe_w_gate, moe_w_up, moe_w_down, layer, seq)
        if layer + 1 < depth and (layer + 1) % 2 == 1:
            pending = (ys, dest3, x1, g2)
        else:
            x2 = moe_combine(ys, dest3, x1, g2, fnorm, seq, final_norm=(layer == depth - 1))
    return x2.reshape(batch, seq, d)
```

```python
import functools

import jax
import jax.numpy as jnp
import numpy as np
from jax import lax
from jax.experimental import pallas as pl
from jax.experimental.pallas import tpu as pltpu

F32 = jnp.float32
BF16 = jnp.bfloat16
HIGHEST = lax.Precision.HIGHEST

D_MODEL = 1024
DEPTH = 2
HEAD_DIM = 64
ROPE_THETA = 10000.0
NORM_EPS = 1e-6
NEG_INF = -1e30
BIG = 1e9
NSA_DIM = 512
N_KV_HEADS = 2
GQA = 4
KV_DIM = 128
CMP_BLOCK = 32
CMP_STRIDE = 16
CMP_HIDDEN = 256
SEL_BLOCK = 64
N_SEL = 8
N_LOCAL = 2
WINDOW = 512
NSA_Q_TILE = 256
ATTN_SCALE = HEAD_DIM ** -0.5
LOG2_E = 1.4426950408889634
CONV_DIM = 512
RWKV_DIM = 512
N_RWKV_HEADS = 8
LNX_EPS = 64e-5
POOL_WINDOWS = (2, 4, 8, 16)
POOL_GROUP = 128
N_EXPERTS = 16
N_EXPERT_GROUPS = 4
EXPERTS_PER_GROUP = 4
D_EXPERT = 512
PAIRS_PER_GROUP = 6
PAIR_LO = (0, 0, 0, 1, 1, 2)
PAIR_HI = (1, 2, 3, 3, 2, 3)
N_CLASSES = N_EXPERT_GROUPS * PAIRS_PER_GROUP
CLASS_ROWS = 32

LANES = 128
SUBLANES = 8
VMEM_LIMIT = 56 * 1024 * 1024

TOK_TILE = 512
MOE_ROW_TILE = 256
MOE_DMA_TILE = 2048
SEL_CHUNK = 512
SCAN_CHUNK = 64
SCAN_CHUNKS_PER_STEP = 4
CONV_HALO = 8
POOL_HALO = 16

EVEN_PAD_COLS = 3072
ODD_PAD_COLS = 2432
ODD_RW_COLS = 1920


def _cparams(sem):
    return pltpu.CompilerParams(dimension_semantics=sem, vmem_limit_bytes=VMEM_LIMIT)


def _nt(a, b, precision=None):
    return lax.dot_general(a, b, (((1,), (1,)), ((), ())), preferred_element_type=F32,
                           precision=precision)


def _mm(a, b, precision=None):
    return jnp.dot(a, b, preferred_element_type=F32, precision=precision)


def _norm_mod(x, g, sh, sc):
    ms = jnp.mean(x * x, axis=-1, keepdims=True)
    return (x * lax.rsqrt(ms + NORM_EPS) * g) * (1.0 + sc) + sh


def _split_sum(x, ones_bf16):
    hi = x.astype(BF16)
    lo = (x - hi.astype(F32)).astype(BF16)
    return _mm(hi, ones_bf16) + _mm(lo, ones_bf16)


def _ada_kernel(c_ref, w_ref, b_ref, o_ref):
    c = c_ref[...]
    cond = c * jax.nn.sigmoid(c)
    o_ref[0] = _mm(cond, w_ref[0], precision=HIGHEST) + b_ref[0]


def ada_modulation(c, ada_w, ada_b):
    depth, d, cols = ada_w.shape
    b = c.shape[0]
    tn = 1536
    return pl.pallas_call(
        _ada_kernel,
        out_shape=jax.ShapeDtypeStruct((depth, b, cols), F32),
        grid=(depth, cols // tn),
        in_specs=[pl.BlockSpec((b, d), lambda l, j: (0, 0)),
                  pl.BlockSpec((1, d, tn), lambda l, j: (l, 0, j)),
                  pl.BlockSpec((1, 1, tn), lambda l, j: (l, 0, j))],
        out_specs=pl.BlockSpec((1, b, tn), lambda l, j: (l, 0, j)),
        compiler_params=_cparams(("parallel", "parallel")),
        name="ada_modulation",
    )(c, ada_w, ada_b.reshape(depth, 1, cols))


def _rope128(t, cos, sin, lane):
    rot = jnp.where((lane % HEAD_DIM) < HEAD_DIM // 2,
                    -pltpu.roll(t, LANES - HEAD_DIM // 2, 1), pltpu.roll(t, HEAD_DIM // 2, 1))
    return t * cos + rot * sin


def _even_in_kernel(x_ref, g_ref, sh_ref, sc_ref, w_ref, cos_ref, sin_ref,
                    qn_ref, qr_ref, kc_ref, vc_ref, ks_ref, vs_ref, kw_ref, vw_ref,
                    gate_ref, u_ref, bg_ref, *, tiles_per_seq):
    h = _norm_mod(x_ref[...], g_ref[...], sh_ref[0], sc_ref[0])
    tm = h.shape[0]
    proj = _mm(h.astype(BF16), w_ref[...])
    cos = cos_ref[...]
    sin = sin_ref[...]
    lane = lax.broadcasted_iota(jnp.int32, (1, LANES), 1)
    low = lane < HEAD_DIM
    for i in range(NSA_DIM // LANES):
        q = proj[:, i * LANES:(i + 1) * LANES] * (ATTN_SCALE * LOG2_E)
        qn_ref[:, i * LANES:(i + 1) * LANES] = q.astype(BF16)
        qr_ref[:, i * LANES:(i + 1) * LANES] = _rope128(q, cos, sin, lane).astype(BF16)
    o = NSA_DIM
    kc_ref[...] = proj[:, o:o + 128]
    vc_ref[...] = proj[:, o + 128:o + 256]
    pos = (pl.program_id(0) % tiles_per_seq) * tm + lax.broadcasted_iota(jnp.int32, (tm, 1), 0)
    blk = pos // SEL_BLOCK
    ks = _rope128(proj[:, o + 256:o + 384], cos, sin, lane)
    ks_ref[:, 0:LANES] = jnp.where(low, ks, jnp.where(lane - HEAD_DIM == blk, 1.0, 0.0)).astype(BF16)
    ks_ref[:, LANES:2 * LANES] = jnp.where(low, jnp.where(lane == blk, 1.0, 0.0), ks).astype(BF16)
    vs = proj[:, o + 384:o + 512]
    vs_ref[:, 0:LANES] = jnp.where(low, vs, 1.0).astype(BF16)
    vs_ref[:, LANES:2 * LANES] = jnp.where(low, 1.0, vs).astype(BF16)
    kw_ref[...] = _rope128(proj[:, o + 512:o + 640], cos, sin, lane).astype(BF16)
    vw = proj[:, o + 640:o + 768]
    vw_ref[:, 0:LANES] = jnp.where(low, vw, 1.0).astype(BF16)
    vw_ref[:, LANES:2 * LANES] = jnp.where(low, 1.0, vw).astype(BF16)
    o += 768
    gate_ref[...] = jax.nn.sigmoid(proj[:, o:o + 256])
    o += 256
    xb = proj[:, o:o + 512]
    bg_ref[...] = proj[:, o + 512:o + 1024]
    u_ref[...] = proj[:, o + 1024:o + 1536] * xb


def even_in_proj(x2, g, sh, sc, w_pad, cos, sin, seq):
    n, d = x2.shape
    tm = min(TOK_TILE, seq)
    tpb = seq // tm
    row = lambda i: (i, 0)
    per_b = lambda i: (i // tpb, 0, 0)
    pos = lambda i: (i % tpb, 0)
    outs = [((n, 512), BF16), ((n, 512), BF16), ((n, 128), F32), ((n, 128), F32),
            ((n, 256), BF16), ((n, 256), BF16), ((n, 128), BF16), ((n, 256), BF16),
            ((n, 256), F32), ((n, 512), F32), ((n, 512), F32)]
    return pl.pallas_call(
        functools.partial(_even_in_kernel, tiles_per_seq=tpb),
        out_shape=[jax.ShapeDtypeStruct(s, t) for s, t in outs],
        grid=(n // tm,),
        in_specs=[pl.BlockSpec((tm, d), row),
                  pl.BlockSpec((1, d), lambda i: (0, 0)),
                  pl.BlockSpec((1, 1, d), per_b),
                  pl.BlockSpec((1, 1, d), per_b),
                  pl.BlockSpec((d, EVEN_PAD_COLS), lambda i: (0, 0)),
                  pl.BlockSpec((tm, LANES), pos),
                  pl.BlockSpec((tm, LANES), pos)],
        out_specs=[pl.BlockSpec((tm, s[1]), row) for s, _ in outs],
        compiler_params=_cparams(("parallel",)),
        name="even_in_proj",
    )(x2, g, sh, sc, w_pad, cos, sin)


def _compress_kernel(k_ref, v_ref, pos_ref, w1_ref, w2_ref, ko_ref, vo_ref):
    n_rows = k_ref.shape[1] // CMP_STRIDE
    for j, (src, dst) in enumerate(((k_ref, ko_ref), (v_ref, vo_ref))):
        a0 = jnp.zeros((n_rows, N_KV_HEADS * CMP_HIDDEN), F32)
        a1 = jnp.zeros((n_rows, N_KV_HEADS * CMP_HIDDEN), F32)
        for m in range(CMP_STRIDE):
            cols = slice(m * KV_DIM, (m + 1) * KV_DIM)
            x_m = src.at[0][pl.ds(m, n_rows, stride=CMP_STRIDE), :]
            a0 = a0 + _mm((x_m + pos_ref[j, 0, :, cols]).astype(BF16), w1_ref[j, 0, cols, :])
            a1 = a1 + _mm((x_m + pos_ref[j, 1, :, cols]).astype(BF16), w1_ref[j, 1, cols, :])
        hid = a0 + pltpu.roll(a1, n_rows - 1, 0)
        hid = jax.nn.gelu(hid)
        dst[0] = _mm(hid.astype(BF16), w2_ref[j]).astype(BF16)


def compress_kv(kc, vc, pos_ext, w1_ext, w2_ext, batch, seq):
    rows = seq // CMP_STRIDE
    kr = kc.reshape(batch, seq, KV_DIM)
    vr = vc.reshape(batch, seq, KV_DIM)
    blk = pl.BlockSpec((1, seq, KV_DIM), lambda b: (b, 0, 0))
    oblk = pl.BlockSpec((1, rows, KV_DIM), lambda b: (b, 0, 0))
    return pl.pallas_call(
        _compress_kernel,
        out_shape=[jax.ShapeDtypeStruct((batch, rows, KV_DIM), BF16)] * 2,
        grid=(batch,),
        in_specs=[blk, blk,
                  pl.BlockSpec(pos_ext.shape, lambda b: (0, 0, 0, 0)),
                  pl.BlockSpec(w1_ext.shape, lambda b: (0, 0, 0, 0)),
                  pl.BlockSpec(w2_ext.shape, lambda b: (0, 0, 0))],
        out_specs=[oblk, oblk],
        compiler_params=_cparams(("parallel",)),
        name="compress_kv",
    )(kr, vr, pos_ext, w1_ext, w2_ext)


def _safe_inv(l):
    return jnp.where(l > 0.0, 1.0 / jnp.where(l > 0.0, l, 1.0), 0.0)


def _nsa_kernel(qn_ref, qr_ref, kc_ref, vc_ref, ks_ref, vs_ref, kw_ref, vw_ref, gate_ref,
                selq_ref, mselt_ref, o_ref, *, seq, n_sel, sel_chunk, win_len):
    h = pl.program_id(1)
    qt = pl.program_id(2)
    t0 = qt * NSA_Q_TILE
    n_blk = seq // SEL_BLOCK
    n_cmp_pad = seq // CMP_STRIDE
    rows = GQA * NSA_Q_TILE
    tpos = t0 + lax.broadcasted_iota(jnp.int32, (1, NSA_Q_TILE, 1), 1)
    lane = lax.broadcasted_iota(jnp.int32, (1, LANES), 1)
    head_lanes = (lane // HEAD_DIM) == h

    def normalise(acc):
        return acc * _safe_inv(pltpu.roll(acc, HEAD_DIM, 1))

    qn = qn_ref[...]
    qr = qr_ref[...]
    qn4 = jnp.concatenate([_mm(qn, selq_ref[0, g]) for g in range(GQA)], axis=0).astype(BF16)
    qr4f = jnp.concatenate([_mm(qr, selq_ref[0, g]) for g in range(GQA)], axis=0)
    qr4 = qr4f.astype(BF16)

    kc = kc_ref[0]
    vc = vc_ref[0]
    cpos = lax.broadcasted_iota(jnp.int32, (1, 1, n_cmp_pad), 2) * CMP_STRIDE + (CMP_BLOCK - 1)
    cmask = cpos <= tpos
    s = jnp.where(cmask, _nt(qn4, kc).reshape(GQA, NSA_Q_TILE, n_cmp_pad), NEG_INF)
    e = jnp.where(cmask, jnp.exp2(s - jnp.max(s, axis=2, keepdims=True)), 0.0)
    p = e * _safe_inv(jnp.sum(e, axis=2, keepdims=True))
    imp = jnp.sum(p, axis=0)
    o_cmp = _mm(p.reshape(rows, n_cmp_pad).astype(BF16), vc)

    pslc = _nt(mselt_ref[...], imp, precision=HIGHEST)[:n_blk]
    tq = t0 + lax.broadcasted_iota(jnp.int32, (1, NSA_Q_TILE), 1)
    jblk = lax.broadcasted_iota(jnp.int32, (n_blk, 1), 0)
    cur = tq // SEL_BLOCK
    valid = jblk * SEL_BLOCK <= tq
    forced = (jblk == 0) | ((cur - jblk >= 0) & (cur - jblk < N_LOCAL))
    score = jnp.where(forced, BIG, jnp.where(valid, pslc, -BIG))
    rank = jnp.zeros((n_blk, NSA_Q_TILE), jnp.int32)
    for jp in range(n_blk):
        row = score[jp:jp + 1, :]
        beats = (row > score) | ((row == score) & (jblk > jp))
        rank = rank + beats.astype(jnp.int32)
    selb = jnp.where((rank < n_sel) & (score > -0.5 * BIG), 0.0, NEG_INF)
    selb_q = jnp.concatenate([selb, jnp.zeros((LANES - n_blk, NSA_Q_TILE), F32)], axis=0).T
    bias = jnp.where(h == 0, pltpu.roll(selb_q, HEAD_DIM, 1), selb_q)
    qs4 = (qr4f + jnp.concatenate([bias] * GQA, axis=0)).astype(BF16)

    def chunk_scores(c):
        start = pl.multiple_of(c * sel_chunk, sel_chunk)
        kblk = ks_ref[0, pl.ds(start, sel_chunk), :]
        vblk = vs_ref[0, pl.ds(start, sel_chunk), :]
        return start, _nt(qs4, kblk).reshape(GQA, NSA_Q_TILE, sel_chunk), vblk

    def online_update(carry, s, vblk):
        m, acc = carry
        m_new = jnp.maximum(m, jnp.max(s, axis=2, keepdims=True))
        alpha = jnp.exp2(m - m_new)
        p = jnp.exp2(s - m_new)
        pv = _mm(p.reshape(rows, sel_chunk).astype(BF16), vblk).reshape(GQA, NSA_Q_TILE, LANES)
        return m_new, alpha * acc + pv

    def sel_body(c, carry):
        _, s, vblk = chunk_scores(c)
        return online_update(carry, s, vblk)

    diag_chunk = (t0 + NSA_Q_TILE - 1) // sel_chunk
    init = (jnp.full((GQA, NSA_Q_TILE, 1), NEG_INF, F32), jnp.zeros((GQA, NSA_Q_TILE, LANES), F32))
    carry = lax.fori_loop(0, diag_chunk, sel_body, init)
    start, s, vblk = chunk_scores(diag_chunk)
    kpos = start + lax.broadcasted_iota(jnp.int32, (1, 1, sel_chunk), 2)
    _, acc = online_update(carry, jnp.where(kpos <= tpos, s, NEG_INF), vblk)
    o_slc = normalise(acc.reshape(rows, LANES))

    ws = pl.multiple_of(jnp.maximum(qt - WINDOW // NSA_Q_TILE, 0) * NSA_Q_TILE, NSA_Q_TILE)
    kwb = kw_ref[0, pl.ds(ws, win_len), :]
    vwb = vw_ref[0, pl.ds(ws, win_len), :]
    diff = tpos - (ws + lax.broadcasted_iota(jnp.int32, (1, 1, win_len), 2))
    wmask = (diff >= 0) & (diff < WINDOW)
    s = jnp.where(wmask, _nt(qr4, kwb).reshape(GQA, NSA_Q_TILE, win_len), NEG_INF)
    e = jnp.exp2(s - jnp.max(s, axis=2, keepdims=True))
    o_win = normalise(_mm(e.reshape(rows, win_len).astype(BF16), vwb))

    gate = gate_ref[...]
    og = []
    for g in range(GQA):
        sl = slice(g * NSA_Q_TILE, (g + 1) * NSA_Q_TILE)
        o = (gate[:, 3 * g:3 * g + 1] * o_cmp[sl] + gate[:, 3 * g + 1:3 * g + 2] * o_slc[sl]
             + gate[:, 3 * g + 2:3 * g + 3] * o_win[sl])
        og.append(jnp.where(head_lanes, o, pltpu.roll(o, HEAD_DIM, 1)))
    low = lane < HEAD_DIM
    o_ref[:, 0:LANES] = jnp.where(low, og[0], og[1]).astype(BF16)
    o_ref[:, LANES:2 * LANES] = jnp.where(low, og[2], og[3]).astype(BF16)


def nsa_attention(qn, qr, kcmp, vcmp, ks, vs, kw, vw, gate, selq, mselt, batch, seq):
    n = batch * seq
    nq = seq // NSA_Q_TILE
    sel_chunk = min(SEL_CHUNK, seq)
    win_len = min(WINDOW + NSA_Q_TILE, seq)
    n_sel = min(N_SEL, seq // SEL_BLOCK)
    qspec = pl.BlockSpec((NSA_Q_TILE, GQA * HEAD_DIM), lambda b, h, q: (b * nq + q, h))
    cspec = pl.BlockSpec((1, seq // CMP_STRIDE, KV_DIM), lambda b, h, q: (b, 0, 0))
    both = pl.BlockSpec((1, seq, KV_DIM), lambda b, h, q: (b, 0, 0))
    mine = pl.BlockSpec((1, seq, KV_DIM), lambda b, h, q: (b, 0, h))
    kern = functools.partial(_nsa_kernel, seq=seq, n_sel=n_sel, sel_chunk=sel_chunk, win_len=win_len)
    return pl.pallas_call(
        kern,
        out_shape=jax.ShapeDtypeStruct((n, NSA_DIM), BF16),
        grid=(batch, N_KV_HEADS, nq),
        in_specs=[qspec, qspec, cspec, cspec, mine, mine, both, mine,
                  pl.BlockSpec((NSA_Q_TILE, LANES), lambda b, h, q: (b * nq + q, h)),
                  pl.BlockSpec((1, GQA, GQA * HEAD_DIM, LANES), lambda b, h, q: (h, 0, 0, 0)),
                  pl.BlockSpec(mselt.shape, lambda b, h, q: (0, 0))],
        out_specs=qspec,
        compiler_params=_cparams(("parallel", "parallel", "arbitrary")),
        name="nsa_attention",
    )(qn, qr, kcmp, vcmp, ks.reshape(batch, seq, 2 * KV_DIM), vs.reshape(batch, seq, 2 * KV_DIM),
      kw.reshape(batch, seq, KV_DIM), vw.reshape(batch, seq, 2 * KV_DIM), gate, selq, mselt)


def _route(h2, rwt_ref, rb_ref):
    rw = rwt_ref[...]
    rw_hi = rw.astype(BF16)
    rw_lo = (rw - rw_hi.astype(F32)).astype(BF16)
    h_hi = h2.astype(BF16)
    h_lo = (h2 - h_hi.astype(F32)).astype(BF16)
    logits = _nt(rw_hi, h_hi) + (_nt(rw_hi, h_lo) + _nt(rw_lo, h_hi))
    scores = jax.nn.sigmoid(logits)
    biased = scores + rb_ref[...]
    rows = [biased[e:e + 1, :] for e in range(N_EXPERTS)]
    srow = [scores[e:e + 1, :] for e in range(N_EXPERTS)]
    gscore = []
    for gi in range(N_EXPERT_GROUPS):
        r = rows[gi * EXPERTS_PER_GROUP:(gi + 1) * EXPERTS_PER_GROUP]
        best = None
        for a in range(EXPERTS_PER_GROUP):
            for b in range(a + 1, EXPERTS_PER_GROUP):
                pair = r[a] + r[b]
                best = pair if best is None else jnp.maximum(best, pair)
        gscore.append(best)
    top_val = gscore[0]
    top_grp = jnp.zeros_like(top_val, dtype=jnp.int32)
    for gi in range(1, N_EXPERT_GROUPS):
        upd = gscore[gi] > top_val
        top_grp = jnp.where(upd, gi, top_grp)
        top_val = jnp.where(upd, gscore[gi], top_val)
    masked = [jnp.where(top_grp == e // EXPERTS_PER_GROUP, rows[e], NEG_INF) for e in range(N_EXPERTS)]
    b1 = masked[0]
    i1 = jnp.zeros_like(top_grp)
    for e in range(1, N_EXPERTS):
        upd = masked[e] > b1
        i1 = jnp.where(upd, e, i1)
        b1 = jnp.where(upd, masked[e], b1)
    b2 = None
    i2 = None
    for e in range(N_EXPERTS):
        v = jnp.where(i1 == e, -jnp.inf, masked[e])
        if b2 is None:
            b2, i2 = v, jnp.zeros_like(top_grp)
        else:
            upd = v > b2
            i2 = jnp.where(upd, e, i2)
            b2 = jnp.where(upd, v, b2)
    s1 = jnp.zeros_like(top_val)
    s2 = jnp.zeros_like(top_val)
    for e in range(N_EXPERTS):
        s1 = s1 + jnp.where(i1 == e, srow[e], 0.0)
        s2 = s2 + jnp.where(i2 == e, srow[e], 0.0)
    tot = s1 + s2
    return i1, i2, s1 / tot, s2 / tot


def _tail(cat_bf16, x_ref, g1_ref, wout_ref, ng_ref, sh2_ref, sc2_ref, rwt_ref, rb_ref, ustrict_ref,
          x1_ref, hw_ref, ridx_ref, cnt_ref, carry_ref):
    y = _mm(cat_bf16, wout_ref[...])
    x1 = x_ref[...] + g1_ref[0] * y
    x1_ref[...] = x1
    h2 = _norm_mod(x1, ng_ref[...], sh2_ref[0], sc2_ref[0])
    tm, d = h2.shape
    hw_ref[:, :d] = h2

    i1, i2, w1, w2 = _route(h2, rwt_ref, rb_ref)
    lo = jnp.minimum(i1, i2) % EXPERTS_PER_GROUP
    hi = jnp.maximum(i1, i2) % EXPERTS_PER_GROUP
    pair = jnp.where(lo == 0, hi - 1, jnp.where(lo == 1, jnp.where(hi == 3, 3, 4), 5))
    cls = (i1 // EXPERTS_PER_GROUP) * PAIRS_PER_GROUP + pair
    w_lo = jnp.where(i1 < i2, w1, w2)
    w_hi = jnp.where(i1 < i2, w2, w1)
    meta_t = jnp.concatenate([w_lo, w_hi, jnp.zeros((LANES - 2, tm), F32)], axis=0)
    hw_ref[:, d:] = meta_t.T

    @pl.when(pl.program_id(0) == 0)
    def _():
        carry_ref[...] = jnp.zeros_like(carry_ref)

    hit = lax.broadcasted_iota(jnp.int32, (CLASS_ROWS, 1), 0) == cls
    cnt = jnp.where(hit, 1.0, 0.0)
    before = _mm(cnt.astype(BF16), ustrict_ref[...]) + carry_ref[:, 0:1]
    carry = carry_ref[...] + jnp.sum(cnt, axis=1, keepdims=True)
    carry_ref[...] = carry
    cnt_ref[...] = carry.astype(jnp.int32)
    ridx_ref[0:1, :] = cls
    ridx_ref[1:2, :] = jnp.sum(jnp.where(hit, before, 0.0), axis=0, keepdims=True).astype(jnp.int32)


def _even_out_kernel(o_ref, u_ref, uh_ref, bg_ref, cw_ref, *tail_refs, tiles_per_seq):
    first = (pl.program_id(0) % tiles_per_seq) == 0
    u = u_ref[...]
    halo = jnp.where(first, 0.0, uh_ref[...])
    ext = jnp.concatenate([halo, u], axis=0)
    u1 = pltpu.roll(ext, 1, 0)[CONV_HALO:]
    u2 = pltpu.roll(ext, 2, 0)[CONV_HALO:]
    cw = cw_ref[...]
    y_conv = bg_ref[...] * (cw[2:3] * u + cw[1:2] * u1 + cw[0:1] * u2)
    cat = jnp.concatenate([o_ref[...], y_conv.astype(BF16)], axis=1)
    _tail(cat, *tail_refs)


def _tail_specs(tm, d, tpb):
    row = lambda i: (i, 0)
    per_b = lambda i: (i // tpb, 0, 0)
    const2 = lambda i: (0, 0)
    ins = [pl.BlockSpec((tm, d), row),
           pl.BlockSpec((1, 1, d), per_b),
           pl.BlockSpec((d, d), const2),
           pl.BlockSpec((1, d), const2),
           pl.BlockSpec((1, 1, d), per_b),
           pl.BlockSpec((1, 1, d), per_b),
           pl.BlockSpec((N_EXPERTS, d), const2),
           pl.BlockSpec((N_EXPERTS, 1), const2),
           pl.BlockSpec((tm, tm), const2)]
    outs = [pl.BlockSpec((tm, d), row), pl.BlockSpec((tm, d + LANES), row),
            pl.BlockSpec((2, tm), lambda i: (0, i)),
            pl.BlockSpec((CLASS_ROWS, LANES), const2)]
    scratch = [pltpu.VMEM((CLASS_ROWS, LANES), F32)]
    return ins, outs, scratch


def _tail_out_shapes(n, d):
    return [jax.ShapeDtypeStruct((n, d), F32), jax.ShapeDtypeStruct((n, d + LANES), F32),
            jax.ShapeDtypeStruct((2, n), jnp.int32), jax.ShapeDtypeStruct((CLASS_ROWS, LANES), jnp.int32)]


def _strict_upper(tm):
    return jnp.asarray(np.triu(np.ones((tm, tm), np.float32), 1), dtype=BF16)


def even_out_proj(o_nsa, u, bg, conv_w, x2, g1, w_out, ng, sh2, sc2, rwt, rb, seq):
    n, d = x2.shape
    tm = min(TOK_TILE, seq)
    tpb = seq // tm
    row = lambda i: (i, 0)
    halo = lambda i: (jnp.maximum(i * (tm // CONV_HALO) - 1, 0), 0)
    tin, tout, tscratch = _tail_specs(tm, d, tpb)
    return pl.pallas_call(
        functools.partial(_even_out_kernel, tiles_per_seq=tpb),
        out_shape=_tail_out_shapes(n, d),
        grid=(n // tm,),
        in_specs=[pl.BlockSpec((tm, NSA_DIM), row),
                  pl.BlockSpec((tm, CONV_DIM), row),
                  pl.BlockSpec((CONV_HALO, CONV_DIM), halo),
                  pl.BlockSpec((tm, CONV_DIM), row),
                  pl.BlockSpec(conv_w.shape, lambda i: (0, 0))] + tin,
        out_specs=tout,
        scratch_shapes=tscratch,
        compiler_params=_cparams(("arbitrary",)),
        name="even_out_proj",
    )(o_nsa, u, u, bg, conv_w, x2, g1, w_out, ng, sh2, sc2, rwt, rb, _strict_upper(tm))


def _dispatch_plan(ridx, counts, n, seq):
    cnt = counts[:N_CLASSES, 0]
    padded = (cnt + MOE_ROW_TILE - 1) // MOE_ROW_TILE * MOE_ROW_TILE
    ends = jnp.cumsum(padded)
    starts = ends - padded
    cids = jnp.arange(N_CLASSES, dtype=jnp.int32)[:, None]
    base = jnp.sum(jnp.where(cids == ridx[0][None, :], starts[:, None], 0), axis=0)
    dest = (base + ridx[1]).astype(jnp.int32)
    td = min(MOE_DMA_TILE, seq)
    dest3 = dest.reshape(n // td, 1, td)
    n_tiles = n // MOE_ROW_TILE + N_CLASSES
    tile_start = jnp.arange(n_tiles, dtype=jnp.int32) * MOE_ROW_TILE
    tile_class = jnp.minimum(jnp.sum(tile_start[:, None] >= ends[None, :], axis=1), N_CLASSES - 1)
    group_base = (tile_class // PAIRS_PER_GROUP) * EXPERTS_PER_GROUP
    pair = tile_class % PAIRS_PER_GROUP
    tile_lo = (group_base + jnp.asarray(PAIR_LO, jnp.int32)[pair]).astype(jnp.int32)
    tile_hi = (group_base + jnp.asarray(PAIR_HI, jnp.int32)[pair]).astype(jnp.int32)
    n_used = (ends[-1] // MOE_ROW_TILE).reshape(1).astype(jnp.int32)
    last_tile = jnp.where(cnt > 0, ends // MOE_ROW_TILE - 1, -1)
    tail = n_used[0] + jnp.arange(N_CLASSES, dtype=jnp.int32)
    zero_tiles = jnp.concatenate([last_tile, jnp.where(tail < n_tiles, tail, -1)]).astype(jnp.int32)
    return dest3, tile_lo, tile_hi, n_used, zero_tiles, n_tiles


def _dispatch_kernel(ztile_ref, dest_ref, hw_ref, xs_hbm, zbuf, zsem, sem):
    td = hw_ref.shape[0]

    @pl.when(pl.program_id(0) == 0)
    def _():
        zbuf[...] = jnp.zeros_like(zbuf)

        def zero_copy(k):
            start = pl.multiple_of(ztile_ref[k] * MOE_ROW_TILE, MOE_ROW_TILE)
            return pltpu.make_async_copy(zbuf, xs_hbm.at[pl.ds(start, MOE_ROW_TILE)], zsem)

        for k in range(2 * N_CLASSES):
            @pl.when(ztile_ref[k] >= 0)
            def _():
                zero_copy(k).start()
        for k in range(2 * N_CLASSES):
            @pl.when(ztile_ref[k] >= 0)
            def _():
                zero_copy(k).wait()

    for r in range(td):
        pltpu.make_async_copy(hw_ref.at[pl.ds(r, 1)],
                              xs_hbm.at[pl.ds(dest_ref[0, 0, r], 1)], sem).start()
    pltpu.make_async_copy(hw_ref, xs_hbm.at[pl.ds(0, td)], sem).wait()


def moe_dispatch(hw, dest3, zero_tiles, n_tiles):
    n, cols = hw.shape
    td = dest3.shape[2]
    rows = n_tiles * MOE_ROW_TILE
    return pl.pallas_call(
        _dispatch_kernel,
        out_shape=jax.ShapeDtypeStruct((rows, cols), F32),
        grid_spec=pltpu.PrefetchScalarGridSpec(
            num_scalar_prefetch=1,
            grid=(n // td,),
            in_specs=[pl.BlockSpec((1, 1, td), lambda i, z: (i, 0, 0), memory_space=pltpu.SMEM),
                      pl.BlockSpec((td, cols), lambda i, z: (i, 0))],
            out_specs=pl.BlockSpec(memory_space=pl.ANY),
            scratch_shapes=[pltpu.VMEM((MOE_ROW_TILE, cols), F32), pltpu.SemaphoreType.DMA(()),
                            pltpu.SemaphoreType.DMA(())]),
        compiler_params=_cparams(("arbitrary",)),
        name="moe_dispatch",
    )(zero_tiles, dest3, hw)


def _expert_kernel(lo_ref, hi_ref, nused_ref, xs_ref, wg_lo, wu_lo, wd_lo, wg_hi, wu_hi, wd_hi, ys_ref,
                   *wb):
    t = pl.program_id(0)
    prev = jnp.maximum(t - 1, 0)

    for ids, srcs, dsts in ((lo_ref, (wg_lo, wu_lo, wd_lo), wb[:3]), (hi_ref, (wg_hi, wu_hi, wd_hi), wb[3:])):
        @pl.when((t == 0) | (ids[t] != ids[prev]))
        def _():
            for src, dst in zip(srcs, dsts):
                dst[...] = src[0, 0].astype(BF16)

    @pl.when(t < nused_ref[0])
    def _():
        d = xs_ref.shape[1] - LANES
        x = xs_ref[:, :d].astype(BF16)
        meta = xs_ref[:, d:]
        y = None
        for k in range(2):
            a = _mm(x, wb[3 * k][...])
            b = _mm(x, wb[3 * k + 1][...])
            he = (a * jax.nn.sigmoid(a)) * b
            yk = meta[:, k:k + 1] * _mm(he.astype(BF16), wb[3 * k + 2][...])
            y = yk if y is None else y + yk
        ys_ref[...] = y

    @pl.when(t >= nused_ref[0])
    def _():
        ys_ref[...] = jnp.zeros_like(ys_ref)


def moe_experts(xs, tile_lo, tile_hi, n_used, w_gate, w_up, w_down, layer, d):
    rows, cols = xs.shape
    n_tiles = rows // MOE_ROW_TILE
    lo_spec = lambda shape: pl.BlockSpec((1, 1) + shape, lambda t, lo, hi, nu: (layer, lo[t], 0, 0))
    hi_spec = lambda shape: pl.BlockSpec((1, 1) + shape, lambda t, lo, hi, nu: (layer, hi[t], 0, 0))
    shapes = ((d, D_EXPERT), (d, D_EXPERT), (D_EXPERT, d))
    return pl.pallas_call(
        _expert_kernel,
        out_shape=jax.ShapeDtypeStruct((rows, d), F32),
        grid_spec=pltpu.PrefetchScalarGridSpec(
            num_scalar_prefetch=3,
            grid=(n_tiles,),
            in_specs=[pl.BlockSpec((MOE_ROW_TILE, cols),
                                   lambda t, lo, hi, nu: (jnp.minimum(t, nu[0] - 1), 0))]
                     + [lo_spec(s) for s in shapes] + [hi_spec(s) for s in shapes],
            out_specs=pl.BlockSpec((MOE_ROW_TILE, d), lambda t, lo, hi, nu: (t, 0)),
            scratch_shapes=[pltpu.VMEM(s, BF16) for s in shapes + shapes]),
        compiler_params=_cparams(("arbitrary",)),
        name="moe_experts",
    )(tile_lo, tile_hi, n_used, xs, w_gate, w_up, w_down, w_gate, w_up, w_down)


def _combine_kernel(dest_ref, ys_hbm, x_ref, g2_ref, fn_ref, o_ref, buf, sem, *, final_norm):
    tc = x_ref.shape[0]

    for r in range(tc):
        pltpu.make_async_copy(ys_hbm.at[pl.ds(dest_ref[0, 0, r], 1)], buf.at[pl.ds(r, 1)], sem).start()
    pltpu.make_async_copy(ys_hbm.at[pl.ds(0, tc)], buf, sem).wait()
    x = x_ref[...] + g2_ref[0] * buf[...]
    if final_norm:
        ms = jnp.mean(x * x, axis=-1, keepdims=True)
        x = x * lax.rsqrt(ms + NORM_EPS) * fn_ref[...]
    o_ref[...] = x


def moe_combine(ys, dest3, x1, g2, fnorm, seq, final_norm):
    n, d = x1.shape
    tc = dest3.shape[2]
    tpb = seq // tc
    return pl.pallas_call(
        functools.partial(_combine_kernel, final_norm=final_norm),
        out_shape=jax.ShapeDtypeStruct((n, d), F32),
        grid=(n // tc,),
        in_specs=[pl.BlockSpec((1, 1, tc), lambda i: (i, 0, 0), memory_space=pltpu.SMEM),
                  pl.BlockSpec(memory_space=pl.ANY),
                  pl.BlockSpec((tc, d), lambda i: (i, 0)),
                  pl.BlockSpec((1, 1, d), lambda i: (i // tpb, 0, 0)),
                  pl.BlockSpec((1, d), lambda i: (0, 0))],
        out_specs=pl.BlockSpec((tc, d), lambda i: (i, 0)),
        scratch_shapes=[pltpu.VMEM((tc, d), F32), pltpu.SemaphoreType.DMA(())],
        compiler_params=_cparams(("arbitrary",)),
        name="moe_combine",
    )(dest3, ys, x1, g2, fnorm)


def moe_routed_experts(hw, ridx, counts, w_gate, w_up, w_down, layer, seq):
    n = hw.shape[0]
    d = hw.shape[1] - LANES
    dest3, tile_lo, tile_hi, n_used, zero_tiles, n_tiles = _dispatch_plan(ridx, counts, n, seq)
    xs = moe_dispatch(hw, dest3, zero_tiles, n_tiles)
    ys = moe_experts(xs, tile_lo, tile_hi, n_used, w_gate, w_up, w_down, layer, d)
    return ys, dest3


def _odd_in_kernel(dcur_ref, dnxt_ref, ys_hbm, x1_ref, gm_ref, g_ref, sh_ref, sc_ref, w_ref, mu_ref,
                   w0_ref, w2_ref, a0_ref, a2_ref, g2_ref, kk_ref, ka_ref, ones_ref, pw_ref, ps_ref,
                   x_ref, r_ref, lw_ref, km_ref, v_ref, kn_ref, kb_ref, gg_ref, op_ref,
                   rw_carry, u_carry, ybuf, ysem, *, tiles_per_seq, tm, n_steps):
    i = pl.program_id(0)
    first = (i % tiles_per_seq) == 0
    slot = i % 2

    def gather(dest_ref, to_slot):
        for r in range(tm):
            pltpu.make_async_copy(ys_hbm.at[pl.ds(dest_ref[0, 0, r], 1)],
                                  ybuf.at[to_slot, pl.ds(r, 1)], ysem.at[to_slot]).start()

    @pl.when(i == 0)
    def _():
        gather(dcur_ref, 0)

    @pl.when(i + 1 < n_steps)
    def _():
        gather(dnxt_ref, 1 - slot)

    pltpu.make_async_copy(ys_hbm.at[pl.ds(0, tm)], ybuf.at[slot], ysem.at[slot]).wait()
    x = x1_ref[...] + gm_ref[0] * ybuf[slot]
    x_ref[...] = x
    h = _norm_mod(x, g_ref[...], sh_ref[0], sc_ref[0])
    proj = _mm(h.astype(BF16), w_ref[...])

    rw = proj[:, :ODD_RW_COLS]
    row0 = jnp.where(first, 0.0, rw_carry[0:1, :])
    ridx = lax.broadcasted_iota(jnp.int32, (tm, 1), 0)
    prev = jnp.where(ridx == 0, row0, pltpu.roll(rw, 1, 0))
    rw_carry[0:1, :] = rw[tm - 1:tm, :]
    rw = rw + (prev - rw) * mu_ref[...]

    r = rw[:, 0:512]
    k = rw[:, 512:1024]
    v = rw[:, 1024:1536]
    wl = rw[:, 1536:1664]
    al = rw[:, 1664:1792]
    gl = rw[:, 1792:1920]
    z = -(w0_ref[...] + _mm(jnp.tanh(wl).astype(BF16), w2_ref[...]))
    softplus = jnp.maximum(z, 0.0) + jnp.log1p(jnp.exp(-jnp.abs(z)))
    w_log = -softplus - 0.5
    a = jax.nn.sigmoid(a0_ref[...] + _mm(al.astype(BF16), a2_ref[...]))
    gg_ref[...] = _mm(jax.nn.sigmoid(gl).astype(BF16), g2_ref[...])
    kk0 = k * kk_ref[...]
    ss = _split_sum(kk0 * kk0, ones_ref[...])
    kk = kk0 / jnp.maximum(jnp.sqrt(ss), 1e-12)
    r_ref[...] = r
    lw_ref[...] = -jnp.exp(w_log)
    km_ref[...] = k * (1.0 + (a - 1.0) * ka_ref[...])
    v_ref[...] = v
    kn_ref[...] = kk
    kb_ref[...] = kk * a

    u = proj[:, ODD_RW_COLS:]
    halo = jnp.where(first, 0.0, u_carry[...])
    u_carry[...] = u[tm - POOL_HALO:, :]
    ext = jnp.concatenate([halo, u], axis=0)
    tseq = (i % tiles_per_seq) * tm + ridx
    for gi, win in enumerate(POOL_WINDOWS):
        xg = ext[:, gi * POOL_GROUP:(gi + 1) * POOL_GROUP]
        s = xg
        step = 1
        while step < win:
            s = s + pltpu.roll(s, step, 0)
            step *= 2
        cnt = jnp.minimum(tseq + 1, win).astype(F32)
        pooled = s[POOL_HALO:] / cnt - xg[POOL_HALO:]
        mixed = _mm(pooled.astype(BF16), pw_ref[gi])
        op_ref[:, gi * POOL_GROUP:(gi + 1) * POOL_GROUP] = (
            mixed * ps_ref[:, gi * POOL_GROUP:(gi + 1) * POOL_GROUP]).astype(BF16)


def odd_in_proj(pending, g, sh, sc, w_pad, mu_pad, w0, w2p, a0, a2p, g2, k_k, k_a, ones_bd, pool_w,
                pool_scale, seq):
    ys, dest, x1, gate_moe = pending
    n, d = x1.shape
    tm = min(TOK_TILE, seq)
    tpb = seq // tm
    n_steps = n // tm
    row = lambda i: (i, 0)
    per_b = lambda i: (i // tpb, 0, 0)
    c2 = lambda i: (0, 0)
    full2 = lambda a: pl.BlockSpec(a.shape, c2)
    dest3 = dest.reshape(n_steps, 1, tm)
    return pl.pallas_call(
        functools.partial(_odd_in_kernel, tiles_per_seq=tpb, tm=tm, n_steps=n_steps),
        out_shape=[jax.ShapeDtypeStruct((n, d), F32)] + [jax.ShapeDtypeStruct((n, RWKV_DIM), F32)] * 7
                  + [jax.ShapeDtypeStruct((n, RWKV_DIM), BF16)],
        grid=(n_steps,),
        in_specs=[pl.BlockSpec((1, 1, tm), lambda i: (i, 0, 0), memory_space=pltpu.SMEM),
                  pl.BlockSpec((1, 1, tm), lambda i: (jnp.minimum(i + 1, n_steps - 1), 0, 0),
                               memory_space=pltpu.SMEM),
                  pl.BlockSpec(memory_space=pl.ANY),
                  pl.BlockSpec((tm, d), row), pl.BlockSpec((1, 1, d), per_b),
                  pl.BlockSpec((1, d), c2),
                  pl.BlockSpec((1, 1, d), per_b), pl.BlockSpec((1, 1, d), per_b),
                  full2(w_pad), full2(mu_pad), full2(w0), full2(w2p), full2(a0), full2(a2p),
                  full2(g2), full2(k_k), full2(k_a), full2(ones_bd),
                  pl.BlockSpec(pool_w.shape, lambda i: (0, 0, 0)), full2(pool_scale)],
        out_specs=[pl.BlockSpec((tm, d), row)] + [pl.BlockSpec((tm, RWKV_DIM), row)] * 8,
        scratch_shapes=[pltpu.VMEM((SUBLANES, ODD_RW_COLS), F32),
                        pltpu.VMEM((POOL_HALO, RWKV_DIM), F32),
                        pltpu.VMEM((2, tm, d), F32), pltpu.SemaphoreType.DMA((2,))],
        compiler_params=_cparams(("arbitrary",)),
        name="odd_in_proj",
    )(dest3, dest3, ys, x1, gate_moe, g, sh, sc, w_pad, mu_pad, w0, w2p, a0, a2p, g2, k_k, k_a, ones_bd,
      pool_w, pool_scale)


def _bmm(a, b):
    return lax.dot_general(a, b, (((2,), (1,)), ((0,), (0,))), preferred_element_type=F32)


def _bnt(a, b):
    return lax.dot_general(a, b, (((2,), (2,)), ((0,), (0,))), preferred_element_type=F32)


def _btn(a, b):
    return lax.dot_general(a, b, (((1,), (1,)), ((0,), (0,))), preferred_element_type=F32)


def _scan_prep_kernel(r_ref, lw_ref, km_ref, v_ref, kn_ref, kb_ref, qe_ref, y0_ref, mt_ref, ct_ref,
                      *, chunk, cb):
    L = chunk
    rows = cb * L
    n_pairs = N_RWKV_HEADS // 2
    two = 2 * L
    rowt = lax.broadcasted_iota(jnp.int32, (rows, 1), 0) % L
    lane = lax.broadcasted_iota(jnp.int32, (1, 1, LANES), 2)
    low = lane < HEAD_DIM
    ri = lax.broadcasted_iota(jnp.int32, (two, two), 0)
    ci = lax.broadcasted_iota(jnp.int32, (two, two), 1)
    same_blk = (ri // L) == (ci // L)
    strict = same_blk & ((ci % L) < (ri % L))
    incl = same_blk & ((ci % L) <= (ri % L))
    li = lax.broadcasted_iota(jnp.int32, (LANES, LANES), 0)
    lj = lax.broadcasted_iota(jnp.int32, (LANES, LANES), 1)
    same_head = (li // HEAD_DIM) == (lj // HEAD_DIM)
    eye = li == lj

    lw = lw_ref[...]
    cum = lw
    step = 1
    while step < L:
        cum = cum + jnp.where(rowt >= step, pltpu.roll(cum, step, 0), 0.0)
        step *= 2

    def to3(x):
        x3 = x.reshape(cb, L, RWKV_DIM)
        return jnp.concatenate([x3[:, :, p * LANES:(p + 1) * LANES] for p in range(n_pairs)], axis=0)

    def stack2(x):
        return jnp.concatenate([jnp.where(low, x, 0.0), jnp.where(low, 0.0, x)], axis=1)

    def fold(x):
        return x[:, :L, :] + x[:, L:, :]

    cum3 = to3(cum)
    lw3 = to3(lw)
    cum_l = cum3[:, L - 1:L, :]
    g_inv = jnp.exp(-cum3)
    g_tail = jnp.exp(cum_l - cum3)
    kb = to3(kb_ref[...])
    km = to3(km_ref[...])
    v = to3(v_ref[...])
    at_s = stack2(-to3(kn_ref[...]) * jnp.exp(cum3 - lw3))
    rt_s = stack2(to3(r_ref[...]) * jnp.exp(cum3))
    v_s = stack2(v).astype(BF16)
    lhs = jnp.concatenate([at_s, rt_s], axis=1).astype(BF16)
    rhs = jnp.concatenate([stack2(kb * g_inv), stack2(km * g_inv)], axis=1).astype(BF16)
    prod = _bnt(lhs, rhs)
    nmat = jnp.where(strict, prod[:, :two, :two], 0.0)
    a_ak = jnp.where(strict, prod[:, :two, two:], 0.0).astype(BF16)
    a_rb = jnp.where(incl, prod[:, two:, :two], 0.0).astype(BF16)
    a_rk = jnp.where(incl, prod[:, two:, two:], 0.0).astype(BF16)

    x = jnp.concatenate([at_s, _bmm(a_ak, v_s)], axis=2)
    npow = nmat
    step = 1
    while step < L:
        nb = npow.astype(BF16)
        x = x + _bmm(nb, x.astype(BF16))
        step *= 2
        if step < L:
            npow = _bmm(nb, nb)
    qy = _bmm(a_rb, x.astype(BF16))
    qe = fold(rt_s + qy[:, :, :LANES])
    y0 = fold(qy[:, :, LANES:] + _bmm(a_rk, v_s))
    wu = fold(x).astype(BF16)
    bwu = _btn((kb * g_tail).astype(BF16), wu)
    kv = _btn((km * g_tail).astype(BF16), v.astype(BF16))
    g_l = jnp.broadcast_to(jnp.exp(cum_l), (n_pairs * cb, LANES, LANES))
    mt = jnp.where(eye, g_l, 0.0) + jnp.where(same_head, bwu[:, :, :LANES], 0.0)
    ct = jnp.where(same_head, bwu[:, :, LANES:] + kv, 0.0)
    for p in range(n_pairs):
        sl = slice(p * LANES, (p + 1) * LANES)
        qe_ref[:, sl] = qe[p * cb:(p + 1) * cb].reshape(rows, LANES)
        y0_ref[:, sl] = y0[p * cb:(p + 1) * cb].reshape(rows, LANES)
        mt_ref[:, p] = mt[p * cb:(p + 1) * cb].astype(BF16)
        ct_ref[:, p] = ct[p * cb:(p + 1) * cb]


def _scan_state_kernel(qe_ref, y0_ref, mt_ref, ct_ref, y_ref, st_ref, *, batch):
    @pl.when(pl.program_id(0) == 0)
    def _():
        st_ref[...] = jnp.zeros_like(st_ref)

    n_pairs = N_RWKV_HEADS // 2
    qe = qe_ref[...]
    qe3 = jnp.concatenate([qe[:, :, p * LANES:(p + 1) * LANES] for p in range(n_pairs)], axis=0)
    st = st_ref[...].astype(BF16)
    y = _bmm(qe3.astype(BF16), st)
    for p in range(n_pairs):
        sl = slice(p * LANES, (p + 1) * LANES)
        y_ref[:, :, sl] = y[p * batch:(p + 1) * batch] + y0_ref[:, :, sl]
    mt = jnp.concatenate([mt_ref[:, 0, p] for p in range(n_pairs)], axis=0)
    ct = jnp.concatenate([ct_ref[:, 0, p] for p in range(n_pairs)], axis=0)
    st_ref[...] = _bmm(mt, st) + ct


def rwkv_scan(r, lw, km, v, kn, kb, batch, seq):
    n = batch * seq
    chunk = min(SCAN_CHUNK, seq)
    nc = seq // chunk
    cb = min(SCAN_CHUNKS_PER_STEP, nc)
    n_pairs = N_RWKV_HEADS // 2
    blk = pl.BlockSpec((cb * chunk, RWKV_DIM), lambda i: (i, 0))
    mblk = pl.BlockSpec((cb, n_pairs, LANES, LANES), lambda i: (i, 0, 0, 0))
    qe, y0, mt, ct = pl.pallas_call(
        functools.partial(_scan_prep_kernel, chunk=chunk, cb=cb),
        out_shape=[jax.ShapeDtypeStruct((n, RWKV_DIM), F32), jax.ShapeDtypeStruct((n, RWKV_DIM), F32),
                   jax.ShapeDtypeStruct((n // chunk, n_pairs, LANES, LANES), BF16),
                   jax.ShapeDtypeStruct((n // chunk, n_pairs, LANES, LANES), F32)],
        grid=(n // (cb * chunk),),
        in_specs=[blk] * 6,
        out_specs=[blk, blk, mblk, mblk],
        compiler_params=_cparams(("parallel",)),
        name="rwkv_scan_prep",
    )(r, lw, km, v, kn, kb)
    sblk = pl.BlockSpec((batch, chunk, RWKV_DIM), lambda c: (0, c, 0))
    smblk = pl.BlockSpec((batch, 1, n_pairs, LANES, LANES), lambda c: (0, c, 0, 0, 0))
    y = pl.pallas_call(
        functools.partial(_scan_state_kernel, batch=batch),
        out_shape=jax.ShapeDtypeStruct((batch, seq, RWKV_DIM), F32),
        grid=(nc,),
        in_specs=[sblk, sblk, smblk, smblk],
        out_specs=sblk,
        scratch_shapes=[pltpu.VMEM((n_pairs * batch, LANES, LANES), F32)],
        compiler_params=_cparams(("arbitrary",)),
        name="rwkv_scan_state",
    )(qe.reshape(batch, seq, RWKV_DIM), y0.reshape(batch, seq, RWKV_DIM),
      mt.reshape(batch, nc, n_pairs, LANES, LANES), ct.reshape(batch, nc, n_pairs, LANES, LANES))
    return y.reshape(n, RWKV_DIM)


def _odd_out_kernel(y_ref, r_ref, km_ref, v_ref, gg_ref, op_ref, rk_ref, lnw_ref, lnb_ref, ones_ref,
                    *tail_refs):
    ones = ones_ref[...]
    inv = 1.0 / HEAD_DIM
    y = y_ref[...]
    mean = _split_sum(y, ones) * inv
    yc = y - mean
    var = _split_sum(yc * yc, ones) * inv
    yn = yc * lax.rsqrt(var + LNX_EPS) * lnw_ref[...] + lnb_ref[...]
    bonus = _split_sum(r_ref[...] * km_ref[...] * rk_ref[...], ones) * v_ref[...]
    o_rwkv = (yn + bonus) * gg_ref[...]
    cat = jnp.concatenate([o_rwkv.astype(BF16), op_ref[...]], axis=1)
    _tail(cat, *tail_refs)


def odd_out_proj(y, r, km, v, gg, opool, r_k, lnx_w, lnx_b, ones_bd, x2, g1, w_out, ng, sh2, sc2,
                 rwt, rb, seq):
    n, d = x2.shape
    tm = min(TOK_TILE, seq)
    tpb = seq // tm
    row = lambda i: (i, 0)
    c2 = lambda i: (0, 0)
    act = pl.BlockSpec((tm, RWKV_DIM), row)
    vec = pl.BlockSpec((1, RWKV_DIM), c2)
    tin, tout, tscratch = _tail_specs(tm, d, tpb)
    return pl.pallas_call(
        _odd_out_kernel,
        out_shape=_tail_out_shapes(n, d),
        grid=(n // tm,),
        in_specs=[act] * 6 + [vec, vec, vec, pl.BlockSpec(ones_bd.shape, c2)] + tin,
        out_specs=tout,
        scratch_shapes=tscratch,
        compiler_params=_cparams(("arbitrary",)),
        name="odd_out_proj",
    )(y, r, km, v, gg, opool, r_k, lnx_w, lnx_b, ones_bd, x2, g1, w_out, ng, sh2, sc2, rwt, rb,
      _strict_upper(tm))


def _rope_tables(seq):
    half = HEAD_DIM // 2
    inv = ROPE_THETA ** (-jnp.arange(half, dtype=F32) / half)
    ang = jnp.arange(seq, dtype=F32)[:, None] * inv[None, :]
    return jnp.tile(jnp.cos(ang), (1, LANES // half)), jnp.tile(jnp.sin(ang), (1, LANES // half))


def _even_w_pad(w_in):
    d = w_in.shape[0]
    q_kv = w_in[:, :NSA_DIM + 6 * KV_DIM]
    gl = w_in[:, NSA_DIM + 6 * KV_DIM:NSA_DIM + 6 * KV_DIM + 24]
    rest = w_in[:, NSA_DIM + 6 * KV_DIM + 24:]
    z = jnp.zeros((d, LANES - 12), w_in.dtype)
    return jnp.concatenate([q_kv, gl[:, :12], z, gl[:, 12:], z, rest], axis=1).astype(BF16)


def _compress_params(cmp_pos, cmp_w1, cmp_w2):
    w1r = cmp_w1.reshape(2, 2, CMP_STRIDE, HEAD_DIM, CMP_HIDDEN).astype(BF16)
    z1 = jnp.zeros_like(w1r)
    w1_ext = jnp.stack([jnp.concatenate([w1r, z1], axis=-1), jnp.concatenate([z1, w1r], axis=-1)],
                       axis=3).reshape(2, 2, CMP_STRIDE * KV_DIM, N_KV_HEADS * CMP_HIDDEN)
    w2b = cmp_w2.astype(BF16)
    z2 = jnp.zeros_like(w2b)
    w2_ext = jnp.concatenate([jnp.concatenate([w2b, z2], axis=-1), jnp.concatenate([z2, w2b], axis=-1)],
                             axis=1)
    pos = cmp_pos.reshape(2, 2, CMP_STRIDE, 1, HEAD_DIM)
    pos_ext = jnp.broadcast_to(pos, (2, 2, CMP_STRIDE, N_KV_HEADS, HEAD_DIM)).reshape(
        2, 2, 1, CMP_STRIDE * KV_DIM)
    return pos_ext, w1_ext, w2_ext


def _nsa_tables(seq):
    n_blk = seq // SEL_BLOCK
    n_cmp = (seq - CMP_BLOCK) // CMP_STRIDE + 1
    n_cmp_pad = seq // CMP_STRIDE
    r = SEL_BLOCK // CMP_STRIDE
    c = CMP_BLOCK // CMP_STRIDE
    msel = np.zeros((n_cmp_pad, LANES), np.float32)
    for j in range(n_blk):
        for m in range(r):
            for n in range(c):
                idx = r * j + m + n
                if idx < n_cmp:
                    msel[idx, j] += 1.0
    selq = np.zeros((N_KV_HEADS, GQA, GQA * HEAD_DIM, LANES), np.float32)
    for h in range(N_KV_HEADS):
        for g in range(GQA):
            for dd in range(HEAD_DIM):
                selq[h, g, g * HEAD_DIM + dd, h * HEAD_DIM + dd] = 1.0
    return jnp.asarray(msel.T), jnp.asarray(selq, dtype=BF16)


def _odd_params(w_in, mu, w2, a2):
    d = w_in.shape[0]
    z64 = jnp.zeros((d, 64), w_in.dtype)
    w_pad = jnp.concatenate([w_in[:, :1536], w_in[:, 1536:1600], z64, w_in[:, 1600:1664], z64,
                             w_in[:, 1664:]], axis=1).astype(BF16)
    m64 = jnp.zeros((64,), mu.dtype)
    mu_pad = jnp.concatenate([mu[:1536], mu[1536:1600], m64, mu[1600:1664], m64, mu[1664:]])[None, :]
    zr = jnp.zeros((64, RWKV_DIM), w2.dtype)
    w2p = jnp.concatenate([w2, zr], axis=0).astype(BF16)
    a2p = jnp.concatenate([a2, zr], axis=0).astype(BF16)
    return w_pad, mu_pad, w2p, a2p


def _head_ones():
    idx = np.arange(RWKV_DIM) // HEAD_DIM
    return jnp.asarray((idx[:, None] == idx[None, :]).astype(np.float32), dtype=BF16)


def kernel(x, c, ada_w, ada_b, norm_mix, norm_ffn, even_w_in, even_cmp_pos, even_cmp_w1, even_cmp_w2,
           even_conv_w, even_w_out, odd_w_in, odd_mu, odd_w0, odd_w2, odd_a0, odd_a2, odd_g2, odd_k_k,
           odd_k_a, odd_r_k, odd_lnx_w, odd_lnx_b, odd_pool_w, odd_pool_scale, odd_w_out,
           router_w, router_b, moe_w_gate, moe_w_up, moe_w_down, final_norm):
    batch, seq, d = x.shape
    n = batch * seq
    depth = ada_w.shape[0]
    x2 = x.reshape(n, d)
    mod = ada_modulation(c, ada_w, ada_b)
    rwt = router_w.T
    rb = router_b.reshape(N_EXPERTS, 1)
    fnorm = final_norm.reshape(1, d)
    cos, sin = _rope_tables(seq)
    mselt, selq = _nsa_tables(seq)
    ones_bd = _head_ones()

    pending = None
    for layer in range(depth):
        m = mod[layer].reshape(batch, 6, 1, d)
        sh1, sc1, g1, sh2, sc2, g2 = (m[:, k] for k in range(6))
        ng_mix = norm_mix[layer].reshape(1, d)
        ng_ffn = norm_ffn[layer].reshape(1, d)
        i = layer // 2
        if layer % 2 == 0:
            (qn, qr, kc, vc, ks, vs, kw, vw, gate, u, bg) = even_in_proj(
                x2, ng_mix, sh1, sc1, _even_w_pad(even_w_in[i]), cos, sin, seq)
            pos_ext, w1_ext, w2_ext = _compress_params(even_cmp_pos[i], even_cmp_w1[i], even_cmp_w2[i])
            kcmp, vcmp = compress_kv(kc, vc, pos_ext, w1_ext, w2_ext, batch, seq)
            o_nsa = nsa_attention(qn, qr, kcmp, vcmp, ks, vs, kw, vw, gate, selq, mselt, batch, seq)
            x1, hw, ridx, counts = even_out_proj(o_nsa, u, bg, even_conv_w[i], x2, g1,
                                                 even_w_out[i].astype(BF16), ng_ffn, sh2, sc2, rwt, rb, seq)
        else:
            w_pad, mu_pad, w2p, a2p = _odd_params(odd_w_in[i], odd_mu[i], odd_w2[i], odd_a2[i])
            vec = lambda a: a.reshape(1, RWKV_DIM)
            (x2, r, lw, km, v, kn, kb, gg, opool) = odd_in_proj(
                pending, ng_mix, sh1, sc1, w_pad, mu_pad, vec(odd_w0[i]), w2p, vec(odd_a0[i]), a2p,
                odd_g2[i].astype(BF16), vec(odd_k_k[i]), vec(odd_k_a[i]), ones_bd,
                odd_pool_w[i].astype(BF16), vec(odd_pool_scale[i]), seq)
            y = rwkv_scan(r, lw, km, v, kn, kb, batch, seq)
            x1, hw, ridx, counts = odd_out_proj(
                y, r, km, v, gg, opool, vec(odd_r_k[i]), vec(odd_lnx_w[i]), vec(odd_lnx_b[i]), ones_bd,
                x2, g1, odd_w_out[i].astype(BF16), ng_ffn, sh2, sc2, rwt, rb, seq)
        ys, dest3 = moe_routed_experts(hw, ridx, counts, moe_w_gate, moe_w_up, moe_w_down, layer, seq)
        if layer + 1 < depth and (layer + 1) % 2 == 1:
            pending = (ys, dest3, x1, g2)
        else:
            x2 = moe_combine(ys, dest3, x1, g2, fnorm, seq, final_norm=(layer == depth - 1))
    return x2.reshape(batch, seq, d)
```

```python
import functools

import jax
import jax.numpy as jnp
import numpy as np
from jax import lax
from jax.experimental import pallas as pl
from jax.experimental.pallas import tpu as pltpu

F32 = jnp.float32
BF16 = jnp.bfloat16
HIGHEST = lax.Precision.HIGHEST

D_MODEL = 1024
DEPTH = 2
HEAD_DIM = 64
ROPE_THETA = 10000.0
NORM_EPS = 1e-6
NEG_INF = -1e30
BIG = 1e9
NSA_DIM = 512
N_KV_HEADS = 2
GQA = 4
KV_DIM = 128
CMP_BLOCK = 32
CMP_STRIDE = 16
CMP_HIDDEN = 256
SEL_BLOCK = 64
N_SEL = 8
N_LOCAL = 2
WINDOW = 512
NSA_Q_TILE = 256
ATTN_SCALE = HEAD_DIM ** -0.5
LOG2_E = 1.4426950408889634
CONV_DIM = 512
RWKV_DIM = 512
N_RWKV_HEADS = 8
LNX_EPS = 64e-5
POOL_WINDOWS = (2, 4, 8, 16)
POOL_GROUP = 128
N_EXPERTS = 16
N_EXPERT_GROUPS = 4
EXPERTS_PER_GROUP = 4
D_EXPERT = 512
PAIRS_PER_GROUP = 6
PAIR_LO = (0, 0, 0, 1, 1, 2)
PAIR_HI = (1, 2, 3, 3, 2, 3)
N_CLASSES = N_EXPERT_GROUPS * PAIRS_PER_GROUP
CLASS_ROWS = 32

LANES = 128
SUBLANES = 8
VMEM_LIMIT = 56 * 1024 * 1024

TOK_TILE = 512
MOE_ROW_TILE = 256
MOE_DMA_TILE = 2048
SEL_CHUNK = 512
SCAN_CHUNK = 64
SCAN_CHUNKS_PER_STEP = 4
CONV_HALO = 8
POOL_HALO = 16

EVEN_PAD_COLS = 3072
ODD_PAD_COLS = 2432
ODD_RW_COLS = 1920


def _cparams(sem):
    return pltpu.CompilerParams(dimension_semantics=sem, vmem_limit_bytes=VMEM_LIMIT)


def _nt(a, b, precision=None):
    return lax.dot_general(a, b, (((1,), (1,)), ((), ())), preferred_element_type=F32,
                           precision=precision)


def _mm(a, b, precision=None):
    return jnp.dot(a, b, preferred_element_type=F32, precision=precision)


def _norm_mod(x, g, sh, sc):
    ms = jnp.mean(x * x, axis=-1, keepdims=True)
    return (x * lax.rsqrt(ms + NORM_EPS) * g) * (1.0 + sc) + sh


def _split_sum(x, ones_bf16):
    hi = x.astype(BF16)
    lo = (x - hi.astype(F32)).astype(BF16)
    return _mm(hi, ones_bf16) + _mm(lo, ones_bf16)


def _ada_kernel(c_ref, w_ref, b_ref, o_ref):
    c = c_ref[...]
    cond = c * jax.nn.sigmoid(c)
    w = w_ref[0]
    c_hi = cond.astype(BF16)
    c_lo = (cond - c_hi.astype(F32)).astype(BF16)
    w_hi = w.astype(BF16)
    w_lo = (w - w_hi.astype(F32)).astype(BF16)
    o_ref[0] = _mm(c_hi, w_hi) + (_mm(c_hi, w_lo) + _mm(c_lo, w_hi)) + b_ref[0]


def ada_modulation(c, ada_w, ada_b):
    depth, d, cols = ada_w.shape
    b = c.shape[0]
    tn = 1536
    return pl.pallas_call(
        _ada_kernel,
        out_shape=jax.ShapeDtypeStruct((depth, b, cols), F32),
        grid=(depth, cols // tn),
        in_specs=[pl.BlockSpec((b, d), lambda l, j: (0, 0)),
                  pl.BlockSpec((1, d, tn), lambda l, j: (l, 0, j)),
                  pl.BlockSpec((1, 1, tn), lambda l, j: (l, 0, j))],
        out_specs=pl.BlockSpec((1, b, tn), lambda l, j: (l, 0, j)),
        compiler_params=_cparams(("parallel", "parallel")),
        name="ada_modulation",
    )(c, ada_w, ada_b.reshape(depth, 1, cols))


def _rope128(t, cos, sin, lane):
    rot = jnp.where((lane % HEAD_DIM) < HEAD_DIM // 2,
                    -pltpu.roll(t, LANES - HEAD_DIM // 2, 1), pltpu.roll(t, HEAD_DIM // 2, 1))
    return t * cos + rot * sin


def _even_in_kernel(x_ref, g_ref, sh_ref, sc_ref, w_ref, cos_ref, sin_ref,
                    qn_ref, qr_ref, kc_ref, vc_ref, ks_ref, vs_ref, kw_ref, vw_ref,
                    gate_ref, u_ref, bg_ref, *, tiles_per_seq):
    h = _norm_mod(x_ref[...], g_ref[...], sh_ref[0], sc_ref[0])
    tm = h.shape[0]
    proj = _mm(h.astype(BF16), w_ref[...])
    cos = cos_ref[...]
    sin = sin_ref[...]
    lane = lax.broadcasted_iota(jnp.int32, (1, LANES), 1)
    low = lane < HEAD_DIM
    for i in range(NSA_DIM // LANES):
        q = proj[:, i * LANES:(i + 1) * LANES] * (ATTN_SCALE * LOG2_E)
        qn_ref[:, i * LANES:(i + 1) * LANES] = q.astype(BF16)
        qr_ref[:, i * LANES:(i + 1) * LANES] = _rope128(q, cos, sin, lane).astype(BF16)
    o = NSA_DIM
    kc_ref[...] = proj[:, o:o + 128]
    vc_ref[...] = proj[:, o + 128:o + 256]
    pos = (pl.program_id(0) % tiles_per_seq) * tm + lax.broadcasted_iota(jnp.int32, (tm, 1), 0)
    blk = pos // SEL_BLOCK
    ks = _rope128(proj[:, o + 256:o + 384], cos, sin, lane)
    ks_ref[:, 0:LANES] = jnp.where(low, ks, jnp.where(lane - HEAD_DIM == blk, 1.0, 0.0)).astype(BF16)
    ks_ref[:, LANES:2 * LANES] = jnp.where(low, jnp.where(lane == blk, 1.0, 0.0), ks).astype(BF16)
    vs = proj[:, o + 384:o + 512]
    vs_ref[:, 0:LANES] = jnp.where(low, vs, 1.0).astype(BF16)
    vs_ref[:, LANES:2 * LANES] = jnp.where(low, 1.0, vs).astype(BF16)
    kw_ref[...] = _rope128(proj[:, o + 512:o + 640], cos, sin, lane).astype(BF16)
    vw = proj[:, o + 640:o + 768]
    vw_ref[:, 0:LANES] = jnp.where(low, vw, 1.0).astype(BF16)
    vw_ref[:, LANES:2 * LANES] = jnp.where(low, 1.0, vw).astype(BF16)
    o += 768
    gate_ref[...] = jax.nn.sigmoid(proj[:, o:o + 256])
    o += 256
    xb = proj[:, o:o + 512]
    bg_ref[...] = proj[:, o + 512:o + 1024]
    u_ref[...] = proj[:, o + 1024:o + 1536] * xb


def even_in_proj(x2, g, sh, sc, w_pad, cos, sin, seq):
    n, d = x2.shape
    tm = min(TOK_TILE, seq)
    tpb = seq // tm
    row = lambda i: (i, 0)
    per_b = lambda i: (i // tpb, 0, 0)
    pos = lambda i: (i % tpb, 0)
    outs = [((n, 512), BF16), ((n, 512), BF16), ((n, 128), F32), ((n, 128), F32),
            ((n, 256), BF16), ((n, 256), BF16), ((n, 128), BF16), ((n, 256), BF16),
            ((n, 256), F32), ((n, 512), F32), ((n, 512), F32)]
    return pl.pallas_call(
        functools.partial(_even_in_kernel, tiles_per_seq=tpb),
        out_shape=[jax.ShapeDtypeStruct(s, t) for s, t in outs],
        grid=(n // tm,),
        in_specs=[pl.BlockSpec((tm, d), row),
                  pl.BlockSpec((1, d), lambda i: (0, 0)),
                  pl.BlockSpec((1, 1, d), per_b),
                  pl.BlockSpec((1, 1, d), per_b),
                  pl.BlockSpec((d, EVEN_PAD_COLS), lambda i: (0, 0)),
                  pl.BlockSpec((tm, LANES), pos),
                  pl.BlockSpec((tm, LANES), pos)],
        out_specs=[pl.BlockSpec((tm, s[1]), row) for s, _ in outs],
        compiler_params=_cparams(("parallel",)),
        name="even_in_proj",
    )(x2, g, sh, sc, w_pad, cos, sin)


def _compress_kernel(k_ref, v_ref, pos_ref, w1_ref, w2_ref, ko_ref, vo_ref):
    n_rows = k_ref.shape[1] // CMP_STRIDE
    for j, (src, dst) in enumerate(((k_ref, ko_ref), (v_ref, vo_ref))):
        a0 = jnp.zeros((n_rows, N_KV_HEADS * CMP_HIDDEN), F32)
        a1 = jnp.zeros((n_rows, N_KV_HEADS * CMP_HIDDEN), F32)
        for m in range(CMP_STRIDE):
            cols = slice(m * KV_DIM, (m + 1) * KV_DIM)
            x_m = src.at[0][pl.ds(m, n_rows, stride=CMP_STRIDE), :]
            a0 = a0 + _mm((x_m + pos_ref[j, 0, :, cols]).astype(BF16), w1_ref[j, 0, cols, :])
            a1 = a1 + _mm((x_m + pos_ref[j, 1, :, cols]).astype(BF16), w1_ref[j, 1, cols, :])
        hid = a0 + pltpu.roll(a1, n_rows - 1, 0)
        hid = jax.nn.gelu(hid)
        dst[0] = _mm(hid.astype(BF16), w2_ref[j]).astype(BF16)


def compress_kv(kc, vc, pos_ext, w1_ext, w2_ext, batch, seq):
    rows = seq // CMP_STRIDE
    kr = kc.reshape(batch, seq, KV_DIM)
    vr = vc.reshape(batch, seq, KV_DIM)
    blk = pl.BlockSpec((1, seq, KV_DIM), lambda b: (b, 0, 0))
    oblk = pl.BlockSpec((1, rows, KV_DIM), lambda b: (b, 0, 0))
    return pl.pallas_call(
        _compress_kernel,
        out_shape=[jax.ShapeDtypeStruct((batch, rows, KV_DIM), BF16)] * 2,
        grid=(batch,),
        in_specs=[blk, blk,
                  pl.BlockSpec(pos_ext.shape, lambda b: (0, 0, 0, 0)),
                  pl.BlockSpec(w1_ext.shape, lambda b: (0, 0, 0, 0)),
                  pl.BlockSpec(w2_ext.shape, lambda b: (0, 0, 0))],
        out_specs=[oblk, oblk],
        compiler_params=_cparams(("parallel",)),
        name="compress_kv",
    )(kr, vr, pos_ext, w1_ext, w2_ext)


def _safe_inv(l):
    return jnp.where(l > 0.0, 1.0 / jnp.where(l > 0.0, l, 1.0), 0.0)


def _nsa_kernel(qn_ref, qr_ref, kc_ref, vc_ref, ks_ref, vs_ref, kw_ref, vw_ref, gate_ref,
                selq_ref, mselt_ref, o_ref, *, seq, n_sel, sel_chunk, win_len):
    h = pl.program_id(1)
    qt = pl.program_id(2)
    t0 = qt * NSA_Q_TILE
    n_blk = seq // SEL_BLOCK
    n_cmp_pad = seq // CMP_STRIDE
    rows = GQA * NSA_Q_TILE
    tpos = t0 + lax.broadcasted_iota(jnp.int32, (1, NSA_Q_TILE, 1), 1)
    lane = lax.broadcasted_iota(jnp.int32, (1, LANES), 1)
    head_lanes = (lane // HEAD_DIM) == h

    def normalise(acc):
        return acc * _safe_inv(pltpu.roll(acc, HEAD_DIM, 1))

    qn = qn_ref[...]
    qr = qr_ref[...]
    qn4 = jnp.concatenate([_mm(qn, selq_ref[0, g]) for g in range(GQA)], axis=0).astype(BF16)
    qr4f = jnp.concatenate([_mm(qr, selq_ref[0, g]) for g in range(GQA)], axis=0)
    qr4 = qr4f.astype(BF16)

    kc = kc_ref[0]
    vc = vc_ref[0]
    cpos = lax.broadcasted_iota(jnp.int32, (1, 1, n_cmp_pad), 2) * CMP_STRIDE + (CMP_BLOCK - 1)
    cmask = cpos <= tpos
    s = jnp.where(cmask, _nt(qn4, kc).reshape(GQA, NSA_Q_TILE, n_cmp_pad), NEG_INF)
    e = jnp.where(cmask, jnp.exp2(s - jnp.max(s, axis=2, keepdims=True)), 0.0)
    p = e * _safe_inv(jnp.sum(e, axis=2, keepdims=True))
    imp = jnp.sum(p, axis=0)
    o_cmp = _mm(p.reshape(rows, n_cmp_pad).astype(BF16), vc)

    pslc = _nt(mselt_ref[...], imp, precision=HIGHEST)[:n_blk]
    tq = t0 + lax.broadcasted_iota(jnp.int32, (1, NSA_Q_TILE), 1)
    jblk = lax.broadcasted_iota(jnp.int32, (n_blk, 1), 0)
    cur = tq // SEL_BLOCK
    valid = jblk * SEL_BLOCK <= tq
    forced = (jblk == 0) | ((cur - jblk >= 0) & (cur - jblk < N_LOCAL))
    score = jnp.where(forced, BIG, jnp.where(valid, pslc, -BIG))
    rank = jnp.zeros((n_blk, NSA_Q_TILE), jnp.int32)
    for jp in range(n_blk):
        row = score[jp:jp + 1, :]
        beats = (row > score) | ((row == score) & (jblk > jp))
        rank = rank + beats.astype(jnp.int32)
    selb = jnp.where((rank < n_sel) & (score > -0.5 * BIG), 0.0, NEG_INF)
    selb_q = jnp.concatenate([selb, jnp.zeros((LANES - n_blk, NSA_Q_TILE), F32)], axis=0).T
    bias = jnp.where(h == 0, pltpu.roll(selb_q, HEAD_DIM, 1), selb_q)
    qs4 = (qr4f + jnp.concatenate([bias] * GQA, axis=0)).astype(BF16)

    def chunk_scores(c):
        start = pl.multiple_of(c * sel_chunk, sel_chunk)
        kblk = ks_ref[0, pl.ds(start, sel_chunk), :]
        vblk = vs_ref[0, pl.ds(start, sel_chunk), :]
        return start, _nt(qs4, kblk).reshape(GQA, NSA_Q_TILE, sel_chunk), vblk

    def online_update(carry, s, vblk):
        m, acc = carry
        m_new = jnp.maximum(m, jnp.max(s, axis=2, keepdims=True))
        alpha = jnp.exp2(m - m_new)
        p = jnp.exp2(s - m_new)
        pv = _mm(p.reshape(rows, sel_chunk).astype(BF16), vblk).reshape(GQA, NSA_Q_TILE, LANES)
        return m_new, alpha * acc + pv

    def sel_body(c, carry):
        _, s, vblk = chunk_scores(c)
        return online_update(carry, s, vblk)

    diag_chunk = (t0 + NSA_Q_TILE - 1) // sel_chunk
    init = (jnp.full((GQA, NSA_Q_TILE, 1), NEG_INF, F32), jnp.zeros((GQA, NSA_Q_TILE, LANES), F32))
    carry = lax.fori_loop(0, diag_chunk, sel_body, init)
    start, s, vblk = chunk_scores(diag_chunk)
    kpos = start + lax.broadcasted_iota(jnp.int32, (1, 1, sel_chunk), 2)
    _, acc = online_update(carry, jnp.where(kpos <= tpos, s, NEG_INF), vblk)
    o_slc = normalise(acc.reshape(rows, LANES))

    ws = pl.multiple_of(jnp.maximum(qt - WINDOW // NSA_Q_TILE, 0) * NSA_Q_TILE, NSA_Q_TILE)
    kwb = kw_ref[0, pl.ds(ws, win_len), :]
    vwb = vw_ref[0, pl.ds(ws, win_len), :]
    diff = tpos - (ws + lax.broadcasted_iota(jnp.int32, (1, 1, win_len), 2))
    wmask = (diff >= 0) & (diff < WINDOW)
    s = jnp.where(wmask, _nt(qr4, kwb).reshape(GQA, NSA_Q_TILE, win_len), NEG_INF)
    e = jnp.exp2(s - jnp.max(s, axis=2, keepdims=True))
    o_win = normalise(_mm(e.reshape(rows, win_len).astype(BF16), vwb))

    gate = gate_ref[...]
    og = []
    for g in range(GQA):
        sl = slice(g * NSA_Q_TILE, (g + 1) * NSA_Q_TILE)
        o = (gate[:, 3 * g:3 * g + 1] * o_cmp[sl] + gate[:, 3 * g + 1:3 * g + 2] * o_slc[sl]
             + gate[:, 3 * g + 2:3 * g + 3] * o_win[sl])
        og.append(jnp.where(head_lanes, o, pltpu.roll(o, HEAD_DIM, 1)))
    low = lane < HEAD_DIM
    o_ref[:, 0:LANES] = jnp.where(low, og[0], og[1]).astype(BF16)
    o_ref[:, LANES:2 * LANES] = jnp.where(low, og[2], og[3]).astype(BF16)


def nsa_attention(qn, qr, kcmp, vcmp, ks, vs, kw, vw, gate, selq, mselt, batch, seq):
    n = batch * seq
    nq = seq // NSA_Q_TILE
    sel_chunk = min(SEL_CHUNK, seq)
    win_len = min(WINDOW + NSA_Q_TILE, seq)
    n_sel = min(N_SEL, seq // SEL_BLOCK)
    qspec = pl.BlockSpec((NSA_Q_TILE, GQA * HEAD_DIM), lambda b, h, q: (b * nq + q, h))
    cspec = pl.BlockSpec((1, seq // CMP_STRIDE, KV_DIM), lambda b, h, q: (b, 0, 0))
    both = pl.BlockSpec((1, seq, KV_DIM), lambda b, h, q: (b, 0, 0))
    mine = pl.BlockSpec((1, seq, KV_DIM), lambda b, h, q: (b, 0, h))
    kern = functools.partial(_nsa_kernel, seq=seq, n_sel=n_sel, sel_chunk=sel_chunk, win_len=win_len)
    return pl.pallas_call(
        kern,
        out_shape=jax.ShapeDtypeStruct((n, NSA_DIM), BF16),
        grid=(batch, N_KV_HEADS, nq),
        in_specs=[qspec, qspec, cspec, cspec, mine, mine, both, mine,
                  pl.BlockSpec((NSA_Q_TILE, LANES), lambda b, h, q: (b * nq + q, h)),
                  pl.BlockSpec((1, GQA, GQA * HEAD_DIM, LANES), lambda b, h, q: (h, 0, 0, 0)),
                  pl.BlockSpec(mselt.shape, lambda b, h, q: (0, 0))],
        out_specs=qspec,
        compiler_params=_cparams(("parallel", "parallel", "arbitrary")),
        name="nsa_attention",
    )(qn, qr, kcmp, vcmp, ks.reshape(batch, seq, 2 * KV_DIM), vs.reshape(batch, seq, 2 * KV_DIM),
      kw.reshape(batch, seq, KV_DIM), vw.reshape(batch, seq, 2 * KV_DIM), gate, selq, mselt)


def _route(h2, rwt_ref, rb_ref):
    rw = rwt_ref[...]
    rw_hi = rw.astype(BF16)
    rw_lo = (rw - rw_hi.astype(F32)).astype(BF16)
    h_hi = h2.astype(BF16)
    h_lo = (h2 - h_hi.astype(F32)).astype(BF16)
    logits = _nt(rw_hi, h_hi) + (_nt(rw_hi, h_lo) + _nt(rw_lo, h_hi))
    scores = jax.nn.sigmoid(logits)
    biased = scores + rb_ref[...]
    rows = [biased[e:e + 1, :] for e in range(N_EXPERTS)]
    srow = [scores[e:e + 1, :] for e in range(N_EXPERTS)]
    gscore = []
    for gi in range(N_EXPERT_GROUPS):
        r = rows[gi * EXPERTS_PER_GROUP:(gi + 1) * EXPERTS_PER_GROUP]
        best = None
        for a in range(EXPERTS_PER_GROUP):
            for b in range(a + 1, EXPERTS_PER_GROUP):
                pair = r[a] + r[b]
                best = pair if best is None else jnp.maximum(best, pair)
        gscore.append(best)
    top_val = gscore[0]
    top_grp = jnp.zeros_like(top_val, dtype=jnp.int32)
    for gi in range(1, N_EXPERT_GROUPS):
        upd = gscore[gi] > top_val
        top_grp = jnp.where(upd, gi, top_grp)
        top_val = jnp.where(upd, gscore[gi], top_val)
    masked = [jnp.where(top_grp == e // EXPERTS_PER_GROUP, rows[e], NEG_INF) for e in range(N_EXPERTS)]
    b1 = masked[0]
    i1 = jnp.zeros_like(top_grp)
    for e in range(1, N_EXPERTS):
        upd = masked[e] > b1
        i1 = jnp.where(upd, e, i1)
        b1 = jnp.where(upd, masked[e], b1)
    b2 = None
    i2 = None
    for e in range(N_EXPERTS):
        v = jnp.where(i1 == e, -jnp.inf, masked[e])
        if b2 is None:
            b2, i2 = v, jnp.zeros_like(top_grp)
        else:
            upd = v > b2
            i2 = jnp.where(upd, e, i2)
            b2 = jnp.where(upd, v, b2)
    s1 = jnp.zeros_like(top_val)
    s2 = jnp.zeros_like(top_val)
    for e in range(N_EXPERTS):
        s1 = s1 + jnp.where(i1 == e, srow[e], 0.0)
        s2 = s2 + jnp.where(i2 == e, srow[e], 0.0)
    tot = s1 + s2
    return i1, i2, s1 / tot, s2 / tot


def _tail(cat_bf16, x_ref, g1_ref, wout_ref, ng_ref, sh2_ref, sc2_ref, rwt_ref, rb_ref, ustrict_ref,
          x1_ref, hw_ref, ridx_ref, cnt_ref, carry_ref):
    y = _mm(cat_bf16, wout_ref[...])
    x1 = x_ref[...] + g1_ref[0] * y
    x1_ref[...] = x1
    h2 = _norm_mod(x1, ng_ref[...], sh2_ref[0], sc2_ref[0])
    tm, d = h2.shape
    hw_ref[:, :d] = h2

    i1, i2, w1, w2 = _route(h2, rwt_ref, rb_ref)
    lo = jnp.minimum(i1, i2) % EXPERTS_PER_GROUP
    hi = jnp.maximum(i1, i2) % EXPERTS_PER_GROUP
    pair = jnp.where(lo == 0, hi - 1, jnp.where(lo == 1, jnp.where(hi == 3, 3, 4), 5))
    cls = (i1 // EXPERTS_PER_GROUP) * PAIRS_PER_GROUP + pair
    w_lo = jnp.where(i1 < i2, w1, w2)
    w_hi = jnp.where(i1 < i2, w2, w1)
    meta_t = jnp.concatenate([w_lo, w_hi, jnp.zeros((LANES - 2, tm), F32)], axis=0)
    hw_ref[:, d:] = meta_t.T

    @pl.when(pl.program_id(0) == 0)
    def _():
        carry_ref[...] = jnp.zeros_like(carry_ref)

    hit = lax.broadcasted_iota(jnp.int32, (CLASS_ROWS, 1), 0) == cls
    cnt = jnp.where(hit, 1.0, 0.0)
    before = _mm(cnt.astype(BF16), ustrict_ref[...]) + carry_ref[:, 0:1]
    carry = carry_ref[...] + jnp.sum(cnt, axis=1, keepdims=True)
    carry_ref[...] = carry
    cnt_ref[...] = carry.astype(jnp.int32)
    ridx_ref[0:1, :] = cls
    ridx_ref[1:2, :] = jnp.sum(jnp.where(hit, before, 0.0), axis=0, keepdims=True).astype(jnp.int32)


def _even_out_kernel(o_ref, u_ref, uh_ref, bg_ref, cw_ref, *tail_refs, tiles_per_seq):
    first = (pl.program_id(0) % tiles_per_seq) == 0
    u = u_ref[...]
    halo = jnp.where(first, 0.0, uh_ref[...])
    ext = jnp.concatenate([halo, u], axis=0)
    u1 = pltpu.roll(ext, 1, 0)[CONV_HALO:]
    u2 = pltpu.roll(ext, 2, 0)[CONV_HALO:]
    cw = cw_ref[...]
    y_conv = bg_ref[...] * (cw[2:3] * u + cw[1:2] * u1 + cw[0:1] * u2)
    cat = jnp.concatenate([o_ref[...], y_conv.astype(BF16)], axis=1)
    _tail(cat, *tail_refs)


def _tail_specs(tm, d, tpb):
    row = lambda i: (i, 0)
    per_b = lambda i: (i // tpb, 0, 0)
    const2 = lambda i: (0, 0)
    ins = [pl.BlockSpec((tm, d), row),
           pl.BlockSpec((1, 1, d), per_b),
           pl.BlockSpec((d, d), const2),
           pl.BlockSpec((1, d), const2),
           pl.BlockSpec((1, 1, d), per_b),
           pl.BlockSpec((1, 1, d), per_b),
           pl.BlockSpec((N_EXPERTS, d), const2),
           pl.BlockSpec((N_EXPERTS, 1), const2),
           pl.BlockSpec((tm, tm), const2)]
    outs = [pl.BlockSpec((tm, d), row), pl.BlockSpec((tm, d + LANES), row),
            pl.BlockSpec((2, tm), lambda i: (0, i)),
            pl.BlockSpec((CLASS_ROWS, LANES), const2)]
    scratch = [pltpu.VMEM((CLASS_ROWS, LANES), F32)]
    return ins, outs, scratch


def _tail_out_shapes(n, d):
    return [jax.ShapeDtypeStruct((n, d), F32), jax.ShapeDtypeStruct((n, d + LANES), F32),
            jax.ShapeDtypeStruct((2, n), jnp.int32), jax.ShapeDtypeStruct((CLASS_ROWS, LANES), jnp.int32)]


def _strict_upper(tm):
    return jnp.asarray(np.triu(np.ones((tm, tm), np.float32), 1), dtype=BF16)


def even_out_proj(o_nsa, u, bg, conv_w, x2, g1, w_out, ng, sh2, sc2, rwt, rb, seq):
    n, d = x2.shape
    tm = min(TOK_TILE, seq)
    tpb = seq // tm
    row = lambda i: (i, 0)
    halo = lambda i: (jnp.maximum(i * (tm // CONV_HALO) - 1, 0), 0)
    tin, tout, tscratch = _tail_specs(tm, d, tpb)
    return pl.pallas_call(
        functools.partial(_even_out_kernel, tiles_per_seq=tpb),
        out_shape=_tail_out_shapes(n, d),
        grid=(n // tm,),
        in_specs=[pl.BlockSpec((tm, NSA_DIM), row),
                  pl.BlockSpec((tm, CONV_DIM), row),
                  pl.BlockSpec((CONV_HALO, CONV_DIM), halo),
                  pl.BlockSpec((tm, CONV_DIM), row),
                  pl.BlockSpec(conv_w.shape, lambda i: (0, 0))] + tin,
        out_specs=tout,
        scratch_shapes=tscratch,
        compiler_params=_cparams(("arbitrary",)),
        name="even_out_proj",
    )(o_nsa, u, u, bg, conv_w, x2, g1, w_out, ng, sh2, sc2, rwt, rb, _strict_upper(tm))


def _dispatch_plan(ridx, counts, n, seq):
    cnt = counts[:N_CLASSES, 0]
    padded = (cnt + MOE_ROW_TILE - 1) // MOE_ROW_TILE * MOE_ROW_TILE
    ends = jnp.cumsum(padded)
    starts = ends - padded
    cids = jnp.arange(N_CLASSES, dtype=jnp.int32)[:, None]
    base = jnp.sum(jnp.where(cids == ridx[0][None, :], starts[:, None], 0), axis=0)
    dest = (base + ridx[1]).astype(jnp.int32)
    td = min(MOE_DMA_TILE, seq)
    dest3 = dest.reshape(n // td, 1, td)
    n_tiles = n // MOE_ROW_TILE + N_CLASSES
    tile_start = jnp.arange(n_tiles, dtype=jnp.int32) * MOE_ROW_TILE
    tile_class = jnp.minimum(jnp.sum(tile_start[:, None] >= ends[None, :], axis=1), N_CLASSES - 1)
    group_base = (tile_class // PAIRS_PER_GROUP) * EXPERTS_PER_GROUP
    pair = tile_class % PAIRS_PER_GROUP
    tile_lo = (group_base + jnp.asarray(PAIR_LO, jnp.int32)[pair]).astype(jnp.int32)
    tile_hi = (group_base + jnp.asarray(PAIR_HI, jnp.int32)[pair]).astype(jnp.int32)
    n_used = (ends[-1] // MOE_ROW_TILE).reshape(1).astype(jnp.int32)
    last_tile = jnp.where(cnt > 0, ends // MOE_ROW_TILE - 1, -1)
    tail = n_used[0] + jnp.arange(N_CLASSES, dtype=jnp.int32)
    zero_tiles = jnp.concatenate([last_tile, jnp.where(tail < n_tiles, tail, -1)]).astype(jnp.int32)
    return dest3, tile_lo, tile_hi, n_used, zero_tiles, n_tiles


def _dispatch_kernel(ztile_ref, dest_ref, hw_ref, xs_hbm, zbuf, zsem, sem):
    td = hw_ref.shape[0]

    @pl.when(pl.program_id(0) == 0)
    def _():
        zbuf[...] = jnp.zeros_like(zbuf)

        def zero_copy(k):
            start = pl.multiple_of(ztile_ref[k] * MOE_ROW_TILE, MOE_ROW_TILE)
            return pltpu.make_async_copy(zbuf, xs_hbm.at[pl.ds(start, MOE_ROW_TILE)], zsem)

        for k in range(2 * N_CLASSES):
            @pl.when(ztile_ref[k] >= 0)
            def _():
                zero_copy(k).start()
        for k in range(2 * N_CLASSES):
            @pl.when(ztile_ref[k] >= 0)
            def _():
                zero_copy(k).wait()

    for r in range(td):
        pltpu.make_async_copy(hw_ref.at[pl.ds(r, 1)],
                              xs_hbm.at[pl.ds(dest_ref[0, 0, r], 1)], sem).start()
    pltpu.make_async_copy(hw_ref, xs_hbm.at[pl.ds(0, td)], sem).wait()


def moe_dispatch(hw, dest3, zero_tiles, n_tiles):
    n, cols = hw.shape
    td = dest3.shape[2]
    rows = n_tiles * MOE_ROW_TILE
    return pl.pallas_call(
        _dispatch_kernel,
        out_shape=jax.ShapeDtypeStruct((rows, cols), F32),
        grid_spec=pltpu.PrefetchScalarGridSpec(
            num_scalar_prefetch=1,
            grid=(n // td,),
            in_specs=[pl.BlockSpec((1, 1, td), lambda i, z: (i, 0, 0), memory_space=pltpu.SMEM),
                      pl.BlockSpec((td, cols), lambda i, z: (i, 0))],
            out_specs=pl.BlockSpec(memory_space=pl.ANY),
            scratch_shapes=[pltpu.VMEM((MOE_ROW_TILE, cols), F32), pltpu.SemaphoreType.DMA(()),
                            pltpu.SemaphoreType.DMA(())]),
        compiler_params=_cparams(("arbitrary",)),
        name="moe_dispatch",
    )(zero_tiles, dest3, hw)


def _expert_kernel(lo_ref, hi_ref, nused_ref, xs_ref, wg_lo, wu_lo, wd_lo, wg_hi, wu_hi, wd_hi, ys_ref,
                   *wb):
    t = pl.program_id(0)
    prev = jnp.maximum(t - 1, 0)

    for ids, srcs, dsts in ((lo_ref, (wg_lo, wu_lo, wd_lo), wb[:3]), (hi_ref, (wg_hi, wu_hi, wd_hi), wb[3:])):
        @pl.when((t == 0) | (ids[t] != ids[prev]))
        def _():
            for src, dst in zip(srcs, dsts):
                dst[...] = src[0, 0].astype(BF16)

    @pl.when(t < nused_ref[0])
    def _():
        d = xs_ref.shape[1] - LANES
        x = xs_ref[:, :d].astype(BF16)
        meta = xs_ref[:, d:]
        y = None
        for k in range(2):
            a = _mm(x, wb[3 * k][...])
            b = _mm(x, wb[3 * k + 1][...])
            he = (a * jax.nn.sigmoid(a)) * b
            yk = meta[:, k:k + 1] * _mm(he.astype(BF16), wb[3 * k + 2][...])
            y = yk if y is None else y + yk
        ys_ref[...] = y

    @pl.when(t >= nused_ref[0])
    def _():
        ys_ref[...] = jnp.zeros_like(ys_ref)


def moe_experts(xs, tile_lo, tile_hi, n_used, w_gate, w_up, w_down, layer, d):
    rows, cols = xs.shape
    n_tiles = rows // MOE_ROW_TILE
    lo_spec = lambda shape: pl.BlockSpec((1, 1) + shape, lambda t, lo, hi, nu: (layer, lo[t], 0, 0))
    hi_spec = lambda shape: pl.BlockSpec((1, 1) + shape, lambda t, lo, hi, nu: (layer, hi[t], 0, 0))
    shapes = ((d, D_EXPERT), (d, D_EXPERT), (D_EXPERT, d))
    return pl.pallas_call(
        _expert_kernel,
        out_shape=jax.ShapeDtypeStruct((rows, d), F32),
        grid_spec=pltpu.PrefetchScalarGridSpec(
            num_scalar_prefetch=3,
            grid=(n_tiles,),
            in_specs=[pl.BlockSpec((MOE_ROW_TILE, cols),
                                   lambda t, lo, hi, nu: (jnp.minimum(t, nu[0] - 1), 0))]
                     + [lo_spec(s) for s in shapes] + [hi_spec(s) for s in shapes],
            out_specs=pl.BlockSpec((MOE_ROW_TILE, d), lambda t, lo, hi, nu: (t, 0)),
            scratch_shapes=[pltpu.VMEM(s, BF16) for s in shapes + shapes]),
        compiler_params=_cparams(("arbitrary",)),
        name="moe_experts",
    )(tile_lo, tile_hi, n_used, xs, w_gate, w_up, w_down, w_gate, w_up, w_down)


def _combine_kernel(dest_ref, ys_hbm, x_ref, g2_ref, fn_ref, o_ref, buf, sem, *, final_norm):
    tc = x_ref.shape[0]

    for r in range(tc):
        pltpu.make_async_copy(ys_hbm.at[pl.ds(dest_ref[0, 0, r], 1)], buf.at[pl.ds(r, 1)], sem).start()
    pltpu.make_async_copy(ys_hbm.at[pl.ds(0, tc)], buf, sem).wait()
    x = x_ref[...] + g2_ref[0] * buf[...]
    if final_norm:
        ms = jnp.mean(x * x, axis=-1, keepdims=True)
        x = x * lax.rsqrt(ms + NORM_EPS) * fn_ref[...]
    o_ref[...] = x


def moe_combine(ys, dest3, x1, g2, fnorm, seq, final_norm):
    n, d = x1.shape
    tc = dest3.shape[2]
    tpb = seq // tc
    return pl.pallas_call(
        functools.partial(_combine_kernel, final_norm=final_norm),
        out_shape=jax.ShapeDtypeStruct((n, d), F32),
        grid=(n // tc,),
        in_specs=[pl.BlockSpec((1, 1, tc), lambda i: (i, 0, 0), memory_space=pltpu.SMEM),
                  pl.BlockSpec(memory_space=pl.ANY),
                  pl.BlockSpec((tc, d), lambda i: (i, 0)),
                  pl.BlockSpec((1, 1, d), lambda i: (i // tpb, 0, 0)),
                  pl.BlockSpec((1, d), lambda i: (0, 0))],
        out_specs=pl.BlockSpec((tc, d), lambda i: (i, 0)),
        scratch_shapes=[pltpu.VMEM((tc, d), F32), pltpu.SemaphoreType.DMA(())],
        compiler_params=_cparams(("arbitrary",)),
        name="moe_combine",
    )(dest3, ys, x1, g2, fnorm)


def moe_routed_experts(hw, ridx, counts, w_gate, w_up, w_down, layer, seq):
    n = hw.shape[0]
    d = hw.shape[1] - LANES
    dest3, tile_lo, tile_hi, n_used, zero_tiles, n_tiles = _dispatch_plan(ridx, counts, n, seq)
    xs = moe_dispatch(hw, dest3, zero_tiles, n_tiles)
    ys = moe_experts(xs, tile_lo, tile_hi, n_used, w_gate, w_up, w_down, layer, d)
    return ys, dest3


def _odd_in_kernel(dcur_ref, dnxt_ref, ys_hbm, x1_ref, gm_ref, g_ref, sh_ref, sc_ref, w_ref, mu_ref,
                   w0_ref, w2_ref, a0_ref, a2_ref, g2_ref, kk_ref, ka_ref, ones_ref, pw_ref, ps_ref,
                   x_ref, r_ref, lw_ref, km_ref, v_ref, kn_ref, kb_ref, gg_ref, op_ref,
                   rw_carry, u_carry, ybuf, ysem, *, tiles_per_seq, tm, n_steps):
    i = pl.program_id(0)
    first = (i % tiles_per_seq) == 0
    slot = i % 2

    def gather(dest_ref, to_slot):
        for r in range(tm):
            pltpu.make_async_copy(ys_hbm.at[pl.ds(dest_ref[0, 0, r], 1)],
                                  ybuf.at[to_slot, pl.ds(r, 1)], ysem.at[to_slot]).start()

    @pl.when(i == 0)
    def _():
        gather(dcur_ref, 0)

    @pl.when(i + 1 < n_steps)
    def _():
        gather(dnxt_ref, 1 - slot)

    pltpu.make_async_copy(ys_hbm.at[pl.ds(0, tm)], ybuf.at[slot], ysem.at[slot]).wait()
    x = x1_ref[...] + gm_ref[0] * ybuf[slot]
    x_ref[...] = x
    h = _norm_mod(x, g_ref[...], sh_ref[0], sc_ref[0])
    proj = _mm(h.astype(BF16), w_ref[...])

    rw = proj[:, :ODD_RW_COLS]
    row0 = jnp.where(first, 0.0, rw_carry[0:1, :])
    ridx = lax.broadcasted_iota(jnp.int32, (tm, 1), 0)
    prev = jnp.where(ridx == 0, row0, pltpu.roll(rw, 1, 0))
    rw_carry[0:1, :] = rw[tm - 1:tm, :]
    rw = rw + (prev - rw) * mu_ref[...]

    r = rw[:, 0:512]
    k = rw[:, 512:1024]
    v = rw[:, 1024:1536]
    wl = rw[:, 1536:1664]
    al = rw[:, 1664:1792]
    gl = rw[:, 1792:1920]
    z = -(w0_ref[...] + _mm(jnp.tanh(wl).astype(BF16), w2_ref[...]))
    softplus = jnp.maximum(z, 0.0) + jnp.log1p(jnp.exp(-jnp.abs(z)))
    w_log = -softplus - 0.5
    a = jax.nn.sigmoid(a0_ref[...] + _mm(al.astype(BF16), a2_ref[...]))
    gg_ref[...] = _mm(jax.nn.sigmoid(gl).astype(BF16), g2_ref[...])
    kk0 = k * kk_ref[...]
    ss = _split_sum(kk0 * kk0, ones_ref[...])
    kk = kk0 / jnp.maximum(jnp.sqrt(ss), 1e-12)
    r_ref[...] = r
    lw_ref[...] = -jnp.exp(w_log)
    km_ref[...] = k * (1.0 + (a - 1.0) * ka_ref[...])
    v_ref[...] = v
    kn_ref[...] = kk
    kb_ref[...] = kk * a

    u = proj[:, ODD_RW_COLS:]
    halo = jnp.where(first, 0.0, u_carry[...])
    u_carry[...] = u[tm - POOL_HALO:, :]
    ext = jnp.concatenate([halo, u], axis=0)
    tseq = (i % tiles_per_seq) * tm + ridx
    for gi, win in enumerate(POOL_WINDOWS):
        xg = ext[:, gi * POOL_GROUP:(gi + 1) * POOL_GROUP]
        s = xg
        step = 1
        while step < win:
            s = s + pltpu.roll(s, step, 0)
            step *= 2
        cnt = jnp.minimum(tseq + 1, win).astype(F32)
        pooled = s[POOL_HALO:] / cnt - xg[POOL_HALO:]
        mixed = _mm(pooled.astype(BF16), pw_ref[gi])
        op_ref[:, gi * POOL_GROUP:(gi + 1) * POOL_GROUP] = (
            mixed * ps_ref[:, gi * POOL_GROUP:(gi + 1) * POOL_GROUP]).astype(BF16)


def odd_in_proj(pending, g, sh, sc, w_pad, mu_pad, w0, w2p, a0, a2p, g2, k_k, k_a, ones_bd, pool_w,
                pool_scale, seq):
    ys, dest, x1, gate_moe = pending
    n, d = x1.shape
    tm = min(TOK_TILE, seq)
    tpb = seq // tm
    n_steps = n // tm
    row = lambda i: (i, 0)
    per_b = lambda i: (i // tpb, 0, 0)
    c2 = lambda i: (0, 0)
    full2 = lambda a: pl.BlockSpec(a.shape, c2)
    dest3 = dest.reshape(n_steps, 1, tm)
    return pl.pallas_call(
        functools.partial(_odd_in_kernel, tiles_per_seq=tpb, tm=tm, n_steps=n_steps),
        out_shape=[jax.ShapeDtypeStruct((n, d), F32)] + [jax.ShapeDtypeStruct((n, RWKV_DIM), F32)] * 7
                  + [jax.ShapeDtypeStruct((n, RWKV_DIM), BF16)],
        grid=(n_steps,),
        in_specs=[pl.BlockSpec((1, 1, tm), lambda i: (i, 0, 0), memory_space=pltpu.SMEM),
                  pl.BlockSpec((1, 1, tm), lambda i: (jnp.minimum(i + 1, n_steps - 1), 0, 0),
                               memory_space=pltpu.SMEM),
                  pl.BlockSpec(memory_space=pl.ANY),
                  pl.BlockSpec((tm, d), row), pl.BlockSpec((1, 1, d), per_b),
                  pl.BlockSpec((1, d), c2),
                  pl.BlockSpec((1, 1, d), per_b), pl.BlockSpec((1, 1, d), per_b),
                  full2(w_pad), full2(mu_pad), full2(w0), full2(w2p), full2(a0), full2(a2p),
                  full2(g2), full2(k_k), full2(k_a), full2(ones_bd),
                  pl.BlockSpec(pool_w.shape, lambda i: (0, 0, 0)), full2(pool_scale)],
        out_specs=[pl.BlockSpec((tm, d), row)] + [pl.BlockSpec((tm, RWKV_DIM), row)] * 8,
        scratch_shapes=[pltpu.VMEM((SUBLANES, ODD_RW_COLS), F32),
                        pltpu.VMEM((POOL_HALO, RWKV_DIM), F32),
                        pltpu.VMEM((2, tm, d), F32), pltpu.SemaphoreType.DMA((2,))],
        compiler_params=_cparams(("arbitrary",)),
        name="odd_in_proj",
    )(dest3, dest3, ys, x1, gate_moe, g, sh, sc, w_pad, mu_pad, w0, w2p, a0, a2p, g2, k_k, k_a, ones_bd,
      pool_w, pool_scale)


def _bmm(a, b):
    return lax.dot_general(a, b, (((2,), (1,)), ((0,), (0,))), preferred_element_type=F32)


def _bnt(a, b):
    return lax.dot_general(a, b, (((2,), (2,)), ((0,), (0,))), preferred_element_type=F32)


def _btn(a, b):
    return lax.dot_general(a, b, (((1,), (1,)), ((0,), (0,))), preferred_element_type=F32)


def _scan_prep_kernel(r_ref, lw_ref, km_ref, v_ref, kn_ref, kb_ref, qe_ref, y0_ref, mt_ref, ct_ref,
                      *, chunk, cb):
    L = chunk
    rows = cb * L
    n_pairs = N_RWKV_HEADS // 2
    two = 2 * L
    rowt = lax.broadcasted_iota(jnp.int32, (rows, 1), 0) % L
    lane = lax.broadcasted_iota(jnp.int32, (1, 1, LANES), 2)
    low = lane < HEAD_DIM
    ri = lax.broadcasted_iota(jnp.int32, (two, two), 0)
    ci = lax.broadcasted_iota(jnp.int32, (two, two), 1)
    same_blk = (ri // L) == (ci // L)
    strict = same_blk & ((ci % L) < (ri % L))
    incl = same_blk & ((ci % L) <= (ri % L))
    li = lax.broadcasted_iota(jnp.int32, (LANES, LANES), 0)
    lj = lax.broadcasted_iota(jnp.int32, (LANES, LANES), 1)
    same_head = (li // HEAD_DIM) == (lj // HEAD_DIM)
    eye = li == lj

    lw = lw_ref[...]
    cum = lw
    step = 1
    while step < L:
        cum = cum + jnp.where(rowt >= step, pltpu.roll(cum, step, 0), 0.0)
        step *= 2

    def to3(x):
        x3 = x.reshape(cb, L, RWKV_DIM)
        return jnp.concatenate([x3[:, :, p * LANES:(p + 1) * LANES] for p in range(n_pairs)], axis=0)

    def stack2(x):
        return jnp.concatenate([jnp.where(low, x, 0.0), jnp.where(low, 0.0, x)], axis=1)

    def fold(x):
        return x[:, :L, :] + x[:, L:, :]

    cum3 = to3(cum)
    lw3 = to3(lw)
    cum_l = cum3[:, L - 1:L, :]
    g_inv = jnp.exp(-cum3)
    g_tail = jnp.exp(cum_l - cum3)
    kb = to3(kb_ref[...])
    km = to3(km_ref[...])
    v = to3(v_ref[...])
    at_s = stack2(-to3(kn_ref[...]) * jnp.exp(cum3 - lw3))
    rt_s = stack2(to3(r_ref[...]) * jnp.exp(cum3))
    v_s = stack2(v).astype(BF16)
    lhs = jnp.concatenate([at_s, rt_s], axis=1).astype(BF16)
    rhs = jnp.concatenate([stack2(kb * g_inv), stack2(km * g_inv)], axis=1).astype(BF16)
    prod = _bnt(lhs, rhs)
    nmat = jnp.where(strict, prod[:, :two, :two], 0.0)
    a_ak = jnp.where(strict, prod[:, :two, two:], 0.0).astype(BF16)
    a_rb = jnp.where(incl, prod[:, two:, :two], 0.0).astype(BF16)
    a_rk = jnp.where(incl, prod[:, two:, two:], 0.0).astype(BF16)

    x = jnp.concatenate([at_s, _bmm(a_ak, v_s)], axis=2)
    npow = nmat
    step = 1
    while step < L:
        nb = npow.astype(BF16)
        x = x + _bmm(nb, x.astype(BF16))
        step *= 2
        if step < L:
            npow = _bmm(nb, nb)
    qy = _bmm(a_rb, x.astype(BF16))
    qe = fold(rt_s + qy[:, :, :LANES])
    y0 = fold(qy[:, :, LANES:] + _bmm(a_rk, v_s))
    wu = fold(x).astype(BF16)
    bwu = _btn((kb * g_tail).astype(BF16), wu)
    kv = _btn((km * g_tail).astype(BF16), v.astype(BF16))
    g_l = jnp.broadcast_to(jnp.exp(cum_l), (n_pairs * cb, LANES, LANES))
    mt = jnp.where(eye, g_l, 0.0) + jnp.where(same_head, bwu[:, :, :LANES], 0.0)
    ct = jnp.where(same_head, bwu[:, :, LANES:] + kv, 0.0)
    for p in range(n_pairs):
        sl = slice(p * LANES, (p + 1) * LANES)
        qe_ref[:, sl] = qe[p * cb:(p + 1) * cb].reshape(rows, LANES)
        y0_ref[:, sl] = y0[p * cb:(p + 1) * cb].reshape(rows, LANES)
        mt_ref[:, p] = mt[p * cb:(p + 1) * cb].astype(BF16)
        ct_ref[:, p] = ct[p * cb:(p + 1) * cb]


def _scan_state_kernel(qe_ref, y0_ref, mt_ref, ct_ref, y_ref, st_ref, *, batch):
    @pl.when(pl.program_id(0) == 0)
    def _():
        st_ref[...] = jnp.zeros_like(st_ref)

    n_pairs = N_RWKV_HEADS // 2
    qe = qe_ref[...]
    qe3 = jnp.concatenate([qe[:, :, p * LANES:(p + 1) * LANES] for p in range(n_pairs)], axis=0)
    st = st_ref[...].astype(BF16)
    y = _bmm(qe3.astype(BF16), st)
    for p in range(n_pairs):
        sl = slice(p * LANES, (p + 1) * LANES)
        y_ref[:, :, sl] = y[p * batch:(p + 1) * batch] + y0_ref[:, :, sl]
    mt = jnp.concatenate([mt_ref[:, 0, p] for p in range(n_pairs)], axis=0)
    ct = jnp.concatenate([ct_ref[:, 0, p] for p in range(n_pairs)], axis=0)
    st_ref[...] = _bmm(mt, st) + ct


def rwkv_scan(r, lw, km, v, kn, kb, batch, seq):
    n = batch * seq
    chunk = min(SCAN_CHUNK, seq)
    nc = seq // chunk
    cb = min(SCAN_CHUNKS_PER_STEP, nc)
    n_pairs = N_RWKV_HEADS // 2
    blk = pl.BlockSpec((cb * chunk, RWKV_DIM), lambda i: (i, 0))
    mblk = pl.BlockSpec((cb, n_pairs, LANES, LANES), lambda i: (i, 0, 0, 0))
    qe, y0, mt, ct = pl.pallas_call(
        functools.partial(_scan_prep_kernel, chunk=chunk, cb=cb),
        out_shape=[jax.ShapeDtypeStruct((n, RWKV_DIM), F32), jax.ShapeDtypeStruct((n, RWKV_DIM), F32),
                   jax.ShapeDtypeStruct((n // chunk, n_pairs, LANES, LANES), BF16),
                   jax.ShapeDtypeStruct((n // chunk, n_pairs, LANES, LANES), F32)],
        grid=(n // (cb * chunk),),
        in_specs=[blk] * 6,
        out_specs=[blk, blk, mblk, mblk],
        compiler_params=_cparams(("parallel",)),
        name="rwkv_scan_prep",
    )(r, lw, km, v, kn, kb)
    sblk = pl.BlockSpec((batch, chunk, RWKV_DIM), lambda c: (0, c, 0))
    smblk = pl.BlockSpec((batch, 1, n_pairs, LANES, LANES), lambda c: (0, c, 0, 0, 0))
    y = pl.pallas_call(
        functools.partial(_scan_state_kernel, batch=batch),
        out_shape=jax.ShapeDtypeStruct((batch, seq, RWKV_DIM), F32),
        grid=(nc,),
        in_specs=[sblk, sblk, smblk, smblk],
        out_specs=sblk,
        scratch_shapes=[pltpu.VMEM((n_pairs * batch, LANES, LANES), F32)],
        compiler_params=_cparams(("arbitrary",)),
        name="rwkv_scan_state",
    )(qe.reshape(batch, seq, RWKV_DIM), y0.reshape(batch, seq, RWKV_DIM),
      mt.reshape(batch, nc, n_pairs, LANES, LANES), ct.reshape(batch, nc, n_pairs, LANES, LANES))
    return y.reshape(n, RWKV_DIM)


def _odd_out_kernel(y_ref, r_ref, km_ref, v_ref, gg_ref, op_ref, rk_ref, lnw_ref, lnb_ref, ones_ref,
                    *tail_refs):
    ones = ones_ref[...]
    inv = 1.0 / HEAD_DIM
    y = y_ref[...]
    mean = _split_sum(y, ones) * inv
    yc = y - mean
    var = _split_sum(yc * yc, ones) * inv
    yn = yc * lax.rsqrt(var + LNX_EPS) * lnw_ref[...] + lnb_ref[...]
    bonus = _split_sum(r_ref[...] * km_ref[...] * rk_ref[...], ones) * v_ref[...]
    o_rwkv = (yn + bonus) * gg_ref[...]
    cat = jnp.concatenate([o_rwkv.astype(BF16), op_ref[...]], axis=1)
    _tail(cat, *tail_refs)


def odd_out_proj(y, r, km, v, gg, opool, r_k, lnx_w, lnx_b, ones_bd, x2, g1, w_out, ng, sh2, sc2,
                 rwt, rb, seq):
    n, d = x2.shape
    tm = min(TOK_TILE, seq)
    tpb = seq // tm
    row = lambda i: (i, 0)
    c2 = lambda i: (0, 0)
    act = pl.BlockSpec((tm, RWKV_DIM), row)
    vec = pl.BlockSpec((1, RWKV_DIM), c2)
    tin, tout, tscratch = _tail_specs(tm, d, tpb)
    return pl.pallas_call(
        _odd_out_kernel,
        out_shape=_tail_out_shapes(n, d),
        grid=(n // tm,),
        in_specs=[act] * 6 + [vec, vec, vec, pl.BlockSpec(ones_bd.shape, c2)] + tin,
        out_specs=tout,
        scratch_shapes=tscratch,
        compiler_params=_cparams(("arbitrary",)),
        name="odd_out_proj",
    )(y, r, km, v, gg, opool, r_k, lnx_w, lnx_b, ones_bd, x2, g1, w_out, ng, sh2, sc2, rwt, rb,
      _strict_upper(tm))


def _rope_tables(seq):
    half = HEAD_DIM // 2
    inv = ROPE_THETA ** (-jnp.arange(half, dtype=F32) / half)
    ang = jnp.arange(seq, dtype=F32)[:, None] * inv[None, :]
    return jnp.tile(jnp.cos(ang), (1, LANES // half)), jnp.tile(jnp.sin(ang), (1, LANES // half))


def _even_w_pad(w_in):
    d = w_in.shape[0]
    q_kv = w_in[:, :NSA_DIM + 6 * KV_DIM]
    gl = w_in[:, NSA_DIM + 6 * KV_DIM:NSA_DIM + 6 * KV_DIM + 24]
    rest = w_in[:, NSA_DIM + 6 * KV_DIM + 24:]
    z = jnp.zeros((d, LANES - 12), w_in.dtype)
    return jnp.concatenate([q_kv, gl[:, :12], z, gl[:, 12:], z, rest], axis=1).astype(BF16)


def _compress_params(cmp_pos, cmp_w1, cmp_w2):
    w1r = cmp_w1.reshape(2, 2, CMP_STRIDE, HEAD_DIM, CMP_HIDDEN).astype(BF16)
    z1 = jnp.zeros_like(w1r)
    w1_ext = jnp.stack([jnp.concatenate([w1r, z1], axis=-1), jnp.concatenate([z1, w1r], axis=-1)],
                       axis=3).reshape(2, 2, CMP_STRIDE * KV_DIM, N_KV_HEADS * CMP_HIDDEN)
    w2b = cmp_w2.astype(BF16)
    z2 = jnp.zeros_like(w2b)
    w2_ext = jnp.concatenate([jnp.concatenate([w2b, z2], axis=-1), jnp.concatenate([z2, w2b], axis=-1)],
                             axis=1)
    pos = cmp_pos.reshape(2, 2, CMP_STRIDE, 1, HEAD_DIM)
    pos_ext = jnp.broadcast_to(pos, (2, 2, CMP_STRIDE, N_KV_HEADS, HEAD_DIM)).reshape(
        2, 2, 1, CMP_STRIDE * KV_DIM)
    return pos_ext, w1_ext, w2_ext


def _nsa_tables(seq):
    n_blk = seq // SEL_BLOCK
    n_cmp = (seq - CMP_BLOCK) // CMP_STRIDE + 1
    n_cmp_pad = seq // CMP_STRIDE
    r = SEL_BLOCK // CMP_STRIDE
    c = CMP_BLOCK // CMP_STRIDE
    msel = np.zeros((n_cmp_pad, LANES), np.float32)
    for j in range(n_blk):
        for m in range(r):
            for n in range(c):
                idx = r * j + m + n
                if idx < n_cmp:
                    msel[idx, j] += 1.0
    selq = np.zeros((N_KV_HEADS, GQA, GQA * HEAD_DIM, LANES), np.float32)
    for h in range(N_KV_HEADS):
        for g in range(GQA):
            for dd in range(HEAD_DIM):
                selq[h, g, g * HEAD_DIM + dd, h * HEAD_DIM + dd] = 1.0
    return jnp.asarray(msel.T), jnp.asarray(selq, dtype=BF16)


def _odd_params(w_in, mu, w2, a2):
    d = w_in.shape[0]
    z64 = jnp.zeros((d, 64), w_in.dtype)
    w_pad = jnp.concatenate([w_in[:, :1536], w_in[:, 1536:1600], z64, w_in[:, 1600:1664], z64,
                             w_in[:, 1664:]], axis=1).astype(BF16)
    m64 = jnp.zeros((64,), mu.dtype)
    mu_pad = jnp.concatenate([mu[:1536], mu[1536:1600], m64, mu[1600:1664], m64, mu[1664:]])[None, :]
    zr = jnp.zeros((64, RWKV_DIM), w2.dtype)
    w2p = jnp.concatenate([w2, zr], axis=0).astype(BF16)
    a2p = jnp.concatenate([a2, zr], axis=0).astype(BF16)
    return w_pad, mu_pad, w2p, a2p


def _head_ones():
    idx = np.arange(RWKV_DIM) // HEAD_DIM
    return jnp.asarray((idx[:, None] == idx[None, :]).astype(np.float32), dtype=BF16)


def kernel(x, c, ada_w, ada_b, norm_mix, norm_ffn, even_w_in, even_cmp_pos, even_cmp_w1, even_cmp_w2,
           even_conv_w, even_w_out, odd_w_in, odd_mu, odd_w0, odd_w2, odd_a0, odd_a2, odd_g2, odd_k_k,
           odd_k_a, odd_r_k, odd_lnx_w, odd_lnx_b, odd_pool_w, odd_pool_scale, odd_w_out,
           router_w, router_b, moe_w_gate, moe_w_up, moe_w_down, final_norm):
    batch, seq, d = x.shape
    n = batch * seq
    depth = ada_w.shape[0]
    x2 = x.reshape(n, d)
    mod = ada_modulation(c, ada_w, ada_b)
    rwt = router_w.T
    rb = router_b.reshape(N_EXPERTS, 1)
    fnorm = final_norm.reshape(1, d)
    cos, sin = _rope_tables(seq)
    mselt, selq = _nsa_tables(seq)
    ones_bd = _head_ones()

    pending = None
    for layer in range(depth):
        m = mod[layer].reshape(batch, 6, 1, d)
        sh1, sc1, g1, sh2, sc2, g2 = (m[:, k] for k in range(6))
        ng_mix = norm_mix[layer].reshape(1, d)
        ng_ffn = norm_ffn[layer].reshape(1, d)
        i = layer // 2
        if layer % 2 == 0:
            (qn, qr, kc, vc, ks, vs, kw, vw, gate, u, bg) = even_in_proj(
                x2, ng_mix, sh1, sc1, _even_w_pad(even_w_in[i]), cos, sin, seq)
            pos_ext, w1_ext, w2_ext = _compress_params(even_cmp_pos[i], even_cmp_w1[i], even_cmp_w2[i])
            kcmp, vcmp = compress_kv(kc, vc, pos_ext, w1_ext, w2_ext, batch, seq)
            o_nsa = nsa_attention(qn, qr, kcmp, vcmp, ks, vs, kw, vw, gate, selq, mselt, batch, seq)
            x1, hw, ridx, counts = even_out_proj(o_nsa, u, bg, even_conv_w[i], x2, g1,
                                                 even_w_out[i].astype(BF16), ng_ffn, sh2, sc2, rwt, rb, seq)
        else:
            w_pad, mu_pad, w2p, a2p = _odd_params(odd_w_in[i], odd_mu[i], odd_w2[i], odd_a2[i])
            vec = lambda a: a.reshape(1, RWKV_DIM)
            (x2, r, lw, km, v, kn, kb, gg, opool) = odd_in_proj(
                pending, ng_mix, sh1, sc1, w_pad, mu_pad, vec(odd_w0[i]), w2p, vec(odd_a0[i]), a2p,
                odd_g2[i].astype(BF16), vec(odd_k_k[i]), vec(odd_k_a[i]), ones_bd,
                odd_pool_w[i].astype(BF16), vec(odd_pool_scale[i]), seq)
            y = rwkv_scan(r, lw, km, v, kn, kb, batch, seq)
            x1, hw, ridx, counts = odd_out_proj(
                y, r, km, v, gg, opool, vec(odd_r_k[i]), vec(odd_lnx_w[i]), vec(odd_lnx_b[i]), ones_bd,
                x2, g1, odd_w_out[i].astype(BF16), ng_ffn, sh2, sc2, rwt, rb, seq)
        ys, dest3 = moe_routed_experts(hw, ridx, counts, moe_w_gate, moe_w_up, moe_w_down, layer, seq)
        if layer + 1 < depth and (layer + 1) % 2 == 1:
            pending = (ys, dest3, x1, g2)
        else:
            x2 = moe_combine(ys, dest3, x1, g2, fnorm, seq, final_norm=(layer == depth - 1))
    return x2.reshape(batch, seq, d)
```
